```python
import math
import jax
import jax.numpy as jnp
from jax import lax
import numpy as np

D_MODEL = 1024
BATCH = 8
SEQ = 4096
DEPTH = 2

HEAD_DIM = 64
N_HEADS = D_MODEL // HEAD_DIM
N_HEADS_FOX = N_HEADS // 2
N_HEADS_MOBA = N_HEADS - N_HEADS_FOX
N_HEADS_DIL = N_HEADS
ATTN_SCALE = HEAD_DIM ** -0.5
Q_BLOCK = 128
MOBA_BLOCK = 256
MOBA_TOPK = 3
MOBA_QCHUNK = 16
DIL_PATTERNS = ((128, 1), (512, 4), (2048, 16))
NUM_BUCKETS = 32
MAX_DISTANCE = 2048
D_FF = 2816
N_EXPERTS = 8
TOP_K = 2
D_FF_EXPERT = 3584
NORM_EPS = 1e-6
NEG_INF = float("-inf")

kernel_name = "hybrid_fox_moba_dilated_moe_trunk"


def rms_norm(x, g):
    xf = x.astype(jnp.float32)
    y = xf * lax.rsqrt(jnp.mean(xf * xf, axis=-1, keepdims=True) + NORM_EPS)
    return (y * g.astype(jnp.float32)).astype(x.dtype)


def modulate(h, g, shift, scale):
    return rms_norm(h, g) * (1 + scale[:, None, :]) + shift[:, None, :]


def to_heads(t, n_heads):
    b, s, _ = t.shape
    return t.reshape(b, s, n_heads, HEAD_DIM).transpose(0, 2, 1, 3)


def from_heads(t):
    b, h, s, dh = t.shape
    return t.transpose(0, 2, 1, 3).reshape(b, s, h * dh)


def t5_bucket(dist):
    n = jnp.maximum(dist, 0)
    max_exact = NUM_BUCKETS // 2
    nf = jnp.maximum(n, 1).astype(jnp.float32)
    log_ratio = jnp.log(nf / max_exact) / math.log(MAX_DISTANCE / max_exact)
    large = max_exact + (log_ratio * (NUM_BUCKETS - max_exact)).astype(jnp.int32)
    large = jnp.minimum(large, NUM_BUCKETS - 1)
    return jnp.where(n < max_exact, n, large)


def fox_attention(q, k, v, log_f):
    b, h, s, dh = q.shape
    n_blk = s // Q_BLOCK
    cum = jnp.cumsum(log_f, axis=-1)
    q_blocks = jnp.moveaxis(q.reshape(b, h, n_blk, Q_BLOCK, dh), 2, 0)
    c_blocks = jnp.moveaxis(cum.reshape(b, h, n_blk, Q_BLOCK), 2, 0)
    k_pos = jnp.arange(s)

    def one_block(args):
        qi, ci, bi = args
        q_pos = bi * Q_BLOCK + jnp.arange(Q_BLOCK)
        sc = jnp.einsum("bhqd,bhkd->bhqk", qi, k).astype(jnp.float32) * ATTN_SCALE
        sc = sc + ci[..., None] - cum[:, :, None, :]
        sc = jnp.where(k_pos[None, :] <= q_pos[:, None], sc, NEG_INF)
        p = jax.nn.softmax(sc, axis=-1)
        return jnp.einsum("bhqk,bhkd->bhqd", p.astype(v.dtype), v)

    out = lax.map(one_block, (q_blocks, c_blocks, jnp.arange(n_blk)))
    return jnp.moveaxis(out, 0, 2).reshape(b, h, s, dh)


def moba_attention(q, k, v, bias_t):
    b, h, s, dh = q.shape
    n_blk = -(-s // MOBA_BLOCK)
    s_pad = n_blk * MOBA_BLOCK
    pad = ((0, 0), (0, 0), (0, s_pad - s), (0, 0))
    k_blocks = jnp.pad(k, pad).reshape(b, h, n_blk, MOBA_BLOCK, dh)
    v_blocks = jnp.pad(v, pad).reshape(b, h, n_blk, MOBA_BLOCK, dh)
    k_mean = jnp.mean(k_blocks.astype(jnp.float32), axis=3)
    gate = jnp.einsum("bhsd,bhnd->bhsn", q.astype(jnp.float32), k_mean)
    q_blk = jnp.arange(s) // MOBA_BLOCK
    fully_past = jnp.arange(n_blk)[None, :] < q_blk[:, None]
    gate = jnp.where(fully_past, gate, NEG_INF)
    if n_blk < MOBA_TOPK:
        gate = jnp.pad(gate, ((0, 0), (0, 0), (0, 0), (0, MOBA_TOPK - n_blk)), constant_values=NEG_INF)
    top_val, top_idx = lax.top_k(gate, MOBA_TOPK)
    sel_valid = jnp.isfinite(top_val)
    top_idx = jnp.minimum(top_idx, n_blk - 1)

    n_qc = s // MOBA_QCHUNK

    def chunks(t):
        return jnp.moveaxis(t.reshape(b, h, n_qc, MOBA_QCHUNK, *t.shape[3:]), 2, 0)

    b_ix = jnp.arange(b)[:, None, None, None]
    h_ix = jnp.arange(h)[None, :, None, None]
    blk_off = jnp.arange(MOBA_BLOCK)
    n_sel = MOBA_TOPK * MOBA_BLOCK

    def one_chunk(args):
        qc, idx, valid, ci = args
        t_pos = ci * MOBA_QCHUNK + jnp.arange(MOBA_QCHUNK)
        k_sel = k_blocks[b_ix, h_ix, idx]
        v_sel = v_blocks[b_ix, h_ix, idx]
        s_sel = jnp.einsum("bhqd,bhqjnd->bhqjn", qc, k_sel).astype(jnp.float32) * ATTN_SCALE
        dist_sel = t_pos[None, None, :, None, None] - (idx[..., None] * MOBA_BLOCK + blk_off)
        s_sel = s_sel + bias_t[h_ix[..., None], t5_bucket(dist_sel)]
        s_sel = jnp.where(valid[..., None], s_sel, NEG_INF)
        own = (ci * MOBA_QCHUNK) // MOBA_BLOCK
        k_own = lax.dynamic_index_in_dim(k_blocks, own, axis=2, keepdims=False)
        v_own = lax.dynamic_index_in_dim(v_blocks, own, axis=2, keepdims=False)
        s_own = jnp.einsum("bhqd,bhnd->bhqn", qc, k_own).astype(jnp.float32) * ATTN_SCALE
        dist_own = t_pos[:, None] - (own * MOBA_BLOCK + blk_off)[None, :]
        s_own = s_own + bias_t[:, t5_bucket(dist_own)][None]
        s_own = jnp.where(dist_own >= 0, s_own, NEG_INF)
        p = jax.nn.softmax(jnp.concatenate([s_sel.reshape(b, h, MOBA_QCHUNK, n_sel), s_own], axis=-1), axis=-1)
        p_sel = p[..., :n_sel].reshape(b, h, MOBA_QCHUNK, MOBA_TOPK, MOBA_BLOCK)
        p_own = p[..., n_sel:]
        return (jnp.einsum("bhqjn,bhqjnd->bhqd", p_sel.astype(v.dtype), v_sel)
                + jnp.einsum("bhqn,bhnd->bhqd", p_own.astype(v.dtype), v_own))

    out = lax.map(one_chunk, (chunks(q), chunks(top_idx), chunks(sel_valid), jnp.arange(n_qc)))
    return jnp.moveaxis(out, 0, 2).reshape(b, h, s, dh)


def dilated_branch(q, k, v, bias_t, span, dil):
    b, h, s, dh = q.shape
    unit = span * dil
    s_pad = -(-s // unit) * unit
    sub_len = s_pad // dil
    nc = sub_len // span
    pad = ((0, 0), (0, 0), (0, s_pad - s), (0, 0))

    def split(t):
        t = jnp.pad(t, pad).reshape(b, h, sub_len, dil, dh).transpose(0, 1, 3, 2, 4)
        return t.reshape(b, h, dil, nc, span, dh)

    def band(t):
        prev = jnp.pad(t, ((0, 0), (0, 0), (0, 0), (1, 0), (0, 0), (0, 0)))[:, :, :, :-1]
        return jnp.concatenate([prev, t], axis=4)

    def unsplit(t):
        rest = t.shape[5:]
        t = t.reshape(b, h, dil, sub_len, *rest)
        t = jnp.swapaxes(t, 2, 3).reshape(b, h, s_pad, *rest)
        return t[:, :, :s]

    qc = split(q)
    k_band = band(split(k))
    v_band = band(split(v))
    rel = jnp.arange(span)[:, None] + span - jnp.arange(2 * span)[None, :]
    in_band = (rel >= 0) & (rel <= span)
    not_before_start = (jnp.arange(nc)[:, None, None] > 0) | (jnp.arange(2 * span)[None, None, :] >= span)
    mask = in_band[None] & not_before_start
    bias = bias_t[:, t5_bucket(rel * dil)]
    sc = jnp.einsum("bhrcqd,bhrckd->bhrcqk", qc, k_band).astype(jnp.float32) * ATTN_SCALE
    sc = jnp.where(mask, sc + bias[None, :, None, None], NEG_INF)
    m = jnp.max(sc, axis=-1)
    e = jnp.exp(sc - m[..., None])
    l = jnp.sum(e, axis=-1)
    o = jnp.einsum("bhrcqk,bhrckd->bhrcqd", e, v_band.astype(jnp.float32)) / l[..., None]
    return unsplit(m), unsplit(l), unsplit(o)


def dilated_attention(q, k, v, bias_t):
    ms, ls, os_ = [], [], []
    for window, dil in DIL_PATTERNS:
        m, l, o = dilated_branch(q, k, v, bias_t, window // dil, dil)
        ms.append(m)
        ls.append(l)
        os_.append(o)
    m_all = jnp.stack(ms)
    w = jnp.stack(ls) * jnp.exp(m_all - jnp.max(m_all, axis=0))
    o = jnp.sum(w[..., None] * jnp.stack(os_), axis=0) / jnp.sum(w, axis=0)[..., None]
    return o.astype(q.dtype)


def even_mixer(u, w_in, gate_bias, w_out, rel_bias_table):
    d_a = N_HEADS_FOX * HEAD_DIM
    d_b = N_HEADS_MOBA * HEAD_DIM
    cuts = np.cumsum([d_a, d_a, d_a, N_HEADS_FOX, d_b, d_b]).tolist()
    q_a, k_a, v_a, f_a, q_b, k_b, v_b = jnp.split(u @ w_in, cuts, axis=-1)
    log_f = jax.nn.log_sigmoid(f_a.astype(jnp.float32) + gate_bias.astype(jnp.float32)).transpose(0, 2, 1)
    o_fox = fox_attention(to_heads(q_a, N_HEADS_FOX), to_heads(k_a, N_HEADS_FOX),
                          to_heads(v_a, N_HEADS_FOX), log_f)
    bias_moba = rel_bias_table.T[N_HEADS_FOX:]
    o_moba = moba_attention(to_heads(q_b, N_HEADS_MOBA), to_heads(k_b, N_HEADS_MOBA),
                            to_heads(v_b, N_HEADS_MOBA), bias_moba)
    o = from_heads(jnp.concatenate([o_fox, o_moba.astype(o_fox.dtype)], axis=1))
    return o @ w_out


def odd_mixer(u, w_in, w_out, rel_bias_table):
    q, k, v = jnp.split(u @ w_in, 3, axis=-1)
    o = dilated_attention(to_heads(q, N_HEADS_DIL), to_heads(k, N_HEADS_DIL),
                          to_heads(v, N_HEADS_DIL), rel_bias_table.T)
    return from_heads(o) @ w_out


def swiglu(u, w_gate, w_up, w_down):
    return (jax.nn.silu(u @ w_gate) * (u @ w_up)) @ w_down


def moe_swiglu(u, router_w, w_gate, w_up, w_down):
    b, s, d = u.shape
    t = u.reshape(b * s, d)
    logits = (t @ router_w).astype(jnp.float32)
    top_v, top_i = lax.top_k(logits, TOP_K)
    top_p = jax.nn.softmax(top_v, axis=-1)
    gates = jnp.einsum("nk,nke->ne", top_p, jax.nn.one_hot(top_i, N_EXPERTS, dtype=jnp.float32))
    out = jnp.zeros_like(t)
    for e in range(N_EXPERTS):
        out = out + gates[:, e:e + 1].astype(t.dtype) * swiglu(t, w_gate[e], w_up[e], w_down[e])
    return out.reshape(b, s, d)


def setup_inputs(seed: int = 0) -> dict:
    key = jax.random.key(seed)
    ks = jax.random.split(key, 18)
    D = D_MODEL
    n_even = (DEPTH + 1) // 2
    n_odd = DEPTH // 2
    d_attn = N_HEADS * HEAD_DIM
    d_in_even = 3 * N_HEADS_FOX * HEAD_DIM + N_HEADS_FOX + 3 * N_HEADS_MOBA * HEAD_DIM

    def rn(k, shape):
        return jax.random.normal(k, shape, jnp.float32)

    def w(k, shape, fan_in, gain=1.0):
        return (gain * fan_in ** -0.5) * rn(k, shape)

    return {
        "x": rn(ks[0], (BATCH, SEQ, D)),
        "c": rn(ks[1], (BATCH, D)),
        "mod_w": w(ks[2], (DEPTH, D, 6 * D), D, 0.5),
        "mod_b": 0.02 * rn(ks[3], (DEPTH, 6 * D)),
        "norm_g": 1.0 + 0.1 * rn(ks[4], (DEPTH, 4, D)),
        "attn_in_w_even": w(ks[5], (n_even, D, d_in_even), D),
        "fox_gate_bias": 1.0 + 0.5 * rn(ks[6], (n_even, N_HEADS_FOX)),
        "attn_out_w_even": w(ks[7], (n_even, d_attn, D), d_attn),
        "attn_in_w_odd": w(ks[8], (n_odd, D, 3 * N_HEADS_DIL * HEAD_DIM), D),
        "attn_out_w_odd": w(ks[9], (n_odd, N_HEADS_DIL * HEAD_DIM, D), N_HEADS_DIL * HEAD_DIM),
        "rel_bias_table": 0.5 * rn(ks[10], (NUM_BUCKETS, N_HEADS)),
        "ffn_w_gate": w(ks[11], (n_even, D, D_FF), D),
        "ffn_w_up": w(ks[12], (n_even, D, D_FF), D),
        "ffn_w_down": w(ks[13], (n_even, D_FF, D), D_FF),
        "router_w": w(ks[14], (n_odd, D, N_EXPERTS), D),
        "exp_w_gate": w(ks[15], (n_odd, N_EXPERTS, D, D_FF_EXPERT), D),
        "exp_w_up": w(ks[16], (n_odd, N_EXPERTS, D, D_FF_EXPERT), D),
        "exp_w_down": w(ks[17], (n_odd, N_EXPERTS, D_FF_EXPERT, D), D_FF_EXPERT),
    }


def reference(x, c, mod_w, mod_b, norm_g, attn_in_w_even, fox_gate_bias, attn_out_w_even,
              attn_in_w_odd, attn_out_w_odd, rel_bias_table, ffn_w_gate, ffn_w_up, ffn_w_down,
              router_w, exp_w_gate, exp_w_up, exp_w_down):
    mods = jnp.einsum("bd,lde->lbe", jax.nn.silu(c), mod_w) + mod_b[:, None, :]
    h = x
    for layer in range(DEPTH):
        j = layer // 2
        sh1, sc1, g1, sh2, sc2, g2 = jnp.split(mods[layer], 6, axis=-1)
        gains = norm_g[layer]
        u = modulate(h, gains[0], sh1, sc1)
        if layer % 2 == 0:
            y = even_mixer(u, attn_in_w_even[j], fox_gate_bias[j], attn_out_w_even[j], rel_bias_table)
        else:
            y = odd_mixer(u, attn_in_w_odd[j], attn_out_w_odd[j], rel_bias_table)
        h = h + g1[:, None, :] * rms_norm(y, gains[1])
        u = modulate(h, gains[2], sh2, sc2)
        if layer % 2 == 0:
            y = swiglu(u, ffn_w_gate[j], ffn_w_up[j], ffn_w_down[j])
        else:
            y = moe_swiglu(u, router_w[j], exp_w_gate[j], exp_w_up[j], exp_w_down[j])
        h = h + g2[:, None, :] * rms_norm(y, gains[3])
    return h
```

```python
import functools
import math

import numpy as np
import jax
import jax.numpy as jnp
from jax import lax
from jax.experimental import pallas as pl
from jax.experimental.pallas import tpu as pltpu

F32 = jnp.float32
BF16 = jnp.bfloat16
I32 = jnp.int32

HEAD_DIM = 64
LANES = 128
N_HEADS = 16
N_HEADS_FOX = 8
ATTN_SCALE = HEAD_DIM ** -0.5
MOBA_BLOCK = 256
MOBA_TOPK = 3
DIL_PATTERNS = ((128, 1), (512, 4), (2048, 16))
NUM_BUCKETS = 32
MAX_DISTANCE = 2048
N_EXPERTS = 8
NORM_EPS = 1e-6
NEG_INF = float("-inf")

VMEM_LIMIT = 56 * 1024 * 1024

TM_PROJ = 512
TM_FFN = 512
TQ_FOX = 512
TM_ROUTE = 512
TM_SCATTER = 256
TM_EXPERT = 512
TM_COMBINE = 256
FF_STEPS_EXPERT = 2


def _cparams(sem):
    return pltpu.CompilerParams(dimension_semantics=sem, vmem_limit_bytes=VMEM_LIMIT)


def _t5_bucket_np(n):
    n = np.maximum(n, 0)
    max_exact = NUM_BUCKETS // 2
    nf = np.maximum(n, 1).astype(np.float64)
    large = max_exact + (np.log(nf / max_exact) / math.log(MAX_DISTANCE / max_exact)
                         * (NUM_BUCKETS - max_exact)).astype(np.int64)
    large = np.minimum(large, NUM_BUCKETS - 1)
    return np.where(n < max_exact, n, large)


_MAX_DIST = 1 << 16
_BUCKET_OF = _t5_bucket_np(np.arange(_MAX_DIST))
_BUCKET_THR = [int(np.searchsorted(_BUCKET_OF, k, side="left")) for k in range(NUM_BUCKETS)]


def _bias_from_dist(tab_ref, h, dist, dlo, dhi):
    lo_b = int(_BUCKET_OF[max(dlo, 0)])
    hi_b = int(_BUCKET_OF[dhi])
    val = jnp.zeros(dist.shape, F32) + tab_ref[lo_b, h]
    for k in range(lo_b + 1, hi_b + 1):
        val = jnp.where(dist >= _BUCKET_THR[k], tab_ref[k, h], val)
    return val


def _dil_bias_kernel(tab_ref, o_ref):
    h = pl.program_id(0)
    for g, (window, dil) in enumerate(DIL_PATTERNS):
        span = window // dil
        i = lax.broadcasted_iota(I32, (span, 2 * span), 0)
        j = lax.broadcasted_iota(I32, (span, 2 * span), 1)
        rel = i + span - j
        val = _bias_from_dist(tab_ref, h, rel * dil, 0, span * dil)
        band = jnp.where(rel >= 0, jnp.where(rel <= span, val, NEG_INF), NEG_INF)
        o_ref[0, 2 * g] = band
        o_ref[0, 2 * g + 1] = jnp.where(j >= span, band, NEG_INF)


def _moba_bias_kernel(tab_ref, o_ref, *, n_blk, head0):
    h = pl.program_id(0) + head0
    i = lax.broadcasted_iota(I32, (MOBA_BLOCK, MOBA_BLOCK), 0)
    j = lax.broadcasted_iota(I32, (MOBA_BLOCK, MOBA_BLOCK), 1)
    for d in range(n_blk):
        dist = d * MOBA_BLOCK + i - j
        val = _bias_from_dist(tab_ref, h, dist, d * MOBA_BLOCK - (MOBA_BLOCK - 1),
                              d * MOBA_BLOCK + (MOBA_BLOCK - 1))
        if d == 0:
            val = jnp.where(dist >= 0, val, NEG_INF)
        o_ref[0, d] = val


def _bias_tiles(rel_bias_table, s_len):
    n_blk = s_len // MOBA_BLOCK
    span = DIL_PATTERNS[0][0]
    n_var = 2 * len(DIL_PATTERNS)
    smem = pl.BlockSpec(memory_space=pltpu.SMEM)
    dil = pl.pallas_call(
        _dil_bias_kernel,
        grid=(N_HEADS,),
        in_specs=[smem],
        out_specs=pl.BlockSpec((1, n_var, span, 2 * span), lambda h: (h, 0, 0, 0)),
        out_shape=jax.ShapeDtypeStruct((N_HEADS, n_var, span, 2 * span), F32),
        compiler_params=_cparams(("parallel",)),
        name="dil_bias",
    )(rel_bias_table)
    n_moba = N_HEADS - N_HEADS_FOX
    moba = pl.pallas_call(
        functools.partial(_moba_bias_kernel, n_blk=n_blk, head0=N_HEADS_FOX),
        grid=(n_moba,),
        in_specs=[smem],
        out_specs=pl.BlockSpec((1, n_blk, MOBA_BLOCK, MOBA_BLOCK), lambda h: (h, 0, 0, 0)),
        out_shape=jax.ShapeDtypeStruct((n_moba, n_blk, MOBA_BLOCK, MOBA_BLOCK), F32),
        compiler_params=_cparams(("parallel",)),
        name="moba_bias",
    )(rel_bias_table)
    return dil, moba


def _split_bf16(a):
    hi = a.astype(BF16)
    lo = (a - hi.astype(F32)).astype(BF16)
    return hi, lo


def _dot(a, b):
    return jnp.dot(a, b, preferred_element_type=F32)


def _dot_nt(a, b):
    return lax.dot_general(a, b, (((1,), (1,)), ((), ())), preferred_element_type=F32)


def _dot_split(a, b):
    a_hi, a_lo = _split_bf16(a)
    b_hi, b_lo = _split_bf16(b)
    return _dot(a_hi, b_hi) + (_dot(a_hi, b_lo) + _dot(a_lo, b_hi))


def _rms(x):
    return x * lax.rsqrt(jnp.mean(x * x, axis=-1, keepdims=True) + NORM_EPS)


def _modulate(x, gain, scale, shift):
    return (_rms(x) * gain) * (1.0 + scale) + shift


def _silu(x):
    return x * jax.nn.sigmoid(x)


def _head_select(lane_shape):
    lane = lax.broadcasted_iota(I32, lane_shape, len(lane_shape) - 1)
    return lane < HEAD_DIM


def _mods_kernel(c_ref, w_ref, b_ref, o_ref):
    o_ref[0] = _dot_split(_silu(c_ref[...]), w_ref[0]) + b_ref[0]


def _mods(c, mod_w, mod_b):
    depth, d, e = mod_w.shape
    b = c.shape[0]
    tn = 1536
    return pl.pallas_call(
        _mods_kernel,
        grid=(depth, e // tn),
        in_specs=[pl.BlockSpec((b, d), lambda l, j: (0, 0)),
                  pl.BlockSpec((1, d, tn), lambda l, j: (l, 0, j)),
                  pl.BlockSpec((1, 1, tn), lambda l, j: (l, 0, j))],
        out_specs=pl.BlockSpec((1, b, tn), lambda l, j: (l, 0, j)),
        out_shape=jax.ShapeDtypeStruct((depth, b, e), F32),
        compiler_params=_cparams(("parallel", "parallel")),
        name="adaln_mods",
    )(c, mod_w, mod_b.reshape(depth, 1, e))


def _inproj_even_kernel(h_ref, g_ref, sc_ref, sh_ref, w_ref, wft_ref, gb_ref,
                        qa_ref, ka_ref, va_ref, qb_ref, kb_ref, vb_ref, cum_ref, carry_ref, *, tm):
    u = _modulate(h_ref[0], g_ref[...], sc_ref[0], sh_ref[0]).astype(BF16)
    outs = (qa_ref, ka_ref, va_ref, qb_ref, kb_ref, vb_ref)
    width = qa_ref.shape[-1]
    for i, o_ref in enumerate(outs):
        r = _dot(u, w_ref[:, i * width:(i + 1) * width])
        if i % 3 == 0:
            r = r * ATTN_SCALE
        o_ref[0] = r.astype(BF16)
    x = _dot_nt(wft_ref[...], u) + gb_ref[...]
    lf = jnp.minimum(x, 0.0) - jnp.log1p(jnp.exp(-jnp.abs(x)))
    col = lax.broadcasted_iota(I32, lf.shape, 1)
    k = 1
    while k < tm:
        lf = lf + jnp.where(col >= k, pltpu.roll(lf, k, axis=1), 0.0)
        k *= 2

    @pl.when(pl.program_id(1) == 0)
    def _():
        carry_ref[...] = jnp.zeros_like(carry_ref)

    cum = lf + carry_ref[:, 0:1]
    cum_ref[0] = cum
    carry_ref[...] = jnp.broadcast_to(cum[:, tm - 1:tm], carry_ref.shape)


def _inproj_even(h, gain, scale, shift, w_in, gate_bias):
    b, s, d = h.shape
    tm = TM_PROJ
    da = N_HEADS_FOX * HEAD_DIM
    cuts = np.cumsum([da, da, da, N_HEADS_FOX, da, da]).tolist()
    q_a, k_a, v_a, f_a, q_b, k_b, v_b = jnp.split(w_in, cuts, axis=1)
    w = jnp.concatenate([q_a, k_a, v_a, q_b, k_b, v_b], axis=1).astype(BF16)
    wft = f_a.T.astype(BF16)
    gb = gate_bias.astype(F32).reshape(N_HEADS_FOX, 1)
    act = jax.ShapeDtypeStruct((b, s, da), BF16)
    act_spec = pl.BlockSpec((1, tm, da), lambda bi, si: (bi, si, 0))
    vec = pl.BlockSpec((1, 1, d), lambda bi, si: (bi, 0, 0))
    outs = pl.pallas_call(
        functools.partial(_inproj_even_kernel, tm=tm),
        grid=(b, s // tm),
        in_specs=[pl.BlockSpec((1, tm, d), lambda bi, si: (bi, si, 0)),
                  pl.BlockSpec((1, d), lambda bi, si: (0, 0)),
                  vec, vec,
                  pl.BlockSpec(w.shape, lambda bi, si: (0, 0)),
                  pl.BlockSpec(wft.shape, lambda bi, si: (0, 0)),
                  pl.BlockSpec(gb.shape, lambda bi, si: (0, 0))],
        out_specs=[act_spec] * 6 + [pl.BlockSpec((1, N_HEADS_FOX, tm), lambda bi, si: (bi, 0, si))],
        out_shape=[act] * 6 + [jax.ShapeDtypeStruct((b, N_HEADS_FOX, s), F32)],
        scratch_shapes=[pltpu.VMEM((N_HEADS_FOX, LANES), F32)],
        compiler_params=_cparams(("parallel", "arbitrary")),
        name="inproj_even",
    )(h, gain.reshape(1, d), scale.reshape(b, 1, d), shift.reshape(b, 1, d), w, wft, gb)
    return outs


def _inproj_odd_kernel(h_ref, g_ref, sc_ref, sh_ref, w_ref, q_ref, k_ref, v_ref):
    u = _modulate(h_ref[0], g_ref[...], sc_ref[0], sh_ref[0]).astype(BF16)
    width = q_ref.shape[-1]
    q_ref[0] = _dot(u, w_ref[:, 0:width]) * ATTN_SCALE
    k_ref[0] = _dot(u, w_ref[:, width:2 * width])
    v_ref[0] = _dot(u, w_ref[:, 2 * width:3 * width])


def _inproj_odd(h, gain, scale, shift, w_in):
    b, s, d = h.shape
    tm = TM_PROJ
    dq = w_in.shape[1] // 3
    act = jax.ShapeDtypeStruct((b, s, dq), F32)
    act_spec = pl.BlockSpec((1, tm, dq), lambda bi, si: (bi, si, 0))
    vec = pl.BlockSpec((1, 1, d), lambda bi, si: (bi, 0, 0))
    return pl.pallas_call(
        _inproj_odd_kernel,
        grid=(b, s // tm),
        in_specs=[pl.BlockSpec((1, tm, d), lambda bi, si: (bi, si, 0)),
                  pl.BlockSpec((1, d), lambda bi, si: (0, 0)),
                  vec, vec,
                  pl.BlockSpec(w_in.shape, lambda bi, si: (0, 0))],
        out_specs=[act_spec] * 3,
        out_shape=[act] * 3,
        compiler_params=_cparams(("parallel", "parallel")),
        name="inproj_odd",
    )(h, gain.reshape(1, d), scale.reshape(b, 1, d), shift.reshape(b, 1, d), w_in.astype(BF16))


def _flash_update(s, v, m_ref, l_ref, acc_ref):
    m_prev = m_ref[...]
    m_new = jnp.maximum(m_prev, jnp.max(s, axis=1, keepdims=True))
    alpha = jnp.exp(m_prev - m_new)
    p = jnp.exp(s - m_new[:, 0:1])
    l_ref[...] = alpha * l_ref[...] + jnp.sum(p, axis=1, keepdims=True)
    acc_ref[...] = alpha * acc_ref[...] + _dot(p.astype(BF16), v)
    m_ref[...] = m_new


def _fox_kernel(q_ref, k_ref, v_ref, cum_ref, o_ref, m_ref, l_ref, acc_ref, *, tq):
    qi = pl.program_id(2)
    q = q_ref[0]
    left = _head_select((1, LANES))
    row = lax.broadcasted_iota(I32, (tq, tq), 0)
    col = lax.broadcasted_iota(I32, (tq, tq), 1)
    causal = col <= row
    for j in range(2):
        qj = jnp.where(left if j == 0 else jnp.logical_not(left), q, jnp.zeros_like(q))
        m_ref[j] = jnp.full(m_ref.shape[1:], NEG_INF, F32)
        l_ref[j] = jnp.zeros(l_ref.shape[1:], F32)
        acc_ref[j] = jnp.zeros(acc_ref.shape[1:], F32)

        def step(kv, masked, j=j, qj=qj):
            off = pl.multiple_of(kv * tq, tq)
            k = k_ref[0, pl.ds(off, tq), :]
            v = v_ref[0, pl.ds(off, tq), :]
            s = _dot_nt(qj, k) - cum_ref[0, 0, pl.ds(j, 1), pl.ds(off, tq)]
            if masked:
                s = jnp.where(causal, s, NEG_INF)
            _flash_update(s, v, m_ref.at[j], l_ref.at[j], acc_ref.at[j])

        def body(kv, carry, step=step):
            step(kv, False)
            return carry

        lax.fori_loop(0, qi, body, 0)
        step(qi, True)
    o = jnp.where(left, acc_ref[0] / l_ref[0], acc_ref[1] / l_ref[1])
    o_ref[0] = o.astype(o_ref.dtype)


def _fox_attention(q, k, v, cum):
    b, s, da = q.shape
    hp = da // LANES
    tq = min(TQ_FOX, s)
    cum4 = cum.reshape(b, hp, 2, s)
    return pl.pallas_call(
        functools.partial(_fox_kernel, tq=tq),
        grid=(b, hp, s // tq),
        in_specs=[pl.BlockSpec((1, tq, LANES), lambda bi, h, qi: (bi, qi, h)),
                  pl.BlockSpec((1, s, LANES), lambda bi, h, qi: (bi, 0, h)),
                  pl.BlockSpec((1, s, LANES), lambda bi, h, qi: (bi, 0, h)),
                  pl.BlockSpec((1, 1, 2, s), lambda bi, h, qi: (bi, h, 0, 0))],
        out_specs=pl.BlockSpec((1, tq, LANES), lambda bi, h, qi: (bi, qi, h)),
        out_shape=jax.ShapeDtypeStruct((b, s, da), BF16),
        scratch_shapes=[pltpu.VMEM((2, tq, LANES), F32)] * 3,
        compiler_params=_cparams(("parallel", "parallel", "arbitrary")),
        name="fox_attention",
    )(q, k, v, cum4)


def _moba_kernel(q_ref, k_ref, v_ref, bias_ref, o_ref, km_ref, m_ref, l_ref, acc_ref, *, n_blk):
    blk = MOBA_BLOCK
    qi = pl.program_id(2)

    @pl.when(qi == 0)
    def _():
        km_ref[...] = jnp.zeros_like(km_ref)
        for n in range(n_blk):
            kb = k_ref[0, n * blk:(n + 1) * blk, :].astype(F32)
            km_ref[n:n + 1, :] = jnp.sum(kb, axis=0, keepdims=True) * (1.0 / blk)

    q = q_ref[0]
    left = _head_select((1, LANES))
    km_hi, km_lo = _split_bf16(km_ref[...])
    colf = lax.broadcasted_iota(I32, (blk, LANES), 1).astype(F32)
    qif = qi.astype(F32)
    own = pl.multiple_of(qi * blk, blk)
    k_own = k_ref[0, pl.ds(own, blk), :]
    v_own = v_ref[0, pl.ds(own, blk), :]
    blk_row = lax.broadcasted_iota(I32, (LANES, blk), 0)
    for j in range(2):
        qj = jnp.where(left if j == 0 else jnp.logical_not(left), q, jnp.zeros_like(q))
        gate = _dot_nt(qj, km_hi) + _dot_nt(qj, km_lo)
        gate = jnp.where(colf < qif, gate, NEG_INF)
        sel = jnp.zeros((blk, LANES), F32)
        for _ in range(MOBA_TOPK):
            mx = jnp.max(gate, axis=1, keepdims=True)
            cand = jnp.where(gate == mx, jnp.where(mx > NEG_INF, colf, float(LANES)), float(LANES))
            idx = jnp.min(cand, axis=1, keepdims=True)
            pick = colf == idx
            sel = jnp.where(pick, 1.0, sel)
            gate = jnp.where(pick, NEG_INF, gate)
        sel_bf = sel.astype(BF16)

        s = _dot_nt(qj, k_own) + bias_ref[j, 0]
        m = jnp.max(s, axis=1, keepdims=True)
        p = jnp.exp(s - m)
        m_ref[j] = jnp.broadcast_to(m, (blk, LANES))
        l_ref[j] = jnp.broadcast_to(jnp.sum(p, axis=1, keepdims=True), (blk, LANES))
        acc_ref[j] = _dot(p.astype(BF16), v_own)

        def body(n, carry, j=j, qj=qj, sel_bf=sel_bf):
            off = pl.multiple_of(n * blk, blk)
            k = k_ref[0, pl.ds(off, blk), :]
            v = v_ref[0, pl.ds(off, blk), :]
            s = _dot_nt(qj, k) + bias_ref[j, qi - n]
            e_n = jnp.where(blk_row == n, 1.0, 0.0).astype(BF16)
            s = jnp.where(_dot(sel_bf, e_n) > 0.5, s, NEG_INF)
            _flash_update(s, v, m_ref.at[j], l_ref.at[j], acc_ref.at[j])
            return carry

        lax.fori_loop(0, qi, body, 0)
    o = jnp.where(left, acc_ref[0] / l_ref[0], acc_ref[1] / l_ref[1])
    o_ref[0] = o.astype(o_ref.dtype)


def _moba_attention(q, k, v, bias_tiles):
    b, s, db = q.shape
    hp = db // LANES
    blk = MOBA_BLOCK
    n_blk = s // blk
    return pl.pallas_call(
        functools.partial(_moba_kernel, n_blk=n_blk),
        grid=(hp, b, n_blk),
        in_specs=[pl.BlockSpec((1, blk, LANES), lambda h, bi, qi: (bi, qi, h)),
                  pl.BlockSpec((1, s, LANES), lambda h, bi, qi: (bi, 0, h)),
                  pl.BlockSpec((1, s, LANES), lambda h, bi, qi: (bi, 0, h)),
                  pl.BlockSpec((2, n_blk, blk, blk), lambda h, bi, qi: (h, 0, 0, 0))],
        out_specs=pl.BlockSpec((1, blk, LANES), lambda h, bi, qi: (bi, qi, h)),
        out_shape=jax.ShapeDtypeStruct((b, s, db), BF16),
        scratch_shapes=[pltpu.VMEM((LANES, LANES), F32)] + [pltpu.VMEM((2, blk, LANES), F32)] * 3,
        compiler_params=_cparams(("parallel", "parallel", "arbitrary")),
        name="moba_attention",
    )(q, k, v, bias_tiles)


def _dilated_kernel(q_ref, k_ref, v_ref, bias_ref, o_ref, m_ref, l_ref, acc_ref, *, s_len):
    left = _head_select((1, LANES))
    for g, (window, dil) in enumerate(DIL_PATTERNS):
        span = window // dil
        unit = span * dil
        nc = s_len // unit

        def rows(ref, start, dil=dil, span=span):
            if dil == 1:
                return ref[0, pl.ds(start, span), :]
            return ref[0, pl.ds(start, span, stride=dil), :]

        def put(ref, start, val, dil=dil, span=span):
            if dil == 1:
                ref[pl.ds(start, span), :] = val
            else:
                ref[pl.ds(start, span, stride=dil), :] = val

        def body(idx, carry, g=g, nc=nc, unit=unit, rows=rows, put=put, dil=dil, span=span):
            r = idx // nc
            c = idx - r * nc
            start = r + c * unit
            is_first = c == 0
            prev = start - jnp.where(is_first, 0, unit)
            var = 2 * g + jnp.where(is_first, 1, 0)
            q = rows(q_ref, start)
            kb = jnp.concatenate([rows(k_ref, prev), rows(k_ref, start)], axis=0).astype(BF16)
            vb = jnp.concatenate([rows(v_ref, prev), rows(v_ref, start)], axis=0).astype(BF16)
            s0 = _dot_nt(jnp.where(left, q, 0.0).astype(BF16), kb) + bias_ref[0, var]
            s1 = _dot_nt(jnp.where(left, 0.0, q).astype(BF16), kb) + bias_ref[1, var]
            m0 = jnp.max(s0, axis=1, keepdims=True)
            m1 = jnp.max(s1, axis=1, keepdims=True)
            if g > 0:
                if dil == 1:
                    m_prev = m_ref[pl.ds(start, span), :]
                else:
                    m_prev = m_ref[pl.ds(start, span, stride=dil), :]
                m0 = jnp.maximum(m0, m_prev[:, 0:1])
                m1 = jnp.maximum(m1, m_prev[:, HEAD_DIM:HEAD_DIM + 1])
            p0 = jnp.exp(s0 - m0)
            p1 = jnp.exp(s1 - m1)
            m_new = jnp.where(left, m0, m1)
            l_new = jnp.where(left, jnp.sum(p0, axis=1, keepdims=True), jnp.sum(p1, axis=1, keepdims=True))
            acc_new = jnp.where(left, _dot(p0.astype(BF16), vb), _dot(p1.astype(BF16), vb))
            if g > 0:
                alpha = jnp.exp(m_prev - m_new)
                if dil == 1:
                    l_prev = l_ref[pl.ds(start, span), :]
                    acc_prev = acc_ref[pl.ds(start, span), :]
                else:
                    l_prev = l_ref[pl.ds(start, span, stride=dil), :]
                    acc_prev = acc_ref[pl.ds(start, span, stride=dil), :]
                l_new = alpha * l_prev + l_new
                acc_new = alpha * acc_prev + acc_new
            put(m_ref, start, m_new)
            put(l_ref, start, l_new)
            put(acc_ref, start, acc_new)
            return carry

        lax.fori_loop(0, s_len // span, body, 0)
    o_ref[0] = (acc_ref[...] / l_ref[...]).astype(o_ref.dtype)


def _dilated_attention(q, k, v, bias_tiles):
    b, s, dq = q.shape
    hp = dq // LANES
    for window, dil in DIL_PATTERNS:
        assert s % window == 0, "sequence must be a whole number of dilated units"
    qkv = pl.BlockSpec((1, s, LANES), lambda h, bi: (bi, 0, h))
    n_var, span, band = bias_tiles.shape[1:]
    return pl.pallas_call(
        functools.partial(_dilated_kernel, s_len=s),
        grid=(hp, b),
        in_specs=[qkv, qkv, qkv,
                  pl.BlockSpec((2, n_var, span, band), lambda h, bi: (h, 0, 0, 0))],
        out_specs=pl.BlockSpec((1, s, LANES), lambda h, bi: (bi, 0, h)),
        out_shape=jax.ShapeDtypeStruct((b, s, dq), BF16),
        scratch_shapes=[pltpu.VMEM((s, LANES), F32)] * 3,
        compiler_params=_cparams(("parallel", "parallel")),
        name="dilated_attention",
    )(q, k, v, bias_tiles)


def _outproj_kernel(*refs, n_parts):
    o_parts = refs[:n_parts]
    w_parts = refs[n_parts:2 * n_parts]
    h_ref, gain_ref, gate_ref, out_ref = refs[2 * n_parts:]
    y = _dot(o_parts[0][0], w_parts[0][...])
    for o_ref, w_ref in zip(o_parts[1:], w_parts[1:]):
        y = y + _dot(o_ref[0], w_ref[...])
    out_ref[0] = h_ref[0] + gate_ref[0] * (_rms(y) * gain_ref[...])


def _outproj(o_parts, w_out, h, gain, gate):
    b, s, d = h.shape
    tm = TM_PROJ
    n = len(o_parts)
    w_out = w_out.astype(BF16)
    cuts = np.cumsum([p.shape[-1] for p in o_parts])[:-1].tolist()
    w_parts = jnp.split(w_out, cuts, axis=0) if cuts else [w_out]
    in_specs = [pl.BlockSpec((1, tm, p.shape[-1]), lambda bi, si: (bi, si, 0)) for p in o_parts]
    in_specs += [pl.BlockSpec(w.shape, lambda bi, si: (0, 0)) for w in w_parts]
    in_specs += [pl.BlockSpec((1, tm, d), lambda bi, si: (bi, si, 0)),
                 pl.BlockSpec((1, d), lambda bi, si: (0, 0)),
                 pl.BlockSpec((1, 1, d), lambda bi, si: (bi, 0, 0))]
    return pl.pallas_call(
        functools.partial(_outproj_kernel, n_parts=n),
        grid=(b, s // tm),
        in_specs=in_specs,
        out_specs=pl.BlockSpec((1, tm, d), lambda bi, si: (bi, si, 0)),
        out_shape=jax.ShapeDtypeStruct((b, s, d), F32),
        compiler_params=_cparams(("parallel", "parallel")),
        name="outproj",
    )(*o_parts, *w_parts, h, gain.reshape(1, d), gate.reshape(b, 1, d))


def _ffn_kernel(h_ref, g_ref, sc_ref, sh_ref, wg_ref, wu_ref, wd_ref, gain_ref, gate_ref, o_ref):
    h = h_ref[0]
    u = _modulate(h, g_ref[...], sc_ref[0], sh_ref[0]).astype(BF16)
    hid = (_silu(_dot(u, wg_ref[...])) * _dot(u, wu_ref[...])).astype(BF16)
    y = _dot(hid, wd_ref[...])
    o_ref[0] = h + gate_ref[0] * (_rms(y) * gain_ref[...])


def _ffn(h, gain_in, scale, shift, w_gate, w_up, w_down, gain_out, gate):
    b, s, d = h.shape
    tm = TM_FFN
    ff = w_gate.shape[1]
    resident = functools.partial(pl.BlockSpec, pipeline_mode=pl.Buffered(1))
    vec = pl.BlockSpec((1, 1, d), lambda bi, si: (bi, 0, 0))
    row = pl.BlockSpec((1, d), lambda bi, si: (0, 0))
    return pl.pallas_call(
        _ffn_kernel,
        grid=(b, s // tm),
        in_specs=[pl.BlockSpec((1, tm, d), lambda bi, si: (bi, si, 0)),
                  row, vec, vec,
                  resident((d, ff), lambda bi, si: (0, 0)),
                  resident((d, ff), lambda bi, si: (0, 0)),
                  resident((ff, d), lambda bi, si: (0, 0)),
                  row, vec],
        out_specs=pl.BlockSpec((1, tm, d), lambda bi, si: (bi, si, 0)),
        out_shape=jax.ShapeDtypeStruct((b, s, d), F32),
        compiler_params=_cparams(("parallel", "parallel")),
        name="dense_swiglu",
    )(h, gain_in.reshape(1, d), scale.reshape(b, 1, d), shift.reshape(b, 1, d),
      w_gate.astype(BF16), w_up.astype(BF16), w_down.astype(BF16),
      gain_out.reshape(1, d), gate.reshape(b, 1, d))


def _router_kernel(h_ref, g_ref, sc_ref, sh_ref, rw_ref, mi_ref, mf_ref, cnt_ref, carry_ref, *, tm):
    @pl.when((pl.program_id(0) == 0) & (pl.program_id(1) == 0))
    def _():
        carry_ref[...] = jnp.zeros_like(carry_ref)

    u = _modulate(h_ref[0], g_ref[...], sc_ref[0], sh_ref[0])
    logits = _dot_split(u, rw_ref[...])
    lanef = lax.broadcasted_iota(I32, (tm, LANES), 1).astype(F32)
    lg = jnp.where(lanef < N_EXPERTS, logits, NEG_INF)
    v1 = jnp.max(lg, axis=1, keepdims=True)
    i1 = jnp.min(jnp.where(lg == v1, lanef, float(LANES)), axis=1, keepdims=True)
    lg2 = jnp.where(lanef == i1, NEG_INF, lg)
    v2 = jnp.max(lg2, axis=1, keepdims=True)
    i2 = jnp.min(jnp.where(lg2 == v2, lanef, float(LANES)), axis=1, keepdims=True)
    e2 = jnp.exp(v2 - v1)
    p1 = 1.0 / (1.0 + e2)
    p2 = e2 / (1.0 + e2)
    oh1 = jnp.where(lanef == i1, 1.0, 0.0)
    oh2 = jnp.where(lanef == i2, 1.0, 0.0)
    oh = oh1 + oh2
    r = lax.broadcasted_iota(I32, (tm, tm), 0)
    c = lax.broadcasted_iota(I32, (tm, tm), 1)
    before = jnp.where(c < r, 1.0, 0.0).astype(BF16)
    tot = _dot(before, oh.astype(BF16)) + carry_ref[0:1, :]
    rank1 = jnp.sum(oh1 * tot, axis=1, keepdims=True)
    rank2 = jnp.sum(oh2 * tot, axis=1, keepdims=True)
    carry_ref[...] = carry_ref[...] + jnp.sum(oh, axis=0, keepdims=True)
    mi = jnp.where(lanef == 0.0, i1, jnp.where(lanef == 1.0, i2,
         jnp.where(lanef == 2.0, rank1, jnp.where(lanef == 3.0, rank2, 0.0))))
    mi_ref[...] = mi.astype(I32)
    mf_ref[...] = jnp.where(lanef == 0.0, p1, jnp.where(lanef == 1.0, p2, 0.0))
    cnt_ref[...] = carry_ref[...]


def _router(h, gain, scale, shift, router_w):
    b, s, d = h.shape
    tm = TM_ROUTE
    n = b * s
    ns = s // tm
    rw = jnp.pad(router_w.astype(F32), ((0, 0), (0, LANES - router_w.shape[1])))
    vec = pl.BlockSpec((1, 1, d), lambda bi, si: (bi, 0, 0))
    meta = pl.BlockSpec((tm, LANES), lambda bi, si: (bi * ns + si, 0))
    return pl.pallas_call(
        functools.partial(_router_kernel, tm=tm),
        grid=(b, ns),
        in_specs=[pl.BlockSpec((1, tm, d), lambda bi, si: (bi, si, 0)),
                  pl.BlockSpec((1, d), lambda bi, si: (0, 0)),
                  vec, vec,
                  pl.BlockSpec(rw.shape, lambda bi, si: (0, 0))],
        out_specs=[meta, meta, pl.BlockSpec((8, LANES), lambda bi, si: (0, 0))],
        out_shape=[jax.ShapeDtypeStruct((n, LANES), I32),
                   jax.ShapeDtypeStruct((n, LANES), F32),
                   jax.ShapeDtypeStruct((8, LANES), F32)],
        scratch_shapes=[pltpu.VMEM((8, LANES), F32)],
        compiler_params=_cparams(("arbitrary", "arbitrary")),
        name="moe_router",
    )(h, gain.reshape(1, d), scale.reshape(b, 1, d), shift.reshape(b, 1, d), rw)


def _scatter_kernel(dest_ref, h_ref, g_ref, sc_ref, sh_ref, xs_in_ref, xs_ref, ubuf, sems, *, tm, ns):
    del xs_in_ref
    step = pl.program_id(0) * ns + pl.program_id(1)
    nsteps = pl.num_programs(0) * ns
    slot = lax.rem(step, 2)

    def wait_slot(sl):
        for _ in range(2):
            pltpu.make_async_copy(ubuf.at[sl], ubuf.at[sl], sems.at[sl]).wait()

    @pl.when(step >= 2)
    def _():
        wait_slot(slot)

    ubuf[slot] = _modulate(h_ref[0], g_ref[...], sc_ref[0], sh_ref[0])
    base = step * tm

    def issue(i, carry):
        t = 2 * (base + i)
        src = ubuf.at[slot, pl.ds(i, 1), :]
        pltpu.make_async_copy(src, xs_ref.at[pl.ds(dest_ref[t], 1), :], sems.at[slot]).start()
        pltpu.make_async_copy(src, xs_ref.at[pl.ds(dest_ref[t + 1], 1), :], sems.at[slot]).start()
        return carry

    lax.fori_loop(0, tm, issue, 0)

    @pl.when(step == nsteps - 1)
    def _():
        wait_slot(slot)

        @pl.when(nsteps >= 2)
        def _():
            wait_slot(1 - slot)


def _scatter(dest, h, gain, scale, shift, m_pad):
    b, s, d = h.shape
    tm = TM_SCATTER
    ns = s // tm
    vec = pl.BlockSpec((1, 1, d), lambda bi, si, dest: (bi, 0, 0))
    grid_spec = pltpu.PrefetchScalarGridSpec(
        num_scalar_prefetch=1,
        grid=(b, ns),
        in_specs=[pl.BlockSpec((1, tm, d), lambda bi, si, dest: (bi, si, 0)),
                  pl.BlockSpec((1, d), lambda bi, si, dest: (0, 0)),
                  vec, vec,
                  pl.BlockSpec(memory_space=pl.ANY)],
        out_specs=pl.BlockSpec(memory_space=pl.ANY),
        scratch_shapes=[pltpu.VMEM((2, tm, d), F32), pltpu.SemaphoreType.DMA((2,))],
    )
    return pl.pallas_call(
        functools.partial(_scatter_kernel, tm=tm, ns=ns),
        grid_spec=grid_spec,
        out_shape=jax.ShapeDtypeStruct((m_pad, d), F32),
        input_output_aliases={5: 0},
        compiler_params=_cparams(("arbitrary", "arbitrary")),
        name="moe_scatter",
    )(dest, h, gain.reshape(1, d), scale.reshape(b, 1, d), shift.reshape(b, 1, d),
      jnp.zeros((m_pad, d), F32))


def _expert_kernel(te_ref, tv_ref, x_ref, wg_ref, wu_ref, wd_ref, o_ref):
    t = pl.program_id(0)
    f = pl.program_id(1)

    @pl.when(tv_ref[t] == 1)
    def _():
        x = x_ref[...].astype(BF16)
        hid = (_silu(_dot(x, wg_ref[0])) * _dot(x, wu_ref[0])).astype(BF16)
        y = _dot(hid, wd_ref[0])

        @pl.when(f == 0)
        def _():
            o_ref[...] = y

        @pl.when(f > 0)
        def _():
            o_ref[...] = o_ref[...] + y

    @pl.when((tv_ref[t] == 0) & (f == 0))
    def _():
        o_ref[...] = jnp.zeros_like(o_ref)


def _experts(tile_expert, tile_valid, xs, w_gate, w_up, w_down):
    m_pad, d = xs.shape
    tm = TM_EXPERT
    n_tiles = tile_expert.shape[0]
    ff = w_gate.shape[2]
    fs = FF_STEPS_EXPERT
    tf = ff // fs

    def ff_idx(f, tv, t):
        return f * tv[t] + (fs - 1) * (1 - tv[t])

    grid_spec = pltpu.PrefetchScalarGridSpec(
        num_scalar_prefetch=2,
        grid=(n_tiles, fs),
        in_specs=[pl.BlockSpec((tm, d), lambda t, f, te, tv: (t, 0)),
                  pl.BlockSpec((1, d, tf), lambda t, f, te, tv: (te[t], 0, ff_idx(f, tv, t))),
                  pl.BlockSpec((1, d, tf), lambda t, f, te, tv: (te[t], 0, ff_idx(f, tv, t))),
                  pl.BlockSpec((1, tf, d), lambda t, f, te, tv: (te[t], ff_idx(f, tv, t), 0))],
        out_specs=pl.BlockSpec((tm, d), lambda t, f, te, tv: (t, 0)),
    )
    return pl.pallas_call(
        _expert_kernel,
        grid_spec=grid_spec,
        out_shape=jax.ShapeDtypeStruct((m_pad, d), F32),
        compiler_params=_cparams(("arbitrary", "arbitrary")),
        name="moe_experts",
    )(tile_expert, tile_valid, xs,
      w_gate.astype(BF16), w_up.astype(BF16), w_down.astype(BF16))


def _combine_kernel(dest_ref, y_ref, mf_ref, h_ref, gain_ref, gate_ref, o_ref, ybuf, sems, *, tm, ns):
    step = pl.program_id(0) * ns + pl.program_id(1)
    nsteps = pl.num_programs(0) * ns
    slot = lax.rem(step, 2)

    def issue(st, sl):
        base = st * tm

        def body(i, carry):
            t = 2 * (base + i)
            pltpu.make_async_copy(y_ref.at[pl.ds(dest_ref[t], 1), :],
                                  ybuf.at[sl, 0, pl.ds(i, 1), :], sems.at[sl]).start()
            pltpu.make_async_copy(y_ref.at[pl.ds(dest_ref[t + 1], 1), :],
                                  ybuf.at[sl, 1, pl.ds(i, 1), :], sems.at[sl]).start()
            return carry

        lax.fori_loop(0, tm, body, 0)

    @pl.when(step == 0)
    def _():
        issue(0, 0)

    @pl.when(step + 1 < nsteps)
    def _():
        issue(step + 1, 1 - slot)

    for k in range(2):
        pltpu.make_async_copy(ybuf.at[slot, k], ybuf.at[slot, k], sems.at[slot]).wait()
    mf = mf_ref[...]
    y = mf[:, 0:1] * ybuf[slot, 0] + mf[:, 1:2] * ybuf[slot, 1]
    o_ref[0] = h_ref[0] + gate_ref[0] * (_rms(y) * gain_ref[...])


def _combine(dest, ys, mf, h, gain, gate):
    b, s, d = h.shape
    tm = TM_COMBINE
    ns = s // tm
    grid_spec = pltpu.PrefetchScalarGridSpec(
        num_scalar_prefetch=1,
        grid=(b, ns),
        in_specs=[pl.BlockSpec(memory_space=pl.ANY),
                  pl.BlockSpec((tm, LANES), lambda bi, si, dest: (bi * ns + si, 0)),
                  pl.BlockSpec((1, tm, d), lambda bi, si, dest: (bi, si, 0)),
                  pl.BlockSpec((1, d), lambda bi, si, dest: (0, 0)),
                  pl.BlockSpec((1, 1, d), lambda bi, si, dest: (bi, 0, 0))],
        out_specs=pl.BlockSpec((1, tm, d), lambda bi, si, dest: (bi, si, 0)),
        scratch_shapes=[pltpu.VMEM((2, 2, tm, d), F32), pltpu.SemaphoreType.DMA((2,))],
    )
    return pl.pallas_call(
        functools.partial(_combine_kernel, tm=tm, ns=ns),
        grid_spec=grid_spec,
        out_shape=jax.ShapeDtypeStruct((b, s, d), F32),
        compiler_params=_cparams(("arbitrary", "arbitrary")),
        name="moe_combine",
    )(dest, ys, mf, h, gain.reshape(1, d), gate.reshape(b, 1, d))


def _moe(h, gain_in, scale, shift, router_w, w_gate, w_up, w_down, gain_out, gate):
    b, s, d = h.shape
    n = b * s
    tm = TM_EXPERT
    mi, mf, cnt = _router(h, gain_in, scale, shift, router_w)
    counts = cnt[0, :N_EXPERTS].astype(I32)
    tiles_per = (counts + tm - 1) // tm
    seg_start = (jnp.cumsum(tiles_per) - tiles_per) * tm
    dest = (seg_start[mi[:, 0:2]] + mi[:, 2:4]).reshape(2 * n)
    n_tiles = (2 * n) // tm + N_EXPERTS
    m_pad = n_tiles * tm
    tile_end = jnp.cumsum(tiles_per)
    tidx = jnp.arange(n_tiles, dtype=I32)
    tile_valid = (tidx < tile_end[-1]).astype(I32)
    tile_expert = jnp.minimum(jnp.searchsorted(tile_end, tidx, side="right"), N_EXPERTS - 1).astype(I32)
    xs = _scatter(dest, h, gain_in, scale, shift, m_pad)
    ys = _experts(tile_expert, tile_valid, xs, w_gate, w_up, w_down)
    return _combine(dest, ys, mf, h, gain_out, gate)


def kernel(x, c, mod_w, mod_b, norm_g, attn_in_w_even, fox_gate_bias, attn_out_w_even,
           attn_in_w_odd, attn_out_w_odd, rel_bias_table, ffn_w_gate, ffn_w_up, ffn_w_down,
           router_w, exp_w_gate, exp_w_up, exp_w_down):
    depth = mod_w.shape[0]
    s_len = x.shape[1]
    mods = _mods(c, mod_w, mod_b)
    dil_bias, moba_bias = _bias_tiles(rel_bias_table, s_len)
    h = x
    for layer in range(depth):
        j = layer // 2
        sh1, sc1, g1, sh2, sc2, g2 = jnp.split(mods[layer], 6, axis=-1)
        gains = norm_g[layer]
        if layer % 2 == 0:
            q_a, k_a, v_a, q_b, k_b, v_b, cum = _inproj_even(
                h, gains[0], sc1, sh1, attn_in_w_even[j], fox_gate_bias[j])
            o_parts = [_fox_attention(q_a, k_a, v_a, cum), _moba_attention(q_b, k_b, v_b, moba_bias)]
            h = _outproj(o_parts, attn_out_w_even[j], h, gains[1], g1)
            h = _ffn(h, gains[2], sc2, sh2, ffn_w_gate[j], ffn_w_up[j], ffn_w_down[j], gains[3], g2)
        else:
            q, k, v = _inproj_odd(h, gains[0], sc1, sh1, attn_in_w_odd[j])
            o = _dilated_attention(q, k, v, dil_bias)
            h = _outproj([o], attn_out_w_odd[j], h, gains[1], g1)
            h = _moe(h, gains[2], sc2, sh2, router_w[j], exp_w_gate[j], exp_w_up[j], exp_w_down[j],
                     gains[3], g2)
    return h
```

```python
import functools
import math

import numpy as np
import jax
import jax.numpy as jnp
from jax import lax
from jax.experimental import pallas as pl
from jax.experimental.pallas import tpu as pltpu

F32 = jnp.float32
BF16 = jnp.bfloat16
I32 = jnp.int32

HEAD_DIM = 64
LANES = 128
N_HEADS = 16
N_HEADS_FOX = 8
ATTN_SCALE = HEAD_DIM ** -0.5
MOBA_BLOCK = 256
MOBA_TOPK = 3
DIL_PATTERNS = ((128, 1), (512, 4), (2048, 16))
NUM_BUCKETS = 32
MAX_DISTANCE = 2048
N_EXPERTS = 8
NORM_EPS = 1e-6
NEG_INF = float("-inf")
MASK_BIG = 1e30

VMEM_LIMIT = 56 * 1024 * 1024

TM_PROJ = 512
TM_FFN = 512
TQ_FOX = 512
TM_ROUTE = 512
TM_SCATTER = 256
TM_EXPERT = 512
TM_COMBINE = 256
FF_STEPS_EXPERT = 2
DIL_CHUNKS_PER_STEP = 4


def _cparams(sem):
    return pltpu.CompilerParams(dimension_semantics=sem, vmem_limit_bytes=VMEM_LIMIT)


def _t5_bucket_np(n):
    n = np.maximum(n, 0)
    max_exact = NUM_BUCKETS // 2
    nf = np.maximum(n, 1).astype(np.float64)
    large = max_exact + (np.log(nf / max_exact) / math.log(MAX_DISTANCE / max_exact)
                         * (NUM_BUCKETS - max_exact)).astype(np.int64)
    large = np.minimum(large, NUM_BUCKETS - 1)
    return np.where(n < max_exact, n, large)


_MAX_DIST = 1 << 16
_BUCKET_OF = _t5_bucket_np(np.arange(_MAX_DIST))
_BUCKET_THR = [int(np.searchsorted(_BUCKET_OF, k, side="left")) for k in range(NUM_BUCKETS)]


def _bias_from_dist(tab_ref, h, dist, dlo, dhi):
    lo_b = int(_BUCKET_OF[max(dlo, 0)])
    hi_b = int(_BUCKET_OF[dhi])
    val = jnp.zeros(dist.shape, F32) + tab_ref[lo_b, h]
    for k in range(lo_b + 1, hi_b + 1):
        val = jnp.where(dist >= _BUCKET_THR[k], tab_ref[k, h], val)
    return val


def _dil_bias_kernel(tab_ref, o_ref):
    h = pl.program_id(0)
    for g, (window, dil) in enumerate(DIL_PATTERNS):
        span = window // dil
        i = lax.broadcasted_iota(I32, (span, 2 * span), 0)
        j = lax.broadcasted_iota(I32, (span, 2 * span), 1)
        rel = i + span - j
        val = _bias_from_dist(tab_ref, h, rel * dil, 0, span * dil)
        band = jnp.where(rel >= 0, jnp.where(rel <= span, val, NEG_INF), NEG_INF)
        o_ref[0, 2 * g] = band
        o_ref[0, 2 * g + 1] = jnp.where(j >= span, band, NEG_INF)


def _moba_bias_kernel(tab_ref, o_ref, *, n_blk, head0):
    h = pl.program_id(0) + head0
    i = lax.broadcasted_iota(I32, (MOBA_BLOCK, MOBA_BLOCK), 0)
    j = lax.broadcasted_iota(I32, (MOBA_BLOCK, MOBA_BLOCK), 1)
    for d in range(n_blk):
        dist = d * MOBA_BLOCK + i - j
        val = _bias_from_dist(tab_ref, h, dist, d * MOBA_BLOCK - (MOBA_BLOCK - 1),
                              d * MOBA_BLOCK + (MOBA_BLOCK - 1))
        if d == 0:
            val = jnp.where(dist >= 0, val, NEG_INF)
        o_ref[0, d] = val


def _bias_tiles(rel_bias_table, s_len):
    n_blk = s_len // MOBA_BLOCK
    span = DIL_PATTERNS[0][0]
    n_var = 2 * len(DIL_PATTERNS)
    smem = pl.BlockSpec(memory_space=pltpu.SMEM)
    dil = pl.pallas_call(
        _dil_bias_kernel,
        grid=(N_HEADS,),
        in_specs=[smem],
        out_specs=pl.BlockSpec((1, n_var, span, 2 * span), lambda h: (h, 0, 0, 0)),
        out_shape=jax.ShapeDtypeStruct((N_HEADS, n_var, span, 2 * span), F32),
        compiler_params=_cparams(("parallel",)),
        name="dil_bias",
    )(rel_bias_table)
    n_moba = N_HEADS - N_HEADS_FOX
    moba = pl.pallas_call(
        functools.partial(_moba_bias_kernel, n_blk=n_blk, head0=N_HEADS_FOX),
        grid=(n_moba,),
        in_specs=[smem],
        out_specs=pl.BlockSpec((1, n_blk, MOBA_BLOCK, MOBA_BLOCK), lambda h: (h, 0, 0, 0)),
        out_shape=jax.ShapeDtypeStruct((n_moba, n_blk, MOBA_BLOCK, MOBA_BLOCK), F32),
        compiler_params=_cparams(("parallel",)),
        name="moba_bias",
    )(rel_bias_table)
    return dil, moba


def _split_bf16(a):
    hi = a.astype(BF16)
    lo = (a - hi.astype(F32)).astype(BF16)
    return hi, lo


def _dot(a, b):
    return jnp.dot(a, b, preferred_element_type=F32)


def _dot_nt(a, b):
    return lax.dot_general(a, b, (((1,), (1,)), ((), ())), preferred_element_type=F32)


def _dot_split(a, b):
    a_hi, a_lo = _split_bf16(a)
    b_hi, b_lo = _split_bf16(b)
    return _dot(a_hi, b_hi) + (_dot(a_hi, b_lo) + _dot(a_lo, b_hi))


def _rms(x):
    return x * lax.rsqrt(jnp.mean(x * x, axis=-1, keepdims=True) + NORM_EPS)


def _modulate(x, gain, scale, shift):
    return (_rms(x) * gain) * (1.0 + scale) + shift


def _silu(x):
    return x * jax.nn.sigmoid(x)


def _head_select(lane_shape):
    lane = lax.broadcasted_iota(I32, lane_shape, len(lane_shape) - 1)
    return lane < HEAD_DIM


def _mods_kernel(c_ref, w_ref, b_ref, o_ref):
    o_ref[0] = _dot_split(_silu(c_ref[...]), w_ref[0]) + b_ref[0]


def _mods(c, mod_w, mod_b):
    depth, d, e = mod_w.shape
    b = c.shape[0]
    tn = 1536
    return pl.pallas_call(
        _mods_kernel,
        grid=(depth, e // tn),
        in_specs=[pl.BlockSpec((b, d), lambda l, j: (0, 0)),
                  pl.BlockSpec((1, d, tn), lambda l, j: (l, 0, j)),
                  pl.BlockSpec((1, 1, tn), lambda l, j: (l, 0, j))],
        out_specs=pl.BlockSpec((1, b, tn), lambda l, j: (l, 0, j)),
        out_shape=jax.ShapeDtypeStruct((depth, b, e), F32),
        compiler_params=_cparams(("parallel", "parallel")),
        name="adaln_mods",
    )(c, mod_w, mod_b.reshape(depth, 1, e))


def _inproj_even_kernel(h_ref, g_ref, sc_ref, sh_ref, w_ref, wft_ref, gb_ref,
                        qa_ref, ka_ref, va_ref, qb_ref, kb_ref, vb_ref, cum_ref, carry_ref, *, tm):
    u = _modulate(h_ref[0], g_ref[...], sc_ref[0], sh_ref[0]).astype(BF16)
    outs = (qa_ref, ka_ref, va_ref, qb_ref, kb_ref, vb_ref)
    width = qa_ref.shape[-1]
    for i, o_ref in enumerate(outs):
        r = _dot(u, w_ref[:, i * width:(i + 1) * width])
        if i % 3 == 0:
            r = r * ATTN_SCALE
        o_ref[0] = r.astype(BF16)
    x = _dot_nt(wft_ref[...], u) + gb_ref[...]
    lf = jnp.minimum(x, 0.0) - jnp.log1p(jnp.exp(-jnp.abs(x)))
    col = lax.broadcasted_iota(I32, lf.shape, 1)
    k = 1
    while k < tm:
        lf = lf + jnp.where(col >= k, pltpu.roll(lf, k, axis=1), 0.0)
        k *= 2

    @pl.when(pl.program_id(1) == 0)
    def _():
        carry_ref[...] = jnp.zeros_like(carry_ref)

    cum = lf + carry_ref[:, 0:1]
    cum_ref[0] = cum
    carry_ref[...] = jnp.broadcast_to(cum[:, tm - 1:tm], carry_ref.shape)


def _inproj_even(h, gain, scale, shift, w_in, gate_bias):
    b, s, d = h.shape
    tm = TM_PROJ
    da = N_HEADS_FOX * HEAD_DIM
    cuts = np.cumsum([da, da, da, N_HEADS_FOX, da, da]).tolist()
    q_a, k_a, v_a, f_a, q_b, k_b, v_b = jnp.split(w_in, cuts, axis=1)
    w = jnp.concatenate([q_a, k_a, v_a, q_b, k_b, v_b], axis=1).astype(BF16)
    wft = f_a.T.astype(BF16)
    gb = gate_bias.astype(F32).reshape(N_HEADS_FOX, 1)
    act = jax.ShapeDtypeStruct((b, s, da), BF16)
    act_spec = pl.BlockSpec((1, tm, da), lambda bi, si: (bi, si, 0))
    vec = pl.BlockSpec((1, 1, d), lambda bi, si: (bi, 0, 0))
    outs = pl.pallas_call(
        functools.partial(_inproj_even_kernel, tm=tm),
        grid=(b, s // tm),
        in_specs=[pl.BlockSpec((1, tm, d), lambda bi, si: (bi, si, 0)),
                  pl.BlockSpec((1, d), lambda bi, si: (0, 0)),
                  vec, vec,
                  pl.BlockSpec(w.shape, lambda bi, si: (0, 0)),
                  pl.BlockSpec(wft.shape, lambda bi, si: (0, 0)),
                  pl.BlockSpec(gb.shape, lambda bi, si: (0, 0))],
        out_specs=[act_spec] * 6 + [pl.BlockSpec((1, N_HEADS_FOX, tm), lambda bi, si: (bi, 0, si))],
        out_shape=[act] * 6 + [jax.ShapeDtypeStruct((b, N_HEADS_FOX, s), F32)],
        scratch_shapes=[pltpu.VMEM((N_HEADS_FOX, LANES), F32)],
        compiler_params=_cparams(("parallel", "arbitrary")),
        name="inproj_even",
    )(h, gain.reshape(1, d), scale.reshape(b, 1, d), shift.reshape(b, 1, d), w, wft, gb)
    return outs


def _inproj_odd_kernel(h_ref, g_ref, sc_ref, sh_ref, w_ref, q_ref, k_ref, v_ref):
    u = _modulate(h_ref[0], g_ref[...], sc_ref[0], sh_ref[0]).astype(BF16)
    width = q_ref.shape[-1]
    q_ref[0] = _dot(u, w_ref[:, 0:width]) * ATTN_SCALE
    k_ref[0] = _dot(u, w_ref[:, width:2 * width])
    v_ref[0] = _dot(u, w_ref[:, 2 * width:3 * width])


def _inproj_odd(h, gain, scale, shift, w_in):
    b, s, d = h.shape
    tm = TM_PROJ
    dq = w_in.shape[1] // 3
    act = jax.ShapeDtypeStruct((b, s, dq), F32)
    act_spec = pl.BlockSpec((1, tm, dq), lambda bi, si: (bi, si, 0))
    vec = pl.BlockSpec((1, 1, d), lambda bi, si: (bi, 0, 0))
    return pl.pallas_call(
        _inproj_odd_kernel,
        grid=(b, s // tm),
        in_specs=[pl.BlockSpec((1, tm, d), lambda bi, si: (bi, si, 0)),
                  pl.BlockSpec((1, d), lambda bi, si: (0, 0)),
                  vec, vec,
                  pl.BlockSpec(w_in.shape, lambda bi, si: (0, 0))],
        out_specs=[act_spec] * 3,
        out_shape=[act] * 3,
        compiler_params=_cparams(("parallel", "parallel")),
        name="inproj_odd",
    )(h, gain.reshape(1, d), scale.reshape(b, 1, d), shift.reshape(b, 1, d), w_in.astype(BF16))


def _flash_update(s, v, m_ref, l_ref, acc_ref):
    m_prev = m_ref[...]
    m_new = jnp.maximum(m_prev, jnp.max(s, axis=1, keepdims=True))
    alpha = jnp.exp(m_prev - m_new)
    p = jnp.exp(s - m_new[:, 0:1])
    l_ref[...] = alpha * l_ref[...] + jnp.sum(p, axis=1, keepdims=True)
    acc_ref[...] = alpha * acc_ref[...] + _dot(p.astype(BF16), v)
    m_ref[...] = m_new


def _fox_kernel(q_ref, k_ref, v_ref, cum_ref, o_ref, m_ref, l_ref, acc_ref, *, tq):
    qi = pl.program_id(2)
    q = q_ref[0]
    left = _head_select((1, LANES))
    row = lax.broadcasted_iota(I32, (tq, tq), 0)
    col = lax.broadcasted_iota(I32, (tq, tq), 1)
    causal = col <= row
    qh = (jnp.where(left, q, jnp.zeros_like(q)), jnp.where(left, jnp.zeros_like(q), q))
    m_ref[...] = jnp.full(m_ref.shape, NEG_INF, F32)
    l_ref[...] = jnp.zeros(l_ref.shape, F32)
    acc_ref[...] = jnp.zeros(acc_ref.shape, F32)

    def step(kv, masked):
        off = pl.multiple_of(kv * tq, tq)
        k = k_ref[0, pl.ds(off, tq), :]
        v = v_ref[0, pl.ds(off, tq), :]
        for j in range(2):
            s = _dot_nt(qh[j], k) - cum_ref[0, 0, pl.ds(j, 1), pl.ds(off, tq)]
            if masked:
                s = jnp.where(causal, s, NEG_INF)
            _flash_update(s, v, m_ref.at[j], l_ref.at[j], acc_ref.at[j])

    def body(kv, carry):
        step(kv, False)
        return carry

    lax.fori_loop(0, qi, body, 0)
    step(qi, True)
    o = jnp.where(left, acc_ref[0] / l_ref[0], acc_ref[1] / l_ref[1])
    o_ref[0] = o.astype(o_ref.dtype)


def _fox_attention(q, k, v, cum):
    b, s, da = q.shape
    hp = da // LANES
    tq = min(TQ_FOX, s)
    cum4 = cum.reshape(b, hp, 2, s)
    return pl.pallas_call(
        functools.partial(_fox_kernel, tq=tq),
        grid=(b, hp, s // tq),
        in_specs=[pl.BlockSpec((1, tq, LANES), lambda bi, h, qi: (bi, qi, h)),
                  pl.BlockSpec((1, s, LANES), lambda bi, h, qi: (bi, 0, h)),
                  pl.BlockSpec((1, s, LANES), lambda bi, h, qi: (bi, 0, h)),
                  pl.BlockSpec((1, 1, 2, s), lambda bi, h, qi: (bi, h, 0, 0))],
        out_specs=pl.BlockSpec((1, tq, LANES), lambda bi, h, qi: (bi, qi, h)),
        out_shape=jax.ShapeDtypeStruct((b, s, da), BF16),
        scratch_shapes=[pltpu.VMEM((2, tq, LANES), F32)] * 3,
        compiler_params=_cparams(("parallel", "parallel", "arbitrary")),
        name="fox_attention",
    )(q, k, v, cum4)


def _moba_kernel(q_ref, k_ref, v_ref, bias_ref, o_ref, km_ref, m_ref, l_ref, acc_ref, *, n_blk):
    blk = MOBA_BLOCK
    qi = pl.program_id(2)

    gate_lane0 = (HEAD_DIM, 0)

    @pl.when(qi == 0)
    def _():
        km_ref[...] = jnp.zeros_like(km_ref)
        for n in range(n_blk):
            kb = k_ref[0, n * blk:(n + 1) * blk, :].astype(F32)
            mean = jnp.sum(kb, axis=0, keepdims=True) * (1.0 / blk)
            for lane0 in gate_lane0:
                km_ref[lane0 + n:lane0 + n + 1, :] = mean

    q = q_ref[0]
    left = _head_select((1, LANES))
    mine = (left, jnp.logical_not(left))
    km_hi, km_lo = _split_bf16(km_ref[...])
    colf = lax.broadcasted_iota(I32, (blk, LANES), 1).astype(F32)
    qif = qi.astype(F32)
    own = pl.multiple_of(qi * blk, blk)
    k_own = k_ref[0, pl.ds(own, blk), :]
    v_own = v_ref[0, pl.ds(own, blk), :]
    qp = []
    for j in range(2):
        qj = jnp.where(mine[j], q, jnp.zeros_like(q))
        gate = _dot_nt(qj, km_hi) + _dot_nt(qj, km_lo)
        nf = colf - float(gate_lane0[j])
        gate = jnp.where(nf >= 0.0, jnp.where(nf < qif, gate, NEG_INF), NEG_INF)
        sel = jnp.zeros((blk, LANES), F32)
        for _ in range(MOBA_TOPK):
            mx = jnp.max(gate, axis=1, keepdims=True)
            cand = jnp.where(gate == mx, jnp.where(mx > NEG_INF, colf, float(LANES)), float(LANES))
            idx = jnp.min(cand, axis=1, keepdims=True)
            pick = colf == idx
            sel = jnp.where(pick, 1.0, sel)
            gate = jnp.where(pick, NEG_INF, gate)
        pen = jnp.where(sel > 0.5, 0.0, -MASK_BIG).astype(BF16)
        qp.append(jnp.where(mine[j], q, pen))

        s = _dot_nt(qj, k_own) + bias_ref[j, 0]
        m = jnp.max(s, axis=1, keepdims=True)
        p = jnp.exp(s - m)
        m_ref[j] = jnp.broadcast_to(m, (blk, LANES))
        l_ref[j] = jnp.broadcast_to(jnp.sum(p, axis=1, keepdims=True), (blk, LANES))
        acc_ref[j] = _dot(p.astype(BF16), v_own)

    lane_i = lax.broadcasted_iota(I32, (1, LANES), 1)

    def body(i, carry):
        n0 = 2 * i
        off = pl.multiple_of(n0 * blk, 2 * blk)
        k2 = k_ref[0, pl.ds(off, 2 * blk), :]
        v2 = v_ref[0, pl.ds(off, 2 * blk), :]
        d0 = qi - n0
        for j in range(2):
            kp = jnp.concatenate(
                [jnp.where(mine[j], k2[u * blk:(u + 1) * blk],
                           jnp.where(lane_i == n0 + (u + gate_lane0[j]), 1.0, 0.0).astype(BF16))
                 for u in range(2)], axis=0)
            bias2 = jnp.concatenate([bias_ref[j, d0], bias_ref[j, d0 - 1]], axis=1)
            s = _dot_nt(qp[j], kp) + bias2
            _flash_update(s, v2, m_ref.at[j], l_ref.at[j], acc_ref.at[j])
        return carry

    lax.fori_loop(0, (qi + 1) // 2, body, 0)
    o = jnp.where(left, acc_ref[0] / l_ref[0], acc_ref[1] / l_ref[1])
    o_ref[0] = o.astype(o_ref.dtype)


def _moba_attention(q, k, v, bias_tiles):
    b, s, db = q.shape
    hp = db // LANES
    blk = MOBA_BLOCK
    n_blk = s // blk
    assert n_blk <= HEAD_DIM, "block gates of one head must fit in the other head's lanes"
    return pl.pallas_call(
        functools.partial(_moba_kernel, n_blk=n_blk),
        grid=(hp, b, n_blk),
        in_specs=[pl.BlockSpec((1, blk, LANES), lambda h, bi, qi: (bi, qi, h)),
                  pl.BlockSpec((1, s, LANES), lambda h, bi, qi: (bi, 0, h)),
                  pl.BlockSpec((1, s, LANES), lambda h, bi, qi: (bi, 0, h)),
                  pl.BlockSpec((2, n_blk, blk, blk), lambda h, bi, qi: (h, 0, 0, 0))],
        out_specs=pl.BlockSpec((1, blk, LANES), lambda h, bi, qi: (bi, qi, h)),
        out_shape=jax.ShapeDtypeStruct((b, s, db), BF16),
        scratch_shapes=[pltpu.VMEM((LANES, LANES), F32)] + [pltpu.VMEM((2, blk, LANES), F32)] * 3,
        compiler_params=_cparams(("parallel", "parallel", "arbitrary")),
        name="moba_attention",
    )(q, k, v, bias_tiles)


def _dilated_kernel(q_ref, k_ref, v_ref, bias_ref, o_ref, m_ref, l_ref, acc_ref, *, s_len):
    left = _head_select((1, LANES))
    for g, (window, dil) in enumerate(DIL_PATTERNS):
        span = window // dil
        unit = span * dil
        nc = s_len // unit

        def rows(ref, start, dil=dil, span=span):
            if dil == 1:
                return ref[0, pl.ds(start, span), :]
            return ref[0, pl.ds(start, span, stride=dil), :]

        def put(ref, start, val, dil=dil, span=span):
            if dil == 1:
                ref[pl.ds(start, span), :] = val
            else:
                ref[pl.ds(start, span, stride=dil), :] = val

        n_u = min(DIL_CHUNKS_PER_STEP, nc)
        groups = nc // n_u

        def get(ref, start, dil=dil, span=span):
            if dil == 1:
                return ref[pl.ds(start, span), :]
            return ref[pl.ds(start, span, stride=dil), :]

        def body(it, carry, g=g, n_u=n_u, groups=groups, unit=unit, rows=rows, put=put, get=get):
            r = it // groups
            start0 = r + (it - r * groups) * (n_u * unit)
            is_first = it - r * groups == 0
            prev0 = start0 - jnp.where(is_first, 0, unit)
            starts = [start0 + u * unit for u in range(n_u)]
            kc = [rows(k_ref, st).astype(BF16) for st in [prev0] + starts]
            vc = [rows(v_ref, st).astype(BF16) for st in [prev0] + starts]
            results = []
            for u, start in enumerate(starts):
                var = 2 * g + jnp.where(is_first, 1, 0) if u == 0 else 2 * g
                q = rows(q_ref, start)
                kb = jnp.concatenate([kc[u], kc[u + 1]], axis=0)
                vb = jnp.concatenate([vc[u], vc[u + 1]], axis=0)
                s0 = _dot_nt(jnp.where(left, q, 0.0).astype(BF16), kb) + bias_ref[0, var]
                s1 = _dot_nt(jnp.where(left, 0.0, q).astype(BF16), kb) + bias_ref[1, var]
                m0 = jnp.max(s0, axis=1, keepdims=True)
                m1 = jnp.max(s1, axis=1, keepdims=True)
                if g > 0:
                    m_prev = get(m_ref, start)
                    m0 = jnp.maximum(m0, m_prev[:, 0:1])
                    m1 = jnp.maximum(m1, m_prev[:, HEAD_DIM:HEAD_DIM + 1])
                p0 = jnp.exp(s0 - m0)
                p1 = jnp.exp(s1 - m1)
                m_new = jnp.where(left, m0, m1)
                l_new = jnp.where(left, jnp.sum(p0, axis=1, keepdims=True),
                                  jnp.sum(p1, axis=1, keepdims=True))
                acc_new = jnp.where(left, _dot(p0.astype(BF16), vb), _dot(p1.astype(BF16), vb))
                if g > 0:
                    alpha = jnp.exp(m_prev - m_new)
                    l_new = alpha * get(l_ref, start) + l_new
                    acc_new = alpha * get(acc_ref, start) + acc_new
                results.append((start, m_new, l_new, acc_new))
            for start, m_new, l_new, acc_new in results:
                put(m_ref, start, m_new)
                put(l_ref, start, l_new)
                put(acc_ref, start, acc_new)
            return carry

        lax.fori_loop(0, dil * groups, body, 0)
    o_ref[0] = (acc_ref[...] / l_ref[...]).astype(o_ref.dtype)


def _dilated_attention(q, k, v, bias_tiles):
    b, s, dq = q.shape
    hp = dq // LANES
    for window, dil in DIL_PATTERNS:
        assert s % window == 0, "sequence must be a whole number of dilated units"
    qkv = pl.BlockSpec((1, s, LANES), lambda h, bi: (bi, 0, h))
    n_var, span, band = bias_tiles.shape[1:]
    return pl.pallas_call(
        functools.partial(_dilated_kernel, s_len=s),
        grid=(hp, b),
        in_specs=[qkv, qkv, qkv,
                  pl.BlockSpec((2, n_var, span, band), lambda h, bi: (h, 0, 0, 0))],
        out_specs=pl.BlockSpec((1, s, LANES), lambda h, bi: (bi, 0, h)),
        out_shape=jax.ShapeDtypeStruct((b, s, dq), BF16),
        scratch_shapes=[pltpu.VMEM((s, LANES), F32)] * 3,
        compiler_params=_cparams(("parallel", "parallel")),
        name="dilated_attention",
    )(q, k, v, bias_tiles)


def _outproj_kernel(*refs, n_parts):
    o_parts = refs[:n_parts]
    w_parts = refs[n_parts:2 * n_parts]
    h_ref, gain_ref, gate_ref, out_ref = refs[2 * n_parts:]
    y = _dot(o_parts[0][0], w_parts[0][...])
    for o_ref, w_ref in zip(o_parts[1:], w_parts[1:]):
        y = y + _dot(o_ref[0], w_ref[...])
    out_ref[0] = h_ref[0] + gate_ref[0] * (_rms(y) * gain_ref[...])


def _outproj(o_parts, w_out, h, gain, gate):
    b, s, d = h.shape
    tm = TM_PROJ
    n = len(o_parts)
    w_out = w_out.astype(BF16)
    cuts = np.cumsum([p.shape[-1] for p in o_parts])[:-1].tolist()
    w_parts = jnp.split(w_out, cuts, axis=0) if cuts else [w_out]
    in_specs = [pl.BlockSpec((1, tm, p.shape[-1]), lambda bi, si: (bi, si, 0)) for p in o_parts]
    in_specs += [pl.BlockSpec(w.shape, lambda bi, si: (0, 0)) for w in w_parts]
    in_specs += [pl.BlockSpec((1, tm, d), lambda bi, si: (bi, si, 0)),
                 pl.BlockSpec((1, d), lambda bi, si: (0, 0)),
                 pl.BlockSpec((1, 1, d), lambda bi, si: (bi, 0, 0))]
    return pl.pallas_call(
        functools.partial(_outproj_kernel, n_parts=n),
        grid=(b, s // tm),
        in_specs=in_specs,
        out_specs=pl.BlockSpec((1, tm, d), lambda bi, si: (bi, si, 0)),
        out_shape=jax.ShapeDtypeStruct((b, s, d), F32),
        compiler_params=_cparams(("parallel", "parallel")),
        name="outproj",
    )(*o_parts, *w_parts, h, gain.reshape(1, d), gate.reshape(b, 1, d))


def _ffn_kernel(h_ref, g_ref, sc_ref, sh_ref, wg_ref, wu_ref, wd_ref, gain_ref, gate_ref, o_ref):
    h = h_ref[0]
    u = _modulate(h, g_ref[...], sc_ref[0], sh_ref[0]).astype(BF16)
    hid = (_silu(_dot(u, wg_ref[...])) * _dot(u, wu_ref[...])).astype(BF16)
    y = _dot(hid, wd_ref[...])
    o_ref[0] = h + gate_ref[0] * (_rms(y) * gain_ref[...])


def _ffn(h, gain_in, scale, shift, w_gate, w_up, w_down, gain_out, gate):
    b, s, d = h.shape
    tm = TM_FFN
    ff = w_gate.shape[1]
    resident = functools.partial(pl.BlockSpec, pipeline_mode=pl.Buffered(1))
    vec = pl.BlockSpec((1, 1, d), lambda bi, si: (bi, 0, 0))
    row = pl.BlockSpec((1, d), lambda bi, si: (0, 0))
    return pl.pallas_call(
        _ffn_kernel,
        grid=(b, s // tm),
        in_specs=[pl.BlockSpec((1, tm, d), lambda bi, si: (bi, si, 0)),
                  row, vec, vec,
                  resident((d, ff), lambda bi, si: (0, 0)),
                  resident((d, ff), lambda bi, si: (0, 0)),
                  resident((ff, d), lambda bi, si: (0, 0)),
                  row, vec],
        out_specs=pl.BlockSpec((1, tm, d), lambda bi, si: (bi, si, 0)),
        out_shape=jax.ShapeDtypeStruct((b, s, d), F32),
        compiler_params=_cparams(("parallel", "parallel")),
        name="dense_swiglu",
    )(h, gain_in.reshape(1, d), scale.reshape(b, 1, d), shift.reshape(b, 1, d),
      w_gate.astype(BF16), w_up.astype(BF16), w_down.astype(BF16),
      gain_out.reshape(1, d), gate.reshape(b, 1, d))


def _router_kernel(h_ref, g_ref, sc_ref, sh_ref, rw_ref, mi_ref, mf_ref, cnt_ref, carry_ref, *, tm):
    @pl.when((pl.program_id(0) == 0) & (pl.program_id(1) == 0))
    def _():
        carry_ref[...] = jnp.zeros_like(carry_ref)

    u = _modulate(h_ref[0], g_ref[...], sc_ref[0], sh_ref[0])
    logits = _dot_split(u, rw_ref[...])
    lanef = lax.broadcasted_iota(I32, (tm, LANES), 1).astype(F32)
    lg = jnp.where(lanef < N_EXPERTS, logits, NEG_INF)
    v1 = jnp.max(lg, axis=1, keepdims=True)
    i1 = jnp.min(jnp.where(lg == v1, lanef, float(LANES)), axis=1, keepdims=True)
    lg2 = jnp.where(lanef == i1, NEG_INF, lg)
    v2 = jnp.max(lg2, axis=1, keepdims=True)
    i2 = jnp.min(jnp.where(lg2 == v2, lanef, float(LANES)), axis=1, keepdims=True)
    e2 = jnp.exp(v2 - v1)
    p1 = 1.0 / (1.0 + e2)
    p2 = e2 / (1.0 + e2)
    oh1 = jnp.where(lanef == i1, 1.0, 0.0)
    oh2 = jnp.where(lanef == i2, 1.0, 0.0)
    oh = oh1 + oh2
    r = lax.broadcasted_iota(I32, (tm, tm), 0)
    c = lax.broadcasted_iota(I32, (tm, tm), 1)
    before = jnp.where(c < r, 1.0, 0.0).astype(BF16)
    tot = _dot(before, oh.astype(BF16)) + carry_ref[0:1, :]
    rank1 = jnp.sum(oh1 * tot, axis=1, keepdims=True)
    rank2 = jnp.sum(oh2 * tot, axis=1, keepdims=True)
    carry_ref[...] = carry_ref[...] + jnp.sum(oh, axis=0, keepdims=True)
    mi = jnp.where(lanef == 0.0, i1, jnp.where(lanef == 1.0, i2,
         jnp.where(lanef == 2.0, rank1, jnp.where(lanef == 3.0, rank2, 0.0))))
    mi_ref[...] = mi.astype(I32)
    mf_ref[...] = jnp.where(lanef == 0.0, p1, jnp.where(lanef == 1.0, p2, 0.0))
    cnt_ref[...] = carry_ref[...]


def _router(h, gain, scale, shift, router_w):
    b, s, d = h.shape
    tm = TM_ROUTE
    n = b * s
    ns = s // tm
    rw = jnp.pad(router_w.astype(F32), ((0, 0), (0, LANES - router_w.shape[1])))
    vec = pl.BlockSpec((1, 1, d), lambda bi, si: (bi, 0, 0))
    meta = pl.BlockSpec((tm, LANES), lambda bi, si: (bi * ns + si, 0))
    return pl.pallas_call(
        functools.partial(_router_kernel, tm=tm),
        grid=(b, ns),
        in_specs=[pl.BlockSpec((1, tm, d), lambda bi, si: (bi, si, 0)),
                  pl.BlockSpec((1, d), lambda bi, si: (0, 0)),
                  vec, vec,
                  pl.BlockSpec(rw.shape, lambda bi, si: (0, 0))],
        out_specs=[meta, meta, pl.BlockSpec((8, LANES), lambda bi, si: (0, 0))],
        out_shape=[jax.ShapeDtypeStruct((n, LANES), I32),
                   jax.ShapeDtypeStruct((n, LANES), F32),
                   jax.ShapeDtypeStruct((8, LANES), F32)],
        scratch_shapes=[pltpu.VMEM((8, LANES), F32)],
        compiler_params=_cparams(("arbitrary", "arbitrary")),
        name="moe_router",
    )(h, gain.reshape(1, d), scale.reshape(b, 1, d), shift.reshape(b, 1, d), rw)


def _scatter_kernel(dest_ref, h_ref, g_ref, sc_ref, sh_ref, xs_in_ref, xs_ref, ubuf, sems, *, tm, ns):
    del xs_in_ref
    step = pl.program_id(0) * ns + pl.program_id(1)
    nsteps = pl.num_programs(0) * ns
    slot = lax.rem(step, 2)

    def wait_slot(sl):
        for _ in range(2):
            pltpu.make_async_copy(ubuf.at[sl], ubuf.at[sl], sems.at[sl]).wait()

    @pl.when(step >= 2)
    def _():
        wait_slot(slot)

    ubuf[slot] = _modulate(h_ref[0], g_ref[...], sc_ref[0], sh_ref[0])
    base = step * tm

    def issue(i, carry):
        t = 2 * (base + i)
        src = ubuf.at[slot, pl.ds(i, 1), :]
        pltpu.make_async_copy(src, xs_ref.at[pl.ds(dest_ref[t], 1), :], sems.at[slot]).start()
        pltpu.make_async_copy(src, xs_ref.at[pl.ds(dest_ref[t + 1], 1), :], sems.at[slot]).start()
        return carry

    lax.fori_loop(0, tm, issue, 0)

    @pl.when(step == nsteps - 1)
    def _():
        wait_slot(slot)

        @pl.when(nsteps >= 2)
        def _():
            wait_slot(1 - slot)


def _scatter(dest, h, gain, scale, shift, m_pad):
    b, s, d = h.shape
    tm = TM_SCATTER
    ns = s // tm
    vec = pl.BlockSpec((1, 1, d), lambda bi, si, dest: (bi, 0, 0))
    grid_spec = pltpu.PrefetchScalarGridSpec(
        num_scalar_prefetch=1,
        grid=(b, ns),
        in_specs=[pl.BlockSpec((1, tm, d), lambda bi, si, dest: (bi, si, 0)),
                  pl.BlockSpec((1, d), lambda bi, si, dest: (0, 0)),
                  vec, vec,
                  pl.BlockSpec(memory_space=pl.ANY)],
        out_specs=pl.BlockSpec(memory_space=pl.ANY),
        scratch_shapes=[pltpu.VMEM((2, tm, d), F32), pltpu.SemaphoreType.DMA((2,))],
    )
    return pl.pallas_call(
        functools.partial(_scatter_kernel, tm=tm, ns=ns),
        grid_spec=grid_spec,
        out_shape=jax.ShapeDtypeStruct((m_pad, d), F32),
        input_output_aliases={5: 0},
        compiler_params=_cparams(("arbitrary", "arbitrary")),
        name="moe_scatter",
    )(dest, h, gain.reshape(1, d), scale.reshape(b, 1, d), shift.reshape(b, 1, d),
      jnp.zeros((m_pad, d), F32))


def _expert_kernel(te_ref, tv_ref, x_ref, wg_ref, wu_ref, wd_ref, o_ref):
    t = pl.program_id(0)
    f = pl.program_id(1)

    @pl.when(tv_ref[t] == 1)
    def _():
        x = x_ref[...].astype(BF16)
        hid = (_silu(_dot(x, wg_ref[0])) * _dot(x, wu_ref[0])).astype(BF16)
        y = _dot(hid, wd_ref[0])

        @pl.when(f == 0)
        def _():
            o_ref[...] = y

        @pl.when(f > 0)
        def _():
            o_ref[...] = o_ref[...] + y

    @pl.when((tv_ref[t] == 0) & (f == 0))
    def _():
        o_ref[...] = jnp.zeros_like(o_ref)


def _experts(tile_expert, tile_valid, xs, w_gate, w_up, w_down):
    m_pad, d = xs.shape
    tm = TM_EXPERT
    n_tiles = tile_expert.shape[0]
    ff = w_gate.shape[2]
    fs = FF_STEPS_EXPERT
    tf = ff // fs

    def ff_idx(f, tv, t):
        return f * tv[t] + (fs - 1) * (1 - tv[t])

    grid_spec = pltpu.PrefetchScalarGridSpec(
        num_scalar_prefetch=2,
        grid=(n_tiles, fs),
        in_specs=[pl.BlockSpec((tm, d), lambda t, f, te, tv: (t, 0)),
                  pl.BlockSpec((1, d, tf), lambda t, f, te, tv: (te[t], 0, ff_idx(f, tv, t))),
                  pl.BlockSpec((1, d, tf), lambda t, f, te, tv: (te[t], 0, ff_idx(f, tv, t))),
                  pl.BlockSpec((1, tf, d), lambda t, f, te, tv: (te[t], ff_idx(f, tv, t), 0))],
        out_specs=pl.BlockSpec((tm, d), lambda t, f, te, tv: (t, 0)),
    )
    return pl.pallas_call(
        _expert_kernel,
        grid_spec=grid_spec,
        out_shape=jax.ShapeDtypeStruct((m_pad, d), F32),
        compiler_params=_cparams(("arbitrary", "arbitrary")),
        name="moe_experts",
    )(tile_expert, tile_valid, xs,
      w_gate.astype(BF16), w_up.astype(BF16), w_down.astype(BF16))


def _combine_kernel(dest_ref, y_ref, mf_ref, h_ref, gain_ref, gate_ref, o_ref, ybuf, sems, *, tm, ns):
    step = pl.program_id(0) * ns + pl.program_id(1)
    nsteps = pl.num_programs(0) * ns
    slot = lax.rem(step, 2)

    def issue(st, sl):
        base = st * tm

        def body(i, carry):
            t = 2 * (base + i)
            pltpu.make_async_copy(y_ref.at[pl.ds(dest_ref[t], 1), :],
                                  ybuf.at[sl, 0, pl.ds(i, 1), :], sems.at[sl]).start()
            pltpu.make_async_copy(y_ref.at[pl.ds(dest_ref[t + 1], 1), :],
                                  ybuf.at[sl, 1, pl.ds(i, 1), :], sems.at[sl]).start()
            return carry

        lax.fori_loop(0, tm, body, 0)

    @pl.when(step == 0)
    def _():
        issue(0, 0)

    @pl.when(step + 1 < nsteps)
    def _():
        issue(step + 1, 1 - slot)

    for k in range(2):
        pltpu.make_async_copy(ybuf.at[slot, k], ybuf.at[slot, k], sems.at[slot]).wait()
    mf = mf_ref[...]
    y = mf[:, 0:1] * ybuf[slot, 0] + mf[:, 1:2] * ybuf[slot, 1]
    o_ref[0] = h_ref[0] + gate_ref[0] * (_rms(y) * gain_ref[...])


def _combine(dest, ys, mf, h, gain, gate):
    b, s, d = h.shape
    tm = TM_COMBINE
    ns = s // tm
    grid_spec = pltpu.PrefetchScalarGridSpec(
        num_scalar_prefetch=1,
        grid=(b, ns),
        in_specs=[pl.BlockSpec(memory_space=pl.ANY),
                  pl.BlockSpec((tm, LANES), lambda bi, si, dest: (bi * ns + si, 0)),
                  pl.BlockSpec((1, tm, d), lambda bi, si, dest: (bi, si, 0)),
                  pl.BlockSpec((1, d), lambda bi, si, dest: (0, 0)),
                  pl.BlockSpec((1, 1, d), lambda bi, si, dest: (bi, 0, 0))],
        out_specs=pl.BlockSpec((1, tm, d), lambda bi, si, dest: (bi, si, 0)),
        scratch_shapes=[pltpu.VMEM((2, 2, tm, d), F32), pltpu.SemaphoreType.DMA((2,))],
    )
    return pl.pallas_call(
        functools.partial(_combine_kernel, tm=tm, ns=ns),
        grid_spec=grid_spec,
        out_shape=jax.ShapeDtypeStruct((b, s, d), F32),
        compiler_params=_cparams(("arbitrary", "arbitrary")),
        name="moe_combine",
    )(dest, ys, mf, h, gain.reshape(1, d), gate.reshape(b, 1, d))


def _moe(h, gain_in, scale, shift, router_w, w_gate, w_up, w_down, gain_out, gate):
    b, s, d = h.shape
    n = b * s
    tm = TM_EXPERT
    mi, mf, cnt = _router(h, gain_in, scale, shift, router_w)
    counts = cnt[0, :N_EXPERTS].astype(I32)
    tiles_per = (counts + tm - 1) // tm
    seg_start = (jnp.cumsum(tiles_per) - tiles_per) * tm
    dest = (seg_start[mi[:, 0:2]] + mi[:, 2:4]).reshape(2 * n)
    n_tiles = (2 * n) // tm + N_EXPERTS
    m_pad = n_tiles * tm
    tile_end = jnp.cumsum(tiles_per)
    tidx = jnp.arange(n_tiles, dtype=I32)
    tile_valid = (tidx < tile_end[-1]).astype(I32)
    tile_expert = jnp.minimum(jnp.searchsorted(tile_end, tidx, side="right"), N_EXPERTS - 1).astype(I32)
    xs = _scatter(dest, h, gain_in, scale, shift, m_pad)
    ys = _experts(tile_expert, tile_valid, xs, w_gate, w_up, w_down)
    return _combine(dest, ys, mf, h, gain_out, gate)


def kernel(x, c, mod_w, mod_b, norm_g, attn_in_w_even, fox_gate_bias, attn_out_w_even,
           attn_in_w_odd, attn_out_w_odd, rel_bias_table, ffn_w_gate, ffn_w_up, ffn_w_down,
           router_w, exp_w_gate, exp_w_up, exp_w_down):
    depth = mod_w.shape[0]
    s_len = x.shape[1]
    mods = _mods(c, mod_w, mod_b)
    dil_bias, moba_bias = _bias_tiles(rel_bias_table, s_len)
    h = x
    for layer in range(depth):
        j = layer // 2
        sh1, sc1, g1, sh2, sc2, g2 = jnp.split(mods[layer], 6, axis=-1)
        gains = norm_g[layer]
        if layer % 2 == 0:
            q_a, k_a, v_a, q_b, k_b, v_b, cum = _inproj_even(
                h, gains[0], sc1, sh1, attn_in_w_even[j], fox_gate_bias[j])
            o_parts = [_fox_attention(q_a, k_a, v_a, cum), _moba_attention(q_b, k_b, v_b, moba_bias)]
            h = _outproj(o_parts, attn_out_w_even[j], h, gains[1], g1)
            h = _ffn(h, gains[2], sc2, sh2, ffn_w_gate[j], ffn_w_up[j], ffn_w_down[j], gains[3], g2)
        else:
            q, k, v = _inproj_odd(h, gains[0], sc1, sh1, attn_in_w_odd[j])
            o = _dilated_attention(q, k, v, dil_bias)
            h = _outproj([o], attn_out_w_odd[j], h, gains[1], g1)
            h = _moe(h, gains[2], sc2, sh2, router_w[j], exp_w_gate[j], exp_w_up[j], exp_w_down[j],
                     gains[3], g2)
    return h
```

```python
import functools
import math

import numpy as np
import jax
import jax.numpy as jnp
from jax import lax
from jax.experimental import pallas as pl
from jax.experimental.pallas import tpu as pltpu

F32 = jnp.float32
BF16 = jnp.bfloat16
I32 = jnp.int32

HEAD_DIM = 64
LANES = 128
N_HEADS = 16
N_HEADS_FOX = 8
ATTN_SCALE = HEAD_DIM ** -0.5
LOG2E = math.log2(math.e)
Q_SCALE = ATTN_SCALE * LOG2E
N_DECAY_PIECES = 3
MOBA_BLOCK = 256
MOBA_TOPK = 3
DIL_PATTERNS = ((128, 1), (512, 4), (2048, 16))
NUM_BUCKETS = 32
MAX_DISTANCE = 2048
N_EXPERTS = 8
NORM_EPS = 1e-6
NEG_INF = float("-inf")
MASK_BIG = 1e30

VMEM_LIMIT = 56 * 1024 * 1024

TM_PROJ = 512
TM_FFN = 512
TQ_FOX = 512
TM_ROUTE = 512
TM_SCATTER = 256
TM_EXPERT = 512
TM_COMBINE = 256
FF_STEPS_EXPERT = 2
DIL_CHUNKS_PER_STEP = 4


def _cparams(sem):
    return pltpu.CompilerParams(dimension_semantics=sem, vmem_limit_bytes=VMEM_LIMIT)


def _t5_bucket_np(n):
    n = np.maximum(n, 0)
    max_exact = NUM_BUCKETS // 2
    nf = np.maximum(n, 1).astype(np.float64)
    large = max_exact + (np.log(nf / max_exact) / math.log(MAX_DISTANCE / max_exact)
                         * (NUM_BUCKETS - max_exact)).astype(np.int64)
    large = np.minimum(large, NUM_BUCKETS - 1)
    return np.where(n < max_exact, n, large)


_MAX_DIST = 1 << 16
_BUCKET_OF = _t5_bucket_np(np.arange(_MAX_DIST))
_BUCKET_THR = [int(np.searchsorted(_BUCKET_OF, k, side="left")) for k in range(NUM_BUCKETS)]


def _bias_from_dist(tab_ref, h, dist, dlo, dhi):
    lo_b = int(_BUCKET_OF[max(dlo, 0)])
    hi_b = int(_BUCKET_OF[dhi])
    val = jnp.zeros(dist.shape, F32) + tab_ref[lo_b, h]
    for k in range(lo_b + 1, hi_b + 1):
        val = jnp.where(dist >= _BUCKET_THR[k], tab_ref[k, h], val)
    return val


def _dil_bias_kernel(tab_ref, o_ref):
    h = pl.program_id(0)
    for g, (window, dil) in enumerate(DIL_PATTERNS):
        span = window // dil
        i = lax.broadcasted_iota(I32, (span, 2 * span), 0)
        j = lax.broadcasted_iota(I32, (span, 2 * span), 1)
        rel = i + span - j
        val = _bias_from_dist(tab_ref, h, rel * dil, 0, span * dil) * LOG2E
        band = jnp.where(rel >= 0, jnp.where(rel <= span, val, NEG_INF), NEG_INF)
        o_ref[0, 2 * g] = band
        o_ref[0, 2 * g + 1] = jnp.where(j >= span, band, NEG_INF)


def _moba_bias_kernel(tab_ref, o_ref, *, n_blk, head0):
    h = pl.program_id(0) + head0
    i = lax.broadcasted_iota(I32, (MOBA_BLOCK, MOBA_BLOCK), 0)
    j = lax.broadcasted_iota(I32, (MOBA_BLOCK, MOBA_BLOCK), 1)
    for d in range(n_blk):
        dist = d * MOBA_BLOCK + i - j
        val = _bias_from_dist(tab_ref, h, dist, d * MOBA_BLOCK - (MOBA_BLOCK - 1),
                              d * MOBA_BLOCK + (MOBA_BLOCK - 1)) * LOG2E
        if d == 0:
            val = jnp.where(dist >= 0, val, NEG_INF)
        o_ref[0, d] = val


def _bias_tiles(rel_bias_table, s_len):
    n_blk = s_len // MOBA_BLOCK
    span = DIL_PATTERNS[0][0]
    n_var = 2 * len(DIL_PATTERNS)
    smem = pl.BlockSpec(memory_space=pltpu.SMEM)
    dil = pl.pallas_call(
        _dil_bias_kernel,
        grid=(N_HEADS,),
        in_specs=[smem],
        out_specs=pl.BlockSpec((1, n_var, span, 2 * span), lambda h: (h, 0, 0, 0)),
        out_shape=jax.ShapeDtypeStruct((N_HEADS, n_var, span, 2 * span), F32),
        compiler_params=_cparams(("parallel",)),
        name="dil_bias",
    )(rel_bias_table)
    n_moba = N_HEADS - N_HEADS_FOX
    moba = pl.pallas_call(
        functools.partial(_moba_bias_kernel, n_blk=n_blk, head0=N_HEADS_FOX),
        grid=(n_moba,),
        in_specs=[smem],
        out_specs=pl.BlockSpec((1, n_blk, MOBA_BLOCK, MOBA_BLOCK), lambda h: (h, 0, 0, 0)),
        out_shape=jax.ShapeDtypeStruct((n_moba, n_blk, MOBA_BLOCK, MOBA_BLOCK), F32),
        compiler_params=_cparams(("parallel",)),
        name="moba_bias",
    )(rel_bias_table)
    return dil, moba


def _split_bf16(a):
    hi = a.astype(BF16)
    lo = (a - hi.astype(F32)).astype(BF16)
    return hi, lo


def _dot(a, b):
    return jnp.dot(a, b, preferred_element_type=F32)


def _dot_nt(a, b):
    return lax.dot_general(a, b, (((1,), (1,)), ((), ())), preferred_element_type=F32)


def _dot_split(a, b):
    a_hi, a_lo = _split_bf16(a)
    b_hi, b_lo = _split_bf16(b)
    return _dot(a_hi, b_hi) + (_dot(a_hi, b_lo) + _dot(a_lo, b_hi))


def _rms(x):
    return x * lax.rsqrt(jnp.mean(x * x, axis=-1, keepdims=True) + NORM_EPS)


def _modulate(x, gain, scale, shift):
    return (_rms(x) * gain) * (1.0 + scale) + shift


def _silu(x):
    return x * jax.nn.sigmoid(x)


def _mods_kernel(c_ref, w_ref, b_ref, o_ref):
    o_ref[0] = _dot_split(_silu(c_ref[...]), w_ref[0]) + b_ref[0]


def _mods(c, mod_w, mod_b):
    depth, d, e = mod_w.shape
    b = c.shape[0]
    tn = 1536
    return pl.pallas_call(
        _mods_kernel,
        grid=(depth, e // tn),
        in_specs=[pl.BlockSpec((b, d), lambda l, j: (0, 0)),
                  pl.BlockSpec((1, d, tn), lambda l, j: (l, 0, j)),
                  pl.BlockSpec((1, 1, tn), lambda l, j: (l, 0, j))],
        out_specs=pl.BlockSpec((1, b, tn), lambda l, j: (l, 0, j)),
        out_shape=jax.ShapeDtypeStruct((depth, b, e), F32),
        compiler_params=_cparams(("parallel", "parallel")),
        name="adaln_mods",
    )(c, mod_w, mod_b.reshape(depth, 1, e))


def _inproj_even_kernel(h_ref, g_ref, sc_ref, sh_ref, w_ref, wf_ref, gb_ref,
                        qa_ref, k0a_ref, k1a_ref, v0a_ref, v1a_ref,
                        qb_ref, k0b_ref, k1b_ref, v0b_ref, v1b_ref, carry_ref, *, tm):
    si = pl.program_id(1)
    u = _modulate(h_ref[0], g_ref[...], sc_ref[0], sh_ref[0]).astype(BF16)
    width = qa_ref.shape[-1]
    n_pairs = width // LANES

    def proj(i):
        return _dot(u, w_ref[:, i * width:(i + 1) * width])

    lane = lax.broadcasted_iota(I32, (1, LANES), 1)
    left = lane < HEAD_DIM
    row = lax.broadcasted_iota(I32, (tm, LANES), 0)

    x = _dot(u, wf_ref[...]) + gb_ref[...]
    lf = jnp.where(lane < N_HEADS_FOX, jnp.minimum(x, 0.0) - jnp.log1p(jnp.exp(-jnp.abs(x))), 0.0)
    k = 1
    while k < tm:
        lf = lf + jnp.where(row >= k, pltpu.roll(lf, k, axis=0), 0.0)
        k *= 2

    @pl.when(si == 0)
    def _():
        carry_ref[...] = jnp.zeros_like(carry_ref)

    cum = lf + carry_ref[0:1, :]
    carry_ref[...] = jnp.broadcast_to(cum[tm - 1:tm, :], carry_ref.shape)
    rest = cum * (-LOG2E)
    decay = jnp.zeros((tm, LANES), F32)
    for p in range(N_DECAY_PIECES):
        piece = rest.astype(BF16).astype(F32)
        rest = rest - piece
        decay = decay + (pltpu.roll(piece, p * N_HEADS_FOX, axis=1) if p else piece)
    decay_lo = decay.astype(BF16)
    decay_hi = pltpu.roll(decay, HEAD_DIM, axis=1).astype(BF16)

    one_lo = jnp.where(lane == 0, 1.0, 0.0).astype(BF16)
    one_hi = jnp.where(lane == HEAD_DIM, 1.0, 0.0).astype(BF16)
    blk = (si * tm + row) // MOBA_BLOCK
    blk_lo = jnp.where(lane == blk, 1.0, 0.0).astype(BF16)
    blk_hi = jnp.where(lane == blk + HEAD_DIM, 1.0, 0.0).astype(BF16)

    def emit(first, k0_ref, k1_ref, v0_ref, v1_ref, k_lo, k_hi):
        kk = proj(first + 1).astype(BF16)
        vv = proj(first + 2).astype(BF16)
        for hp in range(n_pairs):
            sl = slice(hp * LANES, (hp + 1) * LANES)
            k0_ref[0, :, sl] = jnp.where(left, kk[:, sl], k_hi)
            k1_ref[0, :, sl] = jnp.where(left, k_lo, kk[:, sl])
            v0_ref[0, :, sl] = jnp.where(left, vv[:, sl], one_hi)
            v1_ref[0, :, sl] = jnp.where(left, one_lo, vv[:, sl])

    qa_ref[0] = (proj(0) * Q_SCALE).astype(BF16)
    emit(0, k0a_ref, k1a_ref, v0a_ref, v1a_ref, decay_lo, decay_hi)
    qb_ref[0] = (proj(3) * Q_SCALE).astype(BF16)
    emit(3, k0b_ref, k1b_ref, v0b_ref, v1b_ref, blk_lo, blk_hi)


def _inproj_even(h, gain, scale, shift, w_in, gate_bias):
    b, s, d = h.shape
    tm = TM_PROJ
    da = N_HEADS_FOX * HEAD_DIM
    assert s // MOBA_BLOCK <= HEAD_DIM and N_DECAY_PIECES * N_HEADS_FOX <= HEAD_DIM
    cuts = np.cumsum([da, da, da, N_HEADS_FOX, da, da]).tolist()
    q_a, k_a, v_a, f_a, q_b, k_b, v_b = jnp.split(w_in, cuts, axis=1)
    w = jnp.concatenate([q_a, k_a, v_a, q_b, k_b, v_b], axis=1).astype(BF16)
    wf = jnp.pad(f_a, ((0, 0), (0, LANES - N_HEADS_FOX))).astype(BF16)
    gb = jnp.pad(gate_bias.astype(F32), (0, LANES - N_HEADS_FOX)).reshape(1, LANES)
    act = jax.ShapeDtypeStruct((b, s, da), BF16)
    act_spec = pl.BlockSpec((1, tm, da), lambda bi, si: (bi, si, 0))
    vec = pl.BlockSpec((1, 1, d), lambda bi, si: (bi, 0, 0))
    outs = pl.pallas_call(
        functools.partial(_inproj_even_kernel, tm=tm),
        grid=(b, s // tm),
        in_specs=[pl.BlockSpec((1, tm, d), lambda bi, si: (bi, si, 0)),
                  pl.BlockSpec((1, d), lambda bi, si: (0, 0)),
                  vec, vec,
                  pl.BlockSpec(w.shape, lambda bi, si: (0, 0)),
                  pl.BlockSpec(wf.shape, lambda bi, si: (0, 0)),
                  pl.BlockSpec(gb.shape, lambda bi, si: (0, 0))],
        out_specs=[act_spec] * 10,
        out_shape=[act] * 10,
        scratch_shapes=[pltpu.VMEM((8, LANES), F32)],
        compiler_params=_cparams(("parallel", "arbitrary")),
        name="inproj_even",
    )(h, gain.reshape(1, d), scale.reshape(b, 1, d), shift.reshape(b, 1, d), w, wf, gb)
    return outs[:5], outs[5:]


def _inproj_odd_kernel(h_ref, g_ref, sc_ref, sh_ref, w_ref, q_ref, k_ref, v_ref):
    u = _modulate(h_ref[0], g_ref[...], sc_ref[0], sh_ref[0]).astype(BF16)
    width = q_ref.shape[-1]
    q_ref[0] = _dot(u, w_ref[:, 0:width]) * Q_SCALE
    k_ref[0] = _dot(u, w_ref[:, width:2 * width])
    v_ref[0] = _dot(u, w_ref[:, 2 * width:3 * width])


def _inproj_odd(h, gain, scale, shift, w_in):
    b, s, d = h.shape
    tm = TM_PROJ
    dq = w_in.shape[1] // 3
    act = jax.ShapeDtypeStruct((b, s, dq), F32)
    act_spec = pl.BlockSpec((1, tm, dq), lambda bi, si: (bi, si, 0))
    vec = pl.BlockSpec((1, 1, d), lambda bi, si: (bi, 0, 0))
    return pl.pallas_call(
        _inproj_odd_kernel,
        grid=(b, s // tm),
        in_specs=[pl.BlockSpec((1, tm, d), lambda bi, si: (bi, si, 0)),
                  pl.BlockSpec((1, d), lambda bi, si: (0, 0)),
                  vec, vec,
                  pl.BlockSpec(w_in.shape, lambda bi, si: (0, 0))],
        out_specs=[act_spec] * 3,
        out_shape=[act] * 3,
        compiler_params=_cparams(("parallel", "parallel")),
        name="inproj_odd",
    )(h, gain.reshape(1, d), scale.reshape(b, 1, d), shift.reshape(b, 1, d), w_in.astype(BF16))


def _tile_lanes(x, width):
    return jnp.concatenate([x] * (width // LANES), axis=1)


def _flash_update(s, v, m_ref, acc_ref):
    m_prev = m_ref[...]
    m_new = jnp.maximum(m_prev, jnp.max(s, axis=1, keepdims=True))
    p = jnp.exp2(s - _tile_lanes(m_new, s.shape[1]))
    acc_ref[...] = jnp.exp2(m_prev - m_new) * acc_ref[...] + _dot(p.astype(BF16), v)
    m_ref[...] = m_new


def _finish_pair(acc_ref, left):
    acc0 = acc_ref[0]
    acc1 = acc_ref[1]
    return jnp.where(left, acc0 / acc0[:, HEAD_DIM:HEAD_DIM + 1], acc1 / acc1[:, 0:1])


def _fox_kernel(q_ref, k0_ref, k1_ref, v0_ref, v1_ref, o_ref, m_ref, acc_ref, *, tq):
    hp = pl.program_id(1)
    qi = pl.program_id(2)
    q = q_ref[0]
    lane = lax.broadcasted_iota(I32, (1, LANES), 1)
    left = lane < HEAD_DIM

    def piece_lanes(lane0):
        hit = lane == lane0
        for p in range(1, N_DECAY_PIECES):
            hit = jnp.logical_or(hit, lane == lane0 + p * N_HEADS_FOX)
        return jnp.where(hit, 1.0, 0.0).astype(BF16)

    qp = (jnp.where(left, q, piece_lanes(HEAD_DIM + 2 * hp)), jnp.where(left, piece_lanes(2 * hp + 1), q))
    row = lax.broadcasted_iota(I32, (tq, tq), 0)
    col = lax.broadcasted_iota(I32, (tq, tq), 1)
    causal = col <= row
    m_ref[...] = jnp.full(m_ref.shape, NEG_INF, F32)
    acc_ref[...] = jnp.zeros(acc_ref.shape, F32)

    def step(kv, masked):
        off = pl.multiple_of(kv * tq, tq)
        for j, (k_ref, v_ref) in enumerate(((k0_ref, v0_ref), (k1_ref, v1_ref))):
            s = _dot_nt(qp[j], k_ref[0, pl.ds(off, tq), :])
            if masked:
                s = jnp.where(causal, s, NEG_INF)
            _flash_update(s, v_ref[0, pl.ds(off, tq), :], m_ref.at[j], acc_ref.at[j])

    def body(kv, carry):
        step(kv, False)
        return carry

    lax.fori_loop(0, qi, body, 0)
    step(qi, True)
    o_ref[0] = _finish_pair(acc_ref, left).astype(o_ref.dtype)


def _fox_attention(q, k0, k1, v0, v1):
    b, s, da = q.shape
    hp = da // LANES
    tq = min(TQ_FOX, s)
    kv = pl.BlockSpec((1, s, LANES), lambda bi, h, qi: (bi, 0, h))
    return pl.pallas_call(
        functools.partial(_fox_kernel, tq=tq),
        grid=(b, hp, s // tq),
        in_specs=[pl.BlockSpec((1, tq, LANES), lambda bi, h, qi: (bi, qi, h)), kv, kv, kv, kv],
        out_specs=pl.BlockSpec((1, tq, LANES), lambda bi, h, qi: (bi, qi, h)),
        out_shape=jax.ShapeDtypeStruct((b, s, da), BF16),
        scratch_shapes=[pltpu.VMEM((2, tq, LANES), F32)] * 2,
        compiler_params=_cparams(("parallel", "parallel", "arbitrary")),
        name="fox_attention",
    )(q, k0, k1, v0, v1)


def _moba_kernel(q_ref, k0_ref, k1_ref, v0_ref, v1_ref, bias_ref, o_ref, km_ref, m_ref, acc_ref, *, n_blk):
    blk = MOBA_BLOCK
    qi = pl.program_id(2)
    gate_lane0 = (HEAD_DIM, 0)
    lane = lax.broadcasted_iota(I32, (1, LANES), 1)
    left = lane < HEAD_DIM
    mine = (left, jnp.logical_not(left))
    k_refs = (k0_ref, k1_ref)
    v_refs = (v0_ref, v1_ref)

    @pl.when(qi == 0)
    def _():
        km_ref[...] = jnp.zeros_like(km_ref)
        for n in range(n_blk):
            rows = slice(n * blk, (n + 1) * blk)
            kb = jnp.where(left, k0_ref[0, rows, :], k1_ref[0, rows, :]).astype(F32)
            mean = jnp.sum(kb, axis=0, keepdims=True) * (1.0 / blk)
            for lane0 in gate_lane0:
                km_ref[lane0 + n:lane0 + n + 1, :] = mean

    q = q_ref[0]
    km_hi, km_lo = _split_bf16(km_ref[...])
    colf = lax.broadcasted_iota(I32, (blk, LANES), 1).astype(F32)
    qif = qi.astype(F32)
    own = pl.multiple_of(qi * blk, blk)
    qp = []
    for j in range(2):
        qj = jnp.where(mine[j], q, jnp.zeros_like(q))
        gate = _dot_nt(qj, km_hi) + _dot_nt(qj, km_lo)
        nf = colf - float(gate_lane0[j])
        gate = jnp.where(nf >= 0.0, jnp.where(nf < qif, gate, NEG_INF), NEG_INF)
        sel = jnp.zeros((blk, LANES), F32)
        for _ in range(MOBA_TOPK):
            mx = jnp.max(gate, axis=1, keepdims=True)
            cand = jnp.where(gate == mx, jnp.where(mx > NEG_INF, colf, float(LANES)), float(LANES))
            idx = jnp.min(cand, axis=1, keepdims=True)
            pick = colf == idx
            sel = jnp.where(pick, 1.0, sel)
            gate = jnp.where(pick, NEG_INF, gate)
        pen = jnp.where(sel > 0.5, 0.0, -MASK_BIG).astype(BF16)
        qp.append(jnp.where(mine[j], q, pen))

        s = _dot_nt(qj, k_refs[j][0, pl.ds(own, blk), :]) + bias_ref[j, 0]
        m = jnp.max(s, axis=1, keepdims=True)
        m_ref[j] = jnp.broadcast_to(m, (blk, LANES))
        acc_ref[j] = _dot(jnp.exp2(s - m).astype(BF16), v_refs[j][0, pl.ds(own, blk), :])

    def body(i, carry):
        n0 = 2 * i
        off = pl.multiple_of(n0 * blk, 2 * blk)
        d0 = qi - n0
        for j in range(2):
            bias2 = jnp.concatenate([bias_ref[j, d0], bias_ref[j, d0 - 1]], axis=1)
            s = _dot_nt(qp[j], k_refs[j][0, pl.ds(off, 2 * blk), :]) + bias2
            _flash_update(s, v_refs[j][0, pl.ds(off, 2 * blk), :], m_ref.at[j], acc_ref.at[j])
        return carry

    lax.fori_loop(0, (qi + 1) // 2, body, 0)
    o_ref[0] = _finish_pair(acc_ref, left).astype(o_ref.dtype)


def _moba_attention(q, k0, k1, v0, v1, bias_tiles):
    b, s, db = q.shape
    hp = db // LANES
    blk = MOBA_BLOCK
    n_blk = s // blk
    assert n_blk <= HEAD_DIM, "block gates of one head must fit in the other head's lanes"
    kv = pl.BlockSpec((1, s, LANES), lambda h, bi, qi: (bi, 0, h))
    return pl.pallas_call(
        functools.partial(_moba_kernel, n_blk=n_blk),
        grid=(hp, b, n_blk),
        in_specs=[pl.BlockSpec((1, blk, LANES), lambda h, bi, qi: (bi, qi, h)), kv, kv, kv, kv,
                  pl.BlockSpec((2, n_blk, blk, blk), lambda h, bi, qi: (h, 0, 0, 0))],
        out_specs=pl.BlockSpec((1, blk, LANES), lambda h, bi, qi: (bi, qi, h)),
        out_shape=jax.ShapeDtypeStruct((b, s, db), BF16),
        scratch_shapes=[pltpu.VMEM((LANES, LANES), F32)] + [pltpu.VMEM((2, blk, LANES), F32)] * 2,
        compiler_params=_cparams(("parallel", "parallel", "arbitrary")),
        name="moba_attention",
    )(q, k0, k1, v0, v1, bias_tiles)


def _dilated_kernel(q_ref, k_ref, v_ref, bias_ref, o_ref, m_ref, acc_ref, *, s_len):
    lane = lax.broadcasted_iota(I32, (1, LANES), 1)
    left = lane < HEAD_DIM
    one_lo = jnp.where(lane == 0, 1.0, 0.0).astype(BF16)
    one_hi = jnp.where(lane == HEAD_DIM, 1.0, 0.0).astype(BF16)
    for g, (window, dil) in enumerate(DIL_PATTERNS):
        span = window // dil
        unit = span * dil
        nc = s_len // unit
        n_u = min(DIL_CHUNKS_PER_STEP, nc)
        groups = nc // n_u

        def rows(ref, start, dil=dil, span=span):
            if dil == 1:
                return ref[0, pl.ds(start, span), :]
            return ref[0, pl.ds(start, span, stride=dil), :]

        def get(ref, j, start, dil=dil, span=span):
            if dil == 1:
                return ref[j, pl.ds(start, span), :]
            return ref[j, pl.ds(start, span, stride=dil), :]

        def put(ref, j, start, val, dil=dil, span=span):
            if dil == 1:
                ref[j, pl.ds(start, span), :] = val
            else:
                ref[j, pl.ds(start, span, stride=dil), :] = val

        def body(it, carry, g=g, n_u=n_u, groups=groups, unit=unit, rows=rows, put=put, get=get):
            r = it // groups
            start0 = r + (it - r * groups) * (n_u * unit)
            is_first = it - r * groups == 0
            prev0 = start0 - jnp.where(is_first, 0, unit)
            starts = [start0 + u * unit for u in range(n_u)]
            kc = [rows(k_ref, st).astype(BF16) for st in [prev0] + starts]
            vc = [rows(v_ref, st).astype(BF16) for st in [prev0] + starts]
            vcs = ([jnp.where(left, v, one_hi) for v in vc], [jnp.where(left, one_lo, v) for v in vc])
            results = []
            for u, start in enumerate(starts):
                var = 2 * g + jnp.where(is_first, 1, 0) if u == 0 else 2 * g
                q = rows(q_ref, start)
                kb = jnp.concatenate([kc[u], kc[u + 1]], axis=0)
                for j in range(2):
                    qj = (jnp.where(left, q, 0.0) if j == 0 else jnp.where(left, 0.0, q)).astype(BF16)
                    s = _dot_nt(qj, kb) + bias_ref[j, var]
                    m_new = jnp.broadcast_to(jnp.max(s, axis=1, keepdims=True), (span, LANES))
                    if g > 0:
                        m_prev = get(m_ref, j, start)
                        m_new = jnp.maximum(m_new, m_prev)
                    p = jnp.exp2(s - _tile_lanes(m_new, 2 * span))
                    acc_new = _dot(p.astype(BF16), jnp.concatenate([vcs[j][u], vcs[j][u + 1]], axis=0))
                    if g > 0:
                        acc_new = jnp.exp2(m_prev - m_new) * get(acc_ref, j, start) + acc_new
                    results.append((j, start, m_new, acc_new))
            for j, start, m_new, acc_new in results:
                put(m_ref, j, start, m_new)
                put(acc_ref, j, start, acc_new)
            return carry

        lax.fori_loop(0, dil * groups, body, 0)
    o_ref[0] = _finish_pair(acc_ref, left).astype(o_ref.dtype)


def _dilated_attention(q, k, v, bias_tiles):
    b, s, dq = q.shape
    hp = dq // LANES
    for window, dil in DIL_PATTERNS:
        assert s % window == 0, "sequence must be a whole number of dilated units"
    qkv = pl.BlockSpec((1, s, LANES), lambda h, bi: (bi, 0, h))
    n_var, span, band = bias_tiles.shape[1:]
    return pl.pallas_call(
        functools.partial(_dilated_kernel, s_len=s),
        grid=(hp, b),
        in_specs=[qkv, qkv, qkv,
                  pl.BlockSpec((2, n_var, span, band), lambda h, bi: (h, 0, 0, 0))],
        out_specs=pl.BlockSpec((1, s, LANES), lambda h, bi: (bi, 0, h)),
        out_shape=jax.ShapeDtypeStruct((b, s, dq), BF16),
        scratch_shapes=[pltpu.VMEM((2, s, LANES), F32)] * 2,
        compiler_params=_cparams(("parallel", "parallel")),
        name="dilated_attention",
    )(q, k, v, bias_tiles)


def _outproj_kernel(*refs, n_parts):
    o_parts = refs[:n_parts]
    w_parts = refs[n_parts:2 * n_parts]
    h_ref, gain_ref, gate_ref, out_ref = refs[2 * n_parts:]
    y = _dot(o_parts[0][0], w_parts[0][...])
    for o_ref, w_ref in zip(o_parts[1:], w_parts[1:]):
        y = y + _dot(o_ref[0], w_ref[...])
    out_ref[0] = h_ref[0] + gate_ref[0] * (_rms(y) * gain_ref[...])


def _outproj(o_parts, w_out, h, gain, gate):
    b, s, d = h.shape
    tm = TM_PROJ
    n = len(o_parts)
    w_out = w_out.astype(BF16)
    cuts = np.cumsum([p.shape[-1] for p in o_parts])[:-1].tolist()
    w_parts = jnp.split(w_out, cuts, axis=0) if cuts else [w_out]
    in_specs = [pl.BlockSpec((1, tm, p.shape[-1]), lambda bi, si: (bi, si, 0)) for p in o_parts]
    in_specs += [pl.BlockSpec(w.shape, lambda bi, si: (0, 0)) for w in w_parts]
    in_specs += [pl.BlockSpec((1, tm, d), lambda bi, si: (bi, si, 0)),
                 pl.BlockSpec((1, d), lambda bi, si: (0, 0)),
                 pl.BlockSpec((1, 1, d), lambda bi, si: (bi, 0, 0))]
    return pl.pallas_call(
        functools.partial(_outproj_kernel, n_parts=n),
        grid=(b, s // tm),
        in_specs=in_specs,
        out_specs=pl.BlockSpec((1, tm, d), lambda bi, si: (bi, si, 0)),
        out_shape=jax.ShapeDtypeStruct((b, s, d), F32),
        compiler_params=_cparams(("parallel", "parallel")),
        name="outproj",
    )(*o_parts, *w_parts, h, gain.reshape(1, d), gate.reshape(b, 1, d))


def _ffn_kernel(h_ref, g_ref, sc_ref, sh_ref, wg_ref, wu_ref, wd_ref, gain_ref, gate_ref, o_ref):
    h = h_ref[0]
    u = _modulate(h, g_ref[...], sc_ref[0], sh_ref[0]).astype(BF16)
    hid = (_silu(_dot(u, wg_ref[...])) * _dot(u, wu_ref[...])).astype(BF16)
    y = _dot(hid, wd_ref[...])
    o_ref[0] = h + gate_ref[0] * (_rms(y) * gain_ref[...])


def _ffn(h, gain_in, scale, shift, w_gate, w_up, w_down, gain_out, gate):
    b, s, d = h.shape
    tm = TM_FFN
    ff = w_gate.shape[1]
    resident = functools.partial(pl.BlockSpec, pipeline_mode=pl.Buffered(1))
    vec = pl.BlockSpec((1, 1, d), lambda bi, si: (bi, 0, 0))
    row = pl.BlockSpec((1, d), lambda bi, si: (0, 0))
    return pl.pallas_call(
        _ffn_kernel,
        grid=(b, s // tm),
        in_specs=[pl.BlockSpec((1, tm, d), lambda bi, si: (bi, si, 0)),
                  row, vec, vec,
                  resident((d, ff), lambda bi, si: (0, 0)),
                  resident((d, ff), lambda bi, si: (0, 0)),
                  resident((ff, d), lambda bi, si: (0, 0)),
                  row, vec],
        out_specs=pl.BlockSpec((1, tm, d), lambda bi, si: (bi, si, 0)),
        out_shape=jax.ShapeDtypeStruct((b, s, d), F32),
        compiler_params=_cparams(("parallel", "parallel")),
        name="dense_swiglu",
    )(h, gain_in.reshape(1, d), scale.reshape(b, 1, d), shift.reshape(b, 1, d),
      w_gate.astype(BF16), w_up.astype(BF16), w_down.astype(BF16),
      gain_out.reshape(1, d), gate.reshape(b, 1, d))


def _router_kernel(h_ref, g_ref, sc_ref, sh_ref, rw_ref, mi_ref, mf_ref, cnt_ref, carry_ref, *, tm):
    @pl.when((pl.program_id(0) == 0) & (pl.program_id(1) == 0))
    def _():
        carry_ref[...] = jnp.zeros_like(carry_ref)

    u = _modulate(h_ref[0], g_ref[...], sc_ref[0], sh_ref[0])
    logits = _dot_split(u, rw_ref[...])
    lanef = lax.broadcasted_iota(I32, (tm, LANES), 1).astype(F32)
    lg = jnp.where(lanef < N_EXPERTS, logits, NEG_INF)
    v1 = jnp.max(lg, axis=1, keepdims=True)
    i1 = jnp.min(jnp.where(lg == v1, lanef, float(LANES)), axis=1, keepdims=True)
    lg2 = jnp.where(lanef == i1, NEG_INF, lg)
    v2 = jnp.max(lg2, axis=1, keepdims=True)
    i2 = jnp.min(jnp.where(lg2 == v2, lanef, float(LANES)), axis=1, keepdims=True)
    e2 = jnp.exp(v2 - v1)
    p1 = 1.0 / (1.0 + e2)
    p2 = e2 / (1.0 + e2)
    oh1 = jnp.where(lanef == i1, 1.0, 0.0)
    oh2 = jnp.where(lanef == i2, 1.0, 0.0)
    oh = oh1 + oh2
    r = lax.broadcasted_iota(I32, (tm, tm), 0)
    c = lax.broadcasted_iota(I32, (tm, tm), 1)
    before = jnp.where(c < r, 1.0, 0.0).astype(BF16)
    tot = _dot(before, oh.astype(BF16)) + carry_ref[0:1, :]
    rank1 = jnp.sum(oh1 * tot, axis=1, keepdims=True)
    rank2 = jnp.sum(oh2 * tot, axis=1, keepdims=True)
    carry_ref[...] = carry_ref[...] + jnp.sum(oh, axis=0, keepdims=True)
    mi = jnp.where(lanef == 0.0, i1, jnp.where(lanef == 1.0, i2,
         jnp.where(lanef == 2.0, rank1, jnp.where(lanef == 3.0, rank2, 0.0))))
    mi_ref[...] = mi.astype(I32)
    mf_ref[...] = jnp.where(lanef == 0.0, p1, jnp.where(lanef == 1.0, p2, 0.0))
    cnt_ref[...] = carry_ref[...]


def _router(h, gain, scale, shift, router_w):
    b, s, d = h.shape
    tm = TM_ROUTE
    n = b * s
    ns = s // tm
    rw = jnp.pad(router_w.astype(F32), ((0, 0), (0, LANES - router_w.shape[1])))
    vec = pl.BlockSpec((1, 1, d), lambda bi, si: (bi, 0, 0))
    meta = pl.BlockSpec((tm, LANES), lambda bi, si: (bi * ns + si, 0))
    return pl.pallas_call(
        functools.partial(_router_kernel, tm=tm),
        grid=(b, ns),
        in_specs=[pl.BlockSpec((1, tm, d), lambda bi, si: (bi, si, 0)),
                  pl.BlockSpec((1, d), lambda bi, si: (0, 0)),
                  vec, vec,
                  pl.BlockSpec(rw.shape, lambda bi, si: (0, 0))],
        out_specs=[meta, meta, pl.BlockSpec((8, LANES), lambda bi, si: (0, 0))],
        out_shape=[jax.ShapeDtypeStruct((n, LANES), I32),
                   jax.ShapeDtypeStruct((n, LANES), F32),
                   jax.ShapeDtypeStruct((8, LANES), F32)],
        scratch_shapes=[pltpu.VMEM((8, LANES), F32)],
        compiler_params=_cparams(("arbitrary", "arbitrary")),
        name="moe_router",
    )(h, gain.reshape(1, d), scale.reshape(b, 1, d), shift.reshape(b, 1, d), rw)


def _scatter_kernel(dest_ref, h_ref, g_ref, sc_ref, sh_ref, xs_in_ref, xs_ref, ubuf, sems, *, tm, ns):
    del xs_in_ref
    step = pl.program_id(0) * ns + pl.program_id(1)
    nsteps = pl.num_programs(0) * ns
    slot = lax.rem(step, 2)

    def wait_slot(sl):
        for _ in range(2):
            pltpu.make_async_copy(ubuf.at[sl], ubuf.at[sl], sems.at[sl]).wait()

    @pl.when(step >= 2)
    def _():
        wait_slot(slot)

    ubuf[slot] = _modulate(h_ref[0], g_ref[...], sc_ref[0], sh_ref[0])
    base = step * tm

    def issue(i, carry):
        t = 2 * (base + i)
        src = ubuf.at[slot, pl.ds(i, 1), :]
        pltpu.make_async_copy(src, xs_ref.at[pl.ds(dest_ref[t], 1), :], sems.at[slot]).start()
        pltpu.make_async_copy(src, xs_ref.at[pl.ds(dest_ref[t + 1], 1), :], sems.at[slot]).start()
        return carry

    lax.fori_loop(0, tm, issue, 0)

    @pl.when(step == nsteps - 1)
    def _():
        wait_slot(slot)

        @pl.when(nsteps >= 2)
        def _():
            wait_slot(1 - slot)


def _scatter(dest, h, gain, scale, shift, m_pad):
    b, s, d = h.shape
    tm = TM_SCATTER
    ns = s // tm
    vec = pl.BlockSpec((1, 1, d), lambda bi, si, dest: (bi, 0, 0))
    grid_spec = pltpu.PrefetchScalarGridSpec(
        num_scalar_prefetch=1,
        grid=(b, ns),
        in_specs=[pl.BlockSpec((1, tm, d), lambda bi, si, dest: (bi, si, 0)),
                  pl.BlockSpec((1, d), lambda bi, si, dest: (0, 0)),
                  vec, vec,
                  pl.BlockSpec(memory_space=pl.ANY)],
        out_specs=pl.BlockSpec(memory_space=pl.ANY),
        scratch_shapes=[pltpu.VMEM((2, tm, d), F32), pltpu.SemaphoreType.DMA((2,))],
    )
    return pl.pallas_call(
        functools.partial(_scatter_kernel, tm=tm, ns=ns),
        grid_spec=grid_spec,
        out_shape=jax.ShapeDtypeStruct((m_pad, d), F32),
        input_output_aliases={5: 0},
        compiler_params=_cparams(("arbitrary", "arbitrary")),
        name="moe_scatter",
    )(dest, h, gain.reshape(1, d), scale.reshape(b, 1, d), shift.reshape(b, 1, d),
      jnp.zeros((m_pad, d), F32))


def _expert_kernel(te_ref, tv_ref, x_ref, wg_ref, wu_ref, wd_ref, o_ref):
    t = pl.program_id(0)
    f = pl.program_id(1)

    @pl.when(tv_ref[t] == 1)
    def _():
        x = x_ref[...].astype(BF16)
        hid = (_silu(_dot(x, wg_ref[0])) * _dot(x, wu_ref[0])).astype(BF16)
        y = _dot(hid, wd_ref[0])

        @pl.when(f == 0)
        def _():
            o_ref[...] = y

        @pl.when(f > 0)
        def _():
            o_ref[...] = o_ref[...] + y

    @pl.when((tv_ref[t] == 0) & (f == 0))
    def _():
        o_ref[...] = jnp.zeros_like(o_ref)


def _experts(tile_expert, tile_valid, xs, w_gate, w_up, w_down):
    m_pad, d = xs.shape
    tm = TM_EXPERT
    n_tiles = tile_expert.shape[0]
    ff = w_gate.shape[2]
    fs = FF_STEPS_EXPERT
    tf = ff // fs

    def ff_idx(f, tv, t):
        return f * tv[t] + (fs - 1) * (1 - tv[t])

    grid_spec = pltpu.PrefetchScalarGridSpec(
        num_scalar_prefetch=2,
        grid=(n_tiles, fs),
        in_specs=[pl.BlockSpec((tm, d), lambda t, f, te, tv: (t, 0)),
                  pl.BlockSpec((1, d, tf), lambda t, f, te, tv: (te[t], 0, ff_idx(f, tv, t))),
                  pl.BlockSpec((1, d, tf), lambda t, f, te, tv: (te[t], 0, ff_idx(f, tv, t))),
                  pl.BlockSpec((1, tf, d), lambda t, f, te, tv: (te[t], ff_idx(f, tv, t), 0))],
        out_specs=pl.BlockSpec((tm, d), lambda t, f, te, tv: (t, 0)),
    )
    return pl.pallas_call(
        _expert_kernel,
        grid_spec=grid_spec,
        out_shape=jax.ShapeDtypeStruct((m_pad, d), F32),
        compiler_params=_cparams(("arbitrary", "arbitrary")),
        name="moe_experts",
    )(tile_expert, tile_valid, xs,
      w_gate.astype(BF16), w_up.astype(BF16), w_down.astype(BF16))


def _combine_kernel(dest_ref, y_ref, mf_ref, h_ref, gain_ref, gate_ref, o_ref, ybuf, sems, *, tm, ns):
    step = pl.program_id(0) * ns + pl.program_id(1)
    nsteps = pl.num_programs(0) * ns
    slot = lax.rem(step, 2)

    def issue(st, sl):
        base = st * tm

        def body(i, carry):
            t = 2 * (base + i)
            pltpu.make_async_copy(y_ref.at[pl.ds(dest_ref[t], 1), :],
                                  ybuf.at[sl, 0, pl.ds(i, 1), :], sems.at[sl]).start()
            pltpu.make_async_copy(y_ref.at[pl.ds(dest_ref[t + 1], 1), :],
                                  ybuf.at[sl, 1, pl.ds(i, 1), :], sems.at[sl]).start()
            return carry

        lax.fori_loop(0, tm, body, 0)

    @pl.when(step == 0)
    def _():
        issue(0, 0)

    @pl.when(step + 1 < nsteps)
    def _():
        issue(step + 1, 1 - slot)

    for k in range(2):
        pltpu.make_async_copy(ybuf.at[slot, k], ybuf.at[slot, k], sems.at[slot]).wait()
    mf = mf_ref[...]
    y = mf[:, 0:1] * ybuf[slot, 0] + mf[:, 1:2] * ybuf[slot, 1]
    o_ref[0] = h_ref[0] + gate_ref[0] * (_rms(y) * gain_ref[...])


def _combine(dest, ys, mf, h, gain, gate):
    b, s, d = h.shape
    tm = TM_COMBINE
    ns = s // tm
    grid_spec = pltpu.PrefetchScalarGridSpec(
        num_scalar_prefetch=1,
        grid=(b, ns),
        in_specs=[pl.BlockSpec(memory_space=pl.ANY),
                  pl.BlockSpec((tm, LANES), lambda bi, si, dest: (bi * ns + si, 0)),
                  pl.BlockSpec((1, tm, d), lambda bi, si, dest: (bi, si, 0)),
                  pl.BlockSpec((1, d), lambda bi, si, dest: (0, 0)),
                  pl.BlockSpec((1, 1, d), lambda bi, si, dest: (bi, 0, 0))],
        out_specs=pl.BlockSpec((1, tm, d), lambda bi, si, dest: (bi, si, 0)),
        scratch_shapes=[pltpu.VMEM((2, 2, tm, d), F32), pltpu.SemaphoreType.DMA((2,))],
    )
    return pl.pallas_call(
        functools.partial(_combine_kernel, tm=tm, ns=ns),
        grid_spec=grid_spec,
        out_shape=jax.ShapeDtypeStruct((b, s, d), F32),
        compiler_params=_cparams(("arbitrary", "arbitrary")),
        name="moe_combine",
    )(dest, ys, mf, h, gain.reshape(1, d), gate.reshape(b, 1, d))


def _moe(h, gain_in, scale, shift, router_w, w_gate, w_up, w_down, gain_out, gate):
    b, s, d = h.shape
    n = b * s
    tm = TM_EXPERT
    mi, mf, cnt = _router(h, gain_in, scale, shift, router_w)
    counts = cnt[0, :N_EXPERTS].astype(I32)
    tiles_per = (counts + tm - 1) // tm
    seg_start = (jnp.cumsum(tiles_per) - tiles_per) * tm
    dest = (seg_start[mi[:, 0:2]] + mi[:, 2:4]).reshape(2 * n)
    n_tiles = (2 * n) // tm + N_EXPERTS
    m_pad = n_tiles * tm
    tile_end = jnp.cumsum(tiles_per)
    tidx = jnp.arange(n_tiles, dtype=I32)
    tile_valid = (tidx < tile_end[-1]).astype(I32)
    tile_expert = jnp.minimum(jnp.searchsorted(tile_end, tidx, side="right"), N_EXPERTS - 1).astype(I32)
    xs = _scatter(dest, h, gain_in, scale, shift, m_pad)
    ys = _experts(tile_expert, tile_valid, xs, w_gate, w_up, w_down)
    return _combine(dest, ys, mf, h, gain_out, gate)


def kernel(x, c, mod_w, mod_b, norm_g, attn_in_w_even, fox_gate_bias, attn_out_w_even,
           attn_in_w_odd, attn_out_w_odd, rel_bias_table, ffn_w_gate, ffn_w_up, ffn_w_down,
           router_w, exp_w_gate, exp_w_up, exp_w_down):
    depth = mod_w.shape[0]
    s_len = x.shape[1]
    mods = _mods(c, mod_w, mod_b)
    dil_bias, moba_bias = _bias_tiles(rel_bias_table, s_len)
    h = x
    for layer in range(depth):
        j = layer // 2
        sh1, sc1, g1, sh2, sc2, g2 = jnp.split(mods[layer], 6, axis=-1)
        gains = norm_g[layer]
        if layer % 2 == 0:
            fox_in, moba_in = _inproj_even(h, gains[0], sc1, sh1, attn_in_w_even[j], fox_gate_bias[j])
            o_parts = [_fox_attention(*fox_in), _moba_attention(*moba_in, moba_bias)]
            h = _outproj(o_parts, attn_out_w_even[j], h, gains[1], g1)
            h = _ffn(h, gains[2], sc2, sh2, ffn_w_gate[j], ffn_w_up[j], ffn_w_down[j], gains[3], g2)
        else:
            q, k, v = _inproj_odd(h, gains[0], sc1, sh1, attn_in_w_odd[j])
            o = _dilated_attention(q, k, v, dil_bias)
            h = _outproj([o], attn_out_w_odd[j], h, gains[1], g1)
            h = _moe(h, gains[2], sc2, sh2, router_w[j], exp_w_gate[j], exp_w_up[j], exp_w_down[j],
                     gains[3], g2)
    return h
```

```python
import functools
import math

import numpy as np
import jax
import jax.numpy as jnp
from jax import lax
from jax.experimental import pallas as pl
from jax.experimental.pallas import tpu as pltpu

F32 = jnp.float32
BF16 = jnp.bfloat16
I32 = jnp.int32

HEAD_DIM = 64
LANES = 128
N_HEADS = 16
N_HEADS_FOX = 8
ATTN_SCALE = HEAD_DIM ** -0.5
LOG2E = math.log2(math.e)
Q_SCALE = ATTN_SCALE * LOG2E
N_DECAY_PIECES = 3
MOBA_BLOCK = 256
MOBA_TOPK = 3
DIL_PATTERNS = ((128, 1), (512, 4), (2048, 16))
NUM_BUCKETS = 32
MAX_DISTANCE = 2048
N_EXPERTS = 8
NORM_EPS = 1e-6
NEG_INF = float("-inf")
MASK_BIG = 1e30

VMEM_LIMIT = 56 * 1024 * 1024

TM_PROJ = 512
TM_FFN = 512
TQ_FOX = 512
TM_ROUTE = 512
TM_SCATTER = 256
TM_EXPERT = 512
TM_COMBINE = 256
FF_STEPS_EXPERT = 2
DMA_UNROLL = True
DIL_CHUNKS_PER_STEP = 4


def _cparams(sem):
    return pltpu.CompilerParams(dimension_semantics=sem, vmem_limit_bytes=VMEM_LIMIT)


def _t5_bucket_np(n):
    n = np.maximum(n, 0)
    max_exact = NUM_BUCKETS // 2
    nf = np.maximum(n, 1).astype(np.float64)
    large = max_exact + (np.log(nf / max_exact) / math.log(MAX_DISTANCE / max_exact)
                         * (NUM_BUCKETS - max_exact)).astype(np.int64)
    large = np.minimum(large, NUM_BUCKETS - 1)
    return np.where(n < max_exact, n, large)


_MAX_DIST = 1 << 16
_BUCKET_OF = _t5_bucket_np(np.arange(_MAX_DIST))
_BUCKET_THR = [int(np.searchsorted(_BUCKET_OF, k, side="left")) for k in range(NUM_BUCKETS)]


def _bias_from_dist(tab_ref, h, dist, dlo, dhi):
    lo_b = int(_BUCKET_OF[max(dlo, 0)])
    hi_b = int(_BUCKET_OF[dhi])
    val = jnp.zeros(dist.shape, F32) + tab_ref[lo_b, h]
    for k in range(lo_b + 1, hi_b + 1):
        val = jnp.where(dist >= _BUCKET_THR[k], tab_ref[k, h], val)
    return val


def _dil_bias_kernel(tab_ref, o_ref):
    h = pl.program_id(0)
    for g, (window, dil) in enumerate(DIL_PATTERNS):
        span = window // dil
        i = lax.broadcasted_iota(I32, (span, 2 * span), 0)
        j = lax.broadcasted_iota(I32, (span, 2 * span), 1)
        rel = i + span - j
        val = _bias_from_dist(tab_ref, h, rel * dil, 0, span * dil) * LOG2E
        band = jnp.where(rel >= 0, jnp.where(rel <= span, val, NEG_INF), NEG_INF)
        o_ref[0, 2 * g] = band
        o_ref[0, 2 * g + 1] = jnp.where(j >= span, band, NEG_INF)


def _moba_bias_kernel(tab_ref, o_ref, *, n_blk, head0):
    h = pl.program_id(0) + head0
    i = lax.broadcasted_iota(I32, (MOBA_BLOCK, MOBA_BLOCK), 0)
    j = lax.broadcasted_iota(I32, (MOBA_BLOCK, MOBA_BLOCK), 1)
    for d in range(n_blk):
        dist = d * MOBA_BLOCK + i - j
        val = _bias_from_dist(tab_ref, h, dist, d * MOBA_BLOCK - (MOBA_BLOCK - 1),
                              d * MOBA_BLOCK + (MOBA_BLOCK - 1)) * LOG2E
        if d == 0:
            val = jnp.where(dist >= 0, val, NEG_INF)
        o_ref[0, d] = val


def _bias_tiles(rel_bias_table, s_len):
    n_blk = s_len // MOBA_BLOCK
    span = DIL_PATTERNS[0][0]
    n_var = 2 * len(DIL_PATTERNS)
    smem = pl.BlockSpec(memory_space=pltpu.SMEM)
    dil = pl.pallas_call(
        _dil_bias_kernel,
        grid=(N_HEADS,),
        in_specs=[smem],
        out_specs=pl.BlockSpec((1, n_var, span, 2 * span), lambda h: (h, 0, 0, 0)),
        out_shape=jax.ShapeDtypeStruct((N_HEADS, n_var, span, 2 * span), F32),
        compiler_params=_cparams(("parallel",)),
        name="dil_bias",
    )(rel_bias_table)
    n_moba = N_HEADS - N_HEADS_FOX
    moba = pl.pallas_call(
        functools.partial(_moba_bias_kernel, n_blk=n_blk, head0=N_HEADS_FOX),
        grid=(n_moba,),
        in_specs=[smem],
        out_specs=pl.BlockSpec((1, n_blk, MOBA_BLOCK, MOBA_BLOCK), lambda h: (h, 0, 0, 0)),
        out_shape=jax.ShapeDtypeStruct((n_moba, n_blk, MOBA_BLOCK, MOBA_BLOCK), F32),
        compiler_params=_cparams(("parallel",)),
        name="moba_bias",
    )(rel_bias_table)
    return dil, moba


def _split_bf16(a):
    hi = a.astype(BF16)
    lo = (a - hi.astype(F32)).astype(BF16)
    return hi, lo


def _dot(a, b):
    return jnp.dot(a, b, preferred_element_type=F32)


def _dot_nt(a, b):
    return lax.dot_general(a, b, (((1,), (1,)), ((), ())), preferred_element_type=F32)


def _dot_split(a, b):
    a_hi, a_lo = _split_bf16(a)
    b_hi, b_lo = _split_bf16(b)
    return _dot(a_hi, b_hi) + (_dot(a_hi, b_lo) + _dot(a_lo, b_hi))


def _rms(x):
    return x * lax.rsqrt(jnp.mean(x * x, axis=-1, keepdims=True) + NORM_EPS)


def _modulate(x, gain, scale, shift):
    return (_rms(x) * gain) * (1.0 + scale) + shift


def _silu(x):
    return x * jax.nn.sigmoid(x)


def _mods_kernel(c_ref, w_ref, b_ref, o_ref):
    o_ref[0] = _dot_split(_silu(c_ref[...]), w_ref[0]) + b_ref[0]


def _mods(c, mod_w, mod_b):
    depth, d, e = mod_w.shape
    b = c.shape[0]
    tn = 1536
    return pl.pallas_call(
        _mods_kernel,
        grid=(depth, e // tn),
        in_specs=[pl.BlockSpec((b, d), lambda l, j: (0, 0)),
                  pl.BlockSpec((1, d, tn), lambda l, j: (l, 0, j)),
                  pl.BlockSpec((1, 1, tn), lambda l, j: (l, 0, j))],
        out_specs=pl.BlockSpec((1, b, tn), lambda l, j: (l, 0, j)),
        out_shape=jax.ShapeDtypeStruct((depth, b, e), F32),
        compiler_params=_cparams(("parallel", "parallel")),
        name="adaln_mods",
    )(c, mod_w, mod_b.reshape(depth, 1, e))


def _inproj_even_kernel(h_ref, g_ref, sc_ref, sh_ref, w_ref, wf_ref, gb_ref,
                        qa_ref, k0a_ref, k1a_ref, v0a_ref, v1a_ref,
                        qb_ref, k0b_ref, k1b_ref, v0b_ref, v1b_ref, carry_ref, *, tm):
    si = pl.program_id(1)
    u = _modulate(h_ref[0], g_ref[...], sc_ref[0], sh_ref[0]).astype(BF16)
    width = qa_ref.shape[-1]
    n_pairs = width // LANES

    def proj(i):
        return _dot(u, w_ref[:, i * width:(i + 1) * width])

    lane = lax.broadcasted_iota(I32, (1, LANES), 1)
    left = lane < HEAD_DIM
    row = lax.broadcasted_iota(I32, (tm, LANES), 0)

    x = _dot(u, wf_ref[...]) + gb_ref[...]
    lf = jnp.where(lane < N_HEADS_FOX, jnp.minimum(x, 0.0) - jnp.log1p(jnp.exp(-jnp.abs(x))), 0.0)
    k = 1
    while k < tm:
        lf = lf + jnp.where(row >= k, pltpu.roll(lf, k, axis=0), 0.0)
        k *= 2

    @pl.when(si == 0)
    def _():
        carry_ref[...] = jnp.zeros_like(carry_ref)

    cum = lf + carry_ref[0:1, :]
    carry_ref[...] = jnp.broadcast_to(cum[tm - 1:tm, :], carry_ref.shape)
    rest = cum * (-LOG2E)
    decay = jnp.zeros((tm, LANES), F32)
    for p in range(N_DECAY_PIECES):
        piece = rest.astype(BF16).astype(F32)
        rest = rest - piece
        decay = decay + (pltpu.roll(piece, p * N_HEADS_FOX, axis=1) if p else piece)
    decay_lo = decay.astype(BF16)
    decay_hi = pltpu.roll(decay, HEAD_DIM, axis=1).astype(BF16)

    one_lo = jnp.where(lane == 0, 1.0, 0.0).astype(BF16)
    one_hi = jnp.where(lane == HEAD_DIM, 1.0, 0.0).astype(BF16)
    blk = (si * tm + row) // MOBA_BLOCK
    blk_lo = jnp.where(lane == blk, 1.0, 0.0).astype(BF16)
    blk_hi = jnp.where(lane == blk + HEAD_DIM, 1.0, 0.0).astype(BF16)

    def emit(first, k0_ref, k1_ref, v0_ref, v1_ref, k_lo, k_hi):
        kk = proj(first + 1).astype(BF16)
        vv = proj(first + 2).astype(BF16)
        for hp in range(n_pairs):
            sl = slice(hp * LANES, (hp + 1) * LANES)
            k0_ref[0, :, sl] = jnp.where(left, kk[:, sl], k_hi)
            k1_ref[0, :, sl] = jnp.where(left, k_lo, kk[:, sl])
            v0_ref[0, :, sl] = jnp.where(left, vv[:, sl], one_hi)
            v1_ref[0, :, sl] = jnp.where(left, one_lo, vv[:, sl])

    qa_ref[0] = (proj(0) * Q_SCALE).astype(BF16)
    emit(0, k0a_ref, k1a_ref, v0a_ref, v1a_ref, decay_lo, decay_hi)
    qb_ref[0] = (proj(3) * Q_SCALE).astype(BF16)
    emit(3, k0b_ref, k1b_ref, v0b_ref, v1b_ref, blk_lo, blk_hi)


def _inproj_even(h, gain, scale, shift, w_in, gate_bias):
    b, s, d = h.shape
    tm = TM_PROJ
    da = N_HEADS_FOX * HEAD_DIM
    assert s // MOBA_BLOCK <= HEAD_DIM and N_DECAY_PIECES * N_HEADS_FOX <= HEAD_DIM
    cuts = np.cumsum([da, da, da, N_HEADS_FOX, da, da]).tolist()
    q_a, k_a, v_a, f_a, q_b, k_b, v_b = jnp.split(w_in, cuts, axis=1)
    w = jnp.concatenate([q_a, k_a, v_a, q_b, k_b, v_b], axis=1).astype(BF16)
    wf = jnp.pad(f_a, ((0, 0), (0, LANES - N_HEADS_FOX))).astype(BF16)
    gb = jnp.pad(gate_bias.astype(F32), (0, LANES - N_HEADS_FOX)).reshape(1, LANES)
    act = jax.ShapeDtypeStruct((b, s, da), BF16)
    act_spec = pl.BlockSpec((1, tm, da), lambda bi, si: (bi, si, 0))
    vec = pl.BlockSpec((1, 1, d), lambda bi, si: (bi, 0, 0))
    outs = pl.pallas_call(
        functools.partial(_inproj_even_kernel, tm=tm),
        grid=(b, s // tm),
        in_specs=[pl.BlockSpec((1, tm, d), lambda bi, si: (bi, si, 0)),
                  pl.BlockSpec((1, d), lambda bi, si: (0, 0)),
                  vec, vec,
                  pl.BlockSpec(w.shape, lambda bi, si: (0, 0)),
                  pl.BlockSpec(wf.shape, lambda bi, si: (0, 0)),
                  pl.BlockSpec(gb.shape, lambda bi, si: (0, 0))],
        out_specs=[act_spec] * 10,
        out_shape=[act] * 10,
        scratch_shapes=[pltpu.VMEM((8, LANES), F32)],
        compiler_params=_cparams(("parallel", "arbitrary")),
        name="inproj_even",
    )(h, gain.reshape(1, d), scale.reshape(b, 1, d), shift.reshape(b, 1, d), w, wf, gb)
    return outs[:5], outs[5:]


def _inproj_odd_kernel(h_ref, g_ref, sc_ref, sh_ref, w_ref, q_ref, k_ref, v_ref):
    u = _modulate(h_ref[0], g_ref[...], sc_ref[0], sh_ref[0]).astype(BF16)
    width = q_ref.shape[-1]
    q_ref[0] = _dot(u, w_ref[:, 0:width]) * Q_SCALE
    k_ref[0] = _dot(u, w_ref[:, width:2 * width])
    v_ref[0] = _dot(u, w_ref[:, 2 * width:3 * width])


def _inproj_odd(h, gain, scale, shift, w_in):
    b, s, d = h.shape
    tm = TM_PROJ
    dq = w_in.shape[1] // 3
    act = jax.ShapeDtypeStruct((b, s, dq), F32)
    act_spec = pl.BlockSpec((1, tm, dq), lambda bi, si: (bi, si, 0))
    vec = pl.BlockSpec((1, 1, d), lambda bi, si: (bi, 0, 0))
    return pl.pallas_call(
        _inproj_odd_kernel,
        grid=(b, s // tm),
        in_specs=[pl.BlockSpec((1, tm, d), lambda bi, si: (bi, si, 0)),
                  pl.BlockSpec((1, d), lambda bi, si: (0, 0)),
                  vec, vec,
                  pl.BlockSpec(w_in.shape, lambda bi, si: (0, 0))],
        out_specs=[act_spec] * 3,
        out_shape=[act] * 3,
        compiler_params=_cparams(("parallel", "parallel")),
        name="inproj_odd",
    )(h, gain.reshape(1, d), scale.reshape(b, 1, d), shift.reshape(b, 1, d), w_in.astype(BF16))


def _tile_lanes(x, width):
    return jnp.concatenate([x] * (width // LANES), axis=1)


def _flash_update(s, v, m_ref, acc_ref):
    m_prev = m_ref[...]
    m_new = jnp.maximum(m_prev, jnp.max(s, axis=1, keepdims=True))
    p = jnp.exp2(s - _tile_lanes(m_new, s.shape[1]))
    acc_ref[...] = jnp.exp2(m_prev - m_new) * acc_ref[...] + _dot(p.astype(BF16), v)
    m_ref[...] = m_new


def _finish_pair(acc_ref, left):
    acc0 = acc_ref[0]
    acc1 = acc_ref[1]
    return jnp.where(left, acc0 / acc0[:, HEAD_DIM:HEAD_DIM + 1], acc1 / acc1[:, 0:1])


def _fox_kernel(q_ref, k0_ref, k1_ref, v0_ref, v1_ref, o_ref, m_ref, acc_ref, *, tq):
    hp = pl.program_id(1)
    qi = pl.program_id(2)
    q = q_ref[0]
    lane = lax.broadcasted_iota(I32, (1, LANES), 1)
    left = lane < HEAD_DIM

    def piece_lanes(lane0):
        hit = lane == lane0
        for p in range(1, N_DECAY_PIECES):
            hit = jnp.logical_or(hit, lane == lane0 + p * N_HEADS_FOX)
        return jnp.where(hit, 1.0, 0.0).astype(BF16)

    qp = (jnp.where(left, q, piece_lanes(HEAD_DIM + 2 * hp)), jnp.where(left, piece_lanes(2 * hp + 1), q))
    row = lax.broadcasted_iota(I32, (tq, tq), 0)
    col = lax.broadcasted_iota(I32, (tq, tq), 1)
    causal = col <= row
    m_ref[...] = jnp.full(m_ref.shape, NEG_INF, F32)
    acc_ref[...] = jnp.zeros(acc_ref.shape, F32)

    def step(kv, masked):
        off = pl.multiple_of(kv * tq, tq)
        for j, (k_ref, v_ref) in enumerate(((k0_ref, v0_ref), (k1_ref, v1_ref))):
            s = _dot_nt(qp[j], k_ref[0, pl.ds(off, tq), :])
            if masked:
                s = jnp.where(causal, s, NEG_INF)
            _flash_update(s, v_ref[0, pl.ds(off, tq), :], m_ref.at[j], acc_ref.at[j])

    def body(kv, carry):
        step(kv, False)
        return carry

    lax.fori_loop(0, qi, body, 0)
    step(qi, True)
    o_ref[0] = _finish_pair(acc_ref, left).astype(o_ref.dtype)


def _fox_attention(q, k0, k1, v0, v1):
    b, s, da = q.shape
    hp = da // LANES
    tq = min(TQ_FOX, s)
    kv = pl.BlockSpec((1, s, LANES), lambda bi, h, qi: (bi, 0, h))
    return pl.pallas_call(
        functools.partial(_fox_kernel, tq=tq),
        grid=(b, hp, s // tq),
        in_specs=[pl.BlockSpec((1, tq, LANES), lambda bi, h, qi: (bi, qi, h)), kv, kv, kv, kv],
        out_specs=pl.BlockSpec((1, tq, LANES), lambda bi, h, qi: (bi, qi, h)),
        out_shape=jax.ShapeDtypeStruct((b, s, da), BF16),
        scratch_shapes=[pltpu.VMEM((2, tq, LANES), F32)] * 2,
        compiler_params=_cparams(("parallel", "parallel", "arbitrary")),
        name="fox_attention",
    )(q, k0, k1, v0, v1)


def _moba_kernel(q_ref, k0_ref, k1_ref, v0_ref, v1_ref, bias_ref, o_ref, km_ref, m_ref, acc_ref, *, n_blk, tq):
    blk = MOBA_BLOCK
    a = pl.program_id(2)
    gate_lane0 = (HEAD_DIM, 0)
    lane = lax.broadcasted_iota(I32, (1, LANES), 1)
    left = lane < HEAD_DIM
    mine = (left, jnp.logical_not(left))
    k_refs = (k0_ref, k1_ref)
    v_refs = (v0_ref, v1_ref)

    @pl.when(a == 0)
    def _():
        km_ref[...] = jnp.zeros_like(km_ref)
        for n in range(n_blk):
            rows = slice(n * blk, (n + 1) * blk)
            kb = jnp.where(left, k0_ref[0, rows, :], k1_ref[0, rows, :]).astype(F32)
            mean = jnp.sum(kb, axis=0, keepdims=True) * (1.0 / blk)
            for lane0 in gate_lane0:
                km_ref[lane0 + n:lane0 + n + 1, :] = mean

    q = q_ref[0]
    km_hi, km_lo = _split_bf16(km_ref[...])
    colf = lax.broadcasted_iota(I32, (tq, LANES), 1).astype(F32)
    own = (lax.broadcasted_iota(I32, (tq, LANES), 0) // blk + a * (tq // blk)).astype(F32)
    qp = []
    for j in range(2):
        qj = jnp.where(mine[j], q, jnp.zeros_like(q))
        gate = _dot_nt(qj, km_hi) + _dot_nt(qj, km_lo)
        nf = colf - float(gate_lane0[j])
        gate = jnp.where(nf >= 0.0, jnp.where(nf < own, gate, NEG_INF), NEG_INF)
        pen = jnp.where(nf == own, 0.0, -MASK_BIG)
        for _ in range(MOBA_TOPK):
            mx = jnp.max(gate, axis=1, keepdims=True)
            cand = jnp.where(gate == mx, jnp.where(mx > NEG_INF, colf, float(LANES)), float(LANES))
            pick = colf == jnp.min(cand, axis=1, keepdims=True)
            pen = jnp.where(pick, 0.0, pen)
            gate = jnp.where(pick, NEG_INF, gate)
        qp.append(jnp.where(mine[j], q, pen.astype(BF16)))

    m_ref[...] = jnp.full(m_ref.shape, NEG_INF, F32)
    acc_ref[...] = jnp.zeros(acc_ref.shape, F32)

    def body(i, carry):
        off = pl.multiple_of(i * tq, tq)
        d0 = 2 * (a - i)
        for j in range(2):
            top = jnp.concatenate([bias_ref[j, d0], bias_ref[j, jnp.maximum(d0 - 1, 0)]], axis=1)
            bot = jnp.concatenate([bias_ref[j, d0 + 1], bias_ref[j, d0]], axis=1)
            s = _dot_nt(qp[j], k_refs[j][0, pl.ds(off, tq), :]) + jnp.concatenate([top, bot], axis=0)
            _flash_update(s, v_refs[j][0, pl.ds(off, tq), :], m_ref.at[j], acc_ref.at[j])
        return carry

    lax.fori_loop(0, a + 1, body, 0)
    o_ref[0] = _finish_pair(acc_ref, left).astype(o_ref.dtype)


def _moba_attention(q, k0, k1, v0, v1, bias_tiles):
    b, s, db = q.shape
    hp = db // LANES
    blk = MOBA_BLOCK
    n_blk = s // blk
    tq = 2 * blk
    assert n_blk <= HEAD_DIM, "block gates of one head must fit in the other head's lanes"
    assert s % tq == 0
    kv = pl.BlockSpec((1, s, LANES), lambda h, bi, qi: (bi, 0, h))
    return pl.pallas_call(
        functools.partial(_moba_kernel, n_blk=n_blk, tq=tq),
        grid=(hp, b, s // tq),
        in_specs=[pl.BlockSpec((1, tq, LANES), lambda h, bi, qi: (bi, qi, h)), kv, kv, kv, kv,
                  pl.BlockSpec((2, n_blk, blk, blk), lambda h, bi, qi: (h, 0, 0, 0))],
        out_specs=pl.BlockSpec((1, tq, LANES), lambda h, bi, qi: (bi, qi, h)),
        out_shape=jax.ShapeDtypeStruct((b, s, db), BF16),
        scratch_shapes=[pltpu.VMEM((LANES, LANES), F32)] + [pltpu.VMEM((2, tq, LANES), F32)] * 2,
        compiler_params=_cparams(("parallel", "parallel", "arbitrary")),
        name="moba_attention",
    )(q, k0, k1, v0, v1, bias_tiles)


def _dilated_kernel(q_ref, k_ref, v_ref, bias_ref, o_ref, m_ref, acc_ref, *, s_len):
    lane = lax.broadcasted_iota(I32, (1, LANES), 1)
    left = lane < HEAD_DIM
    one_lo = jnp.where(lane == 0, 1.0, 0.0).astype(BF16)
    one_hi = jnp.where(lane == HEAD_DIM, 1.0, 0.0).astype(BF16)
    for g, (window, dil) in enumerate(DIL_PATTERNS):
        span = window // dil
        unit = span * dil
        nc = s_len // unit
        n_u = min(DIL_CHUNKS_PER_STEP, nc)
        groups = nc // n_u

        def rows(ref, start, dil=dil, span=span):
            if dil == 1:
                return ref[0, pl.ds(start, span), :]
            return ref[0, pl.ds(start, span, stride=dil), :]

        def get(ref, j, start, dil=dil, span=span):
            if dil == 1:
                return ref[j, pl.ds(start, span), :]
            return ref[j, pl.ds(start, span, stride=dil), :]

        def put(ref, j, start, val, dil=dil, span=span):
            if dil == 1:
                ref[j, pl.ds(start, span), :] = val
            else:
                ref[j, pl.ds(start, span, stride=dil), :] = val

        def body(it, carry, g=g, n_u=n_u, groups=groups, unit=unit, rows=rows, put=put, get=get):
            r = it // groups
            start0 = r + (it - r * groups) * (n_u * unit)
            is_first = it - r * groups == 0
            prev0 = start0 - jnp.where(is_first, 0, unit)
            starts = [start0 + u * unit for u in range(n_u)]
            kc = [rows(k_ref, st).astype(BF16) for st in [prev0] + starts]
            vc = [rows(v_ref, st).astype(BF16) for st in [prev0] + starts]
            vcs = ([jnp.where(left, v, one_hi) for v in vc], [jnp.where(left, one_lo, v) for v in vc])
            results = []
            for u, start in enumerate(starts):
                var = 2 * g + jnp.where(is_first, 1, 0) if u == 0 else 2 * g
                q = rows(q_ref, start)
                kb = jnp.concatenate([kc[u], kc[u + 1]], axis=0)
                qq = jnp.concatenate([jnp.where(left, q, 0.0), jnp.where(left, 0.0, q)], axis=0).astype(BF16)
                s_both = _dot_nt(qq, kb)
                for j in range(2):
                    s = s_both[j * span:(j + 1) * span] + bias_ref[j, var]
                    m_new = jnp.broadcast_to(jnp.max(s, axis=1, keepdims=True), (span, LANES))
                    if g > 0:
                        m_prev = get(m_ref, j, start)
                        m_new = jnp.maximum(m_new, m_prev)
                    p = jnp.exp2(s - _tile_lanes(m_new, 2 * span))
                    acc_new = _dot(p.astype(BF16), jnp.concatenate([vcs[j][u], vcs[j][u + 1]], axis=0))
                    if g > 0:
                        acc_new = jnp.exp2(m_prev - m_new) * get(acc_ref, j, start) + acc_new
                    results.append((j, start, m_new, acc_new))
            for j, start, m_new, acc_new in results:
                put(m_ref, j, start, m_new)
                put(acc_ref, j, start, acc_new)
            return carry

        lax.fori_loop(0, dil * groups, body, 0)
    o_ref[0] = _finish_pair(acc_ref, left).astype(o_ref.dtype)


def _dilated_attention(q, k, v, bias_tiles):
    b, s, dq = q.shape
    hp = dq // LANES
    for window, dil in DIL_PATTERNS:
        assert s % window == 0, "sequence must be a whole number of dilated units"
    qkv = pl.BlockSpec((1, s, LANES), lambda h, bi: (bi, 0, h))
    n_var, span, band = bias_tiles.shape[1:]
    return pl.pallas_call(
        functools.partial(_dilated_kernel, s_len=s),
        grid=(hp, b),
        in_specs=[qkv, qkv, qkv,
                  pl.BlockSpec((2, n_var, span, band), lambda h, bi: (h, 0, 0, 0))],
        out_specs=pl.BlockSpec((1, s, LANES), lambda h, bi: (bi, 0, h)),
        out_shape=jax.ShapeDtypeStruct((b, s, dq), BF16),
        scratch_shapes=[pltpu.VMEM((2, s, LANES), F32)] * 2,
        compiler_params=_cparams(("parallel", "parallel")),
        name="dilated_attention",
    )(q, k, v, bias_tiles)


def _outproj_kernel(*refs, n_parts):
    o_parts = refs[:n_parts]
    w_parts = refs[n_parts:2 * n_parts]
    h_ref, gain_ref, gate_ref, out_ref = refs[2 * n_parts:]
    y = _dot(o_parts[0][0], w_parts[0][...])
    for o_ref, w_ref in zip(o_parts[1:], w_parts[1:]):
        y = y + _dot(o_ref[0], w_ref[...])
    out_ref[0] = h_ref[0] + gate_ref[0] * (_rms(y) * gain_ref[...])


def _outproj(o_parts, w_out, h, gain, gate):
    b, s, d = h.shape
    tm = TM_PROJ
    n = len(o_parts)
    w_out = w_out.astype(BF16)
    cuts = np.cumsum([p.shape[-1] for p in o_parts])[:-1].tolist()
    w_parts = jnp.split(w_out, cuts, axis=0) if cuts else [w_out]
    in_specs = [pl.BlockSpec((1, tm, p.shape[-1]), lambda bi, si: (bi, si, 0)) for p in o_parts]
    in_specs += [pl.BlockSpec(w.shape, lambda bi, si: (0, 0)) for w in w_parts]
    in_specs += [pl.BlockSpec((1, tm, d), lambda bi, si: (bi, si, 0)),
                 pl.BlockSpec((1, d), lambda bi, si: (0, 0)),
                 pl.BlockSpec((1, 1, d), lambda bi, si: (bi, 0, 0))]
    return pl.pallas_call(
        functools.partial(_outproj_kernel, n_parts=n),
        grid=(b, s // tm),
        in_specs=in_specs,
        out_specs=pl.BlockSpec((1, tm, d), lambda bi, si: (bi, si, 0)),
        out_shape=jax.ShapeDtypeStruct((b, s, d), F32),
        compiler_params=_cparams(("parallel", "parallel")),
        name="outproj",
    )(*o_parts, *w_parts, h, gain.reshape(1, d), gate.reshape(b, 1, d))


def _ffn_kernel(h_ref, g_ref, sc_ref, sh_ref, wg_ref, wu_ref, wd_ref, gain_ref, gate_ref, o_ref):
    h = h_ref[0]
    u = _modulate(h, g_ref[...], sc_ref[0], sh_ref[0]).astype(BF16)
    hid = (_silu(_dot(u, wg_ref[...])) * _dot(u, wu_ref[...])).astype(BF16)
    y = _dot(hid, wd_ref[...])
    o_ref[0] = h + gate_ref[0] * (_rms(y) * gain_ref[...])


def _ffn(h, gain_in, scale, shift, w_gate, w_up, w_down, gain_out, gate):
    b, s, d = h.shape
    tm = TM_FFN
    ff = w_gate.shape[1]
    resident = functools.partial(pl.BlockSpec, pipeline_mode=pl.Buffered(1))
    vec = pl.BlockSpec((1, 1, d), lambda bi, si: (bi, 0, 0))
    row = pl.BlockSpec((1, d), lambda bi, si: (0, 0))
    return pl.pallas_call(
        _ffn_kernel,
        grid=(b, s // tm),
        in_specs=[pl.BlockSpec((1, tm, d), lambda bi, si: (bi, si, 0)),
                  row, vec, vec,
                  resident((d, ff), lambda bi, si: (0, 0)),
                  resident((d, ff), lambda bi, si: (0, 0)),
                  resident((ff, d), lambda bi, si: (0, 0)),
                  row, vec],
        out_specs=pl.BlockSpec((1, tm, d), lambda bi, si: (bi, si, 0)),
        out_shape=jax.ShapeDtypeStruct((b, s, d), F32),
        compiler_params=_cparams(("parallel", "parallel")),
        name="dense_swiglu",
    )(h, gain_in.reshape(1, d), scale.reshape(b, 1, d), shift.reshape(b, 1, d),
      w_gate.astype(BF16), w_up.astype(BF16), w_down.astype(BF16),
      gain_out.reshape(1, d), gate.reshape(b, 1, d))


def _router_kernel(h_ref, g_ref, sc_ref, sh_ref, rw_ref, mi_ref, mf_ref, cnt_ref, carry_ref, *, tm):
    @pl.when((pl.program_id(0) == 0) & (pl.program_id(1) == 0))
    def _():
        carry_ref[...] = jnp.zeros_like(carry_ref)

    u = _modulate(h_ref[0], g_ref[...], sc_ref[0], sh_ref[0])
    logits = _dot_split(u, rw_ref[...])
    lanef = lax.broadcasted_iota(I32, (tm, LANES), 1).astype(F32)
    lg = jnp.where(lanef < N_EXPERTS, logits, NEG_INF)
    v1 = jnp.max(lg, axis=1, keepdims=True)
    i1 = jnp.min(jnp.where(lg == v1, lanef, float(LANES)), axis=1, keepdims=True)
    lg2 = jnp.where(lanef == i1, NEG_INF, lg)
    v2 = jnp.max(lg2, axis=1, keepdims=True)
    i2 = jnp.min(jnp.where(lg2 == v2, lanef, float(LANES)), axis=1, keepdims=True)
    e2 = jnp.exp(v2 - v1)
    p1 = 1.0 / (1.0 + e2)
    p2 = e2 / (1.0 + e2)
    oh1 = jnp.where(lanef == i1, 1.0, 0.0)
    oh2 = jnp.where(lanef == i2, 1.0, 0.0)
    oh = oh1 + oh2
    r = lax.broadcasted_iota(I32, (tm, tm), 0)
    c = lax.broadcasted_iota(I32, (tm, tm), 1)
    before = jnp.where(c < r, 1.0, 0.0).astype(BF16)
    tot = _dot(before, oh.astype(BF16)) + carry_ref[0:1, :]
    rank1 = jnp.sum(oh1 * tot, axis=1, keepdims=True)
    rank2 = jnp.sum(oh2 * tot, axis=1, keepdims=True)
    carry_ref[...] = carry_ref[...] + jnp.sum(oh, axis=0, keepdims=True)
    mi = jnp.where(lanef == 0.0, i1, jnp.where(lanef == 1.0, i2,
         jnp.where(lanef == 2.0, rank1, jnp.where(lanef == 3.0, rank2, 0.0))))
    mi_ref[...] = mi.astype(I32)
    mf_ref[...] = jnp.where(lanef == 0.0, p1, jnp.where(lanef == 1.0, p2, 0.0))
    cnt_ref[...] = carry_ref[...]


def _router(h, gain, scale, shift, router_w):
    b, s, d = h.shape
    tm = TM_ROUTE
    n = b * s
    ns = s // tm
    rw = jnp.pad(router_w.astype(F32), ((0, 0), (0, LANES - router_w.shape[1])))
    vec = pl.BlockSpec((1, 1, d), lambda bi, si: (bi, 0, 0))
    meta = pl.BlockSpec((tm, LANES), lambda bi, si: (bi * ns + si, 0))
    return pl.pallas_call(
        functools.partial(_router_kernel, tm=tm),
        grid=(b, ns),
        in_specs=[pl.BlockSpec((1, tm, d), lambda bi, si: (bi, si, 0)),
                  pl.BlockSpec((1, d), lambda bi, si: (0, 0)),
                  vec, vec,
                  pl.BlockSpec(rw.shape, lambda bi, si: (0, 0))],
        out_specs=[meta, meta, pl.BlockSpec((8, LANES), lambda bi, si: (0, 0))],
        out_shape=[jax.ShapeDtypeStruct((n, LANES), I32),
                   jax.ShapeDtypeStruct((n, LANES), F32),
                   jax.ShapeDtypeStruct((8, LANES), F32)],
        scratch_shapes=[pltpu.VMEM((8, LANES), F32)],
        compiler_params=_cparams(("arbitrary", "arbitrary")),
        name="moe_router",
    )(h, gain.reshape(1, d), scale.reshape(b, 1, d), shift.reshape(b, 1, d), rw)


def _scatter_kernel(dest_ref, ztile_ref, h_ref, g_ref, sc_ref, sh_ref, xs_ref, ubuf, zbuf, sems, zsem,
                    *, tm, ns):
    step = pl.program_id(0) * ns + pl.program_id(1)
    nsteps = pl.num_programs(0) * ns
    slot = lax.rem(step, 2)
    tz = zbuf.shape[0]

    @pl.when(step == 0)
    def _():
        zbuf[...] = jnp.zeros_like(zbuf)
        for e in range(2 * N_EXPERTS):
            @pl.when(ztile_ref[e] >= 0)
            def _():
                row0 = pl.multiple_of(ztile_ref[e] * tz, tz)
                pltpu.make_async_copy(zbuf, xs_ref.at[pl.ds(row0, tz), :], zsem).start()
        for e in range(2 * N_EXPERTS):
            @pl.when(ztile_ref[e] >= 0)
            def _():
                pltpu.make_async_copy(zbuf, xs_ref.at[pl.ds(0, tz), :], zsem).wait()

    def wait_slot(sl):
        for _ in range(2):
            pltpu.make_async_copy(ubuf.at[sl], ubuf.at[sl], sems.at[sl]).wait()

    base = step * tm

    def run(sl):
        @pl.when(step >= 2)
        def _():
            wait_slot(sl)

        ubuf[sl] = _modulate(h_ref[0], g_ref[...], sc_ref[0], sh_ref[0])

        def issue(i, carry):
            t = 2 * (base + i)
            src = ubuf.at[sl, pl.ds(i, 1), :]
            pltpu.make_async_copy(src, xs_ref.at[pl.ds(dest_ref[t], 1), :], sems.at[sl]).start()
            pltpu.make_async_copy(src, xs_ref.at[pl.ds(dest_ref[t + 1], 1), :], sems.at[sl]).start()
            return carry

        lax.fori_loop(0, tm, issue, 0, unroll=DMA_UNROLL)

        @pl.when(step == nsteps - 1)
        def _():
            wait_slot(sl)

            @pl.when(nsteps >= 2)
            def _():
                wait_slot(1 - sl)

    for sl in range(2):
        pl.when(slot == sl)(functools.partial(run, sl))


def _scatter(dest, zero_tile, h, gain, scale, shift, m_pad):
    b, s, d = h.shape
    tm = TM_SCATTER
    ns = s // tm
    vec = pl.BlockSpec((1, 1, d), lambda bi, si, dest, zt: (bi, 0, 0))
    grid_spec = pltpu.PrefetchScalarGridSpec(
        num_scalar_prefetch=2,
        grid=(b, ns),
        in_specs=[pl.BlockSpec((1, tm, d), lambda bi, si, dest, zt: (bi, si, 0)),
                  pl.BlockSpec((1, d), lambda bi, si, dest, zt: (0, 0)),
                  vec, vec],
        out_specs=pl.BlockSpec(memory_space=pl.ANY),
        scratch_shapes=[pltpu.VMEM((2, tm, d), F32), pltpu.VMEM((TM_EXPERT, d), F32),
                        pltpu.SemaphoreType.DMA((2,)), pltpu.SemaphoreType.DMA],
    )
    return pl.pallas_call(
        functools.partial(_scatter_kernel, tm=tm, ns=ns),
        grid_spec=grid_spec,
        out_shape=jax.ShapeDtypeStruct((m_pad, d), F32),
        compiler_params=_cparams(("arbitrary", "arbitrary")),
        name="moe_scatter",
    )(dest, zero_tile, h, gain.reshape(1, d), scale.reshape(b, 1, d), shift.reshape(b, 1, d))


def _expert_kernel(te_ref, tv_ref, tx_ref, x_ref, wg_ref, wu_ref, wd_ref, o_ref):
    t = pl.program_id(0)
    f = pl.program_id(1)

    @pl.when(tv_ref[t] == 1)
    def _():
        x = x_ref[...].astype(BF16)
        hid = (_silu(_dot(x, wg_ref[0])) * _dot(x, wu_ref[0])).astype(BF16)
        y = _dot(hid, wd_ref[0])

        @pl.when(f == 0)
        def _():
            o_ref[...] = y

        @pl.when(f > 0)
        def _():
            o_ref[...] = o_ref[...] + y

    @pl.when((tv_ref[t] == 0) & (f == 0))
    def _():
        o_ref[...] = jnp.zeros_like(o_ref)


def _experts(tile_expert, tile_valid, tile_x, xs, w_gate, w_up, w_down):
    m_pad, d = xs.shape
    tm = TM_EXPERT
    n_tiles = tile_expert.shape[0]
    ff = w_gate.shape[2]
    fs = FF_STEPS_EXPERT
    tf = ff // fs

    def ff_idx(f, tv, t):
        return f * tv[t] + (fs - 1) * (1 - tv[t])

    grid_spec = pltpu.PrefetchScalarGridSpec(
        num_scalar_prefetch=3,
        grid=(n_tiles, fs),
        in_specs=[pl.BlockSpec((tm, d), lambda t, f, te, tv, tx: (tx[t], 0)),
                  pl.BlockSpec((1, d, tf), lambda t, f, te, tv, tx: (te[t], 0, ff_idx(f, tv, t))),
                  pl.BlockSpec((1, d, tf), lambda t, f, te, tv, tx: (te[t], 0, ff_idx(f, tv, t))),
                  pl.BlockSpec((1, tf, d), lambda t, f, te, tv, tx: (te[t], ff_idx(f, tv, t), 0))],
        out_specs=pl.BlockSpec((tm, d), lambda t, f, te, tv, tx: (t, 0)),
    )
    return pl.pallas_call(
        _expert_kernel,
        grid_spec=grid_spec,
        out_shape=jax.ShapeDtypeStruct((m_pad, d), F32),
        compiler_params=_cparams(("arbitrary", "arbitrary")),
        name="moe_experts",
    )(tile_expert, tile_valid, tile_x, xs,
      w_gate.astype(BF16), w_up.astype(BF16), w_down.astype(BF16))


def _combine_kernel(dest_ref, y_ref, mf_ref, h_ref, gain_ref, gate_ref, o_ref, ybuf, sems, *, tm, ns):
    step = pl.program_id(0) * ns + pl.program_id(1)
    nsteps = pl.num_programs(0) * ns
    slot = lax.rem(step, 2)

    def issue(st, sl):
        base = st * tm

        def body(i, carry):
            t = 2 * (base + i)
            pltpu.make_async_copy(y_ref.at[pl.ds(dest_ref[t], 1), :],
                                  ybuf.at[sl, 0, pl.ds(i, 1), :], sems.at[sl]).start()
            pltpu.make_async_copy(y_ref.at[pl.ds(dest_ref[t + 1], 1), :],
                                  ybuf.at[sl, 1, pl.ds(i, 1), :], sems.at[sl]).start()
            return carry

        lax.fori_loop(0, tm, body, 0, unroll=DMA_UNROLL)

    @pl.when(step == 0)
    def _():
        issue(0, 0)

    def run(sl):
        @pl.when(step + 1 < nsteps)
        def _():
            issue(step + 1, 1 - sl)

        for k in range(2):
            pltpu.make_async_copy(ybuf.at[sl, k], ybuf.at[sl, k], sems.at[sl]).wait()
        mf = mf_ref[...]
        y = mf[:, 0:1] * ybuf[sl, 0] + mf[:, 1:2] * ybuf[sl, 1]
        o_ref[0] = h_ref[0] + gate_ref[0] * (_rms(y) * gain_ref[...])

    for sl in range(2):
        pl.when(slot == sl)(functools.partial(run, sl))


def _combine(dest, ys, mf, h, gain, gate):
    b, s, d = h.shape
    tm = TM_COMBINE
    ns = s // tm
    grid_spec = pltpu.PrefetchScalarGridSpec(
        num_scalar_prefetch=1,
        grid=(b, ns),
        in_specs=[pl.BlockSpec(memory_space=pl.ANY),
                  pl.BlockSpec((tm, LANES), lambda bi, si, dest: (bi * ns + si, 0)),
                  pl.BlockSpec((1, tm, d), lambda bi, si, dest: (bi, si, 0)),
                  pl.BlockSpec((1, d), lambda bi, si, dest: (0, 0)),
                  pl.BlockSpec((1, 1, d), lambda bi, si, dest: (bi, 0, 0))],
        out_specs=pl.BlockSpec((1, tm, d), lambda bi, si, dest: (bi, si, 0)),
        scratch_shapes=[pltpu.VMEM((2, 2, tm, d), F32), pltpu.SemaphoreType.DMA((2,))],
    )
    return pl.pallas_call(
        functools.partial(_combine_kernel, tm=tm, ns=ns),
        grid_spec=grid_spec,
        out_shape=jax.ShapeDtypeStruct((b, s, d), F32),
        compiler_params=_cparams(("arbitrary", "arbitrary")),
        name="moe_combine",
    )(dest, ys, mf, h, gain.reshape(1, d), gate.reshape(b, 1, d))


def _moe(h, gain_in, scale, shift, router_w, w_gate, w_up, w_down, gain_out, gate):
    b, s, d = h.shape
    n = b * s
    tm = TM_EXPERT
    mi, mf, cnt = _router(h, gain_in, scale, shift, router_w)
    counts = cnt[0, :N_EXPERTS].astype(I32)
    tiles_per = (counts + tm - 1) // tm
    seg_start = (jnp.cumsum(tiles_per) - tiles_per) * tm
    dest = (seg_start[mi[:, 0:2]] + mi[:, 2:4]).reshape(2 * n)
    n_tiles = (2 * n) // tm + N_EXPERTS
    m_pad = n_tiles * tm
    tile_end = jnp.cumsum(tiles_per)
    tidx = jnp.arange(n_tiles, dtype=I32)
    tile_valid = (tidx < tile_end[-1]).astype(I32)
    tile_expert = jnp.minimum(jnp.searchsorted(tile_end, tidx, side="right"), N_EXPERTS - 1).astype(I32)
    tile_x = jnp.minimum(tidx, tile_end[-1] - 1)
    tail = tile_end[-1] + jnp.arange(N_EXPERTS, dtype=I32)
    zero_tile = jnp.concatenate([jnp.where(tiles_per > 0, tile_end - 1, -1),
                                 jnp.where(tail < n_tiles, tail, -1)]).astype(I32)
    xs = _scatter(dest, zero_tile, h, gain_in, scale, shift, m_pad)
    ys = _experts(tile_expert, tile_valid, tile_x, xs, w_gate, w_up, w_down)
    return _combine(dest, ys, mf, h, gain_out, gate)


def kernel(x, c, mod_w, mod_b, norm_g, attn_in_w_even, fox_gate_bias, attn_out_w_even,
           attn_in_w_odd, attn_out_w_odd, rel_bias_table, ffn_w_gate, ffn_w_up, ffn_w_down,
           router_w, exp_w_gate, exp_w_up, exp_w_down):
    depth = mod_w.shape[0]
    s_len = x.shape[1]
    mods = _mods(c, mod_w, mod_b)
    dil_bias, moba_bias = _bias_tiles(rel_bias_table, s_len)
    h = x
    for layer in range(depth):
        j = layer // 2
        sh1, sc1, g1, sh2, sc2, g2 = jnp.split(mods[layer], 6, axis=-1)
        gains = norm_g[layer]
        if layer % 2 == 0:
            fox_in, moba_in = _inproj_even(h, gains[0], sc1, sh1, attn_in_w_even[j], fox_gate_bias[j])
            o_parts = [_fox_attention(*fox_in), _moba_attention(*moba_in, moba_bias)]
            h = _outproj(o_parts, attn_out_w_even[j], h, gains[1], g1)
            h = _ffn(h, gains[2], sc2, sh2, ffn_w_gate[j], ffn_w_up[j], ffn_w_down[j], gains[3], g2)
        else:
            q, k, v = _inproj_odd(h, gains[0], sc1, sh1, attn_in_w_odd[j])
            o = _dilated_attention(q, k, v, dil_bias)
            h = _outproj([o], attn_out_w_odd[j], h, gains[1], g1)
            h = _moe(h, gains[2], sc2, sh2, router_w[j], exp_w_gate[j], exp_w_up[j], exp_w_down[j],
                     gains[3], g2)
    return h
```

```python
import functools
import math

import numpy as np
import jax
import jax.numpy as jnp
from jax import lax
from jax.experimental import pallas as pl
from jax.experimental.pallas import tpu as pltpu

F32 = jnp.float32
BF16 = jnp.bfloat16
I32 = jnp.int32

HEAD_DIM = 64
LANES = 128
N_HEADS = 16
N_HEADS_FOX = 8
ATTN_SCALE = HEAD_DIM ** -0.5
LOG2E = math.log2(math.e)
Q_SCALE = ATTN_SCALE * LOG2E
N_DECAY_PIECES = 3
MOBA_BLOCK = 256
MOBA_TOPK = 3
DIL_PATTERNS = ((128, 1), (512, 4), (2048, 16))
NUM_BUCKETS = 32
MAX_DISTANCE = 2048
N_EXPERTS = 8
NORM_EPS = 1e-6
NEG_INF = float("-inf")
MASK_BIG = 1e30

VMEM_LIMIT = 56 * 1024 * 1024

TM_PROJ = 512
TM_FFN = 512
TQ_FOX = 512
TK_FOX = 512
FOX_PAIRS_PER_STEP = 2
MOBA_PAIRS_PER_STEP = 2
TM_ROUTE = 512
TM_SCATTER = 256
TM_EXPERT = 512
TM_COMBINE = 256
FF_STEPS_EXPERT = 2
DMA_UNROLL = True
DIL_CHUNKS_PER_STEP = 16


def _cparams(sem):
    return pltpu.CompilerParams(dimension_semantics=sem, vmem_limit_bytes=VMEM_LIMIT)


def _t5_bucket_np(n):
    n = np.maximum(n, 0)
    max_exact = NUM_BUCKETS // 2
    nf = np.maximum(n, 1).astype(np.float64)
    large = max_exact + (np.log(nf / max_exact) / math.log(MAX_DISTANCE / max_exact)
                         * (NUM_BUCKETS - max_exact)).astype(np.int64)
    large = np.minimum(large, NUM_BUCKETS - 1)
    return np.where(n < max_exact, n, large)


_MAX_DIST = 1 << 16
_BUCKET_OF = _t5_bucket_np(np.arange(_MAX_DIST))
_BUCKET_THR = [int(np.searchsorted(_BUCKET_OF, k, side="left")) for k in range(NUM_BUCKETS)]


def _bias_from_dist(tab_ref, h, dist, dlo, dhi):
    lo_b = int(_BUCKET_OF[max(dlo, 0)])
    hi_b = int(_BUCKET_OF[dhi])
    val = jnp.zeros(dist.shape, F32) + tab_ref[lo_b, h]
    for k in range(lo_b + 1, hi_b + 1):
        val = jnp.where(dist >= _BUCKET_THR[k], tab_ref[k, h], val)
    return val


def _dil_bias_kernel(tab_ref, o_ref):
    h = pl.program_id(0)
    for g, (window, dil) in enumerate(DIL_PATTERNS):
        span = window // dil
        i = lax.broadcasted_iota(I32, (span, 2 * span), 0)
        j = lax.broadcasted_iota(I32, (span, 2 * span), 1)
        rel = i + span - j
        val = _bias_from_dist(tab_ref, h, rel * dil, 0, span * dil) * LOG2E
        band = jnp.where(rel >= 0, jnp.where(rel <= span, val, NEG_INF), NEG_INF)
        o_ref[0, 2 * g] = band
        o_ref[0, 2 * g + 1] = jnp.where(j >= span, band, NEG_INF)


def _moba_bias_kernel(tab_ref, o_ref, *, n_blk, head0):
    h = pl.program_id(0) + head0
    i = lax.broadcasted_iota(I32, (MOBA_BLOCK, MOBA_BLOCK), 0)
    j = lax.broadcasted_iota(I32, (MOBA_BLOCK, MOBA_BLOCK), 1)
    for d in range(n_blk):
        dist = d * MOBA_BLOCK + i - j
        val = _bias_from_dist(tab_ref, h, dist, d * MOBA_BLOCK - (MOBA_BLOCK - 1),
                              d * MOBA_BLOCK + (MOBA_BLOCK - 1)) * LOG2E
        if d == 0:
            val = jnp.where(dist >= 0, val, NEG_INF)
        o_ref[0, d] = val


def _bias_tiles(rel_bias_table, s_len):
    n_blk = s_len // MOBA_BLOCK
    span = DIL_PATTERNS[0][0]
    n_var = 2 * len(DIL_PATTERNS)
    smem = pl.BlockSpec(memory_space=pltpu.SMEM)
    dil = pl.pallas_call(
        _dil_bias_kernel,
        grid=(N_HEADS,),
        in_specs=[smem],
        out_specs=pl.BlockSpec((1, n_var, span, 2 * span), lambda h: (h, 0, 0, 0)),
        out_shape=jax.ShapeDtypeStruct((N_HEADS, n_var, span, 2 * span), F32),
        compiler_params=_cparams(("parallel",)),
        name="dil_bias",
    )(rel_bias_table)
    n_moba = N_HEADS - N_HEADS_FOX
    moba = pl.pallas_call(
        functools.partial(_moba_bias_kernel, n_blk=n_blk, head0=N_HEADS_FOX),
        grid=(n_moba,),
        in_specs=[smem],
        out_specs=pl.BlockSpec((1, n_blk, MOBA_BLOCK, MOBA_BLOCK), lambda h: (h, 0, 0, 0)),
        out_shape=jax.ShapeDtypeStruct((n_moba, n_blk, MOBA_BLOCK, MOBA_BLOCK), F32),
        compiler_params=_cparams(("parallel",)),
        name="moba_bias",
    )(rel_bias_table)
    return dil, moba


def _split_bf16(a):
    hi = a.astype(BF16)
    lo = (a - hi.astype(F32)).astype(BF16)
    return hi, lo


def _dot(a, b):
    return jnp.dot(a, b, preferred_element_type=F32)


def _dot_nt(a, b):
    return lax.dot_general(a, b, (((1,), (1,)), ((), ())), preferred_element_type=F32)


def _dot_split(a, b):
    a_hi, a_lo = _split_bf16(a)
    b_hi, b_lo = _split_bf16(b)
    return _dot(a_hi, b_hi) + (_dot(a_hi, b_lo) + _dot(a_lo, b_hi))


def _rms(x):
    return x * lax.rsqrt(jnp.mean(x * x, axis=-1, keepdims=True) + NORM_EPS)


def _modulate(x, gain, scale, shift):
    return (_rms(x) * gain) * (1.0 + scale) + shift


def _silu(x):
    return x * jax.nn.sigmoid(x)


def _mods_kernel(c_ref, w_ref, b_ref, o_ref):
    o_ref[0] = _dot_split(_silu(c_ref[...]), w_ref[0]) + b_ref[0]


def _mods(c, mod_w, mod_b):
    depth, d, e = mod_w.shape
    b = c.shape[0]
    tn = 1536
    return pl.pallas_call(
        _mods_kernel,
        grid=(depth, e // tn),
        in_specs=[pl.BlockSpec((b, d), lambda l, j: (0, 0)),
                  pl.BlockSpec((1, d, tn), lambda l, j: (l, 0, j)),
                  pl.BlockSpec((1, 1, tn), lambda l, j: (l, 0, j))],
        out_specs=pl.BlockSpec((1, b, tn), lambda l, j: (l, 0, j)),
        out_shape=jax.ShapeDtypeStruct((depth, b, e), F32),
        compiler_params=_cparams(("parallel", "parallel")),
        name="adaln_mods",
    )(c, mod_w, mod_b.reshape(depth, 1, e))


def _inproj_even_kernel(h_ref, g_ref, sc_ref, sh_ref, w_ref, wf_ref, gb_ref,
                        qa_ref, k0a_ref, k1a_ref, v0a_ref, v1a_ref,
                        qb_ref, k0b_ref, k1b_ref, v0b_ref, v1b_ref, carry_ref, *, tm):
    si = pl.program_id(1)
    u = _modulate(h_ref[0], g_ref[...], sc_ref[0], sh_ref[0]).astype(BF16)
    width = qa_ref.shape[-1]
    n_pairs = width // LANES

    def proj(i):
        return _dot(u, w_ref[:, i * width:(i + 1) * width])

    lane = lax.broadcasted_iota(I32, (1, LANES), 1)
    left = lane < HEAD_DIM
    row = lax.broadcasted_iota(I32, (tm, LANES), 0)

    x = _dot(u, wf_ref[...]) + gb_ref[...]
    lf = jnp.where(lane < N_HEADS_FOX, jnp.minimum(x, 0.0) - jnp.log1p(jnp.exp(-jnp.abs(x))), 0.0)
    k = 1
    while k < tm:
        lf = lf + jnp.where(row >= k, pltpu.roll(lf, k, axis=0), 0.0)
        k *= 2

    @pl.when(si == 0)
    def _():
        carry_ref[...] = jnp.zeros_like(carry_ref)

    cum = lf + carry_ref[0:1, :]
    carry_ref[...] = jnp.broadcast_to(cum[tm - 1:tm, :], carry_ref.shape)
    rest = cum * (-LOG2E)
    decay = jnp.zeros((tm, LANES), F32)
    for p in range(N_DECAY_PIECES):
        piece = rest.astype(BF16).astype(F32)
        rest = rest - piece
        decay = decay + (pltpu.roll(piece, p * N_HEADS_FOX, axis=1) if p else piece)
    decay_lo = decay.astype(BF16)
    decay_hi = pltpu.roll(decay, HEAD_DIM, axis=1).astype(BF16)

    one_lo = jnp.where(lane == 0, 1.0, 0.0).astype(BF16)
    one_hi = jnp.where(lane == HEAD_DIM, 1.0, 0.0).astype(BF16)
    blk = (si * tm + row) // MOBA_BLOCK
    blk_lo = jnp.where(lane == blk, 1.0, 0.0).astype(BF16)
    blk_hi = jnp.where(lane == blk + HEAD_DIM, 1.0, 0.0).astype(BF16)

    def emit(first, k0_ref, k1_ref, v0_ref, v1_ref, k_lo, k_hi):
        kk = proj(first + 1).astype(BF16)
        vv = proj(first + 2).astype(BF16)
        for hp in range(n_pairs):
            sl = slice(hp * LANES, (hp + 1) * LANES)
            k0_ref[0, :, sl] = jnp.where(left, kk[:, sl], k_hi)
            k1_ref[0, :, sl] = jnp.where(left, k_lo, kk[:, sl])
            v0_ref[0, :, sl] = jnp.where(left, vv[:, sl], one_hi)
            v1_ref[0, :, sl] = jnp.where(left, one_lo, vv[:, sl])

    qa_ref[0] = (proj(0) * Q_SCALE).astype(BF16)
    emit(0, k0a_ref, k1a_ref, v0a_ref, v1a_ref, decay_lo, decay_hi)
    qb_ref[0] = (proj(3) * Q_SCALE).astype(BF16)
    emit(3, k0b_ref, k1b_ref, v0b_ref, v1b_ref, blk_lo, blk_hi)


def _inproj_even(h, gain, scale, shift, w_in, gate_bias):
    b, s, d = h.shape
    tm = TM_PROJ
    da = N_HEADS_FOX * HEAD_DIM
    assert s // MOBA_BLOCK <= HEAD_DIM and N_DECAY_PIECES * N_HEADS_FOX <= HEAD_DIM
    cuts = np.cumsum([da, da, da, N_HEADS_FOX, da, da]).tolist()
    q_a, k_a, v_a, f_a, q_b, k_b, v_b = jnp.split(w_in, cuts, axis=1)
    w = jnp.concatenate([q_a, k_a, v_a, q_b, k_b, v_b], axis=1).astype(BF16)
    wf = jnp.pad(f_a, ((0, 0), (0, LANES - N_HEADS_FOX))).astype(BF16)
    gb = jnp.pad(gate_bias.astype(F32), (0, LANES - N_HEADS_FOX)).reshape(1, LANES)
    act = jax.ShapeDtypeStruct((b, s, da), BF16)
    act_spec = pl.BlockSpec((1, tm, da), lambda bi, si: (bi, si, 0))
    vec = pl.BlockSpec((1, 1, d), lambda bi, si: (bi, 0, 0))
    outs = pl.pallas_call(
        functools.partial(_inproj_even_kernel, tm=tm),
        grid=(b, s // tm),
        in_specs=[pl.BlockSpec((1, tm, d), lambda bi, si: (bi, si, 0)),
                  pl.BlockSpec((1, d), lambda bi, si: (0, 0)),
                  vec, vec,
                  pl.BlockSpec(w.shape, lambda bi, si: (0, 0)),
                  pl.BlockSpec(wf.shape, lambda bi, si: (0, 0)),
                  pl.BlockSpec(gb.shape, lambda bi, si: (0, 0))],
        out_specs=[act_spec] * 10,
        out_shape=[act] * 10,
        scratch_shapes=[pltpu.VMEM((8, LANES), F32)],
        compiler_params=_cparams(("parallel", "arbitrary")),
        name="inproj_even",
    )(h, gain.reshape(1, d), scale.reshape(b, 1, d), shift.reshape(b, 1, d), w, wf, gb)
    return outs[:5], outs[5:]


def _inproj_odd_kernel(h_ref, g_ref, sc_ref, sh_ref, w_ref, q_ref, k_ref, v_ref):
    u = _modulate(h_ref[0], g_ref[...], sc_ref[0], sh_ref[0]).astype(BF16)
    width = q_ref.shape[-1]
    q_ref[0] = _dot(u, w_ref[:, 0:width]) * Q_SCALE
    k_ref[0] = _dot(u, w_ref[:, width:2 * width])
    v_ref[0] = _dot(u, w_ref[:, 2 * width:3 * width])


def _inproj_odd(h, gain, scale, shift, w_in):
    b, s, d = h.shape
    tm = TM_PROJ
    dq = w_in.shape[1] // 3
    act = jax.ShapeDtypeStruct((b, s, dq), F32)
    act_spec = pl.BlockSpec((1, tm, dq), lambda bi, si: (bi, si, 0))
    vec = pl.BlockSpec((1, 1, d), lambda bi, si: (bi, 0, 0))
    return pl.pallas_call(
        _inproj_odd_kernel,
        grid=(b, s // tm),
        in_specs=[pl.BlockSpec((1, tm, d), lambda bi, si: (bi, si, 0)),
                  pl.BlockSpec((1, d), lambda bi, si: (0, 0)),
                  vec, vec,
                  pl.BlockSpec(w_in.shape, lambda bi, si: (0, 0))],
        out_specs=[act_spec] * 3,
        out_shape=[act] * 3,
        compiler_params=_cparams(("parallel", "parallel")),
        name="inproj_odd",
    )(h, gain.reshape(1, d), scale.reshape(b, 1, d), shift.reshape(b, 1, d), w_in.astype(BF16))


def _tile_lanes(x, width):
    return jnp.concatenate([x] * (width // LANES), axis=1)


def _flash_update(s, v, m_ref, acc_ref):
    m_prev = m_ref[...]
    m_new = jnp.maximum(m_prev, jnp.max(s, axis=1, keepdims=True))
    p = jnp.exp2(s - _tile_lanes(m_new, s.shape[1]))
    acc_ref[...] = jnp.exp2(m_prev - m_new) * acc_ref[...] + _dot(p.astype(BF16), v)
    m_ref[...] = m_new


def _finish_pair(acc_ref, left, first=0):
    acc0 = acc_ref[first]
    acc1 = acc_ref[first + 1]
    return jnp.where(left, acc0 / acc0[:, HEAD_DIM:HEAD_DIM + 1], acc1 / acc1[:, 0:1])


def _fox_kernel(q_ref, k0_ref, k1_ref, v0_ref, v1_ref, o_ref, m_ref, acc_ref, *, tq, tk, n_pairs):
    qi = pl.program_id(2)
    lane = lax.broadcasted_iota(I32, (1, LANES), 1)
    left = lane < HEAD_DIM

    def piece_lanes(lane0):
        hit = lane == lane0
        for p in range(1, N_DECAY_PIECES):
            hit = jnp.logical_or(hit, lane == lane0 + p * N_HEADS_FOX)
        return jnp.where(hit, 1.0, 0.0).astype(BF16)

    chains = []
    for pr in range(n_pairs):
        hp = pl.program_id(1) * n_pairs + pr
        lanes = slice(pr * LANES, (pr + 1) * LANES)
        q = q_ref[0, :, lanes]
        chains.append((jnp.where(left, q, piece_lanes(HEAD_DIM + 2 * hp)), k0_ref, v0_ref, lanes))
        chains.append((jnp.where(left, piece_lanes(2 * hp + 1), q), k1_ref, v1_ref, lanes))
    row = lax.broadcasted_iota(I32, (tq, tk), 0)
    col = lax.broadcasted_iota(I32, (tq, tk), 1)
    m_ref[...] = jnp.full(m_ref.shape, NEG_INF, F32)
    acc_ref[...] = jnp.zeros(acc_ref.shape, F32)
    n_sub = tq // tk

    def step(kv, mask):
        off = pl.multiple_of(kv * tk, tk)
        for c, (qc, k_ref, v_ref, lanes) in enumerate(chains):
            s = _dot_nt(qc, k_ref[0, pl.ds(off, tk), lanes])
            if mask is not None:
                s = jnp.where(mask, s, NEG_INF)
            _flash_update(s, v_ref[0, pl.ds(off, tk), lanes], m_ref.at[c], acc_ref.at[c])

    def body(kv, carry):
        step(kv, None)
        return carry

    lax.fori_loop(0, qi * n_sub, body, 0)
    for d in range(n_sub):
        step(qi * n_sub + d, col + d * tk <= row)
    for pr in range(n_pairs):
        o_ref[0, :, pr * LANES:(pr + 1) * LANES] = _finish_pair(acc_ref, left, 2 * pr).astype(o_ref.dtype)


def _fox_attention(q, k0, k1, v0, v1):
    b, s, da = q.shape
    n_pairs = FOX_PAIRS_PER_STEP
    width = n_pairs * LANES
    tq = min(TQ_FOX, s)
    kv = pl.BlockSpec((1, s, width), lambda bi, h, qi: (bi, 0, h))
    return pl.pallas_call(
        functools.partial(_fox_kernel, tq=tq, tk=min(TK_FOX, tq), n_pairs=n_pairs),
        grid=(b, da // width, s // tq),
        in_specs=[pl.BlockSpec((1, tq, width), lambda bi, h, qi: (bi, qi, h)), kv, kv, kv, kv],
        out_specs=pl.BlockSpec((1, tq, width), lambda bi, h, qi: (bi, qi, h)),
        out_shape=jax.ShapeDtypeStruct((b, s, da), BF16),
        scratch_shapes=[pltpu.VMEM((2 * n_pairs, tq, LANES), F32)] * 2,
        compiler_params=_cparams(("parallel", "parallel", "arbitrary")),
        name="fox_attention",
    )(q, k0, k1, v0, v1)


def _moba_kernel(q_ref, k0_ref, k1_ref, v0_ref, v1_ref, bias_ref, o_ref, km_ref, m_ref, acc_ref,
                 *, n_blk, tq, n_pairs):
    blk = MOBA_BLOCK
    a = pl.program_id(2)
    gate_lane0 = (HEAD_DIM, 0)
    lane = lax.broadcasted_iota(I32, (1, LANES), 1)
    left = lane < HEAD_DIM
    mine = (left, jnp.logical_not(left))
    k_refs = (k0_ref, k1_ref)
    v_refs = (v0_ref, v1_ref)

    @pl.when(a == 0)
    def _():
        km_ref[...] = jnp.zeros_like(km_ref)
        for pr in range(n_pairs):
            lanes = slice(pr * LANES, (pr + 1) * LANES)
            for n in range(n_blk):
                rows = slice(n * blk, (n + 1) * blk)
                kb = jnp.where(left, k0_ref[0, rows, lanes], k1_ref[0, rows, lanes]).astype(F32)
                mean = jnp.sum(kb, axis=0, keepdims=True) * (1.0 / blk)
                for lane0 in gate_lane0:
                    km_ref[pr, lane0 + n:lane0 + n + 1, :] = mean

    nb = -(-n_blk // 8) * 8
    blkf = lax.broadcasted_iota(I32, (nb, tq), 0).astype(F32)
    own = (lax.broadcasted_iota(I32, (nb, tq), 1) // blk + a * (tq // blk)).astype(F32)
    chains = []
    for pr in range(n_pairs):
        lanes = slice(pr * LANES, (pr + 1) * LANES)
        q = q_ref[0, :, lanes]
        km_hi, km_lo = _split_bf16(km_ref[pr])
        for j in range(2):
            qj = jnp.where(mine[j], q, jnp.zeros_like(q))
            lane0 = gate_lane0[j]
            gate = (_dot_nt(km_hi, qj) + _dot_nt(km_lo, qj))[lane0:lane0 + nb]
            gate = jnp.where(blkf < own, gate, NEG_INF)
            pen = jnp.where(blkf == own, 0.0, -MASK_BIG)
            for _ in range(MOBA_TOPK):
                mx = jnp.max(gate, axis=0, keepdims=True)
                cand = jnp.where(gate == mx, jnp.where(mx > NEG_INF, blkf, float(LANES)), float(LANES))
                pick = blkf == jnp.min(cand, axis=0, keepdims=True)
                pen = jnp.where(pick, 0.0, pen)
                gate = jnp.where(pick, NEG_INF, gate)
            parts = [pen, jnp.zeros((LANES - lane0 - nb, tq), F32)]
            if lane0:
                parts.insert(0, jnp.zeros((lane0, tq), F32))
            pen_q = jnp.concatenate(parts, axis=0).T
            chains.append((jnp.where(mine[j], q, pen_q.astype(BF16)), k_refs[j], v_refs[j], lanes, 2 * pr + j))

    m_ref[...] = jnp.full(m_ref.shape, NEG_INF, F32)
    acc_ref[...] = jnp.zeros(acc_ref.shape, F32)

    def body(i, carry):
        off = pl.multiple_of(i * tq, tq)
        d0 = 2 * (a - i)
        for qc, k_ref, v_ref, lanes, h in chains:
            top = jnp.concatenate([bias_ref[h, d0], bias_ref[h, jnp.maximum(d0 - 1, 0)]], axis=1)
            bot = jnp.concatenate([bias_ref[h, d0 + 1], bias_ref[h, d0]], axis=1)
            s = _dot_nt(qc, k_ref[0, pl.ds(off, tq), lanes]) + jnp.concatenate([top, bot], axis=0)
            _flash_update(s, v_ref[0, pl.ds(off, tq), lanes], m_ref.at[h], acc_ref.at[h])
        return carry

    lax.fori_loop(0, a + 1, body, 0)
    for pr in range(n_pairs):
        o_ref[0, :, pr * LANES:(pr + 1) * LANES] = _finish_pair(acc_ref, left, 2 * pr).astype(o_ref.dtype)


def _moba_attention(q, k0, k1, v0, v1, bias_tiles):
    b, s, db = q.shape
    n_pairs = MOBA_PAIRS_PER_STEP
    width = n_pairs * LANES
    blk = MOBA_BLOCK
    n_blk = s // blk
    tq = 2 * blk
    assert n_blk <= HEAD_DIM, "block gates of one head must fit in the other head's lanes"
    assert s % tq == 0
    kv = pl.BlockSpec((1, s, width), lambda h, bi, qi: (bi, 0, h))
    return pl.pallas_call(
        functools.partial(_moba_kernel, n_blk=n_blk, tq=tq, n_pairs=n_pairs),
        grid=(db // width, b, s // tq),
        in_specs=[pl.BlockSpec((1, tq, width), lambda h, bi, qi: (bi, qi, h)), kv, kv, kv, kv,
                  pl.BlockSpec((2 * n_pairs, n_blk, blk, blk), lambda h, bi, qi: (h, 0, 0, 0),
                               pipeline_mode=pl.Buffered(1))],
        out_specs=pl.BlockSpec((1, tq, width), lambda h, bi, qi: (bi, qi, h)),
        out_shape=jax.ShapeDtypeStruct((b, s, db), BF16),
        scratch_shapes=[pltpu.VMEM((n_pairs, LANES, LANES), F32)]
                       + [pltpu.VMEM((2 * n_pairs, tq, LANES), F32)] * 2,
        compiler_params=_cparams(("parallel", "parallel", "arbitrary")),
        name="moba_attention",
    )(q, k0, k1, v0, v1, bias_tiles)


def _dilated_kernel(q_ref, k_ref, v_ref, bias_ref, o_ref, m_ref, acc_ref, *, s_len):
    lane = lax.broadcasted_iota(I32, (1, LANES), 1)
    left = lane < HEAD_DIM
    one_lo = jnp.where(lane == 0, 1.0, 0.0).astype(BF16)
    one_hi = jnp.where(lane == HEAD_DIM, 1.0, 0.0).astype(BF16)
    order = sorted(range(len(DIL_PATTERNS)), key=lambda i: -DIL_PATTERNS[i][1])
    for g in order:
        window, dil = DIL_PATTERNS[g]
        merge = g != order[0]
        span = window // dil
        unit = span * dil
        nc = s_len // unit
        n_u = min(DIL_CHUNKS_PER_STEP, nc)
        groups = nc // n_u
        n_res = min(DIL_CHUNKS_PER_STEP // n_u, dil)

        def rows(ref, start, dil=dil, span=span):
            if dil == 1:
                return ref[0, pl.ds(start, span), :]
            return ref[0, pl.ds(start, span, stride=dil), :]

        def get(ref, j, start, dil=dil, span=span):
            if dil == 1:
                return ref[j, pl.ds(start, span), :]
            return ref[j, pl.ds(start, span, stride=dil), :]

        def put(ref, j, start, val, dil=dil, span=span):
            if dil == 1:
                ref[j, pl.ds(start, span), :] = val
            else:
                ref[j, pl.ds(start, span, stride=dil), :] = val

        def body(it, carry, g=g, merge=merge, n_u=n_u, n_res=n_res, groups=groups, unit=unit, rows=rows,
                 put=put, get=get):
            r0 = (it // groups) * n_res
            grp = it - (it // groups) * groups
            is_first = grp == 0
            results = []
            for dr in range(n_res):
                start0 = r0 + dr + grp * (n_u * unit)
                prev0 = start0 - jnp.where(is_first, 0, unit)
                starts = [start0 + u * unit for u in range(n_u)]
                kc = [rows(k_ref, st).astype(BF16) for st in [prev0] + starts]
                vc = [rows(v_ref, st).astype(BF16) for st in [prev0] + starts]
                vcs = ([jnp.where(left, v, one_hi) for v in vc], [jnp.where(left, one_lo, v) for v in vc])
                for u, start in enumerate(starts):
                    var = 2 * g + jnp.where(is_first, 1, 0) if u == 0 else 2 * g
                    q = rows(q_ref, start)
                    kb = jnp.concatenate([kc[u], kc[u + 1]], axis=0)
                    qq = jnp.concatenate([jnp.where(left, q, 0.0), jnp.where(left, 0.0, q)],
                                         axis=0).astype(BF16)
                    s_both = _dot_nt(qq, kb)
                    for j in range(2):
                        s = s_both[j * span:(j + 1) * span] + bias_ref[j, var]
                        m_new = jnp.broadcast_to(jnp.max(s, axis=1, keepdims=True), (span, LANES))
                        if merge:
                            m_prev = get(m_ref, j, start)
                            m_new = jnp.maximum(m_new, m_prev)
                        p = jnp.exp2(s - _tile_lanes(m_new, 2 * span))
                        acc_new = _dot(p.astype(BF16), jnp.concatenate([vcs[j][u], vcs[j][u + 1]], axis=0))
                        if merge:
                            acc_new = jnp.exp2(m_prev - m_new) * get(acc_ref, j, start) + acc_new
                        results.append((j, start, m_new, acc_new))
            for j, start, m_new, acc_new in results:
                put(m_ref, j, start, m_new)
                put(acc_ref, j, start, acc_new)
            return carry

        lax.fori_loop(0, (dil // n_res) * groups, body, 0)
    o_ref[0] = _finish_pair(acc_ref, left).astype(o_ref.dtype)


def _dilated_attention(q, k, v, bias_tiles):
    b, s, dq = q.shape
    hp = dq // LANES
    for window, dil in DIL_PATTERNS:
        assert s % window == 0, "sequence must be a whole number of dilated units"
    qkv = pl.BlockSpec((1, s, LANES), lambda h, bi: (bi, 0, h))
    n_var, span, band = bias_tiles.shape[1:]
    return pl.pallas_call(
        functools.partial(_dilated_kernel, s_len=s),
        grid=(hp, b),
        in_specs=[qkv, qkv, qkv,
                  pl.BlockSpec((2, n_var, span, band), lambda h, bi: (h, 0, 0, 0))],
        out_specs=pl.BlockSpec((1, s, LANES), lambda h, bi: (bi, 0, h)),
        out_shape=jax.ShapeDtypeStruct((b, s, dq), BF16),
        scratch_shapes=[pltpu.VMEM((2, s, LANES), F32)] * 2,
        compiler_params=_cparams(("parallel", "parallel")),
        name="dilated_attention",
    )(q, k, v, bias_tiles)


def _outproj_kernel(*refs, n_parts):
    o_parts = refs[:n_parts]
    w_parts = refs[n_parts:2 * n_parts]
    h_ref, gain_ref, gate_ref, out_ref = refs[2 * n_parts:]
    y = _dot(o_parts[0][0], w_parts[0][...])
    for o_ref, w_ref in zip(o_parts[1:], w_parts[1:]):
        y = y + _dot(o_ref[0], w_ref[...])
    out_ref[0] = h_ref[0] + gate_ref[0] * (_rms(y) * gain_ref[...])


def _outproj(o_parts, w_out, h, gain, gate):
    b, s, d = h.shape
    tm = TM_PROJ
    n = len(o_parts)
    w_out = w_out.astype(BF16)
    cuts = np.cumsum([p.shape[-1] for p in o_parts])[:-1].tolist()
    w_parts = jnp.split(w_out, cuts, axis=0) if cuts else [w_out]
    in_specs = [pl.BlockSpec((1, tm, p.shape[-1]), lambda bi, si: (bi, si, 0)) for p in o_parts]
    in_specs += [pl.BlockSpec(w.shape, lambda bi, si: (0, 0)) for w in w_parts]
    in_specs += [pl.BlockSpec((1, tm, d), lambda bi, si: (bi, si, 0)),
                 pl.BlockSpec((1, d), lambda bi, si: (0, 0)),
                 pl.BlockSpec((1, 1, d), lambda bi, si: (bi, 0, 0))]
    return pl.pallas_call(
        functools.partial(_outproj_kernel, n_parts=n),
        grid=(b, s // tm),
        in_specs=in_specs,
        out_specs=pl.BlockSpec((1, tm, d), lambda bi, si: (bi, si, 0)),
        out_shape=jax.ShapeDtypeStruct((b, s, d), F32),
        compiler_params=_cparams(("parallel", "parallel")),
        name="outproj",
    )(*o_parts, *w_parts, h, gain.reshape(1, d), gate.reshape(b, 1, d))


def _ffn_kernel(h_ref, g_ref, sc_ref, sh_ref, wg_ref, wu_ref, wd_ref, gain_ref, gate_ref, o_ref):
    h = h_ref[0]
    u = _modulate(h, g_ref[...], sc_ref[0], sh_ref[0]).astype(BF16)
    hid = (_silu(_dot(u, wg_ref[...])) * _dot(u, wu_ref[...])).astype(BF16)
    y = _dot(hid, wd_ref[...])
    o_ref[0] = h + gate_ref[0] * (_rms(y) * gain_ref[...])


def _ffn(h, gain_in, scale, shift, w_gate, w_up, w_down, gain_out, gate):
    b, s, d = h.shape
    tm = TM_FFN
    ff = w_gate.shape[1]
    resident = functools.partial(pl.BlockSpec, pipeline_mode=pl.Buffered(1))
    vec = pl.BlockSpec((1, 1, d), lambda bi, si: (bi, 0, 0))
    row = pl.BlockSpec((1, d), lambda bi, si: (0, 0))
    return pl.pallas_call(
        _ffn_kernel,
        grid=(b, s // tm),
        in_specs=[pl.BlockSpec((1, tm, d), lambda bi, si: (bi, si, 0)),
                  row, vec, vec,
                  resident((d, ff), lambda bi, si: (0, 0)),
                  resident((d, ff), lambda bi, si: (0, 0)),
                  resident((ff, d), lambda bi, si: (0, 0)),
                  row, vec],
        out_specs=pl.BlockSpec((1, tm, d), lambda bi, si: (bi, si, 0)),
        out_shape=jax.ShapeDtypeStruct((b, s, d), F32),
        compiler_params=_cparams(("parallel", "parallel")),
        name="dense_swiglu",
    )(h, gain_in.reshape(1, d), scale.reshape(b, 1, d), shift.reshape(b, 1, d),
      w_gate.astype(BF16), w_up.astype(BF16), w_down.astype(BF16),
      gain_out.reshape(1, d), gate.reshape(b, 1, d))


def _router_kernel(h_ref, g_ref, sc_ref, sh_ref, rw_ref, mi_ref, mf_ref, cnt_ref, carry_ref, *, tm):
    @pl.when((pl.program_id(0) == 0) & (pl.program_id(1) == 0))
    def _():
        carry_ref[...] = jnp.zeros_like(carry_ref)

    u = _modulate(h_ref[0], g_ref[...], sc_ref[0], sh_ref[0])
    logits = _dot_split(u, rw_ref[...])
    lanef = lax.broadcasted_iota(I32, (tm, LANES), 1).astype(F32)
    lg = jnp.where(lanef < N_EXPERTS, logits, NEG_INF)
    v1 = jnp.max(lg, axis=1, keepdims=True)
    i1 = jnp.min(jnp.where(lg == v1, lanef, float(LANES)), axis=1, keepdims=True)
    lg2 = jnp.where(lanef == i1, NEG_INF, lg)
    v2 = jnp.max(lg2, axis=1, keepdims=True)
    i2 = jnp.min(jnp.where(lg2 == v2, lanef, float(LANES)), axis=1, keepdims=True)
    e2 = jnp.exp(v2 - v1)
    p1 = 1.0 / (1.0 + e2)
    p2 = e2 / (1.0 + e2)
    oh1 = jnp.where(lanef == i1, 1.0, 0.0)
    oh2 = jnp.where(lanef == i2, 1.0, 0.0)
    oh = oh1 + oh2
    r = lax.broadcasted_iota(I32, (tm, tm), 0)
    c = lax.broadcasted_iota(I32, (tm, tm), 1)
    before = jnp.where(c < r, 1.0, 0.0).astype(BF16)
    tot = _dot(before, oh.astype(BF16)) + carry_ref[0:1, :]
    rank1 = jnp.sum(oh1 * tot, axis=1, keepdims=True)
    rank2 = jnp.sum(oh2 * tot, axis=1, keepdims=True)
    carry_ref[...] = carry_ref[...] + jnp.sum(oh, axis=0, keepdims=True)
    mi = jnp.where(lanef == 0.0, i1, jnp.where(lanef == 1.0, i2,
         jnp.where(lanef == 2.0, rank1, jnp.where(lanef == 3.0, rank2, 0.0))))
    mi_ref[...] = mi.astype(I32)
    mf_ref[...] = jnp.where(lanef == 0.0, p1, jnp.where(lanef == 1.0, p2, 0.0))
    cnt_ref[...] = carry_ref[...]


def _router(h, gain, scale, shift, router_w):
    b, s, d = h.shape
    tm = TM_ROUTE
    n = b * s
    ns = s // tm
    rw = jnp.pad(router_w.astype(F32), ((0, 0), (0, LANES - router_w.shape[1])))
    vec = pl.BlockSpec((1, 1, d), lambda bi, si: (bi, 0, 0))
    meta = pl.BlockSpec((tm, LANES), lambda bi, si: (bi * ns + si, 0))
    return pl.pallas_call(
        functools.partial(_router_kernel, tm=tm),
        grid=(b, ns),
        in_specs=[pl.BlockSpec((1, tm, d), lambda bi, si: (bi, si, 0)),
                  pl.BlockSpec((1, d), lambda bi, si: (0, 0)),
                  vec, vec,
                  pl.BlockSpec(rw.shape, lambda bi, si: (0, 0))],
        out_specs=[meta, meta, pl.BlockSpec((8, LANES), lambda bi, si: (0, 0))],
        out_shape=[jax.ShapeDtypeStruct((n, LANES), I32),
                   jax.ShapeDtypeStruct((n, LANES), F32),
                   jax.ShapeDtypeStruct((8, LANES), F32)],
        scratch_shapes=[pltpu.VMEM((8, LANES), F32)],
        compiler_params=_cparams(("arbitrary", "arbitrary")),
        name="moe_router",
    )(h, gain.reshape(1, d), scale.reshape(b, 1, d), shift.reshape(b, 1, d), rw)


def _scatter_kernel(dest_ref, ztile_ref, h_ref, g_ref, sc_ref, sh_ref, xs_ref, ubuf, zbuf, sems, zsem,
                    *, tm, ns):
    step = pl.program_id(0) * ns + pl.program_id(1)
    nsteps = pl.num_programs(0) * ns
    slot = lax.rem(step, 2)
    tz = zbuf.shape[0]

    @pl.when(step == 0)
    def _():
        zbuf[...] = jnp.zeros_like(zbuf)
        for e in range(2 * N_EXPERTS):
            @pl.when(ztile_ref[e] >= 0)
            def _():
                row0 = pl.multiple_of(ztile_ref[e] * tz, tz)
                pltpu.make_async_copy(zbuf, xs_ref.at[pl.ds(row0, tz), :], zsem).start()
        for e in range(2 * N_EXPERTS):
            @pl.when(ztile_ref[e] >= 0)
            def _():
                pltpu.make_async_copy(zbuf, xs_ref.at[pl.ds(0, tz), :], zsem).wait()

    def wait_slot(sl):
        for _ in range(2):
            pltpu.make_async_copy(ubuf.at[sl], ubuf.at[sl], sems.at[sl]).wait()

    base = step * tm

    def run(sl):
        @pl.when(step >= 2)
        def _():
            wait_slot(sl)

        ubuf[sl] = _modulate(h_ref[0], g_ref[...], sc_ref[0], sh_ref[0])

        def issue(i, carry):
            t = 2 * (base + i)
            src = ubuf.at[sl, pl.ds(i, 1), :]
            pltpu.make_async_copy(src, xs_ref.at[pl.ds(dest_ref[t], 1), :], sems.at[sl]).start()
            pltpu.make_async_copy(src, xs_ref.at[pl.ds(dest_ref[t + 1], 1), :], sems.at[sl]).start()
            return carry

        lax.fori_loop(0, tm, issue, 0, unroll=DMA_UNROLL)

        @pl.when(step == nsteps - 1)
        def _():
            wait_slot(sl)

            @pl.when(nsteps >= 2)
            def _():
                wait_slot(1 - sl)

    for sl in range(2):
        pl.when(slot == sl)(functools.partial(run, sl))


def _scatter(dest, zero_tile, h, gain, scale, shift, m_pad):
    b, s, d = h.shape
    tm = TM_SCATTER
    ns = s // tm
    vec = pl.BlockSpec((1, 1, d), lambda bi, si, dest, zt: (bi, 0, 0))
    grid_spec = pltpu.PrefetchScalarGridSpec(
        num_scalar_prefetch=2,
        grid=(b, ns),
        in_specs=[pl.BlockSpec((1, tm, d), lambda bi, si, dest, zt: (bi, si, 0)),
                  pl.BlockSpec((1, d), lambda bi, si, dest, zt: (0, 0)),
                  vec, vec],
        out_specs=pl.BlockSpec(memory_space=pl.ANY),
        scratch_shapes=[pltpu.VMEM((2, tm, d), F32), pltpu.VMEM((TM_EXPERT, d), F32),
                        pltpu.SemaphoreType.DMA((2,)), pltpu.SemaphoreType.DMA],
    )
    return pl.pallas_call(
        functools.partial(_scatter_kernel, tm=tm, ns=ns),
        grid_spec=grid_spec,
        out_shape=jax.ShapeDtypeStruct((m_pad, d), F32),
        compiler_params=_cparams(("arbitrary", "arbitrary")),
        name="moe_scatter",
    )(dest, zero_tile, h, gain.reshape(1, d), scale.reshape(b, 1, d), shift.reshape(b, 1, d))


def _expert_kernel(te_ref, tv_ref, tx_ref, x_ref, wg_ref, wu_ref, wd_ref, o_ref):
    t = pl.program_id(0)
    f = pl.program_id(1)

    @pl.when(tv_ref[t] == 1)
    def _():
        x = x_ref[...].astype(BF16)
        hid = (_silu(_dot(x, wg_ref[0])) * _dot(x, wu_ref[0])).astype(BF16)
        y = _dot(hid, wd_ref[0])

        @pl.when(f == 0)
        def _():
            o_ref[...] = y

        @pl.when(f > 0)
        def _():
            o_ref[...] = o_ref[...] + y

    @pl.when((tv_ref[t] == 0) & (f == 0))
    def _():
        o_ref[...] = jnp.zeros_like(o_ref)


def _experts(tile_expert, tile_valid, tile_x, xs, w_gate, w_up, w_down):
    m_pad, d = xs.shape
    tm = TM_EXPERT
    n_tiles = tile_expert.shape[0]
    ff = w_gate.shape[2]
    fs = FF_STEPS_EXPERT
    tf = ff // fs

    def ff_idx(f, tv, t):
        return f * tv[t] + (fs - 1) * (1 - tv[t])

    grid_spec = pltpu.PrefetchScalarGridSpec(
        num_scalar_prefetch=3,
        grid=(n_tiles, fs),
        in_specs=[pl.BlockSpec((tm, d), lambda t, f, te, tv, tx: (tx[t], 0)),
                  pl.BlockSpec((1, d, tf), lambda t, f, te, tv, tx: (te[t], 0, ff_idx(f, tv, t))),
                  pl.BlockSpec((1, d, tf), lambda t, f, te, tv, tx: (te[t], 0, ff_idx(f, tv, t))),
                  pl.BlockSpec((1, tf, d), lambda t, f, te, tv, tx: (te[t], ff_idx(f, tv, t), 0))],
        out_specs=pl.BlockSpec((tm, d), lambda t, f, te, tv, tx: (t, 0)),
    )
    return pl.pallas_call(
        _expert_kernel,
        grid_spec=grid_spec,
        out_shape=jax.ShapeDtypeStruct((m_pad, d), F32),
        compiler_params=_cparams(("arbitrary", "arbitrary")),
        name="moe_experts",
    )(tile_expert, tile_valid, tile_x, xs,
      w_gate.astype(BF16), w_up.astype(BF16), w_down.astype(BF16))


def _combine_kernel(dest_ref, y_ref, mf_ref, h_ref, gain_ref, gate_ref, o_ref, ybuf, sems, *, tm, ns):
    step = pl.program_id(0) * ns + pl.program_id(1)
    nsteps = pl.num_programs(0) * ns
    slot = lax.rem(step, 2)

    def issue(st, sl):
        base = st * tm

        def body(i, carry):
            t = 2 * (base + i)
            pltpu.make_async_copy(y_ref.at[pl.ds(dest_ref[t], 1), :],
                                  ybuf.at[sl, 0, pl.ds(i, 1), :], sems.at[sl]).start()
            pltpu.make_async_copy(y_ref.at[pl.ds(dest_ref[t + 1], 1), :],
                                  ybuf.at[sl, 1, pl.ds(i, 1), :], sems.at[sl]).start()
            return carry

        lax.fori_loop(0, tm, body, 0, unroll=DMA_UNROLL)

    @pl.when(step == 0)
    def _():
        issue(0, 0)

    def run(sl):
        @pl.when(step + 1 < nsteps)
        def _():
            issue(step + 1, 1 - sl)

        for k in range(2):
            pltpu.make_async_copy(ybuf.at[sl, k], ybuf.at[sl, k], sems.at[sl]).wait()
        mf = mf_ref[...]
        y = mf[:, 0:1] * ybuf[sl, 0] + mf[:, 1:2] * ybuf[sl, 1]
        o_ref[0] = h_ref[0] + gate_ref[0] * (_rms(y) * gain_ref[...])

    for sl in range(2):
        pl.when(slot == sl)(functools.partial(run, sl))


def _combine(dest, ys, mf, h, gain, gate):
    b, s, d = h.shape
    tm = TM_COMBINE
    ns = s // tm
    grid_spec = pltpu.PrefetchScalarGridSpec(
        num_scalar_prefetch=1,
        grid=(b, ns),
        in_specs=[pl.BlockSpec(memory_space=pl.ANY),
                  pl.BlockSpec((tm, LANES), lambda bi, si, dest: (bi * ns + si, 0)),
                  pl.BlockSpec((1, tm, d), lambda bi, si, dest: (bi, si, 0)),
                  pl.BlockSpec((1, d), lambda bi, si, dest: (0, 0)),
                  pl.BlockSpec((1, 1, d), lambda bi, si, dest: (bi, 0, 0))],
        out_specs=pl.BlockSpec((1, tm, d), lambda bi, si, dest: (bi, si, 0)),
        scratch_shapes=[pltpu.VMEM((2, 2, tm, d), F32), pltpu.SemaphoreType.DMA((2,))],
    )
    return pl.pallas_call(
        functools.partial(_combine_kernel, tm=tm, ns=ns),
        grid_spec=grid_spec,
        out_shape=jax.ShapeDtypeStruct((b, s, d), F32),
        compiler_params=_cparams(("arbitrary", "arbitrary")),
        name="moe_combine",
    )(dest, ys, mf, h, gain.reshape(1, d), gate.reshape(b, 1, d))


def _moe(h, gain_in, scale, shift, router_w, w_gate, w_up, w_down, gain_out, gate):
    b, s, d = h.shape
    n = b * s
    tm = TM_EXPERT
    mi, mf, cnt = _router(h, gain_in, scale, shift, router_w)
    counts = cnt[0, :N_EXPERTS].astype(I32)
    tiles_per = (counts + tm - 1) // tm
    seg_start = (jnp.cumsum(tiles_per) - tiles_per) * tm
    dest = (seg_start[mi[:, 0:2]] + mi[:, 2:4]).reshape(2 * n)
    n_tiles = (2 * n) // tm + N_EXPERTS
    m_pad = n_tiles * tm
    tile_end = jnp.cumsum(tiles_per)
    tidx = jnp.arange(n_tiles, dtype=I32)
    tile_valid = (tidx < tile_end[-1]).astype(I32)
    tile_expert = jnp.minimum(jnp.searchsorted(tile_end, tidx, side="right"), N_EXPERTS - 1).astype(I32)
    tile_x = jnp.minimum(tidx, tile_end[-1] - 1)
    tail = tile_end[-1] + jnp.arange(N_EXPERTS, dtype=I32)
    zero_tile = jnp.concatenate([jnp.where(tiles_per > 0, tile_end - 1, -1),
                                 jnp.where(tail < n_tiles, tail, -1)]).astype(I32)
    xs = _scatter(dest, zero_tile, h, gain_in, scale, shift, m_pad)
    ys = _experts(tile_expert, tile_valid, tile_x, xs, w_gate, w_up, w_down)
    return _combine(dest, ys, mf, h, gain_out, gate)


def kernel(x, c, mod_w, mod_b, norm_g, attn_in_w_even, fox_gate_bias, attn_out_w_even,
           attn_in_w_odd, attn_out_w_odd, rel_bias_table, ffn_w_gate, ffn_w_up, ffn_w_down,
           router_w, exp_w_gate, exp_w_up, exp_w_down):
    depth = mod_w.shape[0]
    s_len = x.shape[1]
    mods = _mods(c, mod_w, mod_b)
    dil_bias, moba_bias = _bias_tiles(rel_bias_table, s_len)
    h = x
    for layer in range(depth):
        j = layer // 2
        sh1, sc1, g1, sh2, sc2, g2 = jnp.split(mods[layer], 6, axis=-1)
        gains = norm_g[layer]
        if layer % 2 == 0:
            fox_in, moba_in = _inproj_even(h, gains[0], sc1, sh1, attn_in_w_even[j], fox_gate_bias[j])
            o_parts = [_fox_attention(*fox_in), _moba_attention(*moba_in, moba_bias)]
            h = _outproj(o_parts, attn_out_w_even[j], h, gains[1], g1)
            h = _ffn(h, gains[2], sc2, sh2, ffn_w_gate[j], ffn_w_up[j], ffn_w_down[j], gains[3], g2)
        else:
            q, k, v = _inproj_odd(h, gains[0], sc1, sh1, attn_in_w_odd[j])
            o = _dilated_attention(q, k, v, dil_bias)
            h = _outproj([o], attn_out_w_odd[j], h, gains[1], g1)
            h = _moe(h, gains[2], sc2, sh2, router_w[j], exp_w_gate[j], exp_w_up[j], exp_w_down[j],
                     gains[3], g2)
    return h
```

```python
import functools
import math

import numpy as np
import jax
import jax.numpy as jnp
from jax import lax
from jax.experimental import pallas as pl
from jax.experimental.pallas import tpu as pltpu

F32 = jnp.float32
BF16 = jnp.bfloat16
I32 = jnp.int32

HEAD_DIM = 64
LANES = 128
N_HEADS = 16
N_HEADS_FOX = 8
ATTN_SCALE = HEAD_DIM ** -0.5
LOG2E = math.log2(math.e)
Q_SCALE = ATTN_SCALE * LOG2E
N_DECAY_PIECES = 3
MOBA_BLOCK = 256
MOBA_TOPK = 3
DIL_PATTERNS = ((128, 1), (512, 4), (2048, 16))
NUM_BUCKETS = 32
MAX_DISTANCE = 2048
N_EXPERTS = 8
NORM_EPS = 1e-6
NEG_INF = float("-inf")
MASK_BIG = 1e30

VMEM_LIMIT = 56 * 1024 * 1024

TM_PROJ = 512
TM_FFN = 512
TQ_FOX = 512
TK_FOX = 512
FOX_PAIRS_PER_STEP = 2
MOBA_PAIRS_PER_STEP = 2
TM_ROUTE = 512
TM_SCATTER = 256
TM_EXPERT = 512
TM_COMBINE = 256
FF_STEPS_EXPERT = 2
DMA_UNROLL = True
DIL_CHUNKS_PER_STEP = 16


def _cparams(sem):
    return pltpu.CompilerParams(dimension_semantics=sem, vmem_limit_bytes=VMEM_LIMIT)


def _t5_bucket_np(n):
    n = np.maximum(n, 0)
    max_exact = NUM_BUCKETS // 2
    nf = np.maximum(n, 1).astype(np.float64)
    large = max_exact + (np.log(nf / max_exact) / math.log(MAX_DISTANCE / max_exact)
                         * (NUM_BUCKETS - max_exact)).astype(np.int64)
    large = np.minimum(large, NUM_BUCKETS - 1)
    return np.where(n < max_exact, n, large)


_MAX_DIST = 1 << 16
_BUCKET_OF = _t5_bucket_np(np.arange(_MAX_DIST))
_BUCKET_THR = [int(np.searchsorted(_BUCKET_OF, k, side="left")) for k in range(NUM_BUCKETS)]


def _bias_from_dist(tab_ref, h, dist, dlo, dhi):
    lo_b = int(_BUCKET_OF[max(dlo, 0)])
    hi_b = int(_BUCKET_OF[dhi])
    val = jnp.zeros(dist.shape, F32) + tab_ref[lo_b, h]
    for k in range(lo_b + 1, hi_b + 1):
        val = jnp.where(dist >= _BUCKET_THR[k], tab_ref[k, h], val)
    return val


def _dil_bias_kernel(tab_ref, o_ref):
    h = pl.program_id(0)
    for g, (window, dil) in enumerate(DIL_PATTERNS):
        span = window // dil
        i = lax.broadcasted_iota(I32, (span, 2 * span), 0)
        j = lax.broadcasted_iota(I32, (span, 2 * span), 1)
        rel = i + span - j
        val = _bias_from_dist(tab_ref, h, rel * dil, 0, span * dil) * LOG2E
        band = jnp.where(rel >= 0, jnp.where(rel <= span, val, NEG_INF), NEG_INF)
        o_ref[0, 2 * g] = band
        o_ref[0, 2 * g + 1] = jnp.where(j >= span, band, NEG_INF)


def _moba_bias_kernel(tab_ref, o_ref, *, n_blk, head0):
    h = pl.program_id(0) + head0
    i = lax.broadcasted_iota(I32, (MOBA_BLOCK, MOBA_BLOCK), 0)
    j = lax.broadcasted_iota(I32, (MOBA_BLOCK, MOBA_BLOCK), 1)
    for d in range(n_blk):
        dist = d * MOBA_BLOCK + i - j
        val = _bias_from_dist(tab_ref, h, dist, d * MOBA_BLOCK - (MOBA_BLOCK - 1),
                              d * MOBA_BLOCK + (MOBA_BLOCK - 1)) * LOG2E
        if d == 0:
            val = jnp.where(dist >= 0, val, NEG_INF)
        o_ref[0, d] = val


def _bias_tiles(rel_bias_table, s_len):
    n_blk = s_len // MOBA_BLOCK
    span = DIL_PATTERNS[0][0]
    n_var = 2 * len(DIL_PATTERNS)
    smem = pl.BlockSpec(memory_space=pltpu.SMEM)
    dil = pl.pallas_call(
        _dil_bias_kernel,
        grid=(N_HEADS,),
        in_specs=[smem],
        out_specs=pl.BlockSpec((1, n_var, span, 2 * span), lambda h: (h, 0, 0, 0)),
        out_shape=jax.ShapeDtypeStruct((N_HEADS, n_var, span, 2 * span), F32),
        compiler_params=_cparams(("parallel",)),
        name="dil_bias",
    )(rel_bias_table)
    n_moba = N_HEADS - N_HEADS_FOX
    moba = pl.pallas_call(
        functools.partial(_moba_bias_kernel, n_blk=n_blk, head0=N_HEADS_FOX),
        grid=(n_moba,),
        in_specs=[smem],
        out_specs=pl.BlockSpec((1, n_blk, MOBA_BLOCK, MOBA_BLOCK), lambda h: (h, 0, 0, 0)),
        out_shape=jax.ShapeDtypeStruct((n_moba, n_blk, MOBA_BLOCK, MOBA_BLOCK), F32),
        compiler_params=_cparams(("parallel",)),
        name="moba_bias",
    )(rel_bias_table)
    return dil, moba


def _split_bf16(a):
    hi = a.astype(BF16)
    lo = (a - hi.astype(F32)).astype(BF16)
    return hi, lo


def _dot(a, b):
    return jnp.dot(a, b, preferred_element_type=F32)


def _dot_nt(a, b):
    return lax.dot_general(a, b, (((1,), (1,)), ((), ())), preferred_element_type=F32)


def _dot_split(a, b):
    a_hi, a_lo = _split_bf16(a)
    b_hi, b_lo = _split_bf16(b)
    return _dot(a_hi, b_hi) + (_dot(a_hi, b_lo) + _dot(a_lo, b_hi))


def _rms(x):
    return x * lax.rsqrt(jnp.mean(x * x, axis=-1, keepdims=True) + NORM_EPS)


def _modulate(x, gain, scale, shift):
    return (_rms(x) * gain) * (1.0 + scale) + shift


def _silu(x):
    return x * jax.nn.sigmoid(x)


def _mods_kernel(c_ref, w_ref, b_ref, o_ref):
    o_ref[0] = _dot_split(_silu(c_ref[...]), w_ref[0]) + b_ref[0]


def _mods(c, mod_w, mod_b):
    depth, d, e = mod_w.shape
    b = c.shape[0]
    tn = 1536
    return pl.pallas_call(
        _mods_kernel,
        grid=(depth, e // tn),
        in_specs=[pl.BlockSpec((b, d), lambda l, j: (0, 0)),
                  pl.BlockSpec((1, d, tn), lambda l, j: (l, 0, j)),
                  pl.BlockSpec((1, 1, tn), lambda l, j: (l, 0, j))],
        out_specs=pl.BlockSpec((1, b, tn), lambda l, j: (l, 0, j)),
        out_shape=jax.ShapeDtypeStruct((depth, b, e), F32),
        compiler_params=_cparams(("parallel", "parallel")),
        name="adaln_mods",
    )(c, mod_w, mod_b.reshape(depth, 1, e))


def _inproj_even_kernel(h_ref, g_ref, sc_ref, sh_ref, w_ref, wf_ref, gb_ref,
                        qa_ref, k0a_ref, k1a_ref, v0a_ref, v1a_ref,
                        qb_ref, k0b_ref, k1b_ref, v0b_ref, v1b_ref, carry_ref, *, tm):
    si = pl.program_id(1)
    u = _modulate(h_ref[0], g_ref[...], sc_ref[0], sh_ref[0]).astype(BF16)
    width = qa_ref.shape[-1]
    n_pairs = width // LANES

    def proj(i):
        return _dot(u, w_ref[:, i * width:(i + 1) * width])

    lane = lax.broadcasted_iota(I32, (1, LANES), 1)
    left = lane < HEAD_DIM
    row = lax.broadcasted_iota(I32, (tm, LANES), 0)

    x = _dot(u, wf_ref[...]) + gb_ref[...]
    lf = jnp.where(lane < N_HEADS_FOX, jnp.minimum(x, 0.0) - jnp.log1p(jnp.exp(-jnp.abs(x))), 0.0)
    k = 1
    while k < tm:
        lf = lf + jnp.where(row >= k, pltpu.roll(lf, k, axis=0), 0.0)
        k *= 2

    @pl.when(si == 0)
    def _():
        carry_ref[...] = jnp.zeros_like(carry_ref)

    cum = lf + carry_ref[0:1, :]
    carry_ref[...] = jnp.broadcast_to(cum[tm - 1:tm, :], carry_ref.shape)
    rest = cum * (-LOG2E)
    decay = jnp.zeros((tm, LANES), F32)
    for p in range(N_DECAY_PIECES):
        piece = rest.astype(BF16).astype(F32)
        rest = rest - piece
        decay = decay + (pltpu.roll(piece, p * N_HEADS_FOX, axis=1) if p else piece)
    decay_lo = decay.astype(BF16)
    decay_hi = pltpu.roll(decay, HEAD_DIM, axis=1).astype(BF16)

    one_lo = jnp.where(lane == 0, 1.0, 0.0).astype(BF16)
    one_hi = jnp.where(lane == HEAD_DIM, 1.0, 0.0).astype(BF16)
    blk = (si * tm + row) // MOBA_BLOCK
    blk_lo = jnp.where(lane == blk, 1.0, 0.0).astype(BF16)
    blk_hi = jnp.where(lane == blk + HEAD_DIM, 1.0, 0.0).astype(BF16)

    def emit(first, k0_ref, k1_ref, v0_ref, v1_ref, k_lo, k_hi):
        kk = proj(first + 1).astype(BF16)
        vv = proj(first + 2).astype(BF16)
        for hp in range(n_pairs):
            sl = slice(hp * LANES, (hp + 1) * LANES)
            k0_ref[0, :, sl] = jnp.where(left, kk[:, sl], k_hi)
            k1_ref[0, :, sl] = jnp.where(left, k_lo, kk[:, sl])
            v0_ref[0, :, sl] = jnp.where(left, vv[:, sl], one_hi)
            v1_ref[0, :, sl] = jnp.where(left, one_lo, vv[:, sl])

    qa_ref[0] = (proj(0) * Q_SCALE).astype(BF16)
    emit(0, k0a_ref, k1a_ref, v0a_ref, v1a_ref, decay_lo, decay_hi)
    qb_ref[0] = (proj(3) * Q_SCALE).astype(BF16)
    emit(3, k0b_ref, k1b_ref, v0b_ref, v1b_ref, blk_lo, blk_hi)


def _inproj_even(h, gain, scale, shift, w_in, gate_bias):
    b, s, d = h.shape
    tm = TM_PROJ
    da = N_HEADS_FOX * HEAD_DIM
    assert s // MOBA_BLOCK <= HEAD_DIM and N_DECAY_PIECES * N_HEADS_FOX <= HEAD_DIM
    cuts = np.cumsum([da, da, da, N_HEADS_FOX, da, da]).tolist()
    q_a, k_a, v_a, f_a, q_b, k_b, v_b = jnp.split(w_in, cuts, axis=1)
    w = jnp.concatenate([q_a, k_a, v_a, q_b, k_b, v_b], axis=1).astype(BF16)
    wf = jnp.pad(f_a, ((0, 0), (0, LANES - N_HEADS_FOX))).astype(BF16)
    gb = jnp.pad(gate_bias.astype(F32), (0, LANES - N_HEADS_FOX)).reshape(1, LANES)
    act = jax.ShapeDtypeStruct((b, s, da), BF16)
    act_spec = pl.BlockSpec((1, tm, da), lambda bi, si: (bi, si, 0))
    vec = pl.BlockSpec((1, 1, d), lambda bi, si: (bi, 0, 0))
    outs = pl.pallas_call(
        functools.partial(_inproj_even_kernel, tm=tm),
        grid=(b, s // tm),
        in_specs=[pl.BlockSpec((1, tm, d), lambda bi, si: (bi, si, 0)),
                  pl.BlockSpec((1, d), lambda bi, si: (0, 0)),
                  vec, vec,
                  pl.BlockSpec(w.shape, lambda bi, si: (0, 0)),
                  pl.BlockSpec(wf.shape, lambda bi, si: (0, 0)),
                  pl.BlockSpec(gb.shape, lambda bi, si: (0, 0))],
        out_specs=[act_spec] * 10,
        out_shape=[act] * 10,
        scratch_shapes=[pltpu.VMEM((8, LANES), F32)],
        compiler_params=_cparams(("parallel", "arbitrary")),
        name="inproj_even",
    )(h, gain.reshape(1, d), scale.reshape(b, 1, d), shift.reshape(b, 1, d), w, wf, gb)
    return outs[:5], outs[5:]


def _inproj_odd_kernel(h_ref, g_ref, sc_ref, sh_ref, w_ref, q_ref, k_ref, v_ref):
    u = _modulate(h_ref[0], g_ref[...], sc_ref[0], sh_ref[0]).astype(BF16)
    width = q_ref.shape[-1]
    q_ref[0] = _dot(u, w_ref[:, 0:width]) * Q_SCALE
    k_ref[0] = _dot(u, w_ref[:, width:2 * width])
    v_ref[0] = _dot(u, w_ref[:, 2 * width:3 * width])


def _inproj_odd(h, gain, scale, shift, w_in):
    b, s, d = h.shape
    tm = TM_PROJ
    dq = w_in.shape[1] // 3
    act = jax.ShapeDtypeStruct((b, s, dq), F32)
    act_spec = pl.BlockSpec((1, tm, dq), lambda bi, si: (bi, si, 0))
    vec = pl.BlockSpec((1, 1, d), lambda bi, si: (bi, 0, 0))
    return pl.pallas_call(
        _inproj_odd_kernel,
        grid=(b, s // tm),
        in_specs=[pl.BlockSpec((1, tm, d), lambda bi, si: (bi, si, 0)),
                  pl.BlockSpec((1, d), lambda bi, si: (0, 0)),
                  vec, vec,
                  pl.BlockSpec(w_in.shape, lambda bi, si: (0, 0))],
        out_specs=[act_spec] * 3,
        out_shape=[act] * 3,
        compiler_params=_cparams(("parallel", "parallel")),
        name="inproj_odd",
    )(h, gain.reshape(1, d), scale.reshape(b, 1, d), shift.reshape(b, 1, d), w_in.astype(BF16))


def _tile_lanes(x, width):
    return jnp.concatenate([x] * (width // LANES), axis=1)


def _flash_update(s, v, m_ref, acc_ref):
    m_prev = m_ref[...]
    m_new = jnp.maximum(m_prev, jnp.max(s, axis=1, keepdims=True))
    p = jnp.exp2(s - _tile_lanes(m_new, s.shape[1]))
    acc_ref[...] = jnp.exp2(m_prev - m_new) * acc_ref[...] + _dot(p.astype(BF16), v)
    m_ref[...] = m_new


def _finish_pair(acc_ref, left, first=0):
    acc0 = acc_ref[first]
    acc1 = acc_ref[first + 1]
    return jnp.where(left, acc0 / acc0[:, HEAD_DIM:HEAD_DIM + 1], acc1 / acc1[:, 0:1])


def _fox_kernel(q_ref, k0_ref, k1_ref, v0_ref, v1_ref, o_ref, m_ref, acc_ref, *, tq, tk, n_pairs):
    qi = pl.program_id(2)
    lane = lax.broadcasted_iota(I32, (1, LANES), 1)
    left = lane < HEAD_DIM

    def piece_lanes(lane0):
        hit = lane == lane0
        for p in range(1, N_DECAY_PIECES):
            hit = jnp.logical_or(hit, lane == lane0 + p * N_HEADS_FOX)
        return jnp.where(hit, 1.0, 0.0).astype(BF16)

    chains = []
    for pr in range(n_pairs):
        hp = pl.program_id(1) * n_pairs + pr
        lanes = slice(pr * LANES, (pr + 1) * LANES)
        q = q_ref[0, :, lanes]
        chains.append((jnp.where(left, q, piece_lanes(HEAD_DIM + 2 * hp)), k0_ref, v0_ref, lanes))
        chains.append((jnp.where(left, piece_lanes(2 * hp + 1), q), k1_ref, v1_ref, lanes))
    row = lax.broadcasted_iota(I32, (tq, tk), 0)
    col = lax.broadcasted_iota(I32, (tq, tk), 1)
    m_ref[...] = jnp.full(m_ref.shape, NEG_INF, F32)
    acc_ref[...] = jnp.zeros(acc_ref.shape, F32)
    n_sub = tq // tk

    def step(kv, mask):
        off = pl.multiple_of(kv * tk, tk)
        for c, (qc, k_ref, v_ref, lanes) in enumerate(chains):
            s = _dot_nt(qc, k_ref[0, pl.ds(off, tk), lanes])
            if mask is not None:
                s = jnp.where(mask, s, NEG_INF)
            _flash_update(s, v_ref[0, pl.ds(off, tk), lanes], m_ref.at[c], acc_ref.at[c])

    def body(kv, carry):
        step(kv, None)
        return carry

    lax.fori_loop(0, qi * n_sub, body, 0)
    for d in range(n_sub):
        step(qi * n_sub + d, col + d * tk <= row)
    for pr in range(n_pairs):
        o_ref[0, :, pr * LANES:(pr + 1) * LANES] = _finish_pair(acc_ref, left, 2 * pr).astype(o_ref.dtype)


def _fox_attention(q, k0, k1, v0, v1):
    b, s, da = q.shape
    n_pairs = FOX_PAIRS_PER_STEP
    width = n_pairs * LANES
    tq = min(TQ_FOX, s)
    kv = pl.BlockSpec((1, s, width), lambda bi, h, qi: (bi, 0, h))
    return pl.pallas_call(
        functools.partial(_fox_kernel, tq=tq, tk=min(TK_FOX, tq), n_pairs=n_pairs),
        grid=(b, da // width, s // tq),
        in_specs=[pl.BlockSpec((1, tq, width), lambda bi, h, qi: (bi, qi, h)), kv, kv, kv, kv],
        out_specs=pl.BlockSpec((1, tq, width), lambda bi, h, qi: (bi, qi, h)),
        out_shape=jax.ShapeDtypeStruct((b, s, da), BF16),
        scratch_shapes=[pltpu.VMEM((2 * n_pairs, tq, LANES), F32)] * 2,
        compiler_params=_cparams(("parallel", "parallel", "arbitrary")),
        name="fox_attention",
    )(q, k0, k1, v0, v1)


def _moba_kernel(q_ref, k0_ref, k1_ref, v0_ref, v1_ref, bias_ref, o_ref, km_ref, m_ref, acc_ref,
                 *, n_blk, tq, n_pairs):
    blk = MOBA_BLOCK
    a = pl.program_id(2)
    gate_lane0 = (HEAD_DIM, 0)
    lane = lax.broadcasted_iota(I32, (1, LANES), 1)
    left = lane < HEAD_DIM
    mine = (left, jnp.logical_not(left))
    k_refs = (k0_ref, k1_ref)
    v_refs = (v0_ref, v1_ref)

    @pl.when(a == 0)
    def _():
        km_ref[...] = jnp.zeros_like(km_ref)
        for pr in range(n_pairs):
            lanes = slice(pr * LANES, (pr + 1) * LANES)
            for n in range(n_blk):
                rows = slice(n * blk, (n + 1) * blk)
                kb = jnp.where(left, k0_ref[0, rows, lanes], k1_ref[0, rows, lanes]).astype(F32)
                mean = jnp.sum(kb, axis=0, keepdims=True) * (1.0 / blk)
                for lane0 in gate_lane0:
                    km_ref[pr, lane0 + n:lane0 + n + 1, :] = mean

    nb = -(-n_blk // 8) * 8
    blkf = lax.broadcasted_iota(I32, (nb, tq), 0).astype(F32)
    own = (lax.broadcasted_iota(I32, (nb, tq), 1) // blk + a * (tq // blk)).astype(F32)
    chains = []
    for pr in range(n_pairs):
        lanes = slice(pr * LANES, (pr + 1) * LANES)
        q = q_ref[0, :, lanes]
        km_hi, km_lo = _split_bf16(km_ref[pr])
        for j in range(2):
            qj = jnp.where(mine[j], q, jnp.zeros_like(q))
            lane0 = gate_lane0[j]
            gate = (_dot_nt(km_hi, qj) + _dot_nt(km_lo, qj))[lane0:lane0 + nb]
            gate = jnp.where(blkf < own, gate, NEG_INF)
            pen = jnp.where(blkf == own, 0.0, -MASK_BIG)
            for _ in range(MOBA_TOPK):
                mx = jnp.max(gate, axis=0, keepdims=True)
                cand = jnp.where(gate == mx, jnp.where(mx > NEG_INF, blkf, float(LANES)), float(LANES))
                pick = blkf == jnp.min(cand, axis=0, keepdims=True)
                pen = jnp.where(pick, 0.0, pen)
                gate = jnp.where(pick, NEG_INF, gate)
            parts = [pen, jnp.zeros((LANES - lane0 - nb, tq), F32)]
            if lane0:
                parts.insert(0, jnp.zeros((lane0, tq), F32))
            pen_q = jnp.concatenate(parts, axis=0).T
            chains.append((jnp.where(mine[j], q, pen_q.astype(BF16)), k_refs[j], v_refs[j], lanes, 2 * pr + j))

    m_ref[...] = jnp.full(m_ref.shape, NEG_INF, F32)
    acc_ref[...] = jnp.zeros(acc_ref.shape, F32)

    def body(i, carry):
        off = pl.multiple_of(i * tq, tq)
        d0 = 2 * (a - i)
        for qc, k_ref, v_ref, lanes, h in chains:
            top = jnp.concatenate([bias_ref[h, d0], bias_ref[h, jnp.maximum(d0 - 1, 0)]], axis=1)
            bot = jnp.concatenate([bias_ref[h, d0 + 1], bias_ref[h, d0]], axis=1)
            s = _dot_nt(qc, k_ref[0, pl.ds(off, tq), lanes]) + jnp.concatenate([top, bot], axis=0)
            _flash_update(s, v_ref[0, pl.ds(off, tq), lanes], m_ref.at[h], acc_ref.at[h])
        return carry

    lax.fori_loop(0, a + 1, body, 0)
    for pr in range(n_pairs):
        o_ref[0, :, pr * LANES:(pr + 1) * LANES] = _finish_pair(acc_ref, left, 2 * pr).astype(o_ref.dtype)


def _moba_attention(q, k0, k1, v0, v1, bias_tiles):
    b, s, db = q.shape
    n_pairs = MOBA_PAIRS_PER_STEP
    width = n_pairs * LANES
    blk = MOBA_BLOCK
    n_blk = s // blk
    tq = 2 * blk
    assert n_blk <= HEAD_DIM, "block gates of one head must fit in the other head's lanes"
    assert s % tq == 0
    kv = pl.BlockSpec((1, s, width), lambda h, bi, qi: (bi, 0, h))
    return pl.pallas_call(
        functools.partial(_moba_kernel, n_blk=n_blk, tq=tq, n_pairs=n_pairs),
        grid=(db // width, b, s // tq),
        in_specs=[pl.BlockSpec((1, tq, width), lambda h, bi, qi: (bi, qi, h)), kv, kv, kv, kv,
                  pl.BlockSpec((2 * n_pairs, n_blk, blk, blk), lambda h, bi, qi: (h, 0, 0, 0),
                               pipeline_mode=pl.Buffered(1))],
        out_specs=pl.BlockSpec((1, tq, width), lambda h, bi, qi: (bi, qi, h)),
        out_shape=jax.ShapeDtypeStruct((b, s, db), BF16),
        scratch_shapes=[pltpu.VMEM((n_pairs, LANES, LANES), F32)]
                       + [pltpu.VMEM((2 * n_pairs, tq, LANES), F32)] * 2,
        compiler_params=_cparams(("parallel", "parallel", "arbitrary")),
        name="moba_attention",
    )(q, k0, k1, v0, v1, bias_tiles)


def _dilated_kernel(q_ref, k_ref, v_ref, bias_ref, o_ref, m_ref, acc_ref, *, s_len):
    lane = lax.broadcasted_iota(I32, (1, LANES), 1)
    left = lane < HEAD_DIM
    one_lo = jnp.where(lane == 0, 1.0, 0.0).astype(BF16)
    one_hi = jnp.where(lane == HEAD_DIM, 1.0, 0.0).astype(BF16)
    order = sorted(range(len(DIL_PATTERNS)), key=lambda i: -DIL_PATTERNS[i][1])
    for g in order:
        window, dil = DIL_PATTERNS[g]
        merge = g != order[0]
        span = window // dil
        unit = span * dil
        nc = s_len // unit
        n_u = min(DIL_CHUNKS_PER_STEP, nc)
        groups = nc // n_u
        n_res = min(DIL_CHUNKS_PER_STEP // n_u, dil)

        def rows(ref, start, dil=dil, span=span):
            if dil == 1:
                return ref[0, pl.ds(start, span), :]
            return ref[0, pl.ds(start, span, stride=dil), :]

        def get(ref, j, start, dil=dil, span=span):
            if dil == 1:
                return ref[j, pl.ds(start, span), :]
            return ref[j, pl.ds(start, span, stride=dil), :]

        def put(ref, j, start, val, dil=dil, span=span):
            if dil == 1:
                ref[j, pl.ds(start, span), :] = val
            else:
                ref[j, pl.ds(start, span, stride=dil), :] = val

        def body(it, carry, g=g, merge=merge, n_u=n_u, n_res=n_res, groups=groups, unit=unit, rows=rows,
                 put=put, get=get):
            r0 = (it // groups) * n_res
            grp = it - (it // groups) * groups
            is_first = grp == 0
            results = []
            for dr in range(n_res):
                start0 = r0 + dr + grp * (n_u * unit)
                prev0 = start0 - jnp.where(is_first, 0, unit)
                starts = [start0 + u * unit for u in range(n_u)]
                kc = [rows(k_ref, st).astype(BF16) for st in [prev0] + starts]
                vc = [rows(v_ref, st).astype(BF16) for st in [prev0] + starts]
                vcs = ([jnp.where(left, v, one_hi) for v in vc], [jnp.where(left, one_lo, v) for v in vc])
                for u, start in enumerate(starts):
                    var = 2 * g + jnp.where(is_first, 1, 0) if u == 0 else 2 * g
                    q = rows(q_ref, start)
                    kb = jnp.concatenate([kc[u], kc[u + 1]], axis=0)
                    qq = jnp.concatenate([jnp.where(left, q, 0.0), jnp.where(left, 0.0, q)],
                                         axis=0).astype(BF16)
                    s_both = _dot_nt(qq, kb)
                    for j in range(2):
                        s = s_both[j * span:(j + 1) * span] + bias_ref[j, var]
                        m_new = jnp.broadcast_to(jnp.max(s, axis=1, keepdims=True), (span, LANES))
                        if merge:
                            m_prev = get(m_ref, j, start)
                            m_new = jnp.maximum(m_new, m_prev)
                        p = jnp.exp2(s - _tile_lanes(m_new, 2 * span))
                        acc_new = _dot(p.astype(BF16), jnp.concatenate([vcs[j][u], vcs[j][u + 1]], axis=0))
                        if merge:
                            acc_new = jnp.exp2(m_prev - m_new) * get(acc_ref, j, start) + acc_new
                        results.append((j, start, m_new, acc_new))
            for j, start, m_new, acc_new in results:
                put(m_ref, j, start, m_new)
                put(acc_ref, j, start, acc_new)
            return carry

        lax.fori_loop(0, (dil // n_res) * groups, body, 0)
    o_ref[0] = _finish_pair(acc_ref, left).astype(o_ref.dtype)


def _dilated_attention(q, k, v, bias_tiles):
    b, s, dq = q.shape
    hp = dq // LANES
    for window, dil in DIL_PATTERNS:
        assert s % window == 0, "sequence must be a whole number of dilated units"
    qkv = pl.BlockSpec((1, s, LANES), lambda h, bi: (bi, 0, h))
    n_var, span, band = bias_tiles.shape[1:]
    return pl.pallas_call(
        functools.partial(_dilated_kernel, s_len=s),
        grid=(hp, b),
        in_specs=[qkv, qkv, qkv,
                  pl.BlockSpec((2, n_var, span, band), lambda h, bi: (h, 0, 0, 0))],
        out_specs=pl.BlockSpec((1, s, LANES), lambda h, bi: (bi, 0, h)),
        out_shape=jax.ShapeDtypeStruct((b, s, dq), BF16),
        scratch_shapes=[pltpu.VMEM((2, s, LANES), F32)] * 2,
        compiler_params=_cparams(("parallel", "parallel")),
        name="dilated_attention",
    )(q, k, v, bias_tiles)


def _mixer_residual(o_parts, w_parts, h, gain, gate):
    y = _dot(o_parts[0][0], w_parts[0][...])
    for o_ref, w_ref in zip(o_parts[1:], w_parts[1:]):
        y = y + _dot(o_ref[0], w_ref[...])
    return h + gate * (_rms(y) * gain)


def _mixer_operands(o_parts, w_out, tm):
    w_out = w_out.astype(BF16)
    cuts = np.cumsum([p.shape[-1] for p in o_parts])[:-1].tolist()
    w_parts = jnp.split(w_out, cuts, axis=0) if cuts else [w_out]
    specs = [pl.BlockSpec((1, tm, p.shape[-1]), lambda bi, si: (bi, si, 0)) for p in o_parts]
    specs += [pl.BlockSpec(w.shape, lambda bi, si: (0, 0)) for w in w_parts]
    return list(o_parts) + list(w_parts), specs


def _ffn_kernel(*refs, n_parts):
    o_parts = refs[:n_parts]
    w_parts = refs[n_parts:2 * n_parts]
    (h_ref, gain1_ref, gate1_ref, g_ref, sc_ref, sh_ref, wg_ref, wu_ref, wd_ref, gain_ref, gate_ref,
     o_ref) = refs[2 * n_parts:]
    h = _mixer_residual(o_parts, w_parts, h_ref[0], gain1_ref[...], gate1_ref[0])
    u = _modulate(h, g_ref[...], sc_ref[0], sh_ref[0]).astype(BF16)
    hid = (_silu(_dot(u, wg_ref[...])) * _dot(u, wu_ref[...])).astype(BF16)
    y = _dot(hid, wd_ref[...])
    o_ref[0] = h + gate_ref[0] * (_rms(y) * gain_ref[...])


def _mixer_tail_ffn(o_parts, w_out, h, gain1, gate1, gain_in, scale, shift, w_gate, w_up, w_down,
                    gain_out, gate):
    b, s, d = h.shape
    tm = TM_FFN
    ff = w_gate.shape[1]
    resident = functools.partial(pl.BlockSpec, pipeline_mode=pl.Buffered(1))
    vec = pl.BlockSpec((1, 1, d), lambda bi, si: (bi, 0, 0))
    row = pl.BlockSpec((1, d), lambda bi, si: (0, 0))
    mix_ops, mix_specs = _mixer_operands(o_parts, w_out, tm)
    return pl.pallas_call(
        functools.partial(_ffn_kernel, n_parts=len(o_parts)),
        grid=(b, s // tm),
        in_specs=mix_specs + [pl.BlockSpec((1, tm, d), lambda bi, si: (bi, si, 0)), row, vec,
                              row, vec, vec,
                              resident((d, ff), lambda bi, si: (0, 0)),
                              resident((d, ff), lambda bi, si: (0, 0)),
                              resident((ff, d), lambda bi, si: (0, 0)),
                              row, vec],
        out_specs=pl.BlockSpec((1, tm, d), lambda bi, si: (bi, si, 0)),
        out_shape=jax.ShapeDtypeStruct((b, s, d), F32),
        compiler_params=_cparams(("parallel", "parallel")),
        name="dense_swiglu",
    )(*mix_ops, h, gain1.reshape(1, d), gate1.reshape(b, 1, d),
      gain_in.reshape(1, d), scale.reshape(b, 1, d), shift.reshape(b, 1, d),
      w_gate.astype(BF16), w_up.astype(BF16), w_down.astype(BF16),
      gain_out.reshape(1, d), gate.reshape(b, 1, d))


def _router_kernel(o_ref, wo_ref, h_ref, gain1_ref, gate1_ref, g_ref, sc_ref, sh_ref, rw_ref,
                   hout_ref, mi_ref, mf_ref, cnt_ref, carry_ref, *, tm):
    @pl.when((pl.program_id(0) == 0) & (pl.program_id(1) == 0))
    def _():
        carry_ref[...] = jnp.zeros_like(carry_ref)

    h = _mixer_residual([o_ref], [wo_ref], h_ref[0], gain1_ref[...], gate1_ref[0])
    hout_ref[0] = h
    u = _modulate(h, g_ref[...], sc_ref[0], sh_ref[0])
    logits = _dot_split(u, rw_ref[...])
    lanef = lax.broadcasted_iota(I32, (tm, LANES), 1).astype(F32)
    lg = jnp.where(lanef < N_EXPERTS, logits, NEG_INF)
    v1 = jnp.max(lg, axis=1, keepdims=True)
    i1 = jnp.min(jnp.where(lg == v1, lanef, float(LANES)), axis=1, keepdims=True)
    lg2 = jnp.where(lanef == i1, NEG_INF, lg)
    v2 = jnp.max(lg2, axis=1, keepdims=True)
    i2 = jnp.min(jnp.where(lg2 == v2, lanef, float(LANES)), axis=1, keepdims=True)
    e2 = jnp.exp(v2 - v1)
    p1 = 1.0 / (1.0 + e2)
    p2 = e2 / (1.0 + e2)
    oh1 = jnp.where(lanef == i1, 1.0, 0.0)
    oh2 = jnp.where(lanef == i2, 1.0, 0.0)
    oh = oh1 + oh2
    r = lax.broadcasted_iota(I32, (tm, tm), 0)
    c = lax.broadcasted_iota(I32, (tm, tm), 1)
    before = jnp.where(c < r, 1.0, 0.0).astype(BF16)
    tot = _dot(before, oh.astype(BF16)) + carry_ref[0:1, :]
    rank1 = jnp.sum(oh1 * tot, axis=1, keepdims=True)
    rank2 = jnp.sum(oh2 * tot, axis=1, keepdims=True)
    carry_ref[...] = carry_ref[...] + jnp.sum(oh, axis=0, keepdims=True)
    mi = jnp.where(lanef == 0.0, i1, jnp.where(lanef == 1.0, i2,
         jnp.where(lanef == 2.0, rank1, jnp.where(lanef == 3.0, rank2, 0.0))))
    mi_ref[...] = mi.astype(I32)
    mf_ref[...] = jnp.where(lanef == 0.0, p1, jnp.where(lanef == 1.0, p2, 0.0))
    cnt_ref[...] = carry_ref[...]


def _mixer_tail_router(o, w_out, h, gain1, gate1, gain, scale, shift, router_w):
    b, s, d = h.shape
    tm = TM_ROUTE
    n = b * s
    ns = s // tm
    rw = jnp.pad(router_w.astype(F32), ((0, 0), (0, LANES - router_w.shape[1])))
    vec = pl.BlockSpec((1, 1, d), lambda bi, si: (bi, 0, 0))
    row = pl.BlockSpec((1, d), lambda bi, si: (0, 0))
    act = pl.BlockSpec((1, tm, d), lambda bi, si: (bi, si, 0))
    meta = pl.BlockSpec((tm, LANES), lambda bi, si: (bi * ns + si, 0))
    mix_ops, mix_specs = _mixer_operands([o], w_out, tm)
    return pl.pallas_call(
        functools.partial(_router_kernel, tm=tm),
        grid=(b, ns),
        in_specs=mix_specs + [act, row, vec, row, vec, vec, pl.BlockSpec(rw.shape, lambda bi, si: (0, 0))],
        out_specs=[act, meta, meta, pl.BlockSpec((8, LANES), lambda bi, si: (0, 0))],
        out_shape=[jax.ShapeDtypeStruct((b, s, d), F32),
                   jax.ShapeDtypeStruct((n, LANES), I32),
                   jax.ShapeDtypeStruct((n, LANES), F32),
                   jax.ShapeDtypeStruct((8, LANES), F32)],
        scratch_shapes=[pltpu.VMEM((8, LANES), F32)],
        compiler_params=_cparams(("arbitrary", "arbitrary")),
        name="moe_router",
    )(*mix_ops, h, gain1.reshape(1, d), gate1.reshape(b, 1, d),
      gain.reshape(1, d), scale.reshape(b, 1, d), shift.reshape(b, 1, d), rw)


def _scatter_kernel(dest_ref, ztile_ref, h_ref, g_ref, sc_ref, sh_ref, xs_ref, ubuf, zbuf, sems, zsem,
                    *, tm, ns):
    step = pl.program_id(0) * ns + pl.program_id(1)
    nsteps = pl.num_programs(0) * ns
    slot = lax.rem(step, 2)
    tz = zbuf.shape[0]

    @pl.when(step == 0)
    def _():
        zbuf[...] = jnp.zeros_like(zbuf)
        for e in range(2 * N_EXPERTS):
            @pl.when(ztile_ref[e] >= 0)
            def _():
                row0 = pl.multiple_of(ztile_ref[e] * tz, tz)
                pltpu.make_async_copy(zbuf, xs_ref.at[pl.ds(row0, tz), :], zsem).start()
        for e in range(2 * N_EXPERTS):
            @pl.when(ztile_ref[e] >= 0)
            def _():
                pltpu.make_async_copy(zbuf, xs_ref.at[pl.ds(0, tz), :], zsem).wait()

    def wait_slot(sl):
        for _ in range(2):
            pltpu.make_async_copy(ubuf.at[sl], ubuf.at[sl], sems.at[sl]).wait()

    base = step * tm

    def run(sl):
        @pl.when(step >= 2)
        def _():
            wait_slot(sl)

        ubuf[sl] = _modulate(h_ref[0], g_ref[...], sc_ref[0], sh_ref[0])

        def issue(i, carry):
            t = 2 * (base + i)
            src = ubuf.at[sl, pl.ds(i, 1), :]
            pltpu.make_async_copy(src, xs_ref.at[pl.ds(dest_ref[t], 1), :], sems.at[sl]).start()
            pltpu.make_async_copy(src, xs_ref.at[pl.ds(dest_ref[t + 1], 1), :], sems.at[sl]).start()
            return carry

        lax.fori_loop(0, tm, issue, 0, unroll=DMA_UNROLL)

        @pl.when(step == nsteps - 1)
        def _():
            wait_slot(sl)

            @pl.when(nsteps >= 2)
            def _():
                wait_slot(1 - sl)

    for sl in range(2):
        pl.when(slot == sl)(functools.partial(run, sl))


def _scatter(dest, zero_tile, h, gain, scale, shift, m_pad):
    b, s, d = h.shape
    tm = TM_SCATTER
    ns = s // tm
    vec = pl.BlockSpec((1, 1, d), lambda bi, si, dest, zt: (bi, 0, 0))
    grid_spec = pltpu.PrefetchScalarGridSpec(
        num_scalar_prefetch=2,
        grid=(b, ns),
        in_specs=[pl.BlockSpec((1, tm, d), lambda bi, si, dest, zt: (bi, si, 0)),
                  pl.BlockSpec((1, d), lambda bi, si, dest, zt: (0, 0)),
                  vec, vec],
        out_specs=pl.BlockSpec(memory_space=pl.ANY),
        scratch_shapes=[pltpu.VMEM((2, tm, d), F32), pltpu.VMEM((TM_EXPERT, d), F32),
                        pltpu.SemaphoreType.DMA((2,)), pltpu.SemaphoreType.DMA],
    )
    return pl.pallas_call(
        functools.partial(_scatter_kernel, tm=tm, ns=ns),
        grid_spec=grid_spec,
        out_shape=jax.ShapeDtypeStruct((m_pad, d), F32),
        compiler_params=_cparams(("arbitrary", "arbitrary")),
        name="moe_scatter",
    )(dest, zero_tile, h, gain.reshape(1, d), scale.reshape(b, 1, d), shift.reshape(b, 1, d))


def _expert_kernel(te_ref, tv_ref, tx_ref, x_ref, wg_ref, wu_ref, wd_ref, o_ref):
    t = pl.program_id(0)
    f = pl.program_id(1)

    @pl.when(tv_ref[t] == 1)
    def _():
        x = x_ref[...].astype(BF16)
        hid = (_silu(_dot(x, wg_ref[0])) * _dot(x, wu_ref[0])).astype(BF16)
        y = _dot(hid, wd_ref[0])

        @pl.when(f == 0)
        def _():
            o_ref[...] = y

        @pl.when(f > 0)
        def _():
            o_ref[...] = o_ref[...] + y

    @pl.when((tv_ref[t] == 0) & (f == 0))
    def _():
        o_ref[...] = jnp.zeros_like(o_ref)


def _experts(tile_expert, tile_valid, tile_x, xs, w_gate, w_up, w_down):
    m_pad, d = xs.shape
    tm = TM_EXPERT
    n_tiles = tile_expert.shape[0]
    ff = w_gate.shape[2]
    fs = FF_STEPS_EXPERT
    tf = ff // fs

    def ff_idx(f, tv, t):
        return f * tv[t] + (fs - 1) * (1 - tv[t])

    grid_spec = pltpu.PrefetchScalarGridSpec(
        num_scalar_prefetch=3,
        grid=(n_tiles, fs),
        in_specs=[pl.BlockSpec((tm, d), lambda t, f, te, tv, tx: (tx[t], 0)),
                  pl.BlockSpec((1, d, tf), lambda t, f, te, tv, tx: (te[t], 0, ff_idx(f, tv, t))),
                  pl.BlockSpec((1, d, tf), lambda t, f, te, tv, tx: (te[t], 0, ff_idx(f, tv, t))),
                  pl.BlockSpec((1, tf, d), lambda t, f, te, tv, tx: (te[t], ff_idx(f, tv, t), 0))],
        out_specs=pl.BlockSpec((tm, d), lambda t, f, te, tv, tx: (t, 0)),
    )
    return pl.pallas_call(
        _expert_kernel,
        grid_spec=grid_spec,
        out_shape=jax.ShapeDtypeStruct((m_pad, d), F32),
        compiler_params=_cparams(("arbitrary", "arbitrary")),
        name="moe_experts",
    )(tile_expert, tile_valid, tile_x, xs,
      w_gate.astype(BF16), w_up.astype(BF16), w_down.astype(BF16))


def _combine_kernel(dest_ref, y_ref, mf_ref, h_ref, gain_ref, gate_ref, o_ref, ybuf, sems, *, tm, ns):
    step = pl.program_id(0) * ns + pl.program_id(1)
    nsteps = pl.num_programs(0) * ns
    slot = lax.rem(step, 2)

    def issue(st, sl):
        base = st * tm

        def body(i, carry):
            t = 2 * (base + i)
            pltpu.make_async_copy(y_ref.at[pl.ds(dest_ref[t], 1), :],
                                  ybuf.at[sl, 0, pl.ds(i, 1), :], sems.at[sl]).start()
            pltpu.make_async_copy(y_ref.at[pl.ds(dest_ref[t + 1], 1), :],
                                  ybuf.at[sl, 1, pl.ds(i, 1), :], sems.at[sl]).start()
            return carry

        lax.fori_loop(0, tm, body, 0, unroll=DMA_UNROLL)

    @pl.when(step == 0)
    def _():
        issue(0, 0)

    def run(sl):
        @pl.when(step + 1 < nsteps)
        def _():
            issue(step + 1, 1 - sl)

        for k in range(2):
            pltpu.make_async_copy(ybuf.at[sl, k], ybuf.at[sl, k], sems.at[sl]).wait()
        mf = mf_ref[...]
        y = mf[:, 0:1] * ybuf[sl, 0] + mf[:, 1:2] * ybuf[sl, 1]
        o_ref[0] = h_ref[0] + gate_ref[0] * (_rms(y) * gain_ref[...])

    for sl in range(2):
        pl.when(slot == sl)(functools.partial(run, sl))


def _combine(dest, ys, mf, h, gain, gate):
    b, s, d = h.shape
    tm = TM_COMBINE
    ns = s // tm
    grid_spec = pltpu.PrefetchScalarGridSpec(
        num_scalar_prefetch=1,
        grid=(b, ns),
        in_specs=[pl.BlockSpec(memory_space=pl.ANY),
                  pl.BlockSpec((tm, LANES), lambda bi, si, dest: (bi * ns + si, 0)),
                  pl.BlockSpec((1, tm, d), lambda bi, si, dest: (bi, si, 0)),
                  pl.BlockSpec((1, d), lambda bi, si, dest: (0, 0)),
                  pl.BlockSpec((1, 1, d), lambda bi, si, dest: (bi, 0, 0))],
        out_specs=pl.BlockSpec((1, tm, d), lambda bi, si, dest: (bi, si, 0)),
        scratch_shapes=[pltpu.VMEM((2, 2, tm, d), F32), pltpu.SemaphoreType.DMA((2,))],
    )
    return pl.pallas_call(
        functools.partial(_combine_kernel, tm=tm, ns=ns),
        grid_spec=grid_spec,
        out_shape=jax.ShapeDtypeStruct((b, s, d), F32),
        compiler_params=_cparams(("arbitrary", "arbitrary")),
        name="moe_combine",
    )(dest, ys, mf, h, gain.reshape(1, d), gate.reshape(b, 1, d))


def _mixer_tail_moe(o, w_out, h, gain1, gate1, gain_in, scale, shift, router_w, w_gate, w_up, w_down,
                    gain_out, gate):
    b, s, d = h.shape
    n = b * s
    tm = TM_EXPERT
    h, mi, mf, cnt = _mixer_tail_router(o, w_out, h, gain1, gate1, gain_in, scale, shift, router_w)
    counts = cnt[0, :N_EXPERTS].astype(I32)
    tiles_per = (counts + tm - 1) // tm
    seg_start = (jnp.cumsum(tiles_per) - tiles_per) * tm
    dest = (seg_start[mi[:, 0:2]] + mi[:, 2:4]).reshape(2 * n)
    n_tiles = (2 * n) // tm + N_EXPERTS
    m_pad = n_tiles * tm
    tile_end = jnp.cumsum(tiles_per)
    tidx = jnp.arange(n_tiles, dtype=I32)
    tile_valid = (tidx < tile_end[-1]).astype(I32)
    tile_expert = jnp.minimum(jnp.searchsorted(tile_end, tidx, side="right"), N_EXPERTS - 1).astype(I32)
    tile_x = jnp.minimum(tidx, tile_end[-1] - 1)
    tail = tile_end[-1] + jnp.arange(N_EXPERTS, dtype=I32)
    zero_tile = jnp.concatenate([jnp.where(tiles_per > 0, tile_end - 1, -1),
                                 jnp.where(tail < n_tiles, tail, -1)]).astype(I32)
    xs = _scatter(dest, zero_tile, h, gain_in, scale, shift, m_pad)
    ys = _experts(tile_expert, tile_valid, tile_x, xs, w_gate, w_up, w_down)
    return _combine(dest, ys, mf, h, gain_out, gate)


def kernel(x, c, mod_w, mod_b, norm_g, attn_in_w_even, fox_gate_bias, attn_out_w_even,
           attn_in_w_odd, attn_out_w_odd, rel_bias_table, ffn_w_gate, ffn_w_up, ffn_w_down,
           router_w, exp_w_gate, exp_w_up, exp_w_down):
    depth = mod_w.shape[0]
    s_len = x.shape[1]
    mods = _mods(c, mod_w, mod_b)
    dil_bias, moba_bias = _bias_tiles(rel_bias_table, s_len)
    h = x
    for layer in range(depth):
        j = layer // 2
        sh1, sc1, g1, sh2, sc2, g2 = jnp.split(mods[layer], 6, axis=-1)
        gains = norm_g[layer]
        if layer % 2 == 0:
            fox_in, moba_in = _inproj_even(h, gains[0], sc1, sh1, attn_in_w_even[j], fox_gate_bias[j])
            o_parts = [_fox_attention(*fox_in), _moba_attention(*moba_in, moba_bias)]
            h = _mixer_tail_ffn(o_parts, attn_out_w_even[j], h, gains[1], g1, gains[2], sc2, sh2,
                                ffn_w_gate[j], ffn_w_up[j], ffn_w_down[j], gains[3], g2)
        else:
            q, k, v = _inproj_odd(h, gains[0], sc1, sh1, attn_in_w_odd[j])
            o = _dilated_attention(q, k, v, dil_bias)
            h = _mixer_tail_moe(o, attn_out_w_odd[j], h, gains[1], g1, gains[2], sc2, sh2, router_w[j],
                                exp_w_gate[j], exp_w_up[j], exp_w_down[j], gains[3], g2)
    return h
```

```python
import functools
import math

import numpy as np
import jax
import jax.numpy as jnp
from jax import lax
from jax.experimental import pallas as pl
from jax.experimental.pallas import tpu as pltpu

F32 = jnp.float32
BF16 = jnp.bfloat16
I32 = jnp.int32

HEAD_DIM = 64
LANES = 128
N_HEADS = 16
N_HEADS_FOX = 8
ATTN_SCALE = HEAD_DIM ** -0.5
LOG2E = math.log2(math.e)
Q_SCALE = ATTN_SCALE * LOG2E
N_DECAY_PIECES = 3
MOBA_BLOCK = 256
MOBA_TOPK = 3
DIL_PATTERNS = ((128, 1), (512, 4), (2048, 16))
NUM_BUCKETS = 32
MAX_DISTANCE = 2048
N_EXPERTS = 8
NORM_EPS = 1e-6
NEG_INF = float("-inf")
MASK_BIG = 1e30

VMEM_LIMIT = 56 * 1024 * 1024

TM_PROJ = 512
TM_FFN = 512
TQ_FOX = 1024
TQ_MOBA = 1024
TK_FOX = 512
FOX_PAIRS_PER_STEP = 2
MOBA_PAIRS_PER_STEP = 2
TM_ROUTE = 512
TM_SCATTER = 256
TM_EXPERT = 512
TM_COMBINE = 256
FF_STEPS_EXPERT = 2
DMA_UNROLL = True
DIL_CHUNKS_PER_STEP = 16


def _cparams(sem):
    return pltpu.CompilerParams(dimension_semantics=sem, vmem_limit_bytes=VMEM_LIMIT)


def _t5_bucket_np(n):
    n = np.maximum(n, 0)
    max_exact = NUM_BUCKETS // 2
    nf = np.maximum(n, 1).astype(np.float64)
    large = max_exact + (np.log(nf / max_exact) / math.log(MAX_DISTANCE / max_exact)
                         * (NUM_BUCKETS - max_exact)).astype(np.int64)
    large = np.minimum(large, NUM_BUCKETS - 1)
    return np.where(n < max_exact, n, large)


_MAX_DIST = 1 << 16
_BUCKET_OF = _t5_bucket_np(np.arange(_MAX_DIST))
_BUCKET_THR = [int(np.searchsorted(_BUCKET_OF, k, side="left")) for k in range(NUM_BUCKETS)]


def _bias_from_dist(tab_ref, h, dist, dlo, dhi):
    lo_b = int(_BUCKET_OF[max(dlo, 0)])
    hi_b = int(_BUCKET_OF[dhi])
    val = jnp.zeros(dist.shape, F32) + tab_ref[lo_b, h]
    for k in range(lo_b + 1, hi_b + 1):
        val = jnp.where(dist >= _BUCKET_THR[k], tab_ref[k, h], val)
    return val


def _dil_bias_kernel(tab_ref, o_ref):
    h = pl.program_id(0)
    for g, (window, dil) in enumerate(DIL_PATTERNS):
        span = window // dil
        i = lax.broadcasted_iota(I32, (span, 2 * span), 0)
        j = lax.broadcasted_iota(I32, (span, 2 * span), 1)
        rel = i + span - j
        val = _bias_from_dist(tab_ref, h, rel * dil, 0, span * dil) * LOG2E
        band = jnp.where(rel >= 0, jnp.where(rel <= span, val, NEG_INF), NEG_INF)
        o_ref[0, 2 * g] = band
        o_ref[0, 2 * g + 1] = jnp.where(j >= span, band, NEG_INF)


def _moba_bias_kernel(tab_ref, o_ref, *, n_blk, head0):
    h = pl.program_id(0) + head0
    i = lax.broadcasted_iota(I32, (MOBA_BLOCK, MOBA_BLOCK), 0)
    j = lax.broadcasted_iota(I32, (MOBA_BLOCK, MOBA_BLOCK), 1)
    for d in range(n_blk):
        dist = d * MOBA_BLOCK + i - j
        val = _bias_from_dist(tab_ref, h, dist, d * MOBA_BLOCK - (MOBA_BLOCK - 1),
                              d * MOBA_BLOCK + (MOBA_BLOCK - 1)) * LOG2E
        if d == 0:
            val = jnp.where(dist >= 0, val, NEG_INF)
        o_ref[0, d] = val


def _bias_tiles(rel_bias_table, s_len):
    n_blk = s_len // MOBA_BLOCK
    span = DIL_PATTERNS[0][0]
    n_var = 2 * len(DIL_PATTERNS)
    smem = pl.BlockSpec(memory_space=pltpu.SMEM)
    dil = pl.pallas_call(
        _dil_bias_kernel,
        grid=(N_HEADS,),
        in_specs=[smem],
        out_specs=pl.BlockSpec((1, n_var, span, 2 * span), lambda h: (h, 0, 0, 0)),
        out_shape=jax.ShapeDtypeStruct((N_HEADS, n_var, span, 2 * span), F32),
        compiler_params=_cparams(("parallel",)),
        name="dil_bias",
    )(rel_bias_table)
    n_moba = N_HEADS - N_HEADS_FOX
    moba = pl.pallas_call(
        functools.partial(_moba_bias_kernel, n_blk=n_blk, head0=N_HEADS_FOX),
        grid=(n_moba,),
        in_specs=[smem],
        out_specs=pl.BlockSpec((1, n_blk, MOBA_BLOCK, MOBA_BLOCK), lambda h: (h, 0, 0, 0)),
        out_shape=jax.ShapeDtypeStruct((n_moba, n_blk, MOBA_BLOCK, MOBA_BLOCK), F32),
        compiler_params=_cparams(("parallel",)),
        name="moba_bias",
    )(rel_bias_table)
    return dil, moba


def _split_bf16(a):
    hi = a.astype(BF16)
    lo = (a - hi.astype(F32)).astype(BF16)
    return hi, lo


def _dot(a, b):
    return jnp.dot(a, b, preferred_element_type=F32)


def _dot_nt(a, b):
    return lax.dot_general(a, b, (((1,), (1,)), ((), ())), preferred_element_type=F32)


def _dot_split(a, b):
    a_hi, a_lo = _split_bf16(a)
    b_hi, b_lo = _split_bf16(b)
    return _dot(a_hi, b_hi) + (_dot(a_hi, b_lo) + _dot(a_lo, b_hi))


def _rms(x):
    return x * lax.rsqrt(jnp.mean(x * x, axis=-1, keepdims=True) + NORM_EPS)


def _modulate(x, gain, scale, shift):
    return (_rms(x) * gain) * (1.0 + scale) + shift


def _silu(x):
    return x * jax.nn.sigmoid(x)


def _mods_kernel(c_ref, w_ref, b_ref, o_ref):
    o_ref[0] = _dot_split(_silu(c_ref[...]), w_ref[0]) + b_ref[0]


def _mods(c, mod_w, mod_b):
    depth, d, e = mod_w.shape
    b = c.shape[0]
    tn = 1536
    return pl.pallas_call(
        _mods_kernel,
        grid=(depth, e // tn),
        in_specs=[pl.BlockSpec((b, d), lambda l, j: (0, 0)),
                  pl.BlockSpec((1, d, tn), lambda l, j: (l, 0, j)),
                  pl.BlockSpec((1, 1, tn), lambda l, j: (l, 0, j))],
        out_specs=pl.BlockSpec((1, b, tn), lambda l, j: (l, 0, j)),
        out_shape=jax.ShapeDtypeStruct((depth, b, e), F32),
        compiler_params=_cparams(("parallel", "parallel")),
        name="adaln_mods",
    )(c, mod_w, mod_b.reshape(depth, 1, e))


def _inproj_even_kernel(h_ref, g_ref, sc_ref, sh_ref, w_ref, wf_ref, gb_ref,
                        qa_ref, k0a_ref, k1a_ref, v0a_ref, v1a_ref,
                        qb_ref, k0b_ref, k1b_ref, v0b_ref, v1b_ref, carry_ref, *, tm):
    si = pl.program_id(1)
    u = _modulate(h_ref[0], g_ref[...], sc_ref[0], sh_ref[0]).astype(BF16)
    width = qa_ref.shape[-1]
    n_pairs = width // LANES

    def proj(i):
        return _dot(u, w_ref[:, i * width:(i + 1) * width])

    lane = lax.broadcasted_iota(I32, (1, LANES), 1)
    left = lane < HEAD_DIM
    row = lax.broadcasted_iota(I32, (tm, LANES), 0)

    x = _dot(u, wf_ref[...]) + gb_ref[...]
    lf = jnp.where(lane < N_HEADS_FOX, jnp.minimum(x, 0.0) - jnp.log1p(jnp.exp(-jnp.abs(x))), 0.0)
    k = 1
    while k < tm:
        lf = lf + jnp.where(row >= k, pltpu.roll(lf, k, axis=0), 0.0)
        k *= 2

    @pl.when(si == 0)
    def _():
        carry_ref[...] = jnp.zeros_like(carry_ref)

    cum = lf + carry_ref[0:1, :]
    carry_ref[...] = jnp.broadcast_to(cum[tm - 1:tm, :], carry_ref.shape)
    rest = cum * (-LOG2E)
    decay = jnp.zeros((tm, LANES), F32)
    for p in range(N_DECAY_PIECES):
        piece = rest.astype(BF16).astype(F32)
        rest = rest - piece
        decay = decay + (pltpu.roll(piece, p * N_HEADS_FOX, axis=1) if p else piece)
    decay_lo = decay.astype(BF16)
    decay_hi = pltpu.roll(decay, HEAD_DIM, axis=1).astype(BF16)

    one_lo = jnp.where(lane == 0, 1.0, 0.0).astype(BF16)
    one_hi = jnp.where(lane == HEAD_DIM, 1.0, 0.0).astype(BF16)
    blk = (si * tm + row) // MOBA_BLOCK
    blk_lo = jnp.where(lane == blk, 1.0, 0.0).astype(BF16)
    blk_hi = jnp.where(lane == blk + HEAD_DIM, 1.0, 0.0).astype(BF16)

    def emit(first, k0_ref, k1_ref, v0_ref, v1_ref, k_lo, k_hi):
        kk = proj(first + 1).astype(BF16)
        vv = proj(first + 2).astype(BF16)
        for hp in range(n_pairs):
            sl = slice(hp * LANES, (hp + 1) * LANES)
            k0_ref[0, :, sl] = jnp.where(left, kk[:, sl], k_hi)
            k1_ref[0, :, sl] = jnp.where(left, k_lo, kk[:, sl])
            v0_ref[0, :, sl] = jnp.where(left, vv[:, sl], one_hi)
            v1_ref[0, :, sl] = jnp.where(left, one_lo, vv[:, sl])

    qa_ref[0] = (proj(0) * Q_SCALE).astype(BF16)
    emit(0, k0a_ref, k1a_ref, v0a_ref, v1a_ref, decay_lo, decay_hi)
    qb_ref[0] = (proj(3) * Q_SCALE).astype(BF16)
    emit(3, k0b_ref, k1b_ref, v0b_ref, v1b_ref, blk_lo, blk_hi)


def _inproj_even(h, gain, scale, shift, w_in, gate_bias):
    b, s, d = h.shape
    tm = TM_PROJ
    da = N_HEADS_FOX * HEAD_DIM
    assert s // MOBA_BLOCK <= HEAD_DIM and N_DECAY_PIECES * N_HEADS_FOX <= HEAD_DIM
    cuts = np.cumsum([da, da, da, N_HEADS_FOX, da, da]).tolist()
    q_a, k_a, v_a, f_a, q_b, k_b, v_b = jnp.split(w_in, cuts, axis=1)
    w = jnp.concatenate([q_a, k_a, v_a, q_b, k_b, v_b], axis=1).astype(BF16)
    wf = jnp.pad(f_a, ((0, 0), (0, LANES - N_HEADS_FOX))).astype(BF16)
    gb = jnp.pad(gate_bias.astype(F32), (0, LANES - N_HEADS_FOX)).reshape(1, LANES)
    act = jax.ShapeDtypeStruct((b, s, da), BF16)
    act_spec = pl.BlockSpec((1, tm, da), lambda bi, si: (bi, si, 0))
    vec = pl.BlockSpec((1, 1, d), lambda bi, si: (bi, 0, 0))
    outs = pl.pallas_call(
        functools.partial(_inproj_even_kernel, tm=tm),
        grid=(b, s // tm),
        in_specs=[pl.BlockSpec((1, tm, d), lambda bi, si: (bi, si, 0)),
                  pl.BlockSpec((1, d), lambda bi, si: (0, 0)),
                  vec, vec,
                  pl.BlockSpec(w.shape, lambda bi, si: (0, 0)),
                  pl.BlockSpec(wf.shape, lambda bi, si: (0, 0)),
                  pl.BlockSpec(gb.shape, lambda bi, si: (0, 0))],
        out_specs=[act_spec] * 10,
        out_shape=[act] * 10,
        scratch_shapes=[pltpu.VMEM((8, LANES), F32)],
        compiler_params=_cparams(("parallel", "arbitrary")),
        name="inproj_even",
    )(h, gain.reshape(1, d), scale.reshape(b, 1, d), shift.reshape(b, 1, d), w, wf, gb)
    return outs[:5], outs[5:]


def _inproj_odd_kernel(h_ref, g_ref, sc_ref, sh_ref, w_ref, q_ref, k_ref, v_ref):
    u = _modulate(h_ref[0], g_ref[...], sc_ref[0], sh_ref[0]).astype(BF16)
    width = q_ref.shape[-1]
    q_ref[0] = _dot(u, w_ref[:, 0:width]) * Q_SCALE
    k_ref[0] = _dot(u, w_ref[:, width:2 * width])
    v_ref[0] = _dot(u, w_ref[:, 2 * width:3 * width])


def _inproj_odd(h, gain, scale, shift, w_in):
    b, s, d = h.shape
    tm = TM_PROJ
    dq = w_in.shape[1] // 3
    act = jax.ShapeDtypeStruct((b, s, dq), F32)
    act_spec = pl.BlockSpec((1, tm, dq), lambda bi, si: (bi, si, 0))
    vec = pl.BlockSpec((1, 1, d), lambda bi, si: (bi, 0, 0))
    return pl.pallas_call(
        _inproj_odd_kernel,
        grid=(b, s // tm),
        in_specs=[pl.BlockSpec((1, tm, d), lambda bi, si: (bi, si, 0)),
                  pl.BlockSpec((1, d), lambda bi, si: (0, 0)),
                  vec, vec,
                  pl.BlockSpec(w_in.shape, lambda bi, si: (0, 0))],
        out_specs=[act_spec] * 3,
        out_shape=[act] * 3,
        compiler_params=_cparams(("parallel", "parallel")),
        name="inproj_odd",
    )(h, gain.reshape(1, d), scale.reshape(b, 1, d), shift.reshape(b, 1, d), w_in.astype(BF16))


def _tile_lanes(x, width):
    return jnp.concatenate([x] * (width // LANES), axis=1)


def _flash_update(s, v, m_ref, acc_ref):
    m_prev = m_ref[...]
    m_new = jnp.maximum(m_prev, jnp.max(s, axis=1, keepdims=True))
    p = jnp.exp2(s - _tile_lanes(m_new, s.shape[1]))
    acc_ref[...] = jnp.exp2(m_prev - m_new) * acc_ref[...] + _dot(p.astype(BF16), v)
    m_ref[...] = m_new


def _finish_pair(acc_ref, left, first=0):
    acc0 = acc_ref[first]
    acc1 = acc_ref[first + 1]
    return jnp.where(left, acc0 / acc0[:, HEAD_DIM:HEAD_DIM + 1], acc1 / acc1[:, 0:1])


def _fox_kernel(q_ref, k0_ref, k1_ref, v0_ref, v1_ref, o_ref, m_ref, acc_ref, *, tq, tk, n_pairs):
    qi = pl.program_id(2)
    lane = lax.broadcasted_iota(I32, (1, LANES), 1)
    left = lane < HEAD_DIM

    def piece_lanes(lane0):
        hit = lane == lane0
        for p in range(1, N_DECAY_PIECES):
            hit = jnp.logical_or(hit, lane == lane0 + p * N_HEADS_FOX)
        return jnp.where(hit, 1.0, 0.0).astype(BF16)

    chains = []
    for pr in range(n_pairs):
        hp = pl.program_id(1) * n_pairs + pr
        lanes = slice(pr * LANES, (pr + 1) * LANES)
        q = q_ref[0, :, lanes]
        chains.append((jnp.where(left, q, piece_lanes(HEAD_DIM + 2 * hp)), k0_ref, v0_ref, lanes))
        chains.append((jnp.where(left, piece_lanes(2 * hp + 1), q), k1_ref, v1_ref, lanes))
    row = lax.broadcasted_iota(I32, (tq, tk), 0)
    col = lax.broadcasted_iota(I32, (tq, tk), 1)
    m_ref[...] = jnp.full(m_ref.shape, NEG_INF, F32)
    acc_ref[...] = jnp.zeros(acc_ref.shape, F32)
    n_sub = tq // tk

    def step(kv, mask, row0=0):
        off = pl.multiple_of(kv * tk, tk)
        rows = pl.ds(row0, tq - row0)
        for c, (qc, k_ref, v_ref, lanes) in enumerate(chains):
            s = _dot_nt(qc[row0:], k_ref[0, pl.ds(off, tk), lanes])
            if mask is not None:
                s = jnp.where(mask[row0:], s, NEG_INF)
            _flash_update(s, v_ref[0, pl.ds(off, tk), lanes], m_ref.at[c, rows], acc_ref.at[c, rows])

    def body(kv, carry):
        step(kv, None)
        return carry

    lax.fori_loop(0, qi * n_sub, body, 0)
    for d in range(n_sub):
        step(qi * n_sub + d, col + d * tk <= row, d * tk)
    for pr in range(n_pairs):
        o_ref[0, :, pr * LANES:(pr + 1) * LANES] = _finish_pair(acc_ref, left, 2 * pr).astype(o_ref.dtype)


def _fox_attention(q, k0, k1, v0, v1):
    b, s, da = q.shape
    n_pairs = FOX_PAIRS_PER_STEP
    width = n_pairs * LANES
    tq = min(TQ_FOX, s)
    kv = pl.BlockSpec((1, s, width), lambda bi, h, qi: (bi, 0, h))
    return pl.pallas_call(
        functools.partial(_fox_kernel, tq=tq, tk=min(TK_FOX, tq), n_pairs=n_pairs),
        grid=(b, da // width, s // tq),
        in_specs=[pl.BlockSpec((1, tq, width), lambda bi, h, qi: (bi, qi, h)), kv, kv, kv, kv],
        out_specs=pl.BlockSpec((1, tq, width), lambda bi, h, qi: (bi, qi, h)),
        out_shape=jax.ShapeDtypeStruct((b, s, da), BF16),
        scratch_shapes=[pltpu.VMEM((2 * n_pairs, tq, LANES), F32)] * 2,
        compiler_params=_cparams(("parallel", "parallel", "arbitrary")),
        name="fox_attention",
    )(q, k0, k1, v0, v1)


def _moba_kernel(q_ref, k0_ref, k1_ref, v0_ref, v1_ref, bias_ref, o_ref, km_ref, m_ref, acc_ref,
                 *, n_blk, tq, n_pairs):
    blk = MOBA_BLOCK
    tk = 2 * blk
    nq = tq // blk
    a = pl.program_id(2)
    gate_lane0 = (HEAD_DIM, 0)
    lane = lax.broadcasted_iota(I32, (1, LANES), 1)
    left = lane < HEAD_DIM
    mine = (left, jnp.logical_not(left))
    k_refs = (k0_ref, k1_ref)
    v_refs = (v0_ref, v1_ref)

    @pl.when(a == 0)
    def _():
        km_ref[...] = jnp.zeros_like(km_ref)
        for pr in range(n_pairs):
            lanes = slice(pr * LANES, (pr + 1) * LANES)
            for n in range(n_blk):
                rows = slice(n * blk, (n + 1) * blk)
                kb = jnp.where(left, k0_ref[0, rows, lanes], k1_ref[0, rows, lanes]).astype(F32)
                mean = jnp.sum(kb, axis=0, keepdims=True) * (1.0 / blk)
                for lane0 in gate_lane0:
                    km_ref[pr, lane0 + n:lane0 + n + 1, :] = mean

    nb = -(-n_blk // 8) * 8
    blkf = lax.broadcasted_iota(I32, (nb, tq), 0).astype(F32)
    own = (lax.broadcasted_iota(I32, (nb, tq), 1) // blk + a * nq).astype(F32)
    chains = []
    for pr in range(n_pairs):
        lanes = slice(pr * LANES, (pr + 1) * LANES)
        q = q_ref[0, :, lanes]
        km_hi, km_lo = _split_bf16(km_ref[pr])
        for j in range(2):
            qj = jnp.where(mine[j], q, jnp.zeros_like(q))
            lane0 = gate_lane0[j]
            gate = (_dot_nt(km_hi, qj) + _dot_nt(km_lo, qj))[lane0:lane0 + nb]
            gate = jnp.where(blkf < own, gate, NEG_INF)
            pen = jnp.where(blkf == own, 0.0, -MASK_BIG)
            for _ in range(MOBA_TOPK):
                mx = jnp.max(gate, axis=0, keepdims=True)
                cand = jnp.where(gate == mx, jnp.where(mx > NEG_INF, blkf, float(LANES)), float(LANES))
                pick = blkf == jnp.min(cand, axis=0, keepdims=True)
                pen = jnp.where(pick, 0.0, pen)
                gate = jnp.where(pick, NEG_INF, gate)
            parts = [pen, jnp.zeros((LANES - lane0 - nb, tq), F32)]
            if lane0:
                parts.insert(0, jnp.zeros((lane0, tq), F32))
            pen_q = jnp.concatenate(parts, axis=0).T
            chains.append((jnp.where(mine[j], q, pen_q.astype(BF16)), k_refs[j], v_refs[j], lanes, 2 * pr + j))

    m_ref[...] = jnp.full(m_ref.shape, NEG_INF, F32)
    acc_ref[...] = jnp.zeros(acc_ref.shape, F32)

    def step(i, dist, r0):
        off = pl.multiple_of(i * tk, tk)
        rows = pl.ds(r0 * blk, tq - r0 * blk)
        for qc, k_ref, v_ref, lanes, h in chains:
            bias = jnp.concatenate(
                [jnp.concatenate([bias_ref[h, dist(r, c)] for c in range(2)], axis=1)
                 for r in range(r0, nq)], axis=0)
            s = _dot_nt(qc[r0 * blk:], k_ref[0, pl.ds(off, tk), lanes]) + bias
            _flash_update(s, v_ref[0, pl.ds(off, tk), lanes], m_ref.at[h, rows], acc_ref.at[h, rows])

    n_full = a * (nq // 2)

    def body(i, carry):
        step(i, lambda r, c: a * nq + r - 2 * i - c, 0)
        return carry

    lax.fori_loop(0, n_full, body, 0)
    for e in range(nq // 2):
        step(n_full + e, lambda r, c, e=e: max(r - 2 * e - c, 0), 2 * e)
    for pr in range(n_pairs):
        o_ref[0, :, pr * LANES:(pr + 1) * LANES] = _finish_pair(acc_ref, left, 2 * pr).astype(o_ref.dtype)


def _moba_attention(q, k0, k1, v0, v1, bias_tiles):
    b, s, db = q.shape
    n_pairs = MOBA_PAIRS_PER_STEP
    width = n_pairs * LANES
    blk = MOBA_BLOCK
    n_blk = s // blk
    tq = min(TQ_MOBA, s)
    assert n_blk <= HEAD_DIM, "block gates of one head must fit in the other head's lanes"
    assert s % tq == 0 and tq % (2 * blk) == 0
    kv = pl.BlockSpec((1, s, width), lambda h, bi, qi: (bi, 0, h))
    return pl.pallas_call(
        functools.partial(_moba_kernel, n_blk=n_blk, tq=tq, n_pairs=n_pairs),
        grid=(db // width, b, s // tq),
        in_specs=[pl.BlockSpec((1, tq, width), lambda h, bi, qi: (bi, qi, h)), kv, kv, kv, kv,
                  pl.BlockSpec((2 * n_pairs, n_blk, blk, blk), lambda h, bi, qi: (h, 0, 0, 0),
                               pipeline_mode=pl.Buffered(1))],
        out_specs=pl.BlockSpec((1, tq, width), lambda h, bi, qi: (bi, qi, h)),
        out_shape=jax.ShapeDtypeStruct((b, s, db), BF16),
        scratch_shapes=[pltpu.VMEM((n_pairs, LANES, LANES), F32)]
                       + [pltpu.VMEM((2 * n_pairs, tq, LANES), F32)] * 2,
        compiler_params=_cparams(("parallel", "parallel", "arbitrary")),
        name="moba_attention",
    )(q, k0, k1, v0, v1, bias_tiles)


def _dilated_kernel(q_ref, k_ref, v_ref, bias_ref, o_ref, m_ref, acc_ref, *, s_len):
    lane = lax.broadcasted_iota(I32, (1, LANES), 1)
    left = lane < HEAD_DIM
    one_lo = jnp.where(lane == 0, 1.0, 0.0).astype(BF16)
    one_hi = jnp.where(lane == HEAD_DIM, 1.0, 0.0).astype(BF16)
    order = sorted(range(len(DIL_PATTERNS)), key=lambda i: -DIL_PATTERNS[i][1])
    for g in order:
        window, dil = DIL_PATTERNS[g]
        merge = g != order[0]
        span = window // dil
        unit = span * dil
        nc = s_len // unit
        n_u = min(DIL_CHUNKS_PER_STEP, nc)
        groups = nc // n_u
        n_res = min(DIL_CHUNKS_PER_STEP // n_u, dil)

        def rows(ref, start, dil=dil, span=span):
            if dil == 1:
                return ref[0, pl.ds(start, span), :]
            return ref[0, pl.ds(start, span, stride=dil), :]

        def get(ref, j, start, dil=dil, span=span):
            if dil == 1:
                return ref[j, pl.ds(start, span), :]
            return ref[j, pl.ds(start, span, stride=dil), :]

        def put(ref, j, start, val, dil=dil, span=span):
            if dil == 1:
                ref[j, pl.ds(start, span), :] = val
            else:
                ref[j, pl.ds(start, span, stride=dil), :] = val

        def body(it, carry, g=g, merge=merge, n_u=n_u, n_res=n_res, groups=groups, unit=unit, rows=rows,
                 put=put, get=get):
            r0 = (it // groups) * n_res
            grp = it - (it // groups) * groups
            is_first = grp == 0
            results = []
            for dr in range(n_res):
                start0 = r0 + dr + grp * (n_u * unit)
                prev0 = start0 - jnp.where(is_first, 0, unit)
                starts = [start0 + u * unit for u in range(n_u)]
                kc = [rows(k_ref, st).astype(BF16) for st in [prev0] + starts]
                vc = [rows(v_ref, st).astype(BF16) for st in [prev0] + starts]
                vcs = ([jnp.where(left, v, one_hi) for v in vc], [jnp.where(left, one_lo, v) for v in vc])
                for u, start in enumerate(starts):
                    var = 2 * g + jnp.where(is_first, 1, 0) if u == 0 else 2 * g
                    q = rows(q_ref, start)
                    kb = jnp.concatenate([kc[u], kc[u + 1]], axis=0)
                    qq = jnp.concatenate([jnp.where(left, q, 0.0), jnp.where(left, 0.0, q)],
                                         axis=0).astype(BF16)
                    s_both = _dot_nt(qq, kb)
                    for j in range(2):
                        s = s_both[j * span:(j + 1) * span] + bias_ref[j, var]
                        m_new = jnp.broadcast_to(jnp.max(s, axis=1, keepdims=True), (span, LANES))
                        if merge:
                            m_prev = get(m_ref, j, start)
                            m_new = jnp.maximum(m_new, m_prev)
                        p = jnp.exp2(s - _tile_lanes(m_new, 2 * span))
                        acc_new = _dot(p.astype(BF16), jnp.concatenate([vcs[j][u], vcs[j][u + 1]], axis=0))
                        if merge:
                            acc_new = jnp.exp2(m_prev - m_new) * get(acc_ref, j, start) + acc_new
                        results.append((j, start, m_new, acc_new))
            for j, start, m_new, acc_new in results:
                put(m_ref, j, start, m_new)
                put(acc_ref, j, start, acc_new)
            return carry

        lax.fori_loop(0, (dil // n_res) * groups, body, 0)
    o_ref[0] = _finish_pair(acc_ref, left).astype(o_ref.dtype)


def _dilated_attention(q, k, v, bias_tiles):
    b, s, dq = q.shape
    hp = dq // LANES
    for window, dil in DIL_PATTERNS:
        assert s % window == 0, "sequence must be a whole number of dilated units"
    qkv = pl.BlockSpec((1, s, LANES), lambda h, bi: (bi, 0, h))
    n_var, span, band = bias_tiles.shape[1:]
    return pl.pallas_call(
        functools.partial(_dilated_kernel, s_len=s),
        grid=(hp, b),
        in_specs=[qkv, qkv, qkv,
                  pl.BlockSpec((2, n_var, span, band), lambda h, bi: (h, 0, 0, 0))],
        out_specs=pl.BlockSpec((1, s, LANES), lambda h, bi: (bi, 0, h)),
        out_shape=jax.ShapeDtypeStruct((b, s, dq), BF16),
        scratch_shapes=[pltpu.VMEM((2, s, LANES), F32)] * 2,
        compiler_params=_cparams(("parallel", "parallel")),
        name="dilated_attention",
    )(q, k, v, bias_tiles)


def _mixer_residual(o_parts, w_parts, h, gain, gate):
    y = _dot(o_parts[0][0], w_parts[0][...])
    for o_ref, w_ref in zip(o_parts[1:], w_parts[1:]):
        y = y + _dot(o_ref[0], w_ref[...])
    return h + gate * (_rms(y) * gain)


def _mixer_operands(o_parts, w_out, tm):
    w_out = w_out.astype(BF16)
    cuts = np.cumsum([p.shape[-1] for p in o_parts])[:-1].tolist()
    w_parts = jnp.split(w_out, cuts, axis=0) if cuts else [w_out]
    specs = [pl.BlockSpec((1, tm, p.shape[-1]), lambda bi, si: (bi, si, 0)) for p in o_parts]
    specs += [pl.BlockSpec(w.shape, lambda bi, si: (0, 0)) for w in w_parts]
    return list(o_parts) + list(w_parts), specs


def _ffn_kernel(*refs, n_parts):
    o_parts = refs[:n_parts]
    w_parts = refs[n_parts:2 * n_parts]
    (h_ref, gain1_ref, gate1_ref, g_ref, sc_ref, sh_ref, wg_ref, wu_ref, wd_ref, gain_ref, gate_ref,
     o_ref) = refs[2 * n_parts:]
    h = _mixer_residual(o_parts, w_parts, h_ref[0], gain1_ref[...], gate1_ref[0])
    u = _modulate(h, g_ref[...], sc_ref[0], sh_ref[0]).astype(BF16)
    hid = (_silu(_dot(u, wg_ref[...])) * _dot(u, wu_ref[...])).astype(BF16)
    y = _dot(hid, wd_ref[...])
    o_ref[0] = h + gate_ref[0] * (_rms(y) * gain_ref[...])


def _mixer_tail_ffn(o_parts, w_out, h, gain1, gate1, gain_in, scale, shift, w_gate, w_up, w_down,
                    gain_out, gate):
    b, s, d = h.shape
    tm = TM_FFN
    ff = w_gate.shape[1]
    resident = functools.partial(pl.BlockSpec, pipeline_mode=pl.Buffered(1))
    vec = pl.BlockSpec((1, 1, d), lambda bi, si: (bi, 0, 0))
    row = pl.BlockSpec((1, d), lambda bi, si: (0, 0))
    mix_ops, mix_specs = _mixer_operands(o_parts, w_out, tm)
    return pl.pallas_call(
        functools.partial(_ffn_kernel, n_parts=len(o_parts)),
        grid=(b, s // tm),
        in_specs=mix_specs + [pl.BlockSpec((1, tm, d), lambda bi, si: (bi, si, 0)), row, vec,
                              row, vec, vec,
                              resident((d, ff), lambda bi, si: (0, 0)),
                              resident((d, ff), lambda bi, si: (0, 0)),
                              resident((ff, d), lambda bi, si: (0, 0)),
                              row, vec],
        out_specs=pl.BlockSpec((1, tm, d), lambda bi, si: (bi, si, 0)),
        out_shape=jax.ShapeDtypeStruct((b, s, d), F32),
        compiler_params=_cparams(("parallel", "parallel")),
        name="dense_swiglu",
    )(*mix_ops, h, gain1.reshape(1, d), gate1.reshape(b, 1, d),
      gain_in.reshape(1, d), scale.reshape(b, 1, d), shift.reshape(b, 1, d),
      w_gate.astype(BF16), w_up.astype(BF16), w_down.astype(BF16),
      gain_out.reshape(1, d), gate.reshape(b, 1, d))


def _router_kernel(o_ref, wo_ref, h_ref, gain1_ref, gate1_ref, g_ref, sc_ref, sh_ref, rw_ref,
                   hout_ref, mi_ref, mf_ref, cnt_ref, carry_ref, *, tm):
    @pl.when((pl.program_id(0) == 0) & (pl.program_id(1) == 0))
    def _():
        carry_ref[...] = jnp.zeros_like(carry_ref)

    h = _mixer_residual([o_ref], [wo_ref], h_ref[0], gain1_ref[...], gate1_ref[0])
    hout_ref[0] = h
    u = _modulate(h, g_ref[...], sc_ref[0], sh_ref[0])
    logits = _dot_split(u, rw_ref[...])
    lanef = lax.broadcasted_iota(I32, (tm, LANES), 1).astype(F32)
    lg = jnp.where(lanef < N_EXPERTS, logits, NEG_INF)
    v1 = jnp.max(lg, axis=1, keepdims=True)
    i1 = jnp.min(jnp.where(lg == v1, lanef, float(LANES)), axis=1, keepdims=True)
    lg2 = jnp.where(lanef == i1, NEG_INF, lg)
    v2 = jnp.max(lg2, axis=1, keepdims=True)
    i2 = jnp.min(jnp.where(lg2 == v2, lanef, float(LANES)), axis=1, keepdims=True)
    e2 = jnp.exp(v2 - v1)
    p1 = 1.0 / (1.0 + e2)
    p2 = e2 / (1.0 + e2)
    oh1 = jnp.where(lanef == i1, 1.0, 0.0)
    oh2 = jnp.where(lanef == i2, 1.0, 0.0)
    oh = oh1 + oh2
    r = lax.broadcasted_iota(I32, (tm, tm), 0)
    c = lax.broadcasted_iota(I32, (tm, tm), 1)
    before = jnp.where(c < r, 1.0, 0.0).astype(BF16)
    tot = _dot(before, oh.astype(BF16)) + carry_ref[0:1, :]
    rank1 = jnp.sum(oh1 * tot, axis=1, keepdims=True)
    rank2 = jnp.sum(oh2 * tot, axis=1, keepdims=True)
    carry_ref[...] = carry_ref[...] + jnp.sum(oh, axis=0, keepdims=True)
    mi = jnp.where(lanef == 0.0, i1, jnp.where(lanef == 1.0, i2,
         jnp.where(lanef == 2.0, rank1, jnp.where(lanef == 3.0, rank2, 0.0))))
    mi_ref[...] = mi.astype(I32)
    mf_ref[...] = jnp.where(lanef == 0.0, p1, jnp.where(lanef == 1.0, p2, 0.0))
    cnt_ref[...] = carry_ref[...]


def _mixer_tail_router(o, w_out, h, gain1, gate1, gain, scale, shift, router_w):
    b, s, d = h.shape
    tm = TM_ROUTE
    n = b * s
    ns = s // tm
    rw = jnp.pad(router_w.astype(F32), ((0, 0), (0, LANES - router_w.shape[1])))
    vec = pl.BlockSpec((1, 1, d), lambda bi, si: (bi, 0, 0))
    row = pl.BlockSpec((1, d), lambda bi, si: (0, 0))
    act = pl.BlockSpec((1, tm, d), lambda bi, si: (bi, si, 0))
    meta = pl.BlockSpec((tm, LANES), lambda bi, si: (bi * ns + si, 0))
    mix_ops, mix_specs = _mixer_operands([o], w_out, tm)
    return pl.pallas_call(
        functools.partial(_router_kernel, tm=tm),
        grid=(b, ns),
        in_specs=mix_specs + [act, row, vec, row, vec, vec, pl.BlockSpec(rw.shape, lambda bi, si: (0, 0))],
        out_specs=[act, meta, meta, pl.BlockSpec((8, LANES), lambda bi, si: (0, 0))],
        out_shape=[jax.ShapeDtypeStruct((b, s, d), F32),
                   jax.ShapeDtypeStruct((n, LANES), I32),
                   jax.ShapeDtypeStruct((n, LANES), F32),
                   jax.ShapeDtypeStruct((8, LANES), F32)],
        scratch_shapes=[pltpu.VMEM((8, LANES), F32)],
        compiler_params=_cparams(("arbitrary", "arbitrary")),
        name="moe_router",
    )(*mix_ops, h, gain1.reshape(1, d), gate1.reshape(b, 1, d),
      gain.reshape(1, d), scale.reshape(b, 1, d), shift.reshape(b, 1, d), rw)


def _scatter_kernel(dest_ref, ztile_ref, h_ref, g_ref, sc_ref, sh_ref, xs_ref, ubuf, zbuf, sems, zsem,
                    *, tm, ns):
    step = pl.program_id(0) * ns + pl.program_id(1)
    nsteps = pl.num_programs(0) * ns
    slot = lax.rem(step, 2)
    tz = zbuf.shape[0]

    @pl.when(step == 0)
    def _():
        zbuf[...] = jnp.zeros_like(zbuf)
        for e in range(2 * N_EXPERTS):
            @pl.when(ztile_ref[e] >= 0)
            def _():
                row0 = pl.multiple_of(ztile_ref[e] * tz, tz)
                pltpu.make_async_copy(zbuf, xs_ref.at[pl.ds(row0, tz), :], zsem).start()
        for e in range(2 * N_EXPERTS):
            @pl.when(ztile_ref[e] >= 0)
            def _():
                pltpu.make_async_copy(zbuf, xs_ref.at[pl.ds(0, tz), :], zsem).wait()

    def wait_slot(sl):
        for _ in range(2):
            pltpu.make_async_copy(ubuf.at[sl], ubuf.at[sl], sems.at[sl]).wait()

    base = step * tm

    def run(sl):
        @pl.when(step >= 2)
        def _():
            wait_slot(sl)

        ubuf[sl] = _modulate(h_ref[0], g_ref[...], sc_ref[0], sh_ref[0])

        def issue(i, carry):
            t = 2 * (base + i)
            src = ubuf.at[sl, pl.ds(i, 1), :]
            pltpu.make_async_copy(src, xs_ref.at[pl.ds(dest_ref[t], 1), :], sems.at[sl]).start()
            pltpu.make_async_copy(src, xs_ref.at[pl.ds(dest_ref[t + 1], 1), :], sems.at[sl]).start()
            return carry

        lax.fori_loop(0, tm, issue, 0, unroll=DMA_UNROLL)

        @pl.when(step == nsteps - 1)
        def _():
            wait_slot(sl)

            @pl.when(nsteps >= 2)
            def _():
                wait_slot(1 - sl)

    for sl in range(2):
        pl.when(slot == sl)(functools.partial(run, sl))


def _scatter(dest, zero_tile, h, gain, scale, shift, m_pad):
    b, s, d = h.shape
    tm = TM_SCATTER
    ns = s // tm
    vec = pl.BlockSpec((1, 1, d), lambda bi, si, dest, zt: (bi, 0, 0))
    grid_spec = pltpu.PrefetchScalarGridSpec(
        num_scalar_prefetch=2,
        grid=(b, ns),
        in_specs=[pl.BlockSpec((1, tm, d), lambda bi, si, dest, zt: (bi, si, 0)),
                  pl.BlockSpec((1, d), lambda bi, si, dest, zt: (0, 0)),
                  vec, vec],
        out_specs=pl.BlockSpec(memory_space=pl.ANY),
        scratch_shapes=[pltpu.VMEM((2, tm, d), F32), pltpu.VMEM((TM_EXPERT, d), F32),
                        pltpu.SemaphoreType.DMA((2,)), pltpu.SemaphoreType.DMA],
    )
    return pl.pallas_call(
        functools.partial(_scatter_kernel, tm=tm, ns=ns),
        grid_spec=grid_spec,
        out_shape=jax.ShapeDtypeStruct((m_pad, d), F32),
        compiler_params=_cparams(("arbitrary", "arbitrary")),
        name="moe_scatter",
    )(dest, zero_tile, h, gain.reshape(1, d), scale.reshape(b, 1, d), shift.reshape(b, 1, d))


def _expert_kernel(te_ref, tv_ref, tx_ref, x_ref, wg_ref, wu_ref, wd_ref, o_ref):
    t = pl.program_id(0)
    f = pl.program_id(1)

    @pl.when(tv_ref[t] == 1)
    def _():
        x = x_ref[...].astype(BF16)
        hid = (_silu(_dot(x, wg_ref[0])) * _dot(x, wu_ref[0])).astype(BF16)
        y = _dot(hid, wd_ref[0])

        @pl.when(f == 0)
        def _():
            o_ref[...] = y

        @pl.when(f > 0)
        def _():
            o_ref[...] = o_ref[...] + y

    @pl.when((tv_ref[t] == 0) & (f == 0))
    def _():
        o_ref[...] = jnp.zeros_like(o_ref)


def _experts(tile_expert, tile_valid, tile_x, xs, w_gate, w_up, w_down):
    m_pad, d = xs.shape
    tm = TM_EXPERT
    n_tiles = tile_expert.shape[0]
    ff = w_gate.shape[2]
    fs = FF_STEPS_EXPERT
    tf = ff // fs

    def ff_idx(f, tv, t):
        return f * tv[t] + (fs - 1) * (1 - tv[t])

    grid_spec = pltpu.PrefetchScalarGridSpec(
        num_scalar_prefetch=3,
        grid=(n_tiles, fs),
        in_specs=[pl.BlockSpec((tm, d), lambda t, f, te, tv, tx: (tx[t], 0)),
                  pl.BlockSpec((1, d, tf), lambda t, f, te, tv, tx: (te[t], 0, ff_idx(f, tv, t))),
                  pl.BlockSpec((1, d, tf), lambda t, f, te, tv, tx: (te[t], 0, ff_idx(f, tv, t))),
                  pl.BlockSpec((1, tf, d), lambda t, f, te, tv, tx: (te[t], ff_idx(f, tv, t), 0))],
        out_specs=pl.BlockSpec((tm, d), lambda t, f, te, tv, tx: (t, 0)),
    )
    return pl.pallas_call(
        _expert_kernel,
        grid_spec=grid_spec,
        out_shape=jax.ShapeDtypeStruct((m_pad, d), F32),
        compiler_params=_cparams(("arbitrary", "arbitrary")),
        name="moe_experts",
    )(tile_expert, tile_valid, tile_x, xs,
      w_gate.astype(BF16), w_up.astype(BF16), w_down.astype(BF16))


def _combine_kernel(dest_ref, y_ref, mf_ref, h_ref, gain_ref, gate_ref, o_ref, ybuf, sems, *, tm, ns):
    step = pl.program_id(0) * ns + pl.program_id(1)
    nsteps = pl.num_programs(0) * ns
    slot = lax.rem(step, 2)

    def issue(st, sl):
        base = st * tm

        def body(i, carry):
            t = 2 * (base + i)
            pltpu.make_async_copy(y_ref.at[pl.ds(dest_ref[t], 1), :],
                                  ybuf.at[sl, 0, pl.ds(i, 1), :], sems.at[sl]).start()
            pltpu.make_async_copy(y_ref.at[pl.ds(dest_ref[t + 1], 1), :],
                                  ybuf.at[sl, 1, pl.ds(i, 1), :], sems.at[sl]).start()
            return carry

        lax.fori_loop(0, tm, body, 0, unroll=DMA_UNROLL)

    @pl.when(step == 0)
    def _():
        issue(0, 0)

    def run(sl):
        @pl.when(step + 1 < nsteps)
        def _():
            issue(step + 1, 1 - sl)

        for k in range(2):
            pltpu.make_async_copy(ybuf.at[sl, k], ybuf.at[sl, k], sems.at[sl]).wait()
        mf = mf_ref[...]
        y = mf[:, 0:1] * ybuf[sl, 0] + mf[:, 1:2] * ybuf[sl, 1]
        o_ref[0] = h_ref[0] + gate_ref[0] * (_rms(y) * gain_ref[...])

    for sl in range(2):
        pl.when(slot == sl)(functools.partial(run, sl))


def _combine(dest, ys, mf, h, gain, gate):
    b, s, d = h.shape
    tm = TM_COMBINE
    ns = s // tm
    grid_spec = pltpu.PrefetchScalarGridSpec(
        num_scalar_prefetch=1,
        grid=(b, ns),
        in_specs=[pl.BlockSpec(memory_space=pl.ANY),
                  pl.BlockSpec((tm, LANES), lambda bi, si, dest: (bi * ns + si, 0)),
                  pl.BlockSpec((1, tm, d), lambda bi, si, dest: (bi, si, 0)),
                  pl.BlockSpec((1, d), lambda bi, si, dest: (0, 0)),
                  pl.BlockSpec((1, 1, d), lambda bi, si, dest: (bi, 0, 0))],
        out_specs=pl.BlockSpec((1, tm, d), lambda bi, si, dest: (bi, si, 0)),
        scratch_shapes=[pltpu.VMEM((2, 2, tm, d), F32), pltpu.SemaphoreType.DMA((2,))],
    )
    return pl.pallas_call(
        functools.partial(_combine_kernel, tm=tm, ns=ns),
        grid_spec=grid_spec,
        out_shape=jax.ShapeDtypeStruct((b, s, d), F32),
        compiler_params=_cparams(("arbitrary", "arbitrary")),
        name="moe_combine",
    )(dest, ys, mf, h, gain.reshape(1, d), gate.reshape(b, 1, d))


def _mixer_tail_moe(o, w_out, h, gain1, gate1, gain_in, scale, shift, router_w, w_gate, w_up, w_down,
                    gain_out, gate):
    b, s, d = h.shape
    n = b * s
    tm = TM_EXPERT
    h, mi, mf, cnt = _mixer_tail_router(o, w_out, h, gain1, gate1, gain_in, scale, shift, router_w)
    counts = cnt[0, :N_EXPERTS].astype(I32)
    tiles_per = (counts + tm - 1) // tm
    seg_start = (jnp.cumsum(tiles_per) - tiles_per) * tm
    dest = (seg_start[mi[:, 0:2]] + mi[:, 2:4]).reshape(2 * n)
    n_tiles = (2 * n) // tm + N_EXPERTS
    m_pad = n_tiles * tm
    tile_end = jnp.cumsum(tiles_per)
    tidx = jnp.arange(n_tiles, dtype=I32)
    tile_valid = (tidx < tile_end[-1]).astype(I32)
    tile_expert = jnp.minimum(jnp.searchsorted(tile_end, tidx, side="right"), N_EXPERTS - 1).astype(I32)
    tile_x = jnp.minimum(tidx, tile_end[-1] - 1)
    tail = tile_end[-1] + jnp.arange(N_EXPERTS, dtype=I32)
    zero_tile = jnp.concatenate([jnp.where(tiles_per > 0, tile_end - 1, -1),
                                 jnp.where(tail < n_tiles, tail, -1)]).astype(I32)
    xs = _scatter(dest, zero_tile, h, gain_in, scale, shift, m_pad)
    ys = _experts(tile_expert, tile_valid, tile_x, xs, w_gate, w_up, w_down)
    return _combine(dest, ys, mf, h, gain_out, gate)


def kernel(x, c, mod_w, mod_b, norm_g, attn_in_w_even, fox_gate_bias, attn_out_w_even,
           attn_in_w_odd, attn_out_w_odd, rel_bias_table, ffn_w_gate, ffn_w_up, ffn_w_down,
           router_w, exp_w_gate, exp_w_up, exp_w_down):
    depth = mod_w.shape[0]
    s_len = x.shape[1]
    mods = _mods(c, mod_w, mod_b)
    dil_bias, moba_bias = _bias_tiles(rel_bias_table, s_len)
    h = x
    for layer in range(depth):
        j = layer // 2
        sh1, sc1, g1, sh2, sc2, g2 = jnp.split(mods[layer], 6, axis=-1)
        gains = norm_g[layer]
        if layer % 2 == 0:
            fox_in, moba_in = _inproj_even(h, gains[0], sc1, sh1, attn_in_w_even[j], fox_gate_bias[j])
            o_parts = [_fox_attention(*fox_in), _moba_attention(*moba_in, moba_bias)]
            h = _mixer_tail_ffn(o_parts, attn_out_w_even[j], h, gains[1], g1, gains[2], sc2, sh2,
                                ffn_w_gate[j], ffn_w_up[j], ffn_w_down[j], gains[3], g2)
        else:
            q, k, v = _inproj_odd(h, gains[0], sc1, sh1, attn_in_w_odd[j])
            o = _dilated_attention(q, k, v, dil_bias)
            h = _mixer_tail_moe(o, attn_out_w_odd[j], h, gains[1], g1, gains[2], sc2, sh2, router_w[j],
                                exp_w_gate[j], exp_w_up[j], exp_w_down[j], gains[3], g2)
    return h
```

```python
import functools
import math

import numpy as np
import jax
import jax.numpy as jnp
from jax import lax
from jax.experimental import pallas as pl
from jax.experimental.pallas import tpu as pltpu

F32 = jnp.float32
BF16 = jnp.bfloat16
I32 = jnp.int32

HEAD_DIM = 64
LANES = 128
N_HEADS = 16
N_HEADS_FOX = 8
ATTN_SCALE = HEAD_DIM ** -0.5
LOG2E = math.log2(math.e)
Q_SCALE = ATTN_SCALE * LOG2E
N_DECAY_PIECES = 3
MOBA_BLOCK = 256
MOBA_TOPK = 3
DIL_PATTERNS = ((128, 1), (512, 4), (2048, 16))
NUM_BUCKETS = 32
MAX_DISTANCE = 2048
N_EXPERTS = 8
NORM_EPS = 1e-6
NEG_INF = float("-inf")
MASK_BIG = 1e30

VMEM_LIMIT = 56 * 1024 * 1024

TM_PROJ = 512
TM_FFN = 512
TQ_FOX = 1024
TQ_MOBA = 1024
TK_FOX = 512
FOX_PAIRS_PER_STEP = 2
MOBA_PAIRS_PER_STEP = 2
TM_ROUTE = 512
ROUTER_ROW_CHAINS = 2
TM_SCATTER = 256
TM_EXPERT = 512
TM_COMBINE = 256
FF_STEPS_EXPERT = 2
DMA_UNROLL = True
DIL_CHUNKS_PER_STEP = 16


def _cparams(sem):
    return pltpu.CompilerParams(dimension_semantics=sem, vmem_limit_bytes=VMEM_LIMIT)


def _t5_bucket_np(n):
    n = np.maximum(n, 0)
    max_exact = NUM_BUCKETS // 2
    nf = np.maximum(n, 1).astype(np.float64)
    large = max_exact + (np.log(nf / max_exact) / math.log(MAX_DISTANCE / max_exact)
                         * (NUM_BUCKETS - max_exact)).astype(np.int64)
    large = np.minimum(large, NUM_BUCKETS - 1)
    return np.where(n < max_exact, n, large)


_MAX_DIST = 1 << 16
_BUCKET_OF = _t5_bucket_np(np.arange(_MAX_DIST))
_BUCKET_THR = [int(np.searchsorted(_BUCKET_OF, k, side="left")) for k in range(NUM_BUCKETS)]


def _bias_from_dist(tab_ref, h, dist, dlo, dhi):
    lo_b = int(_BUCKET_OF[max(dlo, 0)])
    hi_b = int(_BUCKET_OF[dhi])
    val = jnp.zeros(dist.shape, F32) + tab_ref[lo_b, h]
    for k in range(lo_b + 1, hi_b + 1):
        val = jnp.where(dist >= _BUCKET_THR[k], tab_ref[k, h], val)
    return val


def _dil_bias_kernel(tab_ref, o_ref):
    h = pl.program_id(0)
    for g, (window, dil) in enumerate(DIL_PATTERNS):
        span = window // dil
        i = lax.broadcasted_iota(I32, (span, 2 * span), 0)
        j = lax.broadcasted_iota(I32, (span, 2 * span), 1)
        rel = i + span - j
        val = _bias_from_dist(tab_ref, h, rel * dil, 0, span * dil) * LOG2E
        band = jnp.where(rel >= 0, jnp.where(rel <= span, val, NEG_INF), NEG_INF)
        o_ref[0, 2 * g] = band
        o_ref[0, 2 * g + 1] = jnp.where(j >= span, band, NEG_INF)


def _moba_bias_kernel(tab_ref, o_ref, *, n_blk, head0):
    h = pl.program_id(0) + head0
    i = lax.broadcasted_iota(I32, (MOBA_BLOCK, MOBA_BLOCK), 0)
    j = lax.broadcasted_iota(I32, (MOBA_BLOCK, MOBA_BLOCK), 1)
    for d in range(n_blk):
        dist = d * MOBA_BLOCK + i - j
        val = _bias_from_dist(tab_ref, h, dist, d * MOBA_BLOCK - (MOBA_BLOCK - 1),
                              d * MOBA_BLOCK + (MOBA_BLOCK - 1)) * LOG2E
        if d == 0:
            val = jnp.where(dist >= 0, val, NEG_INF)
        o_ref[0, d] = val


def _bias_tiles(rel_bias_table, s_len):
    n_blk = s_len // MOBA_BLOCK
    span = DIL_PATTERNS[0][0]
    n_var = 2 * len(DIL_PATTERNS)
    smem = pl.BlockSpec(memory_space=pltpu.SMEM)
    dil = pl.pallas_call(
        _dil_bias_kernel,
        grid=(N_HEADS,),
        in_specs=[smem],
        out_specs=pl.BlockSpec((1, n_var, span, 2 * span), lambda h: (h, 0, 0, 0)),
        out_shape=jax.ShapeDtypeStruct((N_HEADS, n_var, span, 2 * span), F32),
        compiler_params=_cparams(("parallel",)),
        name="dil_bias",
    )(rel_bias_table)
    n_moba = N_HEADS - N_HEADS_FOX
    moba = pl.pallas_call(
        functools.partial(_moba_bias_kernel, n_blk=n_blk, head0=N_HEADS_FOX),
        grid=(n_moba,),
        in_specs=[smem],
        out_specs=pl.BlockSpec((1, n_blk, MOBA_BLOCK, MOBA_BLOCK), lambda h: (h, 0, 0, 0)),
        out_shape=jax.ShapeDtypeStruct((n_moba, n_blk, MOBA_BLOCK, MOBA_BLOCK), F32),
        compiler_params=_cparams(("parallel",)),
        name="moba_bias",
    )(rel_bias_table)
    return dil, moba


def _split_bf16(a):
    hi = a.astype(BF16)
    lo = (a - hi.astype(F32)).astype(BF16)
    return hi, lo


def _dot(a, b):
    return jnp.dot(a, b, preferred_element_type=F32)


def _dot_nt(a, b):
    return lax.dot_general(a, b, (((1,), (1,)), ((), ())), preferred_element_type=F32)


def _dot_split(a, b):
    a_hi, a_lo = _split_bf16(a)
    b_hi, b_lo = _split_bf16(b)
    return _dot(a_hi, b_hi) + (_dot(a_hi, b_lo) + _dot(a_lo, b_hi))


def _rms(x):
    return x * lax.rsqrt(jnp.mean(x * x, axis=-1, keepdims=True) + NORM_EPS)


def _modulate(x, gain, scale, shift):
    return (_rms(x) * gain) * (1.0 + scale) + shift


def _silu(x):
    return x * jax.nn.sigmoid(x)


def _mods_kernel(c_ref, w_ref, b_ref, o_ref):
    o_ref[0] = _dot_split(_silu(c_ref[...]), w_ref[0]) + b_ref[0]


def _mods(c, mod_w, mod_b):
    depth, d, e = mod_w.shape
    b = c.shape[0]
    tn = 1536
    return pl.pallas_call(
        _mods_kernel,
        grid=(depth, e // tn),
        in_specs=[pl.BlockSpec((b, d), lambda l, j: (0, 0)),
                  pl.BlockSpec((1, d, tn), lambda l, j: (l, 0, j)),
                  pl.BlockSpec((1, 1, tn), lambda l, j: (l, 0, j))],
        out_specs=pl.BlockSpec((1, b, tn), lambda l, j: (l, 0, j)),
        out_shape=jax.ShapeDtypeStruct((depth, b, e), F32),
        compiler_params=_cparams(("parallel", "parallel")),
        name="adaln_mods",
    )(c, mod_w, mod_b.reshape(depth, 1, e))


def _inproj_even_kernel(h_ref, g_ref, sc_ref, sh_ref, w_ref, wf_ref, gb_ref,
                        qa_ref, k0a_ref, k1a_ref, v0a_ref, v1a_ref,
                        qb_ref, k0b_ref, k1b_ref, v0b_ref, v1b_ref, carry_ref, *, tm):
    si = pl.program_id(1)
    u = _modulate(h_ref[0], g_ref[...], sc_ref[0], sh_ref[0]).astype(BF16)
    width = qa_ref.shape[-1]
    n_pairs = width // LANES

    def proj(i):
        return _dot(u, w_ref[:, i * width:(i + 1) * width])

    lane = lax.broadcasted_iota(I32, (1, LANES), 1)
    left = lane < HEAD_DIM
    row = lax.broadcasted_iota(I32, (tm, LANES), 0)

    x = _dot(u, wf_ref[...]) + gb_ref[...]
    lf = jnp.where(lane < N_HEADS_FOX, jnp.minimum(x, 0.0) - jnp.log1p(jnp.exp(-jnp.abs(x))), 0.0)
    k = 1
    while k < tm:
        lf = lf + jnp.where(row >= k, pltpu.roll(lf, k, axis=0), 0.0)
        k *= 2

    @pl.when(si == 0)
    def _():
        carry_ref[...] = jnp.zeros_like(carry_ref)

    cum = lf + carry_ref[0:1, :]
    carry_ref[...] = jnp.broadcast_to(cum[tm - 1:tm, :], carry_ref.shape)
    rest = cum * (-LOG2E)
    decay = jnp.zeros((tm, LANES), F32)
    for p in range(N_DECAY_PIECES):
        piece = rest.astype(BF16).astype(F32)
        rest = rest - piece
        decay = decay + (pltpu.roll(piece, p * N_HEADS_FOX, axis=1) if p else piece)
    decay_lo = decay.astype(BF16)
    decay_hi = pltpu.roll(decay, HEAD_DIM, axis=1).astype(BF16)

    ones = jnp.ones((1, LANES), BF16)
    blk = (si * tm + row) // MOBA_BLOCK
    blk_lo = jnp.where(lane == blk, 1.0, 0.0).astype(BF16)
    blk_hi = jnp.where(lane == blk + HEAD_DIM, 1.0, 0.0).astype(BF16)

    def emit(first, k0_ref, k1_ref, v0_ref, v1_ref, k_lo, k_hi):
        kk = proj(first + 1).astype(BF16)
        vv = proj(first + 2).astype(BF16)
        for hp in range(n_pairs):
            sl = slice(hp * LANES, (hp + 1) * LANES)
            k0_ref[0, :, sl] = jnp.where(left, kk[:, sl], k_hi)
            k1_ref[0, :, sl] = jnp.where(left, k_lo, kk[:, sl])
            v0_ref[0, :, sl] = jnp.where(left, vv[:, sl], ones)
            v1_ref[0, :, sl] = jnp.where(left, ones, vv[:, sl])

    qa_ref[0] = (proj(0) * Q_SCALE).astype(BF16)
    emit(0, k0a_ref, k1a_ref, v0a_ref, v1a_ref, decay_lo, decay_hi)
    qb_ref[0] = (proj(3) * Q_SCALE).astype(BF16)
    emit(3, k0b_ref, k1b_ref, v0b_ref, v1b_ref, blk_lo, blk_hi)


def _inproj_even(h, gain, scale, shift, w_in, gate_bias):
    b, s, d = h.shape
    tm = TM_PROJ
    da = N_HEADS_FOX * HEAD_DIM
    assert s // MOBA_BLOCK <= HEAD_DIM and N_DECAY_PIECES * N_HEADS_FOX <= HEAD_DIM
    cuts = np.cumsum([da, da, da, N_HEADS_FOX, da, da]).tolist()
    q_a, k_a, v_a, f_a, q_b, k_b, v_b = jnp.split(w_in, cuts, axis=1)
    w = jnp.concatenate([q_a, k_a, v_a, q_b, k_b, v_b], axis=1).astype(BF16)
    wf = jnp.pad(f_a, ((0, 0), (0, LANES - N_HEADS_FOX))).astype(BF16)
    gb = jnp.pad(gate_bias.astype(F32), (0, LANES - N_HEADS_FOX)).reshape(1, LANES)
    act = jax.ShapeDtypeStruct((b, s, da), BF16)
    act_spec = pl.BlockSpec((1, tm, da), lambda bi, si: (bi, si, 0))
    vec = pl.BlockSpec((1, 1, d), lambda bi, si: (bi, 0, 0))
    outs = pl.pallas_call(
        functools.partial(_inproj_even_kernel, tm=tm),
        grid=(b, s // tm),
        in_specs=[pl.BlockSpec((1, tm, d), lambda bi, si: (bi, si, 0)),
                  pl.BlockSpec((1, d), lambda bi, si: (0, 0)),
                  vec, vec,
                  pl.BlockSpec(w.shape, lambda bi, si: (0, 0)),
                  pl.BlockSpec(wf.shape, lambda bi, si: (0, 0)),
                  pl.BlockSpec(gb.shape, lambda bi, si: (0, 0))],
        out_specs=[act_spec] * 10,
        out_shape=[act] * 10,
        scratch_shapes=[pltpu.VMEM((8, LANES), F32)],
        compiler_params=_cparams(("parallel", "arbitrary")),
        name="inproj_even",
    )(h, gain.reshape(1, d), scale.reshape(b, 1, d), shift.reshape(b, 1, d), w, wf, gb)
    return outs[:5], outs[5:]


def _inproj_odd_kernel(h_ref, g_ref, sc_ref, sh_ref, w_ref, q_ref, k_ref, v_ref):
    u = _modulate(h_ref[0], g_ref[...], sc_ref[0], sh_ref[0]).astype(BF16)
    width = q_ref.shape[-1]
    q_ref[0] = _dot(u, w_ref[:, 0:width]) * Q_SCALE
    k_ref[0] = _dot(u, w_ref[:, width:2 * width])
    v_ref[0] = _dot(u, w_ref[:, 2 * width:3 * width])


def _inproj_odd(h, gain, scale, shift, w_in):
    b, s, d = h.shape
    tm = TM_PROJ
    dq = w_in.shape[1] // 3
    act = jax.ShapeDtypeStruct((b, s, dq), F32)
    act_spec = pl.BlockSpec((1, tm, dq), lambda bi, si: (bi, si, 0))
    vec = pl.BlockSpec((1, 1, d), lambda bi, si: (bi, 0, 0))
    return pl.pallas_call(
        _inproj_odd_kernel,
        grid=(b, s // tm),
        in_specs=[pl.BlockSpec((1, tm, d), lambda bi, si: (bi, si, 0)),
                  pl.BlockSpec((1, d), lambda bi, si: (0, 0)),
                  vec, vec,
                  pl.BlockSpec(w_in.shape, lambda bi, si: (0, 0))],
        out_specs=[act_spec] * 3,
        out_shape=[act] * 3,
        compiler_params=_cparams(("parallel", "parallel")),
        name="inproj_odd",
    )(h, gain.reshape(1, d), scale.reshape(b, 1, d), shift.reshape(b, 1, d), w_in.astype(BF16))


def _tile_lanes(x, width):
    return jnp.concatenate([x] * (width // LANES), axis=1)


def _flash_update(s, v, m_ref, acc_ref):
    m_prev = m_ref[...]
    m_new = jnp.maximum(m_prev, jnp.max(s, axis=1, keepdims=True))
    p = jnp.exp2(s - _tile_lanes(m_new, s.shape[1]))
    acc_ref[...] = jnp.exp2(m_prev - m_new) * acc_ref[...] + _dot(p.astype(BF16), v)
    m_ref[...] = m_new


def _finish_pair(acc_ref, left, first=0):
    acc0 = acc_ref[first]
    acc1 = acc_ref[first + 1]
    den = pltpu.roll(jnp.where(left, acc1, acc0), HEAD_DIM, axis=1)
    return jnp.where(left, acc0, acc1) / den


def _fox_kernel(q_ref, k0_ref, k1_ref, v0_ref, v1_ref, o_ref, m_ref, acc_ref, *, tq, tk, n_pairs):
    qi = pl.program_id(2)
    lane = lax.broadcasted_iota(I32, (1, LANES), 1)
    left = lane < HEAD_DIM

    def piece_lanes(lane0):
        hit = lane == lane0
        for p in range(1, N_DECAY_PIECES):
            hit = jnp.logical_or(hit, lane == lane0 + p * N_HEADS_FOX)
        return jnp.where(hit, 1.0, 0.0).astype(BF16)

    chains = []
    for pr in range(n_pairs):
        hp = pl.program_id(1) * n_pairs + pr
        lanes = slice(pr * LANES, (pr + 1) * LANES)
        q = q_ref[0, :, lanes]
        chains.append((jnp.where(left, q, piece_lanes(HEAD_DIM + 2 * hp)), k0_ref, v0_ref, lanes))
        chains.append((jnp.where(left, piece_lanes(2 * hp + 1), q), k1_ref, v1_ref, lanes))
    row = lax.broadcasted_iota(I32, (tq, tk), 0)
    col = lax.broadcasted_iota(I32, (tq, tk), 1)
    m_ref[...] = jnp.full(m_ref.shape, NEG_INF, F32)
    acc_ref[...] = jnp.zeros(acc_ref.shape, F32)
    n_sub = tq // tk

    def step(kv, mask, row0=0):
        off = pl.multiple_of(kv * tk, tk)
        rows = pl.ds(row0, tq - row0)
        for c, (qc, k_ref, v_ref, lanes) in enumerate(chains):
            s = _dot_nt(qc[row0:], k_ref[0, pl.ds(off, tk), lanes])
            if mask is not None:
                s = jnp.where(mask[row0:], s, NEG_INF)
            _flash_update(s, v_ref[0, pl.ds(off, tk), lanes], m_ref.at[c, rows], acc_ref.at[c, rows])

    def body(kv, carry):
        step(kv, None)
        return carry

    lax.fori_loop(0, qi * n_sub, body, 0)
    for d in range(n_sub):
        step(qi * n_sub + d, col + d * tk <= row, d * tk)
    for pr in range(n_pairs):
        o_ref[0, :, pr * LANES:(pr + 1) * LANES] = _finish_pair(acc_ref, left, 2 * pr).astype(o_ref.dtype)


def _fox_attention(q, k0, k1, v0, v1):
    b, s, da = q.shape
    n_pairs = FOX_PAIRS_PER_STEP
    width = n_pairs * LANES
    tq = min(TQ_FOX, s)
    kv = pl.BlockSpec((1, s, width), lambda bi, h, qi: (bi, 0, h))
    return pl.pallas_call(
        functools.partial(_fox_kernel, tq=tq, tk=min(TK_FOX, tq), n_pairs=n_pairs),
        grid=(b, da // width, s // tq),
        in_specs=[pl.BlockSpec((1, tq, width), lambda bi, h, qi: (bi, qi, h)), kv, kv, kv, kv],
        out_specs=pl.BlockSpec((1, tq, width), lambda bi, h, qi: (bi, qi, h)),
        out_shape=jax.ShapeDtypeStruct((b, s, da), BF16),
        scratch_shapes=[pltpu.VMEM((2 * n_pairs, tq, LANES), F32)] * 2,
        compiler_params=_cparams(("parallel", "parallel", "arbitrary")),
        name="fox_attention",
    )(q, k0, k1, v0, v1)


def _moba_kernel(q_ref, k0_ref, k1_ref, v0_ref, v1_ref, bias_ref, o_ref, km_ref, m_ref, acc_ref,
                 *, n_blk, tq, n_pairs):
    blk = MOBA_BLOCK
    tk = 2 * blk
    nq = tq // blk
    a = pl.program_id(2)
    gate_lane0 = (HEAD_DIM, 0)
    lane = lax.broadcasted_iota(I32, (1, LANES), 1)
    left = lane < HEAD_DIM
    mine = (left, jnp.logical_not(left))
    k_refs = (k0_ref, k1_ref)
    v_refs = (v0_ref, v1_ref)

    @pl.when(a == 0)
    def _():
        km_ref[...] = jnp.zeros_like(km_ref)
        for pr in range(n_pairs):
            lanes = slice(pr * LANES, (pr + 1) * LANES)
            for n in range(n_blk):
                rows = slice(n * blk, (n + 1) * blk)
                kb = jnp.where(left, k0_ref[0, rows, lanes], k1_ref[0, rows, lanes]).astype(F32)
                mean = jnp.sum(kb, axis=0, keepdims=True) * (1.0 / blk)
                for lane0 in gate_lane0:
                    km_ref[pr, lane0 + n:lane0 + n + 1, :] = mean

    nb = -(-n_blk // 8) * 8
    blkf = lax.broadcasted_iota(I32, (nb, tq), 0).astype(F32)
    own = (lax.broadcasted_iota(I32, (nb, tq), 1) // blk + a * nq).astype(F32)
    chains = []
    for pr in range(n_pairs):
        lanes = slice(pr * LANES, (pr + 1) * LANES)
        q = q_ref[0, :, lanes]
        km_hi, km_lo = _split_bf16(km_ref[pr])
        for j in range(2):
            qj = jnp.where(mine[j], q, jnp.zeros_like(q))
            lane0 = gate_lane0[j]
            gate = (_dot_nt(km_hi, qj) + _dot_nt(km_lo, qj))[lane0:lane0 + nb]
            gate = jnp.where(blkf < own, gate, NEG_INF)
            pen = jnp.where(blkf == own, 0.0, -MASK_BIG)
            for _ in range(MOBA_TOPK):
                mx = jnp.max(gate, axis=0, keepdims=True)
                cand = jnp.where(gate == mx, jnp.where(mx > NEG_INF, blkf, float(LANES)), float(LANES))
                pick = blkf == jnp.min(cand, axis=0, keepdims=True)
                pen = jnp.where(pick, 0.0, pen)
                gate = jnp.where(pick, NEG_INF, gate)
            parts = [pen, jnp.zeros((LANES - lane0 - nb, tq), F32)]
            if lane0:
                parts.insert(0, jnp.zeros((lane0, tq), F32))
            pen_q = jnp.concatenate(parts, axis=0).T
            chains.append((jnp.where(mine[j], q, pen_q.astype(BF16)), k_refs[j], v_refs[j], lanes, 2 * pr + j))

    m_ref[...] = jnp.full(m_ref.shape, NEG_INF, F32)
    acc_ref[...] = jnp.zeros(acc_ref.shape, F32)

    def step(i, dist, r0):
        off = pl.multiple_of(i * tk, tk)
        rows = pl.ds(r0 * blk, tq - r0 * blk)
        for qc, k_ref, v_ref, lanes, h in chains:
            bias = jnp.concatenate(
                [jnp.concatenate([bias_ref[h, dist(r, c)] for c in range(2)], axis=1)
                 for r in range(r0, nq)], axis=0)
            s = _dot_nt(qc[r0 * blk:], k_ref[0, pl.ds(off, tk), lanes]) + bias
            _flash_update(s, v_ref[0, pl.ds(off, tk), lanes], m_ref.at[h, rows], acc_ref.at[h, rows])

    n_full = a * (nq // 2)

    def body(i, carry):
        step(i, lambda r, c: a * nq + r - 2 * i - c, 0)
        return carry

    lax.fori_loop(0, n_full, body, 0)
    for e in range(nq // 2):
        step(n_full + e, lambda r, c, e=e: max(r - 2 * e - c, 0), 2 * e)
    for pr in range(n_pairs):
        o_ref[0, :, pr * LANES:(pr + 1) * LANES] = _finish_pair(acc_ref, left, 2 * pr).astype(o_ref.dtype)


def _moba_attention(q, k0, k1, v0, v1, bias_tiles):
    b, s, db = q.shape
    n_pairs = MOBA_PAIRS_PER_STEP
    width = n_pairs * LANES
    blk = MOBA_BLOCK
    n_blk = s // blk
    tq = min(TQ_MOBA, s)
    assert n_blk <= HEAD_DIM, "block gates of one head must fit in the other head's lanes"
    assert s % tq == 0 and tq % (2 * blk) == 0
    kv = pl.BlockSpec((1, s, width), lambda h, bi, qi: (bi, 0, h))
    return pl.pallas_call(
        functools.partial(_moba_kernel, n_blk=n_blk, tq=tq, n_pairs=n_pairs),
        grid=(db // width, b, s // tq),
        in_specs=[pl.BlockSpec((1, tq, width), lambda h, bi, qi: (bi, qi, h)), kv, kv, kv, kv,
                  pl.BlockSpec((2 * n_pairs, n_blk, blk, blk), lambda h, bi, qi: (h, 0, 0, 0),
                               pipeline_mode=pl.Buffered(1))],
        out_specs=pl.BlockSpec((1, tq, width), lambda h, bi, qi: (bi, qi, h)),
        out_shape=jax.ShapeDtypeStruct((b, s, db), BF16),
        scratch_shapes=[pltpu.VMEM((n_pairs, LANES, LANES), F32)]
                       + [pltpu.VMEM((2 * n_pairs, tq, LANES), F32)] * 2,
        compiler_params=_cparams(("parallel", "parallel", "arbitrary")),
        name="moba_attention",
    )(q, k0, k1, v0, v1, bias_tiles)


def _dilated_kernel(q_ref, k_ref, v_ref, bias_ref, o_ref, m_ref, acc_ref, *, s_len):
    lane = lax.broadcasted_iota(I32, (1, LANES), 1)
    left = lane < HEAD_DIM
    ones = jnp.ones((1, LANES), BF16)
    order = sorted(range(len(DIL_PATTERNS)), key=lambda i: -DIL_PATTERNS[i][1])
    for g in order:
        window, dil = DIL_PATTERNS[g]
        merge = g != order[0]
        span = window // dil
        unit = span * dil
        nc = s_len // unit
        n_u = min(DIL_CHUNKS_PER_STEP, nc)
        groups = nc // n_u
        n_res = min(DIL_CHUNKS_PER_STEP // n_u, dil)

        def rows(ref, start, dil=dil, span=span):
            if dil == 1:
                return ref[0, pl.ds(start, span), :]
            return ref[0, pl.ds(start, span, stride=dil), :]

        def get(ref, j, start, dil=dil, span=span):
            if dil == 1:
                return ref[j, pl.ds(start, span), :]
            return ref[j, pl.ds(start, span, stride=dil), :]

        def put(ref, j, start, val, dil=dil, span=span):
            if dil == 1:
                ref[j, pl.ds(start, span), :] = val
            else:
                ref[j, pl.ds(start, span, stride=dil), :] = val

        def body(it, carry, g=g, merge=merge, n_u=n_u, n_res=n_res, groups=groups, unit=unit, rows=rows,
                 put=put, get=get):
            r0 = (it // groups) * n_res
            grp = it - (it // groups) * groups
            is_first = grp == 0
            results = []
            for dr in range(n_res):
                start0 = r0 + dr + grp * (n_u * unit)
                prev0 = start0 - jnp.where(is_first, 0, unit)
                starts = [start0 + u * unit for u in range(n_u)]
                kc = [rows(k_ref, st).astype(BF16) for st in [prev0] + starts]
                vc = [rows(v_ref, st).astype(BF16) for st in [prev0] + starts]
                vcs = ([jnp.where(left, v, ones) for v in vc], [jnp.where(left, ones, v) for v in vc])
                for u, start in enumerate(starts):
                    var = 2 * g + jnp.where(is_first, 1, 0) if u == 0 else 2 * g
                    q = rows(q_ref, start)
                    kb = jnp.concatenate([kc[u], kc[u + 1]], axis=0)
                    qq = jnp.concatenate([jnp.where(left, q, 0.0), jnp.where(left, 0.0, q)],
                                         axis=0).astype(BF16)
                    s_both = _dot_nt(qq, kb)
                    for j in range(2):
                        s = s_both[j * span:(j + 1) * span] + bias_ref[j, var]
                        m_new = jnp.broadcast_to(jnp.max(s, axis=1, keepdims=True), (span, LANES))
                        if merge:
                            m_prev = get(m_ref, j, start)
                            m_new = jnp.maximum(m_new, m_prev)
                        p = jnp.exp2(s - _tile_lanes(m_new, 2 * span))
                        acc_new = _dot(p.astype(BF16), jnp.concatenate([vcs[j][u], vcs[j][u + 1]], axis=0))
                        if merge:
                            acc_new = jnp.exp2(m_prev - m_new) * get(acc_ref, j, start) + acc_new
                        results.append((j, start, m_new, acc_new))
            for j, start, m_new, acc_new in results:
                put(m_ref, j, start, m_new)
                put(acc_ref, j, start, acc_new)
            return carry

        lax.fori_loop(0, (dil // n_res) * groups, body, 0)
    o_ref[0] = _finish_pair(acc_ref, left).astype(o_ref.dtype)


def _dilated_attention(q, k, v, bias_tiles):
    b, s, dq = q.shape
    hp = dq // LANES
    for window, dil in DIL_PATTERNS:
        assert s % window == 0, "sequence must be a whole number of dilated units"
    qkv = pl.BlockSpec((1, s, LANES), lambda h, bi: (bi, 0, h))
    n_var, span, band = bias_tiles.shape[1:]
    return pl.pallas_call(
        functools.partial(_dilated_kernel, s_len=s),
        grid=(hp, b),
        in_specs=[qkv, qkv, qkv,
                  pl.BlockSpec((2, n_var, span, band), lambda h, bi: (h, 0, 0, 0))],
        out_specs=pl.BlockSpec((1, s, LANES), lambda h, bi: (bi, 0, h)),
        out_shape=jax.ShapeDtypeStruct((b, s, dq), BF16),
        scratch_shapes=[pltpu.VMEM((2, s, LANES), F32)] * 2,
        compiler_params=_cparams(("parallel", "parallel")),
        name="dilated_attention",
    )(q, k, v, bias_tiles)


def _mixer_residual(o_parts, w_parts, h, gain, gate, rows=slice(None)):
    y = _dot(o_parts[0][0, rows], w_parts[0][...])
    for o_ref, w_ref in zip(o_parts[1:], w_parts[1:]):
        y = y + _dot(o_ref[0, rows], w_ref[...])
    return h + gate * (_rms(y) * gain)


def _mixer_operands(o_parts, w_out, tm):
    w_out = w_out.astype(BF16)
    cuts = np.cumsum([p.shape[-1] for p in o_parts])[:-1].tolist()
    w_parts = jnp.split(w_out, cuts, axis=0) if cuts else [w_out]
    specs = [pl.BlockSpec((1, tm, p.shape[-1]), lambda bi, si: (bi, si, 0)) for p in o_parts]
    specs += [pl.BlockSpec(w.shape, lambda bi, si: (0, 0)) for w in w_parts]
    return list(o_parts) + list(w_parts), specs


def _ffn_kernel(*refs, n_parts):
    o_parts = refs[:n_parts]
    w_parts = refs[n_parts:2 * n_parts]
    (h_ref, gain1_ref, gate1_ref, g_ref, sc_ref, sh_ref, wg_ref, wu_ref, wd_ref, gain_ref, gate_ref,
     o_ref) = refs[2 * n_parts:]
    h = _mixer_residual(o_parts, w_parts, h_ref[0], gain1_ref[...], gate1_ref[0])
    u = _modulate(h, g_ref[...], sc_ref[0], sh_ref[0]).astype(BF16)
    hid = (_silu(_dot(u, wg_ref[...])) * _dot(u, wu_ref[...])).astype(BF16)
    y = _dot(hid, wd_ref[...])
    o_ref[0] = h + gate_ref[0] * (_rms(y) * gain_ref[...])


def _mixer_tail_ffn(o_parts, w_out, h, gain1, gate1, gain_in, scale, shift, w_gate, w_up, w_down,
                    gain_out, gate):
    b, s, d = h.shape
    tm = TM_FFN
    ff = w_gate.shape[1]
    resident = functools.partial(pl.BlockSpec, pipeline_mode=pl.Buffered(1))
    vec = pl.BlockSpec((1, 1, d), lambda bi, si: (bi, 0, 0))
    row = pl.BlockSpec((1, d), lambda bi, si: (0, 0))
    mix_ops, mix_specs = _mixer_operands(o_parts, w_out, tm)
    return pl.pallas_call(
        functools.partial(_ffn_kernel, n_parts=len(o_parts)),
        grid=(b, s // tm),
        in_specs=mix_specs + [pl.BlockSpec((1, tm, d), lambda bi, si: (bi, si, 0)), row, vec,
                              row, vec, vec,
                              resident((d, ff), lambda bi, si: (0, 0)),
                              resident((d, ff), lambda bi, si: (0, 0)),
                              resident((ff, d), lambda bi, si: (0, 0)),
                              row, vec],
        out_specs=pl.BlockSpec((1, tm, d), lambda bi, si: (bi, si, 0)),
        out_shape=jax.ShapeDtypeStruct((b, s, d), F32),
        compiler_params=_cparams(("parallel", "parallel")),
        name="dense_swiglu",
    )(*mix_ops, h, gain1.reshape(1, d), gate1.reshape(b, 1, d),
      gain_in.reshape(1, d), scale.reshape(b, 1, d), shift.reshape(b, 1, d),
      w_gate.astype(BF16), w_up.astype(BF16), w_down.astype(BF16),
      gain_out.reshape(1, d), gate.reshape(b, 1, d))


def _router_kernel(o_ref, wo_ref, h_ref, gain1_ref, gate1_ref, g_ref, sc_ref, sh_ref, rw_ref,
                   hout_ref, mi_ref, mf_ref, cnt_ref, carry_ref, *, tm):
    @pl.when((pl.program_id(0) == 0) & (pl.program_id(1) == 0))
    def _():
        carry_ref[...] = jnp.zeros_like(carry_ref)

    tr = tm // ROUTER_ROW_CHAINS
    lanef = lax.broadcasted_iota(I32, (tr, LANES), 1).astype(F32)
    r = lax.broadcasted_iota(I32, (tr, tr), 0)
    c = lax.broadcasted_iota(I32, (tr, tr), 1)
    before = jnp.where(c < r, 1.0, 0.0).astype(BF16)
    counts = carry_ref[0:1, :]
    for ch in range(ROUTER_ROW_CHAINS):
        rows = slice(ch * tr, (ch + 1) * tr)
        h = _mixer_residual([o_ref], [wo_ref], h_ref[0, rows], gain1_ref[...], gate1_ref[0], rows)
        hout_ref[0, rows] = h
        u = _modulate(h, g_ref[...], sc_ref[0], sh_ref[0])
        logits = _dot_split(u, rw_ref[...])
        lg = jnp.where(lanef < N_EXPERTS, logits, NEG_INF)
        v1 = jnp.max(lg, axis=1, keepdims=True)
        i1 = jnp.min(jnp.where(lg == v1, lanef, float(LANES)), axis=1, keepdims=True)
        lg2 = jnp.where(lanef == i1, NEG_INF, lg)
        v2 = jnp.max(lg2, axis=1, keepdims=True)
        i2 = jnp.min(jnp.where(lg2 == v2, lanef, float(LANES)), axis=1, keepdims=True)
        e2 = jnp.exp(v2 - v1)
        p1 = 1.0 / (1.0 + e2)
        p2 = e2 / (1.0 + e2)
        oh1 = jnp.where(lanef == i1, 1.0, 0.0)
        oh2 = jnp.where(lanef == i2, 1.0, 0.0)
        oh = oh1 + oh2
        tot = _dot(before, oh.astype(BF16)) + counts
        rank1 = jnp.sum(oh1 * tot, axis=1, keepdims=True)
        rank2 = jnp.sum(oh2 * tot, axis=1, keepdims=True)
        counts = counts + jnp.sum(oh, axis=0, keepdims=True)
        mi = jnp.where(lanef == 0.0, i1, jnp.where(lanef == 1.0, i2,
             jnp.where(lanef == 2.0, rank1, jnp.where(lanef == 3.0, rank2, 0.0))))
        mi_ref[rows] = mi.astype(I32)
        mf_ref[rows] = jnp.where(lanef == 0.0, p1, jnp.where(lanef == 1.0, p2, 0.0))
    carry_ref[...] = jnp.broadcast_to(counts, carry_ref.shape)
    cnt_ref[...] = carry_ref[...]


def _mixer_tail_router(o, w_out, h, gain1, gate1, gain, scale, shift, router_w):
    b, s, d = h.shape
    tm = TM_ROUTE
    n = b * s
    ns = s // tm
    rw = jnp.pad(router_w.astype(F32), ((0, 0), (0, LANES - router_w.shape[1])))
    vec = pl.BlockSpec((1, 1, d), lambda bi, si: (bi, 0, 0))
    row = pl.BlockSpec((1, d), lambda bi, si: (0, 0))
    act = pl.BlockSpec((1, tm, d), lambda bi, si: (bi, si, 0))
    meta = pl.BlockSpec((tm, LANES), lambda bi, si: (bi * ns + si, 0))
    mix_ops, mix_specs = _mixer_operands([o], w_out, tm)
    return pl.pallas_call(
        functools.partial(_router_kernel, tm=tm),
        grid=(b, ns),
        in_specs=mix_specs + [act, row, vec, row, vec, vec, pl.BlockSpec(rw.shape, lambda bi, si: (0, 0))],
        out_specs=[act, meta, meta, pl.BlockSpec((8, LANES), lambda bi, si: (0, 0))],
        out_shape=[jax.ShapeDtypeStruct((b, s, d), F32),
                   jax.ShapeDtypeStruct((n, LANES), I32),
                   jax.ShapeDtypeStruct((n, LANES), F32),
                   jax.ShapeDtypeStruct((8, LANES), F32)],
        scratch_shapes=[pltpu.VMEM((8, LANES), F32)],
        compiler_params=_cparams(("arbitrary", "arbitrary")),
        name="moe_router",
    )(*mix_ops, h, gain1.reshape(1, d), gate1.reshape(b, 1, d),
      gain.reshape(1, d), scale.reshape(b, 1, d), shift.reshape(b, 1, d), rw)


def _scatter_kernel(dest_ref, ztile_ref, h_ref, g_ref, sc_ref, sh_ref, xs_ref, ubuf, zbuf, sems, zsem,
                    *, tm, ns):
    step = pl.program_id(0) * ns + pl.program_id(1)
    nsteps = pl.num_programs(0) * ns
    slot = lax.rem(step, 2)
    tz = zbuf.shape[0]

    @pl.when(step == 0)
    def _():
        zbuf[...] = jnp.zeros_like(zbuf)
        for e in range(2 * N_EXPERTS):
            @pl.when(ztile_ref[e] >= 0)
            def _():
                row0 = pl.multiple_of(ztile_ref[e] * tz, tz)
                pltpu.make_async_copy(zbuf, xs_ref.at[pl.ds(row0, tz), :], zsem).start()
        for e in range(2 * N_EXPERTS):
            @pl.when(ztile_ref[e] >= 0)
            def _():
                pltpu.make_async_copy(zbuf, xs_ref.at[pl.ds(0, tz), :], zsem).wait()

    def wait_slot(sl):
        for _ in range(2):
            pltpu.make_async_copy(ubuf.at[sl], ubuf.at[sl], sems.at[sl]).wait()

    base = step * tm

    def run(sl):
        @pl.when(step >= 2)
        def _():
            wait_slot(sl)

        ubuf[sl] = _modulate(h_ref[0], g_ref[...], sc_ref[0], sh_ref[0])

        def issue(i, carry):
            t = 2 * (base + i)
            src = ubuf.at[sl, pl.ds(i, 1), :]
            pltpu.make_async_copy(src, xs_ref.at[pl.ds(dest_ref[t], 1), :], sems.at[sl]).start()
            pltpu.make_async_copy(src, xs_ref.at[pl.ds(dest_ref[t + 1], 1), :], sems.at[sl]).start()
            return carry

        lax.fori_loop(0, tm, issue, 0, unroll=DMA_UNROLL)

        @pl.when(step == nsteps - 1)
        def _():
            wait_slot(sl)

            @pl.when(nsteps >= 2)
            def _():
                wait_slot(1 - sl)

    for sl in range(2):
        pl.when(slot == sl)(functools.partial(run, sl))


def _scatter(dest, zero_tile, h, gain, scale, shift, m_pad):
    b, s, d = h.shape
    tm = TM_SCATTER
    ns = s // tm
    vec = pl.BlockSpec((1, 1, d), lambda bi, si, dest, zt: (bi, 0, 0))
    grid_spec = pltpu.PrefetchScalarGridSpec(
        num_scalar_prefetch=2,
        grid=(b, ns),
        in_specs=[pl.BlockSpec((1, tm, d), lambda bi, si, dest, zt: (bi, si, 0)),
                  pl.BlockSpec((1, d), lambda bi, si, dest, zt: (0, 0)),
                  vec, vec],
        out_specs=pl.BlockSpec(memory_space=pl.ANY),
        scratch_shapes=[pltpu.VMEM((2, tm, d), F32), pltpu.VMEM((TM_EXPERT, d), F32),
                        pltpu.SemaphoreType.DMA((2,)), pltpu.SemaphoreType.DMA],
    )
    return pl.pallas_call(
        functools.partial(_scatter_kernel, tm=tm, ns=ns),
        grid_spec=grid_spec,
        out_shape=jax.ShapeDtypeStruct((m_pad, d), F32),
        compiler_params=_cparams(("arbitrary", "arbitrary")),
        name="moe_scatter",
    )(dest, zero_tile, h, gain.reshape(1, d), scale.reshape(b, 1, d), shift.reshape(b, 1, d))


def _expert_kernel(te_ref, tv_ref, tx_ref, x_ref, wg_ref, wu_ref, wd_ref, o_ref):
    t = pl.program_id(0)
    f = pl.program_id(1)

    @pl.when(tv_ref[t] == 1)
    def _():
        x = x_ref[...].astype(BF16)
        hid = (_silu(_dot(x, wg_ref[0])) * _dot(x, wu_ref[0])).astype(BF16)
        y = _dot(hid, wd_ref[0])

        @pl.when(f == 0)
        def _():
            o_ref[...] = y

        @pl.when(f > 0)
        def _():
            o_ref[...] = o_ref[...] + y

    @pl.when((tv_ref[t] == 0) & (f == 0))
    def _():
        o_ref[...] = jnp.zeros_like(o_ref)


def _experts(tile_expert, tile_valid, tile_x, xs, w_gate, w_up, w_down):
    m_pad, d = xs.shape
    tm = TM_EXPERT
    n_tiles = tile_expert.shape[0]
    ff = w_gate.shape[2]
    fs = FF_STEPS_EXPERT
    tf = ff // fs

    def ff_idx(f, tv, t):
        return f * tv[t] + (fs - 1) * (1 - tv[t])

    grid_spec = pltpu.PrefetchScalarGridSpec(
        num_scalar_prefetch=3,
        grid=(n_tiles, fs),
        in_specs=[pl.BlockSpec((tm, d), lambda t, f, te, tv, tx: (tx[t], 0)),
                  pl.BlockSpec((1, d, tf), lambda t, f, te, tv, tx: (te[t], 0, ff_idx(f, tv, t))),
                  pl.BlockSpec((1, d, tf), lambda t, f, te, tv, tx: (te[t], 0, ff_idx(f, tv, t))),
                  pl.BlockSpec((1, tf, d), lambda t, f, te, tv, tx: (te[t], ff_idx(f, tv, t), 0))],
        out_specs=pl.BlockSpec((tm, d), lambda t, f, te, tv, tx: (t, 0)),
    )
    return pl.pallas_call(
        _expert_kernel,
        grid_spec=grid_spec,
        out_shape=jax.ShapeDtypeStruct((m_pad, d), F32),
        compiler_params=_cparams(("arbitrary", "arbitrary")),
        name="moe_experts",
    )(tile_expert, tile_valid, tile_x, xs,
      w_gate.astype(BF16), w_up.astype(BF16), w_down.astype(BF16))


def _combine_kernel(dest_ref, y_ref, mf_ref, h_ref, gain_ref, gate_ref, o_ref, ybuf, sems, *, tm, ns):
    step = pl.program_id(0) * ns + pl.program_id(1)
    nsteps = pl.num_programs(0) * ns
    slot = lax.rem(step, 2)

    def issue(st, sl):
        base = st * tm

        def body(i, carry):
            t = 2 * (base + i)
            pltpu.make_async_copy(y_ref.at[pl.ds(dest_ref[t], 1), :],
                                  ybuf.at[sl, 0, pl.ds(i, 1), :], sems.at[sl]).start()
            pltpu.make_async_copy(y_ref.at[pl.ds(dest_ref[t + 1], 1), :],
                                  ybuf.at[sl, 1, pl.ds(i, 1), :], sems.at[sl]).start()
            return carry

        lax.fori_loop(0, tm, body, 0, unroll=DMA_UNROLL)

    @pl.when(step == 0)
    def _():
        issue(0, 0)

    def run(sl):
        @pl.when(step + 1 < nsteps)
        def _():
            issue(step + 1, 1 - sl)

        for k in range(2):
            pltpu.make_async_copy(ybuf.at[sl, k], ybuf.at[sl, k], sems.at[sl]).wait()
        mf = mf_ref[...]
        y = mf[:, 0:1] * ybuf[sl, 0] + mf[:, 1:2] * ybuf[sl, 1]
        o_ref[0] = h_ref[0] + gate_ref[0] * (_rms(y) * gain_ref[...])

    for sl in range(2):
        pl.when(slot == sl)(functools.partial(run, sl))


def _combine(dest, ys, mf, h, gain, gate):
    b, s, d = h.shape
    tm = TM_COMBINE
    ns = s // tm
    grid_spec = pltpu.PrefetchScalarGridSpec(
        num_scalar_prefetch=1,
        grid=(b, ns),
        in_specs=[pl.BlockSpec(memory_space=pl.ANY),
                  pl.BlockSpec((tm, LANES), lambda bi, si, dest: (bi * ns + si, 0)),
                  pl.BlockSpec((1, tm, d), lambda bi, si, dest: (bi, si, 0)),
                  pl.BlockSpec((1, d), lambda bi, si, dest: (0, 0)),
                  pl.BlockSpec((1, 1, d), lambda bi, si, dest: (bi, 0, 0))],
        out_specs=pl.BlockSpec((1, tm, d), lambda bi, si, dest: (bi, si, 0)),
        scratch_shapes=[pltpu.VMEM((2, 2, tm, d), F32), pltpu.SemaphoreType.DMA((2,))],
    )
    return pl.pallas_call(
        functools.partial(_combine_kernel, tm=tm, ns=ns),
        grid_spec=grid_spec,
        out_shape=jax.ShapeDtypeStruct((b, s, d), F32),
        compiler_params=_cparams(("arbitrary", "arbitrary")),
        name="moe_combine",
    )(dest, ys, mf, h, gain.reshape(1, d), gate.reshape(b, 1, d))


def _mixer_tail_moe(o, w_out, h, gain1, gate1, gain_in, scale, shift, router_w, w_gate, w_up, w_down,
                    gain_out, gate):
    b, s, d = h.shape
    n = b * s
    tm = TM_EXPERT
    h, mi, mf, cnt = _mixer_tail_router(o, w_out, h, gain1, gate1, gain_in, scale, shift, router_w)
    counts = cnt[0, :N_EXPERTS].astype(I32)
    tiles_per = (counts + tm - 1) // tm
    seg_start = (jnp.cumsum(tiles_per) - tiles_per) * tm
    dest = (seg_start[mi[:, 0:2]] + mi[:, 2:4]).reshape(2 * n)
    n_tiles = (2 * n) // tm + N_EXPERTS
    m_pad = n_tiles * tm
    tile_end = jnp.cumsum(tiles_per)
    tidx = jnp.arange(n_tiles, dtype=I32)
    tile_valid = (tidx < tile_end[-1]).astype(I32)
    tile_expert = jnp.minimum(jnp.searchsorted(tile_end, tidx, side="right"), N_EXPERTS - 1).astype(I32)
    tile_x = jnp.minimum(tidx, tile_end[-1] - 1)
    tail = tile_end[-1] + jnp.arange(N_EXPERTS, dtype=I32)
    zero_tile = jnp.concatenate([jnp.where(tiles_per > 0, tile_end - 1, -1),
                                 jnp.where(tail < n_tiles, tail, -1)]).astype(I32)
    xs = _scatter(dest, zero_tile, h, gain_in, scale, shift, m_pad)
    ys = _experts(tile_expert, tile_valid, tile_x, xs, w_gate, w_up, w_down)
    return _combine(dest, ys, mf, h, gain_out, gate)


def kernel(x, c, mod_w, mod_b, norm_g, attn_in_w_even, fox_gate_bias, attn_out_w_even,
           attn_in_w_odd, attn_out_w_odd, rel_bias_table, ffn_w_gate, ffn_w_up, ffn_w_down,
           router_w, exp_w_gate, exp_w_up, exp_w_down):
    depth = mod_w.shape[0]
    s_len = x.shape[1]
    mods = _mods(c, mod_w, mod_b)
    dil_bias, moba_bias = _bias_tiles(rel_bias_table, s_len)
    h = x
    for layer in range(depth):
        j = layer // 2
        sh1, sc1, g1, sh2, sc2, g2 = jnp.split(mods[layer], 6, axis=-1)
        gains = norm_g[layer]
        if layer % 2 == 0:
            fox_in, moba_in = _inproj_even(h, gains[0], sc1, sh1, attn_in_w_even[j], fox_gate_bias[j])
            o_parts = [_fox_attention(*fox_in), _moba_attention(*moba_in, moba_bias)]
            h = _mixer_tail_ffn(o_parts, attn_out_w_even[j], h, gains[1], g1, gains[2], sc2, sh2,
                                ffn_w_gate[j], ffn_w_up[j], ffn_w_down[j], gains[3], g2)
        else:
            q, k, v = _inproj_odd(h, gains[0], sc1, sh1, attn_in_w_odd[j])
            o = _dilated_attention(q, k, v, dil_bias)
            h = _mixer_tail_moe(o, attn_out_w_odd[j], h, gains[1], g1, gains[2], sc2, sh2, router_w[j],
                                exp_w_gate[j], exp_w_up[j], exp_w_down[j], gains[3], g2)
    return h
```

```python
import functools
import math

import numpy as np
import jax
import jax.numpy as jnp
from jax import lax
from jax.experimental import pallas as pl
from jax.experimental.pallas import tpu as pltpu

F32 = jnp.float32
BF16 = jnp.bfloat16
I32 = jnp.int32

HEAD_DIM = 64
LANES = 128
N_HEADS = 16
N_HEADS_FOX = 8
ATTN_SCALE = HEAD_DIM ** -0.5
LOG2E = math.log2(math.e)
Q_SCALE = ATTN_SCALE * LOG2E
N_DECAY_PIECES = 3
MOBA_BLOCK = 256
MOBA_TOPK = 3
DIL_PATTERNS = ((128, 1), (512, 4), (2048, 16))
NUM_BUCKETS = 32
MAX_DISTANCE = 2048
N_EXPERTS = 8
NORM_EPS = 1e-6
NEG_INF = float("-inf")
MASK_BIG = 1e30

VMEM_LIMIT = 56 * 1024 * 1024

TM_PROJ = 512
TM_FFN = 512
TQ_FOX = 2048
TQ_MOBA = 2048
TK_FOX = 512
FOX_PAIRS_PER_STEP = 1
MOBA_PAIRS_PER_STEP = 1
TM_ROUTE = 512
ROUTER_ROW_CHAINS = 2
TM_SCATTER = 256
TM_EXPERT = 512
TM_COMBINE = 256
FF_STEPS_EXPERT = 2
DMA_UNROLL = True
DIL_CHUNKS_PER_STEP = 16


def _cparams(sem):
    return pltpu.CompilerParams(dimension_semantics=sem, vmem_limit_bytes=VMEM_LIMIT)


def _t5_bucket_np(n):
    n = np.maximum(n, 0)
    max_exact = NUM_BUCKETS // 2
    nf = np.maximum(n, 1).astype(np.float64)
    large = max_exact + (np.log(nf / max_exact) / math.log(MAX_DISTANCE / max_exact)
                         * (NUM_BUCKETS - max_exact)).astype(np.int64)
    large = np.minimum(large, NUM_BUCKETS - 1)
    return np.where(n < max_exact, n, large)


_MAX_DIST = 1 << 16
_BUCKET_OF = _t5_bucket_np(np.arange(_MAX_DIST))
_BUCKET_THR = [int(np.searchsorted(_BUCKET_OF, k, side="left")) for k in range(NUM_BUCKETS)]


def _bias_from_dist(tab_ref, h, dist, dlo, dhi):
    lo_b = int(_BUCKET_OF[max(dlo, 0)])
    hi_b = int(_BUCKET_OF[dhi])
    val = jnp.zeros(dist.shape, F32) + tab_ref[lo_b, h]
    for k in range(lo_b + 1, hi_b + 1):
        val = jnp.where(dist >= _BUCKET_THR[k], tab_ref[k, h], val)
    return val


def _dil_bias_kernel(tab_ref, o_ref):
    h = pl.program_id(0)
    for g, (window, dil) in enumerate(DIL_PATTERNS):
        span = window // dil
        i = lax.broadcasted_iota(I32, (span, 2 * span), 0)
        j = lax.broadcasted_iota(I32, (span, 2 * span), 1)
        rel = i + span - j
        val = _bias_from_dist(tab_ref, h, rel * dil, 0, span * dil) * LOG2E
        band = jnp.where(rel >= 0, jnp.where(rel <= span, val, NEG_INF), NEG_INF)
        o_ref[0, 2 * g] = band
        o_ref[0, 2 * g + 1] = jnp.where(j >= span, band, NEG_INF)


def _moba_bias_kernel(tab_ref, o_ref, *, n_blk, head0):
    h = pl.program_id(0) + head0
    i = lax.broadcasted_iota(I32, (MOBA_BLOCK, MOBA_BLOCK), 0)
    j = lax.broadcasted_iota(I32, (MOBA_BLOCK, MOBA_BLOCK), 1)
    for d in range(n_blk):
        dist = d * MOBA_BLOCK + i - j
        val = _bias_from_dist(tab_ref, h, dist, d * MOBA_BLOCK - (MOBA_BLOCK - 1),
                              d * MOBA_BLOCK + (MOBA_BLOCK - 1)) * LOG2E
        if d == 0:
            val = jnp.where(dist >= 0, val, NEG_INF)
        o_ref[0, d] = val


def _bias_tiles(rel_bias_table, s_len):
    n_blk = s_len // MOBA_BLOCK
    span = DIL_PATTERNS[0][0]
    n_var = 2 * len(DIL_PATTERNS)
    smem = pl.BlockSpec(memory_space=pltpu.SMEM)
    dil = pl.pallas_call(
        _dil_bias_kernel,
        grid=(N_HEADS,),
        in_specs=[smem],
        out_specs=pl.BlockSpec((1, n_var, span, 2 * span), lambda h: (h, 0, 0, 0)),
        out_shape=jax.ShapeDtypeStruct((N_HEADS, n_var, span, 2 * span), F32),
        compiler_params=_cparams(("parallel",)),
        name="dil_bias",
    )(rel_bias_table)
    n_moba = N_HEADS - N_HEADS_FOX
    moba = pl.pallas_call(
        functools.partial(_moba_bias_kernel, n_blk=n_blk, head0=N_HEADS_FOX),
        grid=(n_moba,),
        in_specs=[smem],
        out_specs=pl.BlockSpec((1, n_blk, MOBA_BLOCK, MOBA_BLOCK), lambda h: (h, 0, 0, 0)),
        out_shape=jax.ShapeDtypeStruct((n_moba, n_blk, MOBA_BLOCK, MOBA_BLOCK), F32),
        compiler_params=_cparams(("parallel",)),
        name="moba_bias",
    )(rel_bias_table)
    return dil, moba


def _split_bf16(a):
    hi = a.astype(BF16)
    lo = (a - hi.astype(F32)).astype(BF16)
    return hi, lo


def _dot(a, b):
    return jnp.dot(a, b, preferred_element_type=F32)


def _dot_nt(a, b):
    return lax.dot_general(a, b, (((1,), (1,)), ((), ())), preferred_element_type=F32)


def _dot_split(a, b):
    a_hi, a_lo = _split_bf16(a)
    b_hi, b_lo = _split_bf16(b)
    return _dot(a_hi, b_hi) + (_dot(a_hi, b_lo) + _dot(a_lo, b_hi))


def _rms(x):
    return x * lax.rsqrt(jnp.mean(x * x, axis=-1, keepdims=True) + NORM_EPS)


def _modulate(x, gain, scale, shift):
    return (_rms(x) * gain) * (1.0 + scale) + shift


def _silu(x):
    return x * jax.nn.sigmoid(x)


def _mods_kernel(c_ref, w_ref, b_ref, o_ref):
    o_ref[0] = _dot_split(_silu(c_ref[...]), w_ref[0]) + b_ref[0]


def _mods(c, mod_w, mod_b):
    depth, d, e = mod_w.shape
    b = c.shape[0]
    tn = 1536
    return pl.pallas_call(
        _mods_kernel,
        grid=(depth, e // tn),
        in_specs=[pl.BlockSpec((b, d), lambda l, j: (0, 0)),
                  pl.BlockSpec((1, d, tn), lambda l, j: (l, 0, j)),
                  pl.BlockSpec((1, 1, tn), lambda l, j: (l, 0, j))],
        out_specs=pl.BlockSpec((1, b, tn), lambda l, j: (l, 0, j)),
        out_shape=jax.ShapeDtypeStruct((depth, b, e), F32),
        compiler_params=_cparams(("parallel", "parallel")),
        name="adaln_mods",
    )(c, mod_w, mod_b.reshape(depth, 1, e))


def _inproj_even_kernel(h_ref, g_ref, sc_ref, sh_ref, w_ref, wf_ref, gb_ref,
                        qa_ref, k0a_ref, k1a_ref, v0a_ref, v1a_ref,
                        qb_ref, k0b_ref, k1b_ref, v0b_ref, v1b_ref, carry_ref, *, tm):
    si = pl.program_id(1)
    u = _modulate(h_ref[0], g_ref[...], sc_ref[0], sh_ref[0]).astype(BF16)
    width = qa_ref.shape[-1]
    n_pairs = width // LANES

    def proj(i):
        return _dot(u, w_ref[:, i * width:(i + 1) * width])

    lane = lax.broadcasted_iota(I32, (1, LANES), 1)
    left = lane < HEAD_DIM
    row = lax.broadcasted_iota(I32, (tm, LANES), 0)

    x = _dot(u, wf_ref[...]) + gb_ref[...]
    lf = jnp.where(lane < N_HEADS_FOX, jnp.minimum(x, 0.0) - jnp.log1p(jnp.exp(-jnp.abs(x))), 0.0)
    k = 1
    while k < tm:
        lf = lf + jnp.where(row >= k, pltpu.roll(lf, k, axis=0), 0.0)
        k *= 2

    @pl.when(si == 0)
    def _():
        carry_ref[...] = jnp.zeros_like(carry_ref)

    cum = lf + carry_ref[0:1, :]
    carry_ref[...] = jnp.broadcast_to(cum[tm - 1:tm, :], carry_ref.shape)
    rest = cum * (-LOG2E)
    decay = jnp.zeros((tm, LANES), F32)
    for p in range(N_DECAY_PIECES):
        piece = rest.astype(BF16).astype(F32)
        rest = rest - piece
        decay = decay + (pltpu.roll(piece, p * N_HEADS_FOX, axis=1) if p else piece)
    decay_lo = decay.astype(BF16)
    decay_hi = pltpu.roll(decay, HEAD_DIM, axis=1).astype(BF16)

    ones = jnp.ones((1, LANES), BF16)
    blk = (si * tm + row) // MOBA_BLOCK
    blk_lo = jnp.where(lane == blk, 1.0, 0.0).astype(BF16)
    blk_hi = jnp.where(lane == blk + HEAD_DIM, 1.0, 0.0).astype(BF16)

    def emit(first, k0_ref, k1_ref, v0_ref, v1_ref, k_lo, k_hi):
        kk = proj(first + 1).astype(BF16)
        vv = proj(first + 2).astype(BF16)
        for hp in range(n_pairs):
            sl = slice(hp * LANES, (hp + 1) * LANES)
            k0_ref[0, :, sl] = jnp.where(left, kk[:, sl], k_hi)
            k1_ref[0, :, sl] = jnp.where(left, k_lo, kk[:, sl])
            v0_ref[0, :, sl] = jnp.where(left, vv[:, sl], ones)
            v1_ref[0, :, sl] = jnp.where(left, ones, vv[:, sl])

    qa_ref[0] = (proj(0) * Q_SCALE).astype(BF16)
    emit(0, k0a_ref, k1a_ref, v0a_ref, v1a_ref, decay_lo, decay_hi)
    qb_ref[0] = (proj(3) * Q_SCALE).astype(BF16)
    emit(3, k0b_ref, k1b_ref, v0b_ref, v1b_ref, blk_lo, blk_hi)


def _inproj_even(h, gain, scale, shift, w_in, gate_bias):
    b, s, d = h.shape
    tm = TM_PROJ
    da = N_HEADS_FOX * HEAD_DIM
    assert s // MOBA_BLOCK <= HEAD_DIM and N_DECAY_PIECES * N_HEADS_FOX <= HEAD_DIM
    cuts = np.cumsum([da, da, da, N_HEADS_FOX, da, da]).tolist()
    q_a, k_a, v_a, f_a, q_b, k_b, v_b = jnp.split(w_in, cuts, axis=1)
    w = jnp.concatenate([q_a, k_a, v_a, q_b, k_b, v_b], axis=1).astype(BF16)
    wf = jnp.pad(f_a, ((0, 0), (0, LANES - N_HEADS_FOX))).astype(BF16)
    gb = jnp.pad(gate_bias.astype(F32), (0, LANES - N_HEADS_FOX)).reshape(1, LANES)
    act = jax.ShapeDtypeStruct((b, s, da), BF16)
    act_spec = pl.BlockSpec((1, tm, da), lambda bi, si: (bi, si, 0))
    vec = pl.BlockSpec((1, 1, d), lambda bi, si: (bi, 0, 0))
    outs = pl.pallas_call(
        functools.partial(_inproj_even_kernel, tm=tm),
        grid=(b, s // tm),
        in_specs=[pl.BlockSpec((1, tm, d), lambda bi, si: (bi, si, 0)),
                  pl.BlockSpec((1, d), lambda bi, si: (0, 0)),
                  vec, vec,
                  pl.BlockSpec(w.shape, lambda bi, si: (0, 0)),
                  pl.BlockSpec(wf.shape, lambda bi, si: (0, 0)),
                  pl.BlockSpec(gb.shape, lambda bi, si: (0, 0))],
        out_specs=[act_spec] * 10,
        out_shape=[act] * 10,
        scratch_shapes=[pltpu.VMEM((8, LANES), F32)],
        compiler_params=_cparams(("parallel", "arbitrary")),
        name="inproj_even",
    )(h, gain.reshape(1, d), scale.reshape(b, 1, d), shift.reshape(b, 1, d), w, wf, gb)
    return outs[:5], outs[5:]


def _inproj_odd_kernel(h_ref, g_ref, sc_ref, sh_ref, w_ref, q_ref, k_ref, v_ref):
    u = _modulate(h_ref[0], g_ref[...], sc_ref[0], sh_ref[0]).astype(BF16)
    width = q_ref.shape[-1]
    q_ref[0] = _dot(u, w_ref[:, 0:width]) * Q_SCALE
    k_ref[0] = _dot(u, w_ref[:, width:2 * width])
    v_ref[0] = _dot(u, w_ref[:, 2 * width:3 * width])


def _inproj_odd(h, gain, scale, shift, w_in):
    b, s, d = h.shape
    tm = TM_PROJ
    dq = w_in.shape[1] // 3
    act = jax.ShapeDtypeStruct((b, s, dq), F32)
    act_spec = pl.BlockSpec((1, tm, dq), lambda bi, si: (bi, si, 0))
    vec = pl.BlockSpec((1, 1, d), lambda bi, si: (bi, 0, 0))
    return pl.pallas_call(
        _inproj_odd_kernel,
        grid=(b, s // tm),
        in_specs=[pl.BlockSpec((1, tm, d), lambda bi, si: (bi, si, 0)),
                  pl.BlockSpec((1, d), lambda bi, si: (0, 0)),
                  vec, vec,
                  pl.BlockSpec(w_in.shape, lambda bi, si: (0, 0))],
        out_specs=[act_spec] * 3,
        out_shape=[act] * 3,
        compiler_params=_cparams(("parallel", "parallel")),
        name="inproj_odd",
    )(h, gain.reshape(1, d), scale.reshape(b, 1, d), shift.reshape(b, 1, d), w_in.astype(BF16))


def _tile_lanes(x, width):
    return jnp.concatenate([x] * (width // LANES), axis=1)


def _flash_update(s, v, m_ref, acc_ref):
    m_prev = m_ref[...]
    m_new = jnp.maximum(m_prev, jnp.max(s, axis=1, keepdims=True))
    p = jnp.exp2(s - _tile_lanes(m_new, s.shape[1]))
    acc_ref[...] = jnp.exp2(m_prev - m_new) * acc_ref[...] + _dot(p.astype(BF16), v)
    m_ref[...] = m_new


def _finish_pair(acc_ref, left, first=0):
    acc0 = acc_ref[first]
    acc1 = acc_ref[first + 1]
    den = pltpu.roll(jnp.where(left, acc1, acc0), HEAD_DIM, axis=1)
    return jnp.where(left, acc0, acc1) / den


def _fox_kernel(q_ref, k0_ref, k1_ref, v0_ref, v1_ref, o_ref, m_ref, acc_ref, *, tq, tk, n_pairs):
    qi = pl.program_id(2)
    lane = lax.broadcasted_iota(I32, (1, LANES), 1)
    left = lane < HEAD_DIM

    def piece_lanes(lane0):
        hit = lane == lane0
        for p in range(1, N_DECAY_PIECES):
            hit = jnp.logical_or(hit, lane == lane0 + p * N_HEADS_FOX)
        return jnp.where(hit, 1.0, 0.0).astype(BF16)

    chains = []
    for pr in range(n_pairs):
        hp = pl.program_id(1) * n_pairs + pr
        lanes = slice(pr * LANES, (pr + 1) * LANES)
        q = q_ref[0, :, lanes]
        chains.append((jnp.where(left, q, piece_lanes(HEAD_DIM + 2 * hp)), k0_ref, v0_ref, lanes))
        chains.append((jnp.where(left, piece_lanes(2 * hp + 1), q), k1_ref, v1_ref, lanes))
    row = lax.broadcasted_iota(I32, (tq, tk), 0)
    col = lax.broadcasted_iota(I32, (tq, tk), 1)
    m_ref[...] = jnp.full(m_ref.shape, NEG_INF, F32)
    acc_ref[...] = jnp.zeros(acc_ref.shape, F32)
    n_sub = tq // tk

    def step(kv, mask, row0=0):
        off = pl.multiple_of(kv * tk, tk)
        rows = pl.ds(row0, tq - row0)
        for c, (qc, k_ref, v_ref, lanes) in enumerate(chains):
            s = _dot_nt(qc[row0:], k_ref[0, pl.ds(off, tk), lanes])
            if mask is not None:
                s = jnp.where(mask[row0:], s, NEG_INF)
            _flash_update(s, v_ref[0, pl.ds(off, tk), lanes], m_ref.at[c, rows], acc_ref.at[c, rows])

    def body(kv, carry):
        step(kv, None)
        return carry

    lax.fori_loop(0, qi * n_sub, body, 0)
    for d in range(n_sub):
        step(qi * n_sub + d, col + d * tk <= row, d * tk)
    for pr in range(n_pairs):
        o_ref[0, :, pr * LANES:(pr + 1) * LANES] = _finish_pair(acc_ref, left, 2 * pr).astype(o_ref.dtype)


def _fox_attention(q, k0, k1, v0, v1):
    b, s, da = q.shape
    n_pairs = FOX_PAIRS_PER_STEP
    width = n_pairs * LANES
    tq = min(TQ_FOX, s)
    kv = pl.BlockSpec((1, s, width), lambda bi, h, qi: (bi, 0, h))
    return pl.pallas_call(
        functools.partial(_fox_kernel, tq=tq, tk=min(TK_FOX, tq), n_pairs=n_pairs),
        grid=(b, da // width, s // tq),
        in_specs=[pl.BlockSpec((1, tq, width), lambda bi, h, qi: (bi, qi, h)), kv, kv, kv, kv],
        out_specs=pl.BlockSpec((1, tq, width), lambda bi, h, qi: (bi, qi, h)),
        out_shape=jax.ShapeDtypeStruct((b, s, da), BF16),
        scratch_shapes=[pltpu.VMEM((2 * n_pairs, tq, LANES), F32)] * 2,
        compiler_params=_cparams(("parallel", "parallel", "arbitrary")),
        name="fox_attention",
    )(q, k0, k1, v0, v1)


def _moba_kernel(q_ref, k0_ref, k1_ref, v0_ref, v1_ref, bias_ref, o_ref, km_ref, m_ref, acc_ref,
                 *, n_blk, tq, n_pairs):
    blk = MOBA_BLOCK
    tk = 2 * blk
    nq = tq // blk
    a = pl.program_id(2)
    gate_lane0 = (HEAD_DIM, 0)
    lane = lax.broadcasted_iota(I32, (1, LANES), 1)
    left = lane < HEAD_DIM
    mine = (left, jnp.logical_not(left))
    k_refs = (k0_ref, k1_ref)
    v_refs = (v0_ref, v1_ref)

    @pl.when(a == 0)
    def _():
        km_ref[...] = jnp.zeros_like(km_ref)
        for pr in range(n_pairs):
            lanes = slice(pr * LANES, (pr + 1) * LANES)
            for n in range(n_blk):
                rows = slice(n * blk, (n + 1) * blk)
                kb = jnp.where(left, k0_ref[0, rows, lanes], k1_ref[0, rows, lanes]).astype(F32)
                mean = jnp.sum(kb, axis=0, keepdims=True) * (1.0 / blk)
                for lane0 in gate_lane0:
                    km_ref[pr, lane0 + n:lane0 + n + 1, :] = mean

    nb = -(-n_blk // 8) * 8
    blkf = lax.broadcasted_iota(I32, (nb, tq), 0).astype(F32)
    own = (lax.broadcasted_iota(I32, (nb, tq), 1) // blk + a * nq).astype(F32)
    chains = []
    for pr in range(n_pairs):
        lanes = slice(pr * LANES, (pr + 1) * LANES)
        q = q_ref[0, :, lanes]
        km_hi, km_lo = _split_bf16(km_ref[pr])
        for j in range(2):
            qj = jnp.where(mine[j], q, jnp.zeros_like(q))
            lane0 = gate_lane0[j]
            gate = (_dot_nt(km_hi, qj) + _dot_nt(km_lo, qj))[lane0:lane0 + nb]
            gate = jnp.where(blkf < own, gate, NEG_INF)
            pen = jnp.where(blkf == own, 0.0, -MASK_BIG)
            for _ in range(MOBA_TOPK):
                mx = jnp.max(gate, axis=0, keepdims=True)
                cand = jnp.where(gate == mx, jnp.where(mx > NEG_INF, blkf, float(LANES)), float(LANES))
                pick = blkf == jnp.min(cand, axis=0, keepdims=True)
                pen = jnp.where(pick, 0.0, pen)
                gate = jnp.where(pick, NEG_INF, gate)
            parts = [pen, jnp.zeros((LANES - lane0 - nb, tq), F32)]
            if lane0:
                parts.insert(0, jnp.zeros((lane0, tq), F32))
            pen_q = jnp.concatenate(parts, axis=0).T
            chains.append((jnp.where(mine[j], q, pen_q.astype(BF16)), k_refs[j], v_refs[j], lanes, 2 * pr + j))

    m_ref[...] = jnp.full(m_ref.shape, NEG_INF, F32)
    acc_ref[...] = jnp.zeros(acc_ref.shape, F32)

    def step(i, dist, r0):
        off = pl.multiple_of(i * tk, tk)
        rows = pl.ds(r0 * blk, tq - r0 * blk)
        for qc, k_ref, v_ref, lanes, h in chains:
            bias = jnp.concatenate(
                [jnp.concatenate([bias_ref[h, dist(r, c)] for c in range(2)], axis=1)
                 for r in range(r0, nq)], axis=0)
            s = _dot_nt(qc[r0 * blk:], k_ref[0, pl.ds(off, tk), lanes]) + bias
            _flash_update(s, v_ref[0, pl.ds(off, tk), lanes], m_ref.at[h, rows], acc_ref.at[h, rows])

    n_full = a * (nq // 2)

    def body(i, carry):
        step(i, lambda r, c: a * nq + r - 2 * i - c, 0)
        return carry

    lax.fori_loop(0, n_full, body, 0)
    for e in range(nq // 2):
        step(n_full + e, lambda r, c, e=e: max(r - 2 * e - c, 0), 2 * e)
    for pr in range(n_pairs):
        o_ref[0, :, pr * LANES:(pr + 1) * LANES] = _finish_pair(acc_ref, left, 2 * pr).astype(o_ref.dtype)


def _moba_attention(q, k0, k1, v0, v1, bias_tiles):
    b, s, db = q.shape
    n_pairs = MOBA_PAIRS_PER_STEP
    width = n_pairs * LANES
    blk = MOBA_BLOCK
    n_blk = s // blk
    tq = min(TQ_MOBA, s)
    assert n_blk <= HEAD_DIM, "block gates of one head must fit in the other head's lanes"
    assert s % tq == 0 and tq % (2 * blk) == 0
    kv = pl.BlockSpec((1, s, width), lambda h, bi, qi: (bi, 0, h))
    return pl.pallas_call(
        functools.partial(_moba_kernel, n_blk=n_blk, tq=tq, n_pairs=n_pairs),
        grid=(db // width, b, s // tq),
        in_specs=[pl.BlockSpec((1, tq, width), lambda h, bi, qi: (bi, qi, h)), kv, kv, kv, kv,
                  pl.BlockSpec((2 * n_pairs, n_blk, blk, blk), lambda h, bi, qi: (h, 0, 0, 0),
                               pipeline_mode=pl.Buffered(1))],
        out_specs=pl.BlockSpec((1, tq, width), lambda h, bi, qi: (bi, qi, h)),
        out_shape=jax.ShapeDtypeStruct((b, s, db), BF16),
        scratch_shapes=[pltpu.VMEM((n_pairs, LANES, LANES), F32)]
                       + [pltpu.VMEM((2 * n_pairs, tq, LANES), F32)] * 2,
        compiler_params=_cparams(("parallel", "parallel", "arbitrary")),
        name="moba_attention",
    )(q, k0, k1, v0, v1, bias_tiles)


def _dilated_kernel(q_ref, k_ref, v_ref, bias_ref, o_ref, m_ref, acc_ref, *, s_len):
    lane = lax.broadcasted_iota(I32, (1, LANES), 1)
    left = lane < HEAD_DIM
    ones = jnp.ones((1, LANES), BF16)
    order = sorted(range(len(DIL_PATTERNS)), key=lambda i: -DIL_PATTERNS[i][1])
    for g in order:
        window, dil = DIL_PATTERNS[g]
        merge = g != order[0]
        span = window // dil
        unit = span * dil
        nc = s_len // unit
        n_u = min(DIL_CHUNKS_PER_STEP, nc)
        groups = nc // n_u
        n_res = min(DIL_CHUNKS_PER_STEP // n_u, dil)

        def rows(ref, start, dil=dil, span=span):
            if dil == 1:
                return ref[0, pl.ds(start, span), :]
            return ref[0, pl.ds(start, span, stride=dil), :]

        def get(ref, j, start, dil=dil, span=span):
            if dil == 1:
                return ref[j, pl.ds(start, span), :]
            return ref[j, pl.ds(start, span, stride=dil), :]

        def put(ref, j, start, val, dil=dil, span=span):
            if dil == 1:
                ref[j, pl.ds(start, span), :] = val
            else:
                ref[j, pl.ds(start, span, stride=dil), :] = val

        def body(it, carry, g=g, merge=merge, n_u=n_u, n_res=n_res, groups=groups, unit=unit, rows=rows,
                 put=put, get=get):
            r0 = (it // groups) * n_res
            grp = it - (it // groups) * groups
            is_first = grp == 0
            results = []
            for dr in range(n_res):
                start0 = r0 + dr + grp * (n_u * unit)
                prev0 = start0 - jnp.where(is_first, 0, unit)
                starts = [start0 + u * unit for u in range(n_u)]
                kc = [rows(k_ref, st).astype(BF16) for st in [prev0] + starts]
                vc = [rows(v_ref, st).astype(BF16) for st in [prev0] + starts]
                vcs = ([jnp.where(left, v, ones) for v in vc], [jnp.where(left, ones, v) for v in vc])
                for u, start in enumerate(starts):
                    var = 2 * g + jnp.where(is_first, 1, 0) if u == 0 else 2 * g
                    q = rows(q_ref, start)
                    kb = jnp.concatenate([kc[u], kc[u + 1]], axis=0)
                    qq = jnp.concatenate([jnp.where(left, q, 0.0), jnp.where(left, 0.0, q)],
                                         axis=0).astype(BF16)
                    s_both = _dot_nt(qq, kb)
                    for j in range(2):
                        s = s_both[j * span:(j + 1) * span] + bias_ref[j, var]
                        m_new = jnp.broadcast_to(jnp.max(s, axis=1, keepdims=True), (span, LANES))
                        if merge:
                            m_prev = get(m_ref, j, start)
                            m_new = jnp.maximum(m_new, m_prev)
                        p = jnp.exp2(s - _tile_lanes(m_new, 2 * span))
                        acc_new = _dot(p.astype(BF16), jnp.concatenate([vcs[j][u], vcs[j][u + 1]], axis=0))
                        if merge:
                            acc_new = jnp.exp2(m_prev - m_new) * get(acc_ref, j, start) + acc_new
                        results.append((j, start, m_new, acc_new))
            for j, start, m_new, acc_new in results:
                put(m_ref, j, start, m_new)
                put(acc_ref, j, start, acc_new)
            return carry

        lax.fori_loop(0, (dil // n_res) * groups, body, 0)
    o_ref[0] = _finish_pair(acc_ref, left).astype(o_ref.dtype)


def _dilated_attention(q, k, v, bias_tiles):
    b, s, dq = q.shape
    hp = dq // LANES
    for window, dil in DIL_PATTERNS:
        assert s % window == 0, "sequence must be a whole number of dilated units"
    qkv = pl.BlockSpec((1, s, LANES), lambda h, bi: (bi, 0, h))
    n_var, span, band = bias_tiles.shape[1:]
    return pl.pallas_call(
        functools.partial(_dilated_kernel, s_len=s),
        grid=(hp, b),
        in_specs=[qkv, qkv, qkv,
                  pl.BlockSpec((2, n_var, span, band), lambda h, bi: (h, 0, 0, 0))],
        out_specs=pl.BlockSpec((1, s, LANES), lambda h, bi: (bi, 0, h)),
        out_shape=jax.ShapeDtypeStruct((b, s, dq), BF16),
        scratch_shapes=[pltpu.VMEM((2, s, LANES), F32)] * 2,
        compiler_params=_cparams(("parallel", "parallel")),
        name="dilated_attention",
    )(q, k, v, bias_tiles)


def _mixer_residual(o_parts, w_parts, h, gain, gate, rows=slice(None)):
    y = _dot(o_parts[0][0, rows], w_parts[0][...])
    for o_ref, w_ref in zip(o_parts[1:], w_parts[1:]):
        y = y + _dot(o_ref[0, rows], w_ref[...])
    return h + gate * (_rms(y) * gain)


def _mixer_operands(o_parts, w_out, tm):
    w_out = w_out.astype(BF16)
    cuts = np.cumsum([p.shape[-1] for p in o_parts])[:-1].tolist()
    w_parts = jnp.split(w_out, cuts, axis=0) if cuts else [w_out]
    specs = [pl.BlockSpec((1, tm, p.shape[-1]), lambda bi, si: (bi, si, 0)) for p in o_parts]
    specs += [pl.BlockSpec(w.shape, lambda bi, si: (0, 0)) for w in w_parts]
    return list(o_parts) + list(w_parts), specs


def _ffn_kernel(*refs, n_parts):
    o_parts = refs[:n_parts]
    w_parts = refs[n_parts:2 * n_parts]
    (h_ref, gain1_ref, gate1_ref, g_ref, sc_ref, sh_ref, wg_ref, wu_ref, wd_ref, gain_ref, gate_ref,
     o_ref) = refs[2 * n_parts:]
    h = _mixer_residual(o_parts, w_parts, h_ref[0], gain1_ref[...], gate1_ref[0])
    u = _modulate(h, g_ref[...], sc_ref[0], sh_ref[0]).astype(BF16)
    hid = (_silu(_dot(u, wg_ref[...])) * _dot(u, wu_ref[...])).astype(BF16)
    y = _dot(hid, wd_ref[...])
    o_ref[0] = h + gate_ref[0] * (_rms(y) * gain_ref[...])


def _mixer_tail_ffn(o_parts, w_out, h, gain1, gate1, gain_in, scale, shift, w_gate, w_up, w_down,
                    gain_out, gate):
    b, s, d = h.shape
    tm = TM_FFN
    ff = w_gate.shape[1]
    resident = functools.partial(pl.BlockSpec, pipeline_mode=pl.Buffered(1))
    vec = pl.BlockSpec((1, 1, d), lambda bi, si: (bi, 0, 0))
    row = pl.BlockSpec((1, d), lambda bi, si: (0, 0))
    mix_ops, mix_specs = _mixer_operands(o_parts, w_out, tm)
    return pl.pallas_call(
        functools.partial(_ffn_kernel, n_parts=len(o_parts)),
        grid=(b, s // tm),
        in_specs=mix_specs + [pl.BlockSpec((1, tm, d), lambda bi, si: (bi, si, 0)), row, vec,
                              row, vec, vec,
                              resident((d, ff), lambda bi, si: (0, 0)),
                              resident((d, ff), lambda bi, si: (0, 0)),
                              resident((ff, d), lambda bi, si: (0, 0)),
                              row, vec],
        out_specs=pl.BlockSpec((1, tm, d), lambda bi, si: (bi, si, 0)),
        out_shape=jax.ShapeDtypeStruct((b, s, d), F32),
        compiler_params=_cparams(("parallel", "parallel")),
        name="dense_swiglu",
    )(*mix_ops, h, gain1.reshape(1, d), gate1.reshape(b, 1, d),
      gain_in.reshape(1, d), scale.reshape(b, 1, d), shift.reshape(b, 1, d),
      w_gate.astype(BF16), w_up.astype(BF16), w_down.astype(BF16),
      gain_out.reshape(1, d), gate.reshape(b, 1, d))


def _router_kernel(o_ref, wo_ref, h_ref, gain1_ref, gate1_ref, g_ref, sc_ref, sh_ref, rw_ref,
                   hout_ref, mi_ref, mf_ref, cnt_ref, carry_ref, *, tm):
    @pl.when((pl.program_id(0) == 0) & (pl.program_id(1) == 0))
    def _():
        carry_ref[...] = jnp.zeros_like(carry_ref)

    tr = tm // ROUTER_ROW_CHAINS
    lanef = lax.broadcasted_iota(I32, (tr, LANES), 1).astype(F32)
    r = lax.broadcasted_iota(I32, (tr, tr), 0)
    c = lax.broadcasted_iota(I32, (tr, tr), 1)
    before = jnp.where(c < r, 1.0, 0.0).astype(BF16)
    counts = carry_ref[0:1, :]
    for ch in range(ROUTER_ROW_CHAINS):
        rows = slice(ch * tr, (ch + 1) * tr)
        h = _mixer_residual([o_ref], [wo_ref], h_ref[0, rows], gain1_ref[...], gate1_ref[0], rows)
        hout_ref[0, rows] = h
        u = _modulate(h, g_ref[...], sc_ref[0], sh_ref[0])
        logits = _dot_split(u, rw_ref[...])
        lg = jnp.where(lanef < N_EXPERTS, logits, NEG_INF)
        v1 = jnp.max(lg, axis=1, keepdims=True)
        i1 = jnp.min(jnp.where(lg == v1, lanef, float(LANES)), axis=1, keepdims=True)
        lg2 = jnp.where(lanef == i1, NEG_INF, lg)
        v2 = jnp.max(lg2, axis=1, keepdims=True)
        i2 = jnp.min(jnp.where(lg2 == v2, lanef, float(LANES)), axis=1, keepdims=True)
        e2 = jnp.exp(v2 - v1)
        p1 = 1.0 / (1.0 + e2)
        p2 = e2 / (1.0 + e2)
        oh1 = jnp.where(lanef == i1, 1.0, 0.0)
        oh2 = jnp.where(lanef == i2, 1.0, 0.0)
        oh = oh1 + oh2
        tot = _dot(before, oh.astype(BF16)) + counts
        rank1 = jnp.sum(oh1 * tot, axis=1, keepdims=True)
        rank2 = jnp.sum(oh2 * tot, axis=1, keepdims=True)
        counts = counts + jnp.sum(oh, axis=0, keepdims=True)
        mi = jnp.where(lanef == 0.0, i1, jnp.where(lanef == 1.0, i2,
             jnp.where(lanef == 2.0, rank1, jnp.where(lanef == 3.0, rank2, 0.0))))
        mi_ref[rows] = mi.astype(I32)
        mf_ref[rows] = jnp.where(lanef == 0.0, p1, jnp.where(lanef == 1.0, p2, 0.0))
    carry_ref[...] = jnp.broadcast_to(counts, carry_ref.shape)
    cnt_ref[...] = carry_ref[...]


def _mixer_tail_router(o, w_out, h, gain1, gate1, gain, scale, shift, router_w):
    b, s, d = h.shape
    tm = TM_ROUTE
    n = b * s
    ns = s // tm
    rw = jnp.pad(router_w.astype(F32), ((0, 0), (0, LANES - router_w.shape[1])))
    vec = pl.BlockSpec((1, 1, d), lambda bi, si: (bi, 0, 0))
    row = pl.BlockSpec((1, d), lambda bi, si: (0, 0))
    act = pl.BlockSpec((1, tm, d), lambda bi, si: (bi, si, 0))
    meta = pl.BlockSpec((tm, LANES), lambda bi, si: (bi * ns + si, 0))
    mix_ops, mix_specs = _mixer_operands([o], w_out, tm)
    return pl.pallas_call(
        functools.partial(_router_kernel, tm=tm),
        grid=(b, ns),
        in_specs=mix_specs + [act, row, vec, row, vec, vec, pl.BlockSpec(rw.shape, lambda bi, si: (0, 0))],
        out_specs=[act, meta, meta, pl.BlockSpec((8, LANES), lambda bi, si: (0, 0))],
        out_shape=[jax.ShapeDtypeStruct((b, s, d), F32),
                   jax.ShapeDtypeStruct((n, LANES), I32),
                   jax.ShapeDtypeStruct((n, LANES), F32),
                   jax.ShapeDtypeStruct((8, LANES), F32)],
        scratch_shapes=[pltpu.VMEM((8, LANES), F32)],
        compiler_params=_cparams(("arbitrary", "arbitrary")),
        name="moe_router",
    )(*mix_ops, h, gain1.reshape(1, d), gate1.reshape(b, 1, d),
      gain.reshape(1, d), scale.reshape(b, 1, d), shift.reshape(b, 1, d), rw)


def _scatter_kernel(dest_ref, ztile_ref, h_ref, g_ref, sc_ref, sh_ref, xs_ref, ubuf, zbuf, sems, zsem,
                    *, tm, ns):
    step = pl.program_id(0) * ns + pl.program_id(1)
    nsteps = pl.num_programs(0) * ns
    slot = lax.rem(step, 2)
    tz = zbuf.shape[0]

    @pl.when(step == 0)
    def _():
        zbuf[...] = jnp.zeros_like(zbuf)
        for e in range(2 * N_EXPERTS):
            @pl.when(ztile_ref[e] >= 0)
            def _():
                row0 = pl.multiple_of(ztile_ref[e] * tz, tz)
                pltpu.make_async_copy(zbuf, xs_ref.at[pl.ds(row0, tz), :], zsem).start()
        for e in range(2 * N_EXPERTS):
            @pl.when(ztile_ref[e] >= 0)
            def _():
                pltpu.make_async_copy(zbuf, xs_ref.at[pl.ds(0, tz), :], zsem).wait()

    def wait_slot(sl):
        for _ in range(2):
            pltpu.make_async_copy(ubuf.at[sl], ubuf.at[sl], sems.at[sl]).wait()

    base = step * tm

    def run(sl):
        @pl.when(step >= 2)
        def _():
            wait_slot(sl)

        ubuf[sl] = _modulate(h_ref[0], g_ref[...], sc_ref[0], sh_ref[0])

        def issue(i, carry):
            t = 2 * (base + i)
            src = ubuf.at[sl, pl.ds(i, 1), :]
            pltpu.make_async_copy(src, xs_ref.at[pl.ds(dest_ref[t], 1), :], sems.at[sl]).start()
            pltpu.make_async_copy(src, xs_ref.at[pl.ds(dest_ref[t + 1], 1), :], sems.at[sl]).start()
            return carry

        lax.fori_loop(0, tm, issue, 0, unroll=DMA_UNROLL)

        @pl.when(step == nsteps - 1)
        def _():
            wait_slot(sl)

            @pl.when(nsteps >= 2)
            def _():
                wait_slot(1 - sl)

    for sl in range(2):
        pl.when(slot == sl)(functools.partial(run, sl))


def _scatter(dest, zero_tile, h, gain, scale, shift, m_pad):
    b, s, d = h.shape
    tm = TM_SCATTER
    ns = s // tm
    vec = pl.BlockSpec((1, 1, d), lambda bi, si, dest, zt: (bi, 0, 0))
    grid_spec = pltpu.PrefetchScalarGridSpec(
        num_scalar_prefetch=2,
        grid=(b, ns),
        in_specs=[pl.BlockSpec((1, tm, d), lambda bi, si, dest, zt: (bi, si, 0)),
                  pl.BlockSpec((1, d), lambda bi, si, dest, zt: (0, 0)),
                  vec, vec],
        out_specs=pl.BlockSpec(memory_space=pl.ANY),
        scratch_shapes=[pltpu.VMEM((2, tm, d), F32), pltpu.VMEM((TM_EXPERT, d), F32),
                        pltpu.SemaphoreType.DMA((2,)), pltpu.SemaphoreType.DMA],
    )
    return pl.pallas_call(
        functools.partial(_scatter_kernel, tm=tm, ns=ns),
        grid_spec=grid_spec,
        out_shape=jax.ShapeDtypeStruct((m_pad, d), F32),
        compiler_params=_cparams(("arbitrary", "arbitrary")),
        name="moe_scatter",
    )(dest, zero_tile, h, gain.reshape(1, d), scale.reshape(b, 1, d), shift.reshape(b, 1, d))


def _expert_kernel(te_ref, tv_ref, tx_ref, x_ref, wg_ref, wu_ref, wd_ref, o_ref):
    t = pl.program_id(0)
    f = pl.program_id(1)

    @pl.when(tv_ref[t] == 1)
    def _():
        x = x_ref[...].astype(BF16)
        hid = (_silu(_dot(x, wg_ref[0])) * _dot(x, wu_ref[0])).astype(BF16)
        y = _dot(hid, wd_ref[0])

        @pl.when(f == 0)
        def _():
            o_ref[...] = y

        @pl.when(f > 0)
        def _():
            o_ref[...] = o_ref[...] + y

    @pl.when((tv_ref[t] == 0) & (f == 0))
    def _():
        o_ref[...] = jnp.zeros_like(o_ref)


def _experts(tile_expert, tile_valid, tile_x, xs, w_gate, w_up, w_down):
    m_pad, d = xs.shape
    tm = TM_EXPERT
    n_tiles = tile_expert.shape[0]
    ff = w_gate.shape[2]
    fs = FF_STEPS_EXPERT
    tf = ff // fs

    def ff_idx(f, tv, t):
        return f * tv[t] + (fs - 1) * (1 - tv[t])

    grid_spec = pltpu.PrefetchScalarGridSpec(
        num_scalar_prefetch=3,
        grid=(n_tiles, fs),
        in_specs=[pl.BlockSpec((tm, d), lambda t, f, te, tv, tx: (tx[t], 0)),
                  pl.BlockSpec((1, d, tf), lambda t, f, te, tv, tx: (te[t], 0, ff_idx(f, tv, t))),
                  pl.BlockSpec((1, d, tf), lambda t, f, te, tv, tx: (te[t], 0, ff_idx(f, tv, t))),
                  pl.BlockSpec((1, tf, d), lambda t, f, te, tv, tx: (te[t], ff_idx(f, tv, t), 0))],
        out_specs=pl.BlockSpec((tm, d), lambda t, f, te, tv, tx: (t, 0)),
    )
    return pl.pallas_call(
        _expert_kernel,
        grid_spec=grid_spec,
        out_shape=jax.ShapeDtypeStruct((m_pad, d), F32),
        compiler_params=_cparams(("arbitrary", "arbitrary")),
        name="moe_experts",
    )(tile_expert, tile_valid, tile_x, xs,
      w_gate.astype(BF16), w_up.astype(BF16), w_down.astype(BF16))


def _combine_kernel(dest_ref, y_ref, mf_ref, h_ref, gain_ref, gate_ref, o_ref, ybuf, sems, *, tm, ns):
    step = pl.program_id(0) * ns + pl.program_id(1)
    nsteps = pl.num_programs(0) * ns
    slot = lax.rem(step, 2)

    def issue(st, sl):
        base = st * tm

        def body(i, carry):
            t = 2 * (base + i)
            pltpu.make_async_copy(y_ref.at[pl.ds(dest_ref[t], 1), :],
                                  ybuf.at[sl, 0, pl.ds(i, 1), :], sems.at[sl]).start()
            pltpu.make_async_copy(y_ref.at[pl.ds(dest_ref[t + 1], 1), :],
                                  ybuf.at[sl, 1, pl.ds(i, 1), :], sems.at[sl]).start()
            return carry

        lax.fori_loop(0, tm, body, 0, unroll=DMA_UNROLL)

    @pl.when(step == 0)
    def _():
        issue(0, 0)

    def run(sl):
        @pl.when(step + 1 < nsteps)
        def _():
            issue(step + 1, 1 - sl)

        for k in range(2):
            pltpu.make_async_copy(ybuf.at[sl, k], ybuf.at[sl, k], sems.at[sl]).wait()
        mf = mf_ref[...]
        y = mf[:, 0:1] * ybuf[sl, 0] + mf[:, 1:2] * ybuf[sl, 1]
        o_ref[0] = h_ref[0] + gate_ref[0] * (_rms(y) * gain_ref[...])

    for sl in range(2):
        pl.when(slot == sl)(functools.partial(run, sl))


def _combine(dest, ys, mf, h, gain, gate):
    b, s, d = h.shape
    tm = TM_COMBINE
    ns = s // tm
    grid_spec = pltpu.PrefetchScalarGridSpec(
        num_scalar_prefetch=1,
        grid=(b, ns),
        in_specs=[pl.BlockSpec(memory_space=pl.ANY),
                  pl.BlockSpec((tm, LANES), lambda bi, si, dest: (bi * ns + si, 0)),
                  pl.BlockSpec((1, tm, d), lambda bi, si, dest: (bi, si, 0)),
                  pl.BlockSpec((1, d), lambda bi, si, dest: (0, 0)),
                  pl.BlockSpec((1, 1, d), lambda bi, si, dest: (bi, 0, 0))],
        out_specs=pl.BlockSpec((1, tm, d), lambda bi, si, dest: (bi, si, 0)),
        scratch_shapes=[pltpu.VMEM((2, 2, tm, d), F32), pltpu.SemaphoreType.DMA((2,))],
    )
    return pl.pallas_call(
        functools.partial(_combine_kernel, tm=tm, ns=ns),
        grid_spec=grid_spec,
        out_shape=jax.ShapeDtypeStruct((b, s, d), F32),
        compiler_params=_cparams(("arbitrary", "arbitrary")),
        name="moe_combine",
    )(dest, ys, mf, h, gain.reshape(1, d), gate.reshape(b, 1, d))


def _mixer_tail_moe(o, w_out, h, gain1, gate1, gain_in, scale, shift, router_w, w_gate, w_up, w_down,
                    gain_out, gate):
    b, s, d = h.shape
    n = b * s
    tm = TM_EXPERT
    h, mi, mf, cnt = _mixer_tail_router(o, w_out, h, gain1, gate1, gain_in, scale, shift, router_w)
    counts = cnt[0, :N_EXPERTS].astype(I32)
    tiles_per = (counts + tm - 1) // tm
    seg_start = (jnp.cumsum(tiles_per) - tiles_per) * tm
    dest = (seg_start[mi[:, 0:2]] + mi[:, 2:4]).reshape(2 * n)
    n_tiles = (2 * n) // tm + N_EXPERTS
    m_pad = n_tiles * tm
    tile_end = jnp.cumsum(tiles_per)
    tidx = jnp.arange(n_tiles, dtype=I32)
    tile_valid = (tidx < tile_end[-1]).astype(I32)
    tile_expert = jnp.minimum(jnp.searchsorted(tile_end, tidx, side="right"), N_EXPERTS - 1).astype(I32)
    tile_x = jnp.minimum(tidx, tile_end[-1] - 1)
    tail = tile_end[-1] + jnp.arange(N_EXPERTS, dtype=I32)
    zero_tile = jnp.concatenate([jnp.where(tiles_per > 0, tile_end - 1, -1),
                                 jnp.where(tail < n_tiles, tail, -1)]).astype(I32)
    xs = _scatter(dest, zero_tile, h, gain_in, scale, shift, m_pad)
    ys = _experts(tile_expert, tile_valid, tile_x, xs, w_gate, w_up, w_down)
    return _combine(dest, ys, mf, h, gain_out, gate)


def kernel(x, c, mod_w, mod_b, norm_g, attn_in_w_even, fox_gate_bias, attn_out_w_even,
           attn_in_w_odd, attn_out_w_odd, rel_bias_table, ffn_w_gate, ffn_w_up, ffn_w_down,
           router_w, exp_w_gate, exp_w_up, exp_w_down):
    depth = mod_w.shape[0]
    s_len = x.shape[1]
    mods = _mods(c, mod_w, mod_b)
    dil_bias, moba_bias = _bias_tiles(rel_bias_table, s_len)
    h = x
    for layer in range(depth):
        j = layer // 2
        sh1, sc1, g1, sh2, sc2, g2 = jnp.split(mods[layer], 6, axis=-1)
        gains = norm_g[layer]
        if layer % 2 == 0:
            fox_in, moba_in = _inproj_even(h, gains[0], sc1, sh1, attn_in_w_even[j], fox_gate_bias[j])
            o_parts = [_fox_attention(*fox_in), _moba_attention(*moba_in, moba_bias)]
            h = _mixer_tail_ffn(o_parts, attn_out_w_even[j], h, gains[1], g1, gains[2], sc2, sh2,
                                ffn_w_gate[j], ffn_w_up[j], ffn_w_down[j], gains[3], g2)
        else:
            q, k, v = _inproj_odd(h, gains[0], sc1, sh1, attn_in_w_odd[j])
            o = _dilated_attention(q, k, v, dil_bias)
            h = _mixer_tail_moe(o, attn_out_w_odd[j], h, gains[1], g1, gains[2], sc2, sh2, router_w[j],
                                exp_w_gate[j], exp_w_up[j], exp_w_down[j], gains[3], g2)
    return h
```

```python
import functools
import math

import numpy as np
import jax
import jax.numpy as jnp
from jax import lax
from jax.experimental import pallas as pl
from jax.experimental.pallas import tpu as pltpu

F32 = jnp.float32
BF16 = jnp.bfloat16
I32 = jnp.int32

HEAD_DIM = 64
LANES = 128
N_HEADS = 16
N_HEADS_FOX = 8
ATTN_SCALE = HEAD_DIM ** -0.5
LOG2E = math.log2(math.e)
Q_SCALE = ATTN_SCALE * LOG2E
N_DECAY_PIECES = 3
MOBA_BLOCK = 256
MOBA_TOPK = 3
DIL_PATTERNS = ((128, 1), (512, 4), (2048, 16))
NUM_BUCKETS = 32
MAX_DISTANCE = 2048
N_EXPERTS = 8
NORM_EPS = 1e-6
NEG_INF = float("-inf")
MASK_BIG = 1e30

VMEM_LIMIT = 56 * 1024 * 1024

TM_PROJ = 512
TM_FFN = 512
TQ_FOX = 2048
TQ_MOBA = 2048
TK_FOX = 512
FOX_PAIRS_PER_STEP = 1
MOBA_PAIRS_PER_STEP = 1
TM_ROUTE = 512
ROUTER_ROW_CHAINS = 2
TM_SCATTER = 256
TM_EXPERT = 512
TM_COMBINE = 256
FF_STEPS_EXPERT = 2
DMA_UNROLL = True
DIL_CHUNKS_PER_STEP = 16


def _cparams(sem):
    return pltpu.CompilerParams(dimension_semantics=sem, vmem_limit_bytes=VMEM_LIMIT)


def _t5_bucket_np(n):
    n = np.maximum(n, 0)
    max_exact = NUM_BUCKETS // 2
    nf = np.maximum(n, 1).astype(np.float64)
    large = max_exact + (np.log(nf / max_exact) / math.log(MAX_DISTANCE / max_exact)
                         * (NUM_BUCKETS - max_exact)).astype(np.int64)
    large = np.minimum(large, NUM_BUCKETS - 1)
    return np.where(n < max_exact, n, large)


_MAX_DIST = 1 << 16
_BUCKET_OF = _t5_bucket_np(np.arange(_MAX_DIST))
_BUCKET_THR = [int(np.searchsorted(_BUCKET_OF, k, side="left")) for k in range(NUM_BUCKETS)]


def _bias_from_dist(tab_ref, h, dist, dlo, dhi):
    lo_b = int(_BUCKET_OF[max(dlo, 0)])
    hi_b = int(_BUCKET_OF[dhi])
    val = jnp.zeros(dist.shape, F32) + tab_ref[lo_b, h]
    for k in range(lo_b + 1, hi_b + 1):
        val = jnp.where(dist >= _BUCKET_THR[k], tab_ref[k, h], val)
    return val


def _dil_bias_kernel(tab_ref, o_ref):
    h = pl.program_id(0)
    for g, (window, dil) in enumerate(DIL_PATTERNS):
        span = window // dil
        i = lax.broadcasted_iota(I32, (span, 2 * span), 0)
        j = lax.broadcasted_iota(I32, (span, 2 * span), 1)
        rel = i + span - j
        val = _bias_from_dist(tab_ref, h, rel * dil, 0, span * dil) * LOG2E
        band = jnp.where(rel >= 0, jnp.where(rel <= span, val, NEG_INF), NEG_INF)
        o_ref[0, 2 * g] = band
        o_ref[0, 2 * g + 1] = jnp.where(j >= span, band, NEG_INF)


def _moba_bias_kernel(tab_ref, o_ref, *, n_blk, head0):
    h = pl.program_id(0) + head0
    i = lax.broadcasted_iota(I32, (MOBA_BLOCK, MOBA_BLOCK), 0)
    j = lax.broadcasted_iota(I32, (MOBA_BLOCK, MOBA_BLOCK), 1)
    for d in range(n_blk):
        dist = d * MOBA_BLOCK + i - j
        val = _bias_from_dist(tab_ref, h, dist, d * MOBA_BLOCK - (MOBA_BLOCK - 1),
                              d * MOBA_BLOCK + (MOBA_BLOCK - 1)) * LOG2E
        if d == 0:
            val = jnp.where(dist >= 0, val, NEG_INF)
        o_ref[0, d] = val


def _bias_tiles(rel_bias_table, s_len):
    n_blk = s_len // MOBA_BLOCK
    span = DIL_PATTERNS[0][0]
    n_var = 2 * len(DIL_PATTERNS)
    smem = pl.BlockSpec(memory_space=pltpu.SMEM)
    dil = pl.pallas_call(
        _dil_bias_kernel,
        grid=(N_HEADS,),
        in_specs=[smem],
        out_specs=pl.BlockSpec((1, n_var, span, 2 * span), lambda h: (h, 0, 0, 0)),
        out_shape=jax.ShapeDtypeStruct((N_HEADS, n_var, span, 2 * span), F32),
        compiler_params=_cparams(("parallel",)),
        name="dil_bias",
    )(rel_bias_table)
    n_moba = N_HEADS - N_HEADS_FOX
    moba = pl.pallas_call(
        functools.partial(_moba_bias_kernel, n_blk=n_blk, head0=N_HEADS_FOX),
        grid=(n_moba,),
        in_specs=[smem],
        out_specs=pl.BlockSpec((1, n_blk, MOBA_BLOCK, MOBA_BLOCK), lambda h: (h, 0, 0, 0)),
        out_shape=jax.ShapeDtypeStruct((n_moba, n_blk, MOBA_BLOCK, MOBA_BLOCK), F32),
        compiler_params=_cparams(("parallel",)),
        name="moba_bias",
    )(rel_bias_table)
    return dil, moba


def _split_bf16(a):
    hi = a.astype(BF16)
    lo = (a - hi.astype(F32)).astype(BF16)
    return hi, lo


def _dot(a, b):
    return jnp.dot(a, b, preferred_element_type=F32)


def _dot_nt(a, b):
    return lax.dot_general(a, b, (((1,), (1,)), ((), ())), preferred_element_type=F32)


def _dot_split(a, b):
    a_hi, a_lo = _split_bf16(a)
    b_hi, b_lo = _split_bf16(b)
    return _dot(a_hi, b_hi) + (_dot(a_hi, b_lo) + _dot(a_lo, b_hi))


def _rms(x):
    return x * lax.rsqrt(jnp.mean(x * x, axis=-1, keepdims=True) + NORM_EPS)


def _modulate(x, gain, scale, shift):
    return (_rms(x) * gain) * (1.0 + scale) + shift


def _silu(x):
    return x * jax.nn.sigmoid(x)


def _mods_kernel(c_ref, w_ref, b_ref, o_ref):
    o_ref[0] = _dot_split(_silu(c_ref[...]), w_ref[0]) + b_ref[0]


def _mods(c, mod_w, mod_b):
    depth, d, e = mod_w.shape
    b = c.shape[0]
    tn = 1536
    return pl.pallas_call(
        _mods_kernel,
        grid=(depth, e // tn),
        in_specs=[pl.BlockSpec((b, d), lambda l, j: (0, 0)),
                  pl.BlockSpec((1, d, tn), lambda l, j: (l, 0, j)),
                  pl.BlockSpec((1, 1, tn), lambda l, j: (l, 0, j))],
        out_specs=pl.BlockSpec((1, b, tn), lambda l, j: (l, 0, j)),
        out_shape=jax.ShapeDtypeStruct((depth, b, e), F32),
        compiler_params=_cparams(("parallel", "parallel")),
        name="adaln_mods",
    )(c, mod_w, mod_b.reshape(depth, 1, e))


def _inproj_even_kernel(h_ref, g_ref, sc_ref, sh_ref, w_ref, wf_ref, gb_ref,
                        qa_ref, k0a_ref, k1a_ref, v0a_ref, v1a_ref,
                        qb_ref, k0b_ref, k1b_ref, v0b_ref, v1b_ref, carry_ref, *, tm):
    si = pl.program_id(1)
    u = _modulate(h_ref[0], g_ref[...], sc_ref[0], sh_ref[0]).astype(BF16)
    width = qa_ref.shape[-1]
    n_pairs = width // LANES

    def proj(i):
        return _dot(u, w_ref[:, i * width:(i + 1) * width])

    lane = lax.broadcasted_iota(I32, (1, LANES), 1)
    left = lane < HEAD_DIM
    row = lax.broadcasted_iota(I32, (tm, LANES), 0)

    x = _dot(u, wf_ref[...]) + gb_ref[...]
    lf = jnp.where(lane < N_HEADS_FOX, jnp.minimum(x, 0.0) - jnp.log1p(jnp.exp(-jnp.abs(x))), 0.0)
    k = 1
    while k < tm:
        lf = lf + jnp.where(row >= k, pltpu.roll(lf, k, axis=0), 0.0)
        k *= 2

    @pl.when(si == 0)
    def _():
        carry_ref[...] = jnp.zeros_like(carry_ref)

    cum = lf + carry_ref[0:1, :]
    carry_ref[...] = jnp.broadcast_to(cum[tm - 1:tm, :], carry_ref.shape)
    rest = cum * (-LOG2E)
    decay = jnp.zeros((tm, LANES), F32)
    for p in range(N_DECAY_PIECES):
        piece = rest.astype(BF16).astype(F32)
        rest = rest - piece
        decay = decay + (pltpu.roll(piece, p * N_HEADS_FOX, axis=1) if p else piece)
    decay_lo = decay.astype(BF16)
    decay_hi = pltpu.roll(decay, HEAD_DIM, axis=1).astype(BF16)

    ones = jnp.ones((1, LANES), BF16)
    blk = (si * tm + row) // MOBA_BLOCK
    blk_lo = jnp.where(lane == blk, 1.0, 0.0).astype(BF16)
    blk_hi = jnp.where(lane == blk + HEAD_DIM, 1.0, 0.0).astype(BF16)

    def emit(first, k0_ref, k1_ref, v0_ref, v1_ref, k_lo, k_hi):
        kk = proj(first + 1).astype(BF16)
        vv = proj(first + 2).astype(BF16)
        for hp in range(n_pairs):
            sl = slice(hp * LANES, (hp + 1) * LANES)
            k0_ref[0, :, sl] = jnp.where(left, kk[:, sl], k_hi)
            k1_ref[0, :, sl] = jnp.where(left, k_lo, kk[:, sl])
            v0_ref[0, :, sl] = jnp.where(left, vv[:, sl], ones)
            v1_ref[0, :, sl] = jnp.where(left, ones, vv[:, sl])

    qa_ref[0] = (proj(0) * Q_SCALE).astype(BF16)
    emit(0, k0a_ref, k1a_ref, v0a_ref, v1a_ref, decay_lo, decay_hi)
    qb_ref[0] = (proj(3) * Q_SCALE).astype(BF16)
    emit(3, k0b_ref, k1b_ref, v0b_ref, v1b_ref, blk_lo, blk_hi)


def _inproj_even(h, gain, scale, shift, w_in, gate_bias):
    b, s, d = h.shape
    tm = TM_PROJ
    da = N_HEADS_FOX * HEAD_DIM
    assert s // MOBA_BLOCK <= HEAD_DIM and N_DECAY_PIECES * N_HEADS_FOX <= HEAD_DIM
    cuts = np.cumsum([da, da, da, N_HEADS_FOX, da, da]).tolist()
    q_a, k_a, v_a, f_a, q_b, k_b, v_b = jnp.split(w_in, cuts, axis=1)
    w = jnp.concatenate([q_a, k_a, v_a, q_b, k_b, v_b], axis=1).astype(BF16)
    wf = jnp.pad(f_a, ((0, 0), (0, LANES - N_HEADS_FOX))).astype(BF16)
    gb = jnp.pad(gate_bias.astype(F32), (0, LANES - N_HEADS_FOX)).reshape(1, LANES)
    act = jax.ShapeDtypeStruct((b, s, da), BF16)
    act_spec = pl.BlockSpec((1, tm, da), lambda bi, si: (bi, si, 0))
    vec = pl.BlockSpec((1, 1, d), lambda bi, si: (bi, 0, 0))
    outs = pl.pallas_call(
        functools.partial(_inproj_even_kernel, tm=tm),
        grid=(b, s // tm),
        in_specs=[pl.BlockSpec((1, tm, d), lambda bi, si: (bi, si, 0)),
                  pl.BlockSpec((1, d), lambda bi, si: (0, 0)),
                  vec, vec,
                  pl.BlockSpec(w.shape, lambda bi, si: (0, 0)),
                  pl.BlockSpec(wf.shape, lambda bi, si: (0, 0)),
                  pl.BlockSpec(gb.shape, lambda bi, si: (0, 0))],
        out_specs=[act_spec] * 10,
        out_shape=[act] * 10,
        scratch_shapes=[pltpu.VMEM((8, LANES), F32)],
        compiler_params=_cparams(("parallel", "arbitrary")),
        name="inproj_even",
    )(h, gain.reshape(1, d), scale.reshape(b, 1, d), shift.reshape(b, 1, d), w, wf, gb)
    return outs[:5], outs[5:]


def _inproj_odd_kernel(h_ref, g_ref, sc_ref, sh_ref, w_ref, q_ref, k_ref, v_ref):
    u = _modulate(h_ref[0], g_ref[...], sc_ref[0], sh_ref[0]).astype(BF16)
    width = q_ref.shape[-1]
    q_ref[0] = _dot(u, w_ref[:, 0:width]) * Q_SCALE
    k_ref[0] = _dot(u, w_ref[:, width:2 * width])
    v_ref[0] = _dot(u, w_ref[:, 2 * width:3 * width])


def _inproj_odd(h, gain, scale, shift, w_in):
    b, s, d = h.shape
    tm = TM_PROJ
    dq = w_in.shape[1] // 3
    act = jax.ShapeDtypeStruct((b, s, dq), F32)
    act_spec = pl.BlockSpec((1, tm, dq), lambda bi, si: (bi, si, 0))
    vec = pl.BlockSpec((1, 1, d), lambda bi, si: (bi, 0, 0))
    return pl.pallas_call(
        _inproj_odd_kernel,
        grid=(b, s // tm),
        in_specs=[pl.BlockSpec((1, tm, d), lambda bi, si: (bi, si, 0)),
                  pl.BlockSpec((1, d), lambda bi, si: (0, 0)),
                  vec, vec,
                  pl.BlockSpec(w_in.shape, lambda bi, si: (0, 0))],
        out_specs=[act_spec] * 3,
        out_shape=[act] * 3,
        compiler_params=_cparams(("parallel", "parallel")),
        name="inproj_odd",
    )(h, gain.reshape(1, d), scale.reshape(b, 1, d), shift.reshape(b, 1, d), w_in.astype(BF16))


def _tile_lanes(x, width):
    return jnp.concatenate([x] * (width // LANES), axis=1)


def _flash_update(s, v, m_ref, acc_ref):
    m_prev = m_ref[...]
    m_new = jnp.maximum(m_prev, jnp.max(s, axis=1, keepdims=True))
    p = jnp.exp2(s - _tile_lanes(m_new, s.shape[1]))
    acc_ref[...] = jnp.exp2(m_prev - m_new) * acc_ref[...] + _dot(p.astype(BF16), v)
    m_ref[...] = m_new


def _finish_pair(acc_ref, left, first=0):
    acc0 = acc_ref[first]
    acc1 = acc_ref[first + 1]
    den = pltpu.roll(jnp.where(left, acc1, acc0), HEAD_DIM, axis=1)
    return jnp.where(left, acc0, acc1) / den


def _fox_kernel(q_ref, k0_ref, k1_ref, v0_ref, v1_ref, o_ref, m_ref, acc_ref, *, tq, tk, n_pairs):
    qi = pl.program_id(2)
    lane = lax.broadcasted_iota(I32, (1, LANES), 1)
    left = lane < HEAD_DIM

    def piece_lanes(lane0):
        hit = lane == lane0
        for p in range(1, N_DECAY_PIECES):
            hit = jnp.logical_or(hit, lane == lane0 + p * N_HEADS_FOX)
        return jnp.where(hit, 1.0, 0.0).astype(BF16)

    chains = []
    for pr in range(n_pairs):
        hp = pl.program_id(1) * n_pairs + pr
        lanes = slice(pr * LANES, (pr + 1) * LANES)
        q = q_ref[0, :, lanes]
        chains.append((jnp.where(left, q, piece_lanes(HEAD_DIM + 2 * hp)), k0_ref, v0_ref, lanes))
        chains.append((jnp.where(left, piece_lanes(2 * hp + 1), q), k1_ref, v1_ref, lanes))
    row = lax.broadcasted_iota(I32, (tq, tk), 0)
    col = lax.broadcasted_iota(I32, (tq, tk), 1)
    m_ref[...] = jnp.full(m_ref.shape, NEG_INF, F32)
    acc_ref[...] = jnp.zeros(acc_ref.shape, F32)
    n_sub = tq // tk

    def step(kv, mask, row0=0):
        off = pl.multiple_of(kv * tk, tk)
        rows = pl.ds(row0, tq - row0)
        for c, (qc, k_ref, v_ref, lanes) in enumerate(chains):
            s = _dot_nt(qc[row0:], k_ref[0, pl.ds(off, tk), lanes])
            if mask is not None:
                s = jnp.where(mask[row0:], s, NEG_INF)
            _flash_update(s, v_ref[0, pl.ds(off, tk), lanes], m_ref.at[c, rows], acc_ref.at[c, rows])

    def body(kv, carry):
        step(kv, None)
        return carry

    lax.fori_loop(0, qi * n_sub, body, 0)
    for d in range(n_sub):
        step(qi * n_sub + d, col + d * tk <= row, d * tk)
    for pr in range(n_pairs):
        o_ref[0, :, pr * LANES:(pr + 1) * LANES] = _finish_pair(acc_ref, left, 2 * pr).astype(o_ref.dtype)


def _fox_attention(q, k0, k1, v0, v1):
    b, s, da = q.shape
    n_pairs = FOX_PAIRS_PER_STEP
    width = n_pairs * LANES
    tq = min(TQ_FOX, s)
    kv = pl.BlockSpec((1, s, width), lambda bi, h, qi: (bi, 0, h))
    return pl.pallas_call(
        functools.partial(_fox_kernel, tq=tq, tk=min(TK_FOX, tq), n_pairs=n_pairs),
        grid=(b, da // width, s // tq),
        in_specs=[pl.BlockSpec((1, tq, width), lambda bi, h, qi: (bi, qi, h)), kv, kv, kv, kv],
        out_specs=pl.BlockSpec((1, tq, width), lambda bi, h, qi: (bi, qi, h)),
        out_shape=jax.ShapeDtypeStruct((b, s, da), BF16),
        scratch_shapes=[pltpu.VMEM((2 * n_pairs, tq, LANES), F32)] * 2,
        compiler_params=_cparams(("parallel", "parallel", "arbitrary")),
        name="fox_attention",
    )(q, k0, k1, v0, v1)


def _moba_kernel(q_ref, k0_ref, k1_ref, v0_ref, v1_ref, bias_ref, o_ref, km_ref, m_ref, acc_ref,
                 *, n_blk, tq, n_pairs):
    blk = MOBA_BLOCK
    tk = 2 * blk
    nq = tq // blk
    a = pl.program_id(2)
    gate_lane0 = (HEAD_DIM, 0)
    lane = lax.broadcasted_iota(I32, (1, LANES), 1)
    left = lane < HEAD_DIM
    mine = (left, jnp.logical_not(left))
    k_refs = (k0_ref, k1_ref)
    v_refs = (v0_ref, v1_ref)

    @pl.when(a == 0)
    def _():
        km_ref[...] = jnp.zeros_like(km_ref)
        for pr in range(n_pairs):
            lanes = slice(pr * LANES, (pr + 1) * LANES)
            for n in range(n_blk):
                rows = slice(n * blk, (n + 1) * blk)
                kb = jnp.where(left, k0_ref[0, rows, lanes], k1_ref[0, rows, lanes]).astype(F32)
                mean = jnp.sum(kb, axis=0, keepdims=True) * (1.0 / blk)
                for lane0 in gate_lane0:
                    km_ref[pr, lane0 + n:lane0 + n + 1, :] = mean

    nb = -(-n_blk // 8) * 8
    blkf = lax.broadcasted_iota(I32, (nb, tq), 0).astype(F32)
    own = (lax.broadcasted_iota(I32, (nb, tq), 1) // blk + a * nq).astype(F32)
    chains = []
    for pr in range(n_pairs):
        lanes = slice(pr * LANES, (pr + 1) * LANES)
        q = q_ref[0, :, lanes]
        km_hi, km_lo = _split_bf16(km_ref[pr])
        for j in range(2):
            qj = jnp.where(mine[j], q, jnp.zeros_like(q))
            lane0 = gate_lane0[j]
            gate = (_dot_nt(km_hi, qj) + _dot_nt(km_lo, qj))[lane0:lane0 + nb]
            gate = jnp.where(blkf < own, gate, NEG_INF)
            pen = jnp.where(blkf == own, 0.0, -MASK_BIG)
            for _ in range(MOBA_TOPK):
                mx = jnp.max(gate, axis=0, keepdims=True)
                cand = jnp.where(gate == mx, jnp.where(mx > NEG_INF, blkf, float(LANES)), float(LANES))
                pick = blkf == jnp.min(cand, axis=0, keepdims=True)
                pen = jnp.where(pick, 0.0, pen)
                gate = jnp.where(pick, NEG_INF, gate)
            parts = [pen, jnp.zeros((LANES - lane0 - nb, tq), F32)]
            if lane0:
                parts.insert(0, jnp.zeros((lane0, tq), F32))
            pen_q = jnp.concatenate(parts, axis=0).T
            chains.append((jnp.where(mine[j], q, pen_q.astype(BF16)), k_refs[j], v_refs[j], lanes, 2 * pr + j))

    m_ref[...] = jnp.full(m_ref.shape, NEG_INF, F32)
    acc_ref[...] = jnp.zeros(acc_ref.shape, F32)

    def step(i, dist, r0):
        off = pl.multiple_of(i * tk, tk)
        rows = pl.ds(r0 * blk, tq - r0 * blk)
        for qc, k_ref, v_ref, lanes, h in chains:
            bias = jnp.concatenate(
                [jnp.concatenate([bias_ref[h, dist(r, c)] for c in range(2)], axis=1)
                 for r in range(r0, nq)], axis=0)
            s = _dot_nt(qc[r0 * blk:], k_ref[0, pl.ds(off, tk), lanes]) + bias
            _flash_update(s, v_ref[0, pl.ds(off, tk), lanes], m_ref.at[h, rows], acc_ref.at[h, rows])

    n_full = a * (nq // 2)

    def body(i, carry):
        step(i, lambda r, c: a * nq + r - 2 * i - c, 0)
        return carry

    lax.fori_loop(0, n_full, body, 0)
    for e in range(nq // 2):
        step(n_full + e, lambda r, c, e=e: max(r - 2 * e - c, 0), 2 * e)
    for pr in range(n_pairs):
        o_ref[0, :, pr * LANES:(pr + 1) * LANES] = _finish_pair(acc_ref, left, 2 * pr).astype(o_ref.dtype)


def _moba_attention(q, k0, k1, v0, v1, bias_tiles):
    b, s, db = q.shape
    n_pairs = MOBA_PAIRS_PER_STEP
    width = n_pairs * LANES
    blk = MOBA_BLOCK
    n_blk = s // blk
    tq = min(TQ_MOBA, s)
    assert n_blk <= HEAD_DIM, "block gates of one head must fit in the other head's lanes"
    assert s % tq == 0 and tq % (2 * blk) == 0
    kv = pl.BlockSpec((1, s, width), lambda h, bi, qi: (bi, 0, h))
    return pl.pallas_call(
        functools.partial(_moba_kernel, n_blk=n_blk, tq=tq, n_pairs=n_pairs),
        grid=(db // width, b, s // tq),
        in_specs=[pl.BlockSpec((1, tq, width), lambda h, bi, qi: (bi, qi, h)), kv, kv, kv, kv,
                  pl.BlockSpec((2 * n_pairs, n_blk, blk, blk), lambda h, bi, qi: (h, 0, 0, 0),
                               pipeline_mode=pl.Buffered(1))],
        out_specs=pl.BlockSpec((1, tq, width), lambda h, bi, qi: (bi, qi, h)),
        out_shape=jax.ShapeDtypeStruct((b, s, db), BF16),
        scratch_shapes=[pltpu.VMEM((n_pairs, LANES, LANES), F32)]
                       + [pltpu.VMEM((2 * n_pairs, tq, LANES), F32)] * 2,
        compiler_params=_cparams(("parallel", "parallel", "arbitrary")),
        name="moba_attention",
    )(q, k0, k1, v0, v1, bias_tiles)


def _dilated_kernel(q_ref, k_ref, v_ref, bias_ref, o_ref, m_ref, acc_ref, *, s_len):
    lane = lax.broadcasted_iota(I32, (1, LANES), 1)
    left = lane < HEAD_DIM
    ones = jnp.ones((1, LANES), BF16)
    order = sorted(range(len(DIL_PATTERNS)), key=lambda i: -DIL_PATTERNS[i][1])
    for g in order:
        window, dil = DIL_PATTERNS[g]
        merge = g != order[0]
        span = window // dil
        unit = span * dil
        nc = s_len // unit
        n_u = min(DIL_CHUNKS_PER_STEP, nc)
        groups = nc // n_u
        n_res = min(DIL_CHUNKS_PER_STEP // n_u, dil)

        def rows(ref, start, dil=dil, span=span):
            if dil == 1:
                return ref[0, pl.ds(start, span), :]
            return ref[0, pl.ds(start, span, stride=dil), :]

        def get(ref, j, start, dil=dil, span=span):
            if dil == 1:
                return ref[j, pl.ds(start, span), :]
            return ref[j, pl.ds(start, span, stride=dil), :]

        def put(ref, j, start, val, dil=dil, span=span):
            if dil == 1:
                ref[j, pl.ds(start, span), :] = val
            else:
                ref[j, pl.ds(start, span, stride=dil), :] = val

        def body(it, carry, g=g, merge=merge, n_u=n_u, n_res=n_res, groups=groups, unit=unit, rows=rows,
                 put=put, get=get):
            r0 = (it // groups) * n_res
            grp = it - (it // groups) * groups
            is_first = grp == 0
            results = []
            for dr in range(n_res):
                start0 = r0 + dr + grp * (n_u * unit)
                prev0 = start0 - jnp.where(is_first, 0, unit)
                starts = [start0 + u * unit for u in range(n_u)]
                kc = [rows(k_ref, st).astype(BF16) for st in [prev0] + starts]
                vc = [rows(v_ref, st).astype(BF16) for st in [prev0] + starts]
                vcs = ([jnp.where(left, v, ones) for v in vc], [jnp.where(left, ones, v) for v in vc])
                for u, start in enumerate(starts):
                    var = 2 * g + jnp.where(is_first, 1, 0) if u == 0 else 2 * g
                    q = rows(q_ref, start)
                    kb = jnp.concatenate([kc[u], kc[u + 1]], axis=0)
                    qq = jnp.concatenate([jnp.where(left, q, 0.0), jnp.where(left, 0.0, q)],
                                         axis=0).astype(BF16)
                    s_both = _dot_nt(qq, kb)
                    for j in range(2):
                        s = s_both[j * span:(j + 1) * span] + bias_ref[j, var]
                        m_new = jnp.broadcast_to(jnp.max(s, axis=1, keepdims=True), (span, LANES))
                        if merge:
                            m_prev = get(m_ref, j, start)
                            m_new = jnp.maximum(m_new, m_prev)
                        p = jnp.exp2(s - _tile_lanes(m_new, 2 * span))
                        acc_new = _dot(p.astype(BF16), jnp.concatenate([vcs[j][u], vcs[j][u + 1]], axis=0))
                        if merge:
                            acc_new = jnp.exp2(m_prev - m_new) * get(acc_ref, j, start) + acc_new
                        results.append((j, start, m_new, acc_new))
            for j, start, m_new, acc_new in results:
                put(m_ref, j, start, m_new)
                put(acc_ref, j, start, acc_new)
            return carry

        lax.fori_loop(0, (dil // n_res) * groups, body, 0)
    o_ref[0] = _finish_pair(acc_ref, left).astype(o_ref.dtype)


def _dilated_attention(q, k, v, bias_tiles):
    b, s, dq = q.shape
    hp = dq // LANES
    for window, dil in DIL_PATTERNS:
        assert s % window == 0, "sequence must be a whole number of dilated units"
    qkv = pl.BlockSpec((1, s, LANES), lambda h, bi: (bi, 0, h))
    n_var, span, band = bias_tiles.shape[1:]
    return pl.pallas_call(
        functools.partial(_dilated_kernel, s_len=s),
        grid=(hp, b),
        in_specs=[qkv, qkv, qkv,
                  pl.BlockSpec((2, n_var, span, band), lambda h, bi: (h, 0, 0, 0))],
        out_specs=pl.BlockSpec((1, s, LANES), lambda h, bi: (bi, 0, h)),
        out_shape=jax.ShapeDtypeStruct((b, s, dq), BF16),
        scratch_shapes=[pltpu.VMEM((2, s, LANES), F32)] * 2,
        compiler_params=_cparams(("parallel", "parallel")),
        name="dilated_attention",
    )(q, k, v, bias_tiles)


def _mixer_residual(o_parts, w_parts, h, gain, gate, rows=slice(None)):
    y = _dot(o_parts[0][0, rows], w_parts[0][...])
    for o_ref, w_ref in zip(o_parts[1:], w_parts[1:]):
        y = y + _dot(o_ref[0, rows], w_ref[...])
    return h + gate * (_rms(y) * gain)


def _mixer_operands(o_parts, w_out, tm):
    w_out = w_out.astype(BF16)
    cuts = np.cumsum([p.shape[-1] for p in o_parts])[:-1].tolist()
    w_parts = jnp.split(w_out, cuts, axis=0) if cuts else [w_out]
    specs = [pl.BlockSpec((1, tm, p.shape[-1]), lambda bi, si: (bi, si, 0)) for p in o_parts]
    specs += [pl.BlockSpec(w.shape, lambda bi, si: (0, 0)) for w in w_parts]
    return list(o_parts) + list(w_parts), specs


def _ffn_kernel(*refs, n_parts):
    o_parts = refs[:n_parts]
    w_parts = refs[n_parts:2 * n_parts]
    (h_ref, gain1_ref, gate1_ref, g_ref, sc_ref, sh_ref, wg_ref, wu_ref, wd_ref, gain_ref, gate_ref,
     o_ref) = refs[2 * n_parts:]
    h = _mixer_residual(o_parts, w_parts, h_ref[0], gain1_ref[...], gate1_ref[0])
    u = _modulate(h, g_ref[...], sc_ref[0], sh_ref[0]).astype(BF16)
    hid = (_silu(_dot(u, wg_ref[...])) * _dot(u, wu_ref[...])).astype(BF16)
    y = _dot(hid, wd_ref[...])
    o_ref[0] = h + gate_ref[0] * (_rms(y) * gain_ref[...])


def _mixer_tail_ffn(o_parts, w_out, h, gain1, gate1, gain_in, scale, shift, w_gate, w_up, w_down,
                    gain_out, gate):
    b, s, d = h.shape
    tm = TM_FFN
    ff = w_gate.shape[1]
    resident = functools.partial(pl.BlockSpec, pipeline_mode=pl.Buffered(1))
    vec = pl.BlockSpec((1, 1, d), lambda bi, si: (bi, 0, 0))
    row = pl.BlockSpec((1, d), lambda bi, si: (0, 0))
    mix_ops, mix_specs = _mixer_operands(o_parts, w_out, tm)
    return pl.pallas_call(
        functools.partial(_ffn_kernel, n_parts=len(o_parts)),
        grid=(b, s // tm),
        in_specs=mix_specs + [pl.BlockSpec((1, tm, d), lambda bi, si: (bi, si, 0)), row, vec,
                              row, vec, vec,
                              resident((d, ff), lambda bi, si: (0, 0)),
                              resident((d, ff), lambda bi, si: (0, 0)),
                              resident((ff, d), lambda bi, si: (0, 0)),
                              row, vec],
        out_specs=pl.BlockSpec((1, tm, d), lambda bi, si: (bi, si, 0)),
        out_shape=jax.ShapeDtypeStruct((b, s, d), F32),
        compiler_params=_cparams(("parallel", "parallel")),
        name="dense_swiglu",
    )(*mix_ops, h, gain1.reshape(1, d), gate1.reshape(b, 1, d),
      gain_in.reshape(1, d), scale.reshape(b, 1, d), shift.reshape(b, 1, d),
      w_gate.astype(BF16), w_up.astype(BF16), w_down.astype(BF16),
      gain_out.reshape(1, d), gate.reshape(b, 1, d))


def _router_kernel(o_ref, wo_ref, h_ref, gain1_ref, gate1_ref, g_ref, sc_ref, sh_ref, rw_ref,
                   hout_ref, mi_ref, mf_ref, cnt_ref, carry_ref, *, tm):
    @pl.when((pl.program_id(0) == 0) & (pl.program_id(1) == 0))
    def _():
        carry_ref[...] = jnp.zeros_like(carry_ref)

    tr = tm // ROUTER_ROW_CHAINS
    lanef = lax.broadcasted_iota(I32, (tr, LANES), 1).astype(F32)
    r = lax.broadcasted_iota(I32, (tr, tr), 0)
    c = lax.broadcasted_iota(I32, (tr, tr), 1)
    before = jnp.where(c < r, 1.0, 0.0).astype(BF16)
    counts = carry_ref[0:1, :]
    for ch in range(ROUTER_ROW_CHAINS):
        rows = slice(ch * tr, (ch + 1) * tr)
        h = _mixer_residual([o_ref], [wo_ref], h_ref[0, rows], gain1_ref[...], gate1_ref[0], rows)
        hout_ref[0, rows] = h
        u = _modulate(h, g_ref[...], sc_ref[0], sh_ref[0])
        logits = _dot_split(u, rw_ref[...])
        lg = jnp.where(lanef < N_EXPERTS, logits, NEG_INF)
        v1 = jnp.max(lg, axis=1, keepdims=True)
        i1 = jnp.min(jnp.where(lg == v1, lanef, float(LANES)), axis=1, keepdims=True)
        lg2 = jnp.where(lanef == i1, NEG_INF, lg)
        v2 = jnp.max(lg2, axis=1, keepdims=True)
        i2 = jnp.min(jnp.where(lg2 == v2, lanef, float(LANES)), axis=1, keepdims=True)
        e2 = jnp.exp(v2 - v1)
        p1 = 1.0 / (1.0 + e2)
        p2 = e2 / (1.0 + e2)
        oh1 = jnp.where(lanef == i1, 1.0, 0.0)
        oh2 = jnp.where(lanef == i2, 1.0, 0.0)
        oh = oh1 + oh2
        tot = _dot(before, oh.astype(BF16)) + counts
        rank1 = jnp.sum(oh1 * tot, axis=1, keepdims=True)
        rank2 = jnp.sum(oh2 * tot, axis=1, keepdims=True)
        counts = counts + jnp.sum(oh, axis=0, keepdims=True)
        mi = jnp.where(lanef == 0.0, i1, jnp.where(lanef == 1.0, i2,
             jnp.where(lanef == 2.0, rank1, jnp.where(lanef == 3.0, rank2, 0.0))))
        mi_ref[rows] = mi.astype(I32)
        mf_ref[rows] = jnp.where(lanef == 0.0, p1, jnp.where(lanef == 1.0, p2, 0.0))
    carry_ref[...] = jnp.broadcast_to(counts, carry_ref.shape)
    cnt_ref[...] = carry_ref[...]


def _mixer_tail_router(o, w_out, h, gain1, gate1, gain, scale, shift, router_w):
    b, s, d = h.shape
    tm = TM_ROUTE
    n = b * s
    ns = s // tm
    rw = jnp.pad(router_w.astype(F32), ((0, 0), (0, LANES - router_w.shape[1])))
    vec = pl.BlockSpec((1, 1, d), lambda bi, si: (bi, 0, 0))
    row = pl.BlockSpec((1, d), lambda bi, si: (0, 0))
    act = pl.BlockSpec((1, tm, d), lambda bi, si: (bi, si, 0))
    meta = pl.BlockSpec((tm, LANES), lambda bi, si: (bi * ns + si, 0))
    mix_ops, mix_specs = _mixer_operands([o], w_out, tm)
    return pl.pallas_call(
        functools.partial(_router_kernel, tm=tm),
        grid=(b, ns),
        in_specs=mix_specs + [act, row, vec, row, vec, vec, pl.BlockSpec(rw.shape, lambda bi, si: (0, 0))],
        out_specs=[act, meta, meta, pl.BlockSpec((8, LANES), lambda bi, si: (0, 0))],
        out_shape=[jax.ShapeDtypeStruct((b, s, d), F32),
                   jax.ShapeDtypeStruct((n, LANES), I32),
                   jax.ShapeDtypeStruct((n, LANES), F32),
                   jax.ShapeDtypeStruct((8, LANES), F32)],
        scratch_shapes=[pltpu.VMEM((8, LANES), F32)],
        compiler_params=_cparams(("arbitrary", "arbitrary")),
        name="moe_router",
    )(*mix_ops, h, gain1.reshape(1, d), gate1.reshape(b, 1, d),
      gain.reshape(1, d), scale.reshape(b, 1, d), shift.reshape(b, 1, d), rw)


def _scatter_kernel(dest_ref, ztile_ref, h_ref, g_ref, sc_ref, sh_ref, xs_ref, ubuf, zbuf, sems, zsem,
                    *, tm, ns):
    step = pl.program_id(0) * ns + pl.program_id(1)
    nsteps = pl.num_programs(0) * ns
    slot = lax.rem(step, 2)
    tz = zbuf.shape[0]

    @pl.when(step == 0)
    def _():
        zbuf[...] = jnp.zeros_like(zbuf)
        for e in range(2 * N_EXPERTS):
            @pl.when(ztile_ref[e] >= 0)
            def _():
                row0 = pl.multiple_of(ztile_ref[e] * tz, tz)
                pltpu.make_async_copy(zbuf, xs_ref.at[pl.ds(row0, tz), :], zsem).start()
        for e in range(2 * N_EXPERTS):
            @pl.when(ztile_ref[e] >= 0)
            def _():
                pltpu.make_async_copy(zbuf, xs_ref.at[pl.ds(0, tz), :], zsem).wait()

    def wait_slot(sl):
        for _ in range(2):
            pltpu.make_async_copy(ubuf.at[sl], ubuf.at[sl], sems.at[sl]).wait()

    base = step * tm

    def run(sl):
        @pl.when(step >= 2)
        def _():
            wait_slot(sl)

        ubuf[sl] = _modulate(h_ref[0], g_ref[...], sc_ref[0], sh_ref[0])

        def issue(i, carry):
            t = 2 * (base + i)
            src = ubuf.at[sl, pl.ds(i, 1), :]
            pltpu.make_async_copy(src, xs_ref.at[pl.ds(dest_ref[t], 1), :], sems.at[sl]).start()
            pltpu.make_async_copy(src, xs_ref.at[pl.ds(dest_ref[t + 1], 1), :], sems.at[sl]).start()
            return carry

        lax.fori_loop(0, tm, issue, 0, unroll=DMA_UNROLL)

        @pl.when(step == nsteps - 1)
        def _():
            wait_slot(sl)

            @pl.when(nsteps >= 2)
            def _():
                wait_slot(1 - sl)

    for sl in range(2):
        pl.when(slot == sl)(functools.partial(run, sl))


def _scatter(dest, zero_tile, h, gain, scale, shift, m_pad):
    b, s, d = h.shape
    tm = TM_SCATTER
    ns = s // tm
    vec = pl.BlockSpec((1, 1, d), lambda bi, si, dest, zt: (bi, 0, 0))
    grid_spec = pltpu.PrefetchScalarGridSpec(
        num_scalar_prefetch=2,
        grid=(b, ns),
        in_specs=[pl.BlockSpec((1, tm, d), lambda bi, si, dest, zt: (bi, si, 0)),
                  pl.BlockSpec((1, d), lambda bi, si, dest, zt: (0, 0)),
                  vec, vec],
        out_specs=pl.BlockSpec(memory_space=pl.ANY),
        scratch_shapes=[pltpu.VMEM((2, tm, d), F32), pltpu.VMEM((TM_EXPERT, d), F32),
                        pltpu.SemaphoreType.DMA((2,)), pltpu.SemaphoreType.DMA],
    )
    return pl.pallas_call(
        functools.partial(_scatter_kernel, tm=tm, ns=ns),
        grid_spec=grid_spec,
        out_shape=jax.ShapeDtypeStruct((m_pad, d), F32),
        compiler_params=_cparams(("arbitrary", "arbitrary")),
        name="moe_scatter",
    )(dest, zero_tile, h, gain.reshape(1, d), scale.reshape(b, 1, d), shift.reshape(b, 1, d))


def _expert_kernel(te_ref, tv_ref, tx_ref, x_ref, wg_ref, wu_ref, wd_ref, o_ref):
    t = pl.program_id(0)
    f = pl.program_id(1)

    @pl.when(tv_ref[t] == 1)
    def _():
        x = x_ref[...].astype(BF16)
        hid = (_silu(_dot(x, wg_ref[0])) * _dot(x, wu_ref[0])).astype(BF16)
        y = _dot(hid, wd_ref[0])

        @pl.when(f == 0)
        def _():
            o_ref[...] = y

        @pl.when(f > 0)
        def _():
            o_ref[...] = o_ref[...] + y

    @pl.when((tv_ref[t] == 0) & (f == 0))
    def _():
        o_ref[...] = jnp.zeros_like(o_ref)


def _experts(tile_expert, tile_valid, tile_x, xs, w_gate, w_up, w_down):
    m_pad, d = xs.shape
    tm = TM_EXPERT
    n_tiles = tile_expert.shape[0]
    ff = w_gate.shape[2]
    fs = FF_STEPS_EXPERT
    tf = ff // fs

    def ff_idx(f, tv, t):
        return f * tv[t] + (fs - 1) * (1 - tv[t])

    grid_spec = pltpu.PrefetchScalarGridSpec(
        num_scalar_prefetch=3,
        grid=(n_tiles, fs),
        in_specs=[pl.BlockSpec((tm, d), lambda t, f, te, tv, tx: (tx[t], 0)),
                  pl.BlockSpec((1, d, tf), lambda t, f, te, tv, tx: (te[t], 0, ff_idx(f, tv, t))),
                  pl.BlockSpec((1, d, tf), lambda t, f, te, tv, tx: (te[t], 0, ff_idx(f, tv, t))),
                  pl.BlockSpec((1, tf, d), lambda t, f, te, tv, tx: (te[t], ff_idx(f, tv, t), 0))],
        out_specs=pl.BlockSpec((tm, d), lambda t, f, te, tv, tx: (t, 0)),
    )
    return pl.pallas_call(
        _expert_kernel,
        grid_spec=grid_spec,
        out_shape=jax.ShapeDtypeStruct((m_pad, d), F32),
        compiler_params=_cparams(("arbitrary", "arbitrary")),
        name="moe_experts",
    )(tile_expert, tile_valid, tile_x, xs,
      w_gate.astype(BF16), w_up.astype(BF16), w_down.astype(BF16))


def _combine_kernel(dest_ref, y_ref, mf_ref, h_ref, gain_ref, gate_ref, o_ref, ybuf, sems, *, tm, ns):
    step = pl.program_id(0) * ns + pl.program_id(1)
    nsteps = pl.num_programs(0) * ns
    slot = lax.rem(step, 2)

    def issue(st, sl):
        base = st * tm

        def body(i, carry):
            t = 2 * (base + i)
            pltpu.make_async_copy(y_ref.at[pl.ds(dest_ref[t], 1), :],
                                  ybuf.at[sl, 0, pl.ds(i, 1), :], sems.at[sl]).start()
            pltpu.make_async_copy(y_ref.at[pl.ds(dest_ref[t + 1], 1), :],
                                  ybuf.at[sl, 1, pl.ds(i, 1), :], sems.at[sl]).start()
            return carry

        lax.fori_loop(0, tm, body, 0, unroll=DMA_UNROLL)

    @pl.when(step == 0)
    def _():
        issue(0, 0)

    def run(sl):
        @pl.when(step + 1 < nsteps)
        def _():
            issue(step + 1, 1 - sl)

        for k in range(2):
            pltpu.make_async_copy(ybuf.at[sl, k], ybuf.at[sl, k], sems.at[sl]).wait()
        mf = mf_ref[...]
        y = mf[:, 0:1] * ybuf[sl, 0] + mf[:, 1:2] * ybuf[sl, 1]
        o_ref[0] = h_ref[0] + gate_ref[0] * (_rms(y) * gain_ref[...])

    for sl in range(2):
        pl.when(slot == sl)(functools.partial(run, sl))


def _combine(dest, ys, mf, h, gain, gate):
    b, s, d = h.shape
    tm = TM_COMBINE
    ns = s // tm
    grid_spec = pltpu.PrefetchScalarGridSpec(
        num_scalar_prefetch=1,
        grid=(b, ns),
        in_specs=[pl.BlockSpec(memory_space=pl.ANY),
                  pl.BlockSpec((tm, LANES), lambda bi, si, dest: (bi * ns + si, 0)),
                  pl.BlockSpec((1, tm, d), lambda bi, si, dest: (bi, si, 0)),
                  pl.BlockSpec((1, d), lambda bi, si, dest: (0, 0)),
                  pl.BlockSpec((1, 1, d), lambda bi, si, dest: (bi, 0, 0))],
        out_specs=pl.BlockSpec((1, tm, d), lambda bi, si, dest: (bi, si, 0)),
        scratch_shapes=[pltpu.VMEM((2, 2, tm, d), F32), pltpu.SemaphoreType.DMA((2,))],
    )
    return pl.pallas_call(
        functools.partial(_combine_kernel, tm=tm, ns=ns),
        grid_spec=grid_spec,
        out_shape=jax.ShapeDtypeStruct((b, s, d), F32),
        compiler_params=_cparams(("arbitrary", "arbitrary")),
        name="moe_combine",
    )(dest, ys, mf, h, gain.reshape(1, d), gate.reshape(b, 1, d))


def _mixer_tail_moe(o, w_out, h, gain1, gate1, gain_in, scale, shift, router_w, w_gate, w_up, w_down,
                    gain_out, gate):
    b, s, d = h.shape
    n = b * s
    tm = TM_EXPERT
    h, mi, mf, cnt = _mixer_tail_router(o, w_out, h, gain1, gate1, gain_in, scale, shift, router_w)
    counts = cnt[0, :N_EXPERTS].astype(I32)
    tiles_per = (counts + tm - 1) // tm
    seg_start = (jnp.cumsum(tiles_per) - tiles_per) * tm
    picked = mi[:, 0:2, None] == jnp.arange(N_EXPERTS, dtype=I32)
    dest = (jnp.sum(jnp.where(picked, seg_start, 0), axis=-1) + mi[:, 2:4]).reshape(2 * n)
    n_tiles = (2 * n) // tm + N_EXPERTS
    m_pad = n_tiles * tm
    tile_end = jnp.cumsum(tiles_per)
    tidx = jnp.arange(n_tiles, dtype=I32)
    tile_valid = (tidx < tile_end[-1]).astype(I32)
    tile_expert = jnp.minimum(jnp.sum(tile_end[None, :] <= tidx[:, None], axis=1), N_EXPERTS - 1).astype(I32)
    tile_x = jnp.minimum(tidx, tile_end[-1] - 1)
    tail = tile_end[-1] + jnp.arange(N_EXPERTS, dtype=I32)
    zero_tile = jnp.concatenate([jnp.where(tiles_per > 0, tile_end - 1, -1),
                                 jnp.where(tail < n_tiles, tail, -1)]).astype(I32)
    xs = _scatter(dest, zero_tile, h, gain_in, scale, shift, m_pad)
    ys = _experts(tile_expert, tile_valid, tile_x, xs, w_gate, w_up, w_down)
    return _combine(dest, ys, mf, h, gain_out, gate)


def kernel(x, c, mod_w, mod_b, norm_g, attn_in_w_even, fox_gate_bias, attn_out_w_even,
           attn_in_w_odd, attn_out_w_odd, rel_bias_table, ffn_w_gate, ffn_w_up, ffn_w_down,
           router_w, exp_w_gate, exp_w_up, exp_w_down):
    depth = mod_w.shape[0]
    s_len = x.shape[1]
    mods = _mods(c, mod_w, mod_b)
    dil_bias, moba_bias = _bias_tiles(rel_bias_table, s_len)
    h = x
    for layer in range(depth):
        j = layer // 2
        sh1, sc1, g1, sh2, sc2, g2 = jnp.split(mods[layer], 6, axis=-1)
        gains = norm_g[layer]
        if layer % 2 == 0:
            fox_in, moba_in = _inproj_even(h, gains[0], sc1, sh1, attn_in_w_even[j], fox_gate_bias[j])
            o_parts = [_fox_attention(*fox_in), _moba_attention(*moba_in, moba_bias)]
            h = _mixer_tail_ffn(o_parts, attn_out_w_even[j], h, gains[1], g1, gains[2], sc2, sh2,
                                ffn_w_gate[j], ffn_w_up[j], ffn_w_down[j], gains[3], g2)
        else:
            q, k, v = _inproj_odd(h, gains[0], sc1, sh1, attn_in_w_odd[j])
            o = _dilated_attention(q, k, v, dil_bias)
            h = _mixer_tail_moe(o, attn_out_w_odd[j], h, gains[1], g1, gains[2], sc2, sh2, router_w[j],
                                exp_w_gate[j], exp_w_up[j], exp_w_down[j], gains[3], g2)
    return h
```

```python
import functools
import math

import numpy as np
import jax
import jax.numpy as jnp
from jax import lax
from jax.experimental import pallas as pl
from jax.experimental.pallas import tpu as pltpu

F32 = jnp.float32
BF16 = jnp.bfloat16
I32 = jnp.int32

HEAD_DIM = 64
LANES = 128
N_HEADS = 16
N_HEADS_FOX = 8
ATTN_SCALE = HEAD_DIM ** -0.5
LOG2E = math.log2(math.e)
Q_SCALE = ATTN_SCALE * LOG2E
N_DECAY_PIECES = 3
MOBA_BLOCK = 256
MOBA_TOPK = 3
DIL_PATTERNS = ((128, 1), (512, 4), (2048, 16))
NUM_BUCKETS = 32
MAX_DISTANCE = 2048
N_EXPERTS = 8
NORM_EPS = 1e-6
NEG_INF = float("-inf")
MASK_BIG = 1e30

VMEM_LIMIT = 56 * 1024 * 1024

TM_PROJ = 1024
TM_FFN = 1024
TQ_FOX = 2048
TQ_MOBA = 2048
TK_FOX = 512
FOX_PAIRS_PER_STEP = 1
MOBA_PAIRS_PER_STEP = 1
TM_ROUTE = 512
ROUTER_ROW_CHAINS = 2
TM_SCATTER = 256
TM_EXPERT = 512
TM_COMBINE = 256
FF_STEPS_EXPERT = 2
DMA_UNROLL = True
DIL_CHUNKS_PER_STEP = 16


def _cparams(sem):
    return pltpu.CompilerParams(dimension_semantics=sem, vmem_limit_bytes=VMEM_LIMIT)


def _t5_bucket_np(n):
    n = np.maximum(n, 0)
    max_exact = NUM_BUCKETS // 2
    nf = np.maximum(n, 1).astype(np.float64)
    large = max_exact + (np.log(nf / max_exact) / math.log(MAX_DISTANCE / max_exact)
                         * (NUM_BUCKETS - max_exact)).astype(np.int64)
    large = np.minimum(large, NUM_BUCKETS - 1)
    return np.where(n < max_exact, n, large)


_MAX_DIST = 1 << 16
_BUCKET_OF = _t5_bucket_np(np.arange(_MAX_DIST))
_BUCKET_THR = [int(np.searchsorted(_BUCKET_OF, k, side="left")) for k in range(NUM_BUCKETS)]


def _bias_from_dist(tab_ref, h, dist, dlo, dhi):
    lo_b = int(_BUCKET_OF[max(dlo, 0)])
    hi_b = int(_BUCKET_OF[dhi])
    val = jnp.zeros(dist.shape, F32) + tab_ref[lo_b, h]
    for k in range(lo_b + 1, hi_b + 1):
        val = jnp.where(dist >= _BUCKET_THR[k], tab_ref[k, h], val)
    return val


def _dil_bias_kernel(tab_ref, o_ref):
    h = pl.program_id(0)
    for g, (window, dil) in enumerate(DIL_PATTERNS):
        span = window // dil
        i = lax.broadcasted_iota(I32, (span, 2 * span), 0)
        j = lax.broadcasted_iota(I32, (span, 2 * span), 1)
        rel = i + span - j
        val = _bias_from_dist(tab_ref, h, rel * dil, 0, span * dil) * LOG2E
        band = jnp.where(rel >= 0, jnp.where(rel <= span, val, NEG_INF), NEG_INF)
        o_ref[0, 2 * g] = band
        o_ref[0, 2 * g + 1] = jnp.where(j >= span, band, NEG_INF)


def _moba_bias_kernel(tab_ref, o_ref, *, n_blk, head0):
    h = pl.program_id(0) + head0
    i = lax.broadcasted_iota(I32, (MOBA_BLOCK, MOBA_BLOCK), 0)
    j = lax.broadcasted_iota(I32, (MOBA_BLOCK, MOBA_BLOCK), 1)
    for d in range(n_blk):
        dist = d * MOBA_BLOCK + i - j
        val = _bias_from_dist(tab_ref, h, dist, d * MOBA_BLOCK - (MOBA_BLOCK - 1),
                              d * MOBA_BLOCK + (MOBA_BLOCK - 1)) * LOG2E
        if d == 0:
            val = jnp.where(dist >= 0, val, NEG_INF)
        o_ref[0, d] = val


def _bias_tiles(rel_bias_table, s_len):
    n_blk = s_len // MOBA_BLOCK
    span = DIL_PATTERNS[0][0]
    n_var = 2 * len(DIL_PATTERNS)
    smem = pl.BlockSpec(memory_space=pltpu.SMEM)
    dil = pl.pallas_call(
        _dil_bias_kernel,
        grid=(N_HEADS,),
        in_specs=[smem],
        out_specs=pl.BlockSpec((1, n_var, span, 2 * span), lambda h: (h, 0, 0, 0)),
        out_shape=jax.ShapeDtypeStruct((N_HEADS, n_var, span, 2 * span), F32),
        compiler_params=_cparams(("parallel",)),
        name="dil_bias",
    )(rel_bias_table)
    n_moba = N_HEADS - N_HEADS_FOX
    moba = pl.pallas_call(
        functools.partial(_moba_bias_kernel, n_blk=n_blk, head0=N_HEADS_FOX),
        grid=(n_moba,),
        in_specs=[smem],
        out_specs=pl.BlockSpec((1, n_blk, MOBA_BLOCK, MOBA_BLOCK), lambda h: (h, 0, 0, 0)),
        out_shape=jax.ShapeDtypeStruct((n_moba, n_blk, MOBA_BLOCK, MOBA_BLOCK), F32),
        compiler_params=_cparams(("parallel",)),
        name="moba_bias",
    )(rel_bias_table)
    return dil, moba


def _split_bf16(a):
    hi = a.astype(BF16)
    lo = (a - hi.astype(F32)).astype(BF16)
    return hi, lo


def _dot(a, b):
    return jnp.dot(a, b, preferred_element_type=F32)


def _dot_nt(a, b):
    return lax.dot_general(a, b, (((1,), (1,)), ((), ())), preferred_element_type=F32)


def _dot_split(a, b):
    a_hi, a_lo = _split_bf16(a)
    b_hi, b_lo = _split_bf16(b)
    return _dot(a_hi, b_hi) + (_dot(a_hi, b_lo) + _dot(a_lo, b_hi))


def _rms(x):
    return x * lax.rsqrt(jnp.mean(x * x, axis=-1, keepdims=True) + NORM_EPS)


def _modulate(x, gain, scale, shift):
    return (_rms(x) * gain) * (1.0 + scale) + shift


def _silu(x):
    return x * jax.nn.sigmoid(x)


def _mods_kernel(c_ref, w_ref, b_ref, o_ref):
    o_ref[0] = _dot_split(_silu(c_ref[...]), w_ref[0]) + b_ref[0]


def _mods(c, mod_w, mod_b):
    depth, d, e = mod_w.shape
    b = c.shape[0]
    tn = 1536
    return pl.pallas_call(
        _mods_kernel,
        grid=(depth, e // tn),
        in_specs=[pl.BlockSpec((b, d), lambda l, j: (0, 0)),
                  pl.BlockSpec((1, d, tn), lambda l, j: (l, 0, j)),
                  pl.BlockSpec((1, 1, tn), lambda l, j: (l, 0, j))],
        out_specs=pl.BlockSpec((1, b, tn), lambda l, j: (l, 0, j)),
        out_shape=jax.ShapeDtypeStruct((depth, b, e), F32),
        compiler_params=_cparams(("parallel", "parallel")),
        name="adaln_mods",
    )(c, mod_w, mod_b.reshape(depth, 1, e))


def _inproj_even_kernel(h_ref, g_ref, sc_ref, sh_ref, w_ref, wf_ref, gb_ref,
                        qa_ref, k0a_ref, k1a_ref, v0a_ref, v1a_ref,
                        qb_ref, k0b_ref, k1b_ref, v0b_ref, v1b_ref, carry_ref, *, tm):
    si = pl.program_id(1)
    u = _modulate(h_ref[0], g_ref[...], sc_ref[0], sh_ref[0]).astype(BF16)
    width = qa_ref.shape[-1]
    n_pairs = width // LANES

    def proj(i):
        return _dot(u, w_ref[:, i * width:(i + 1) * width])

    lane = lax.broadcasted_iota(I32, (1, LANES), 1)
    left = lane < HEAD_DIM
    row = lax.broadcasted_iota(I32, (tm, LANES), 0)

    x = _dot(u, wf_ref[...]) + gb_ref[...]
    lf = jnp.where(lane < N_HEADS_FOX, jnp.minimum(x, 0.0) - jnp.log1p(jnp.exp(-jnp.abs(x))), 0.0)
    k = 1
    while k < tm:
        lf = lf + jnp.where(row >= k, pltpu.roll(lf, k, axis=0), 0.0)
        k *= 2

    @pl.when(si == 0)
    def _():
        carry_ref[...] = jnp.zeros_like(carry_ref)

    cum = lf + carry_ref[0:1, :]
    carry_ref[...] = jnp.broadcast_to(cum[tm - 1:tm, :], carry_ref.shape)
    rest = cum * (-LOG2E)
    decay = jnp.zeros((tm, LANES), F32)
    for p in range(N_DECAY_PIECES):
        piece = rest.astype(BF16).astype(F32)
        rest = rest - piece
        decay = decay + (pltpu.roll(piece, p * N_HEADS_FOX, axis=1) if p else piece)
    decay_lo = decay.astype(BF16)
    decay_hi = pltpu.roll(decay, HEAD_DIM, axis=1).astype(BF16)

    ones = jnp.ones((1, LANES), BF16)
    blk = (si * tm + row) // MOBA_BLOCK
    blk_lo = jnp.where(lane == blk, 1.0, 0.0).astype(BF16)
    blk_hi = jnp.where(lane == blk + HEAD_DIM, 1.0, 0.0).astype(BF16)

    def emit(first, k0_ref, k1_ref, v0_ref, v1_ref, k_lo, k_hi):
        kk = proj(first + 1).astype(BF16)
        vv = proj(first + 2).astype(BF16)
        for hp in range(n_pairs):
            sl = slice(hp * LANES, (hp + 1) * LANES)
            k0_ref[0, :, sl] = jnp.where(left, kk[:, sl], k_hi)
            k1_ref[0, :, sl] = jnp.where(left, k_lo, kk[:, sl])
            v0_ref[0, :, sl] = jnp.where(left, vv[:, sl], ones)
            v1_ref[0, :, sl] = jnp.where(left, ones, vv[:, sl])

    qa_ref[0] = (proj(0) * Q_SCALE).astype(BF16)
    emit(0, k0a_ref, k1a_ref, v0a_ref, v1a_ref, decay_lo, decay_hi)
    qb_ref[0] = (proj(3) * Q_SCALE).astype(BF16)
    emit(3, k0b_ref, k1b_ref, v0b_ref, v1b_ref, blk_lo, blk_hi)


def _inproj_even(h, gain, scale, shift, w_in, gate_bias):
    b, s, d = h.shape
    tm = TM_PROJ
    da = N_HEADS_FOX * HEAD_DIM
    assert s // MOBA_BLOCK <= HEAD_DIM and N_DECAY_PIECES * N_HEADS_FOX <= HEAD_DIM
    cuts = np.cumsum([da, da, da, N_HEADS_FOX, da, da]).tolist()
    q_a, k_a, v_a, f_a, q_b, k_b, v_b = jnp.split(w_in, cuts, axis=1)
    w = jnp.concatenate([q_a, k_a, v_a, q_b, k_b, v_b], axis=1).astype(BF16)
    wf = jnp.pad(f_a, ((0, 0), (0, LANES - N_HEADS_FOX))).astype(BF16)
    gb = jnp.pad(gate_bias.astype(F32), (0, LANES - N_HEADS_FOX)).reshape(1, LANES)
    act = jax.ShapeDtypeStruct((b, s, da), BF16)
    act_spec = pl.BlockSpec((1, tm, da), lambda bi, si: (bi, si, 0))
    vec = pl.BlockSpec((1, 1, d), lambda bi, si: (bi, 0, 0))
    outs = pl.pallas_call(
        functools.partial(_inproj_even_kernel, tm=tm),
        grid=(b, s // tm),
        in_specs=[pl.BlockSpec((1, tm, d), lambda bi, si: (bi, si, 0)),
                  pl.BlockSpec((1, d), lambda bi, si: (0, 0)),
                  vec, vec,
                  pl.BlockSpec(w.shape, lambda bi, si: (0, 0)),
                  pl.BlockSpec(wf.shape, lambda bi, si: (0, 0)),
                  pl.BlockSpec(gb.shape, lambda bi, si: (0, 0))],
        out_specs=[act_spec] * 10,
        out_shape=[act] * 10,
        scratch_shapes=[pltpu.VMEM((8, LANES), F32)],
        compiler_params=_cparams(("parallel", "arbitrary")),
        name="inproj_even",
    )(h, gain.reshape(1, d), scale.reshape(b, 1, d), shift.reshape(b, 1, d), w, wf, gb)
    return outs[:5], outs[5:]


def _inproj_odd_kernel(h_ref, g_ref, sc_ref, sh_ref, w_ref, q_ref, k_ref, v_ref):
    u = _modulate(h_ref[0], g_ref[...], sc_ref[0], sh_ref[0]).astype(BF16)
    width = q_ref.shape[-1]
    q_ref[0] = _dot(u, w_ref[:, 0:width]) * Q_SCALE
    k_ref[0] = _dot(u, w_ref[:, width:2 * width])
    v_ref[0] = _dot(u, w_ref[:, 2 * width:3 * width])


def _inproj_odd(h, gain, scale, shift, w_in):
    b, s, d = h.shape
    tm = TM_PROJ
    dq = w_in.shape[1] // 3
    act = jax.ShapeDtypeStruct((b, s, dq), F32)
    act_spec = pl.BlockSpec((1, tm, dq), lambda bi, si: (bi, si, 0))
    vec = pl.BlockSpec((1, 1, d), lambda bi, si: (bi, 0, 0))
    return pl.pallas_call(
        _inproj_odd_kernel,
        grid=(b, s // tm),
        in_specs=[pl.BlockSpec((1, tm, d), lambda bi, si: (bi, si, 0)),
                  pl.BlockSpec((1, d), lambda bi, si: (0, 0)),
                  vec, vec,
                  pl.BlockSpec(w_in.shape, lambda bi, si: (0, 0))],
        out_specs=[act_spec] * 3,
        out_shape=[act] * 3,
        compiler_params=_cparams(("parallel", "parallel")),
        name="inproj_odd",
    )(h, gain.reshape(1, d), scale.reshape(b, 1, d), shift.reshape(b, 1, d), w_in.astype(BF16))


def _tile_lanes(x, width):
    return jnp.concatenate([x] * (width // LANES), axis=1)


def _flash_update(s, v, m_ref, acc_ref):
    m_prev = m_ref[...]
    m_new = jnp.maximum(m_prev, jnp.max(s, axis=1, keepdims=True))
    p = jnp.exp2(s - _tile_lanes(m_new, s.shape[1]))
    acc_ref[...] = jnp.exp2(m_prev - m_new) * acc_ref[...] + _dot(p.astype(BF16), v)
    m_ref[...] = m_new


def _finish_pair(acc_ref, left, first=0):
    acc0 = acc_ref[first]
    acc1 = acc_ref[first + 1]
    den = pltpu.roll(jnp.where(left, acc1, acc0), HEAD_DIM, axis=1)
    return jnp.where(left, acc0, acc1) / den


def _fox_kernel(q_ref, k0_ref, k1_ref, v0_ref, v1_ref, o_ref, m_ref, acc_ref, *, tq, tk, n_pairs):
    qi = pl.program_id(2)
    lane = lax.broadcasted_iota(I32, (1, LANES), 1)
    left = lane < HEAD_DIM

    def piece_lanes(lane0):
        hit = lane == lane0
        for p in range(1, N_DECAY_PIECES):
            hit = jnp.logical_or(hit, lane == lane0 + p * N_HEADS_FOX)
        return jnp.where(hit, 1.0, 0.0).astype(BF16)

    chains = []
    for pr in range(n_pairs):
        hp = pl.program_id(1) * n_pairs + pr
        lanes = slice(pr * LANES, (pr + 1) * LANES)
        q = q_ref[0, :, lanes]
        chains.append((jnp.where(left, q, piece_lanes(HEAD_DIM + 2 * hp)), k0_ref, v0_ref, lanes))
        chains.append((jnp.where(left, piece_lanes(2 * hp + 1), q), k1_ref, v1_ref, lanes))
    row = lax.broadcasted_iota(I32, (tq, tk), 0)
    col = lax.broadcasted_iota(I32, (tq, tk), 1)
    m_ref[...] = jnp.full(m_ref.shape, NEG_INF, F32)
    acc_ref[...] = jnp.zeros(acc_ref.shape, F32)
    n_sub = tq // tk

    def step(kv, mask, row0=0):
        off = pl.multiple_of(kv * tk, tk)
        rows = pl.ds(row0, tq - row0)
        for c, (qc, k_ref, v_ref, lanes) in enumerate(chains):
            s = _dot_nt(qc[row0:], k_ref[0, pl.ds(off, tk), lanes])
            if mask is not None:
                s = jnp.where(mask[row0:], s, NEG_INF)
            _flash_update(s, v_ref[0, pl.ds(off, tk), lanes], m_ref.at[c, rows], acc_ref.at[c, rows])

    def body(kv, carry):
        step(kv, None)
        return carry

    lax.fori_loop(0, qi * n_sub, body, 0)
    for d in range(n_sub):
        step(qi * n_sub + d, col + d * tk <= row, d * tk)
    for pr in range(n_pairs):
        o_ref[0, :, pr * LANES:(pr + 1) * LANES] = _finish_pair(acc_ref, left, 2 * pr).astype(o_ref.dtype)


def _fox_attention(q, k0, k1, v0, v1):
    b, s, da = q.shape
    n_pairs = FOX_PAIRS_PER_STEP
    width = n_pairs * LANES
    tq = min(TQ_FOX, s)
    kv = pl.BlockSpec((1, s, width), lambda bi, h, qi: (bi, 0, h))
    return pl.pallas_call(
        functools.partial(_fox_kernel, tq=tq, tk=min(TK_FOX, tq), n_pairs=n_pairs),
        grid=(b, da // width, s // tq),
        in_specs=[pl.BlockSpec((1, tq, width), lambda bi, h, qi: (bi, qi, h)), kv, kv, kv, kv],
        out_specs=pl.BlockSpec((1, tq, width), lambda bi, h, qi: (bi, qi, h)),
        out_shape=jax.ShapeDtypeStruct((b, s, da), BF16),
        scratch_shapes=[pltpu.VMEM((2 * n_pairs, tq, LANES), F32)] * 2,
        compiler_params=_cparams(("parallel", "parallel", "arbitrary")),
        name="fox_attention",
    )(q, k0, k1, v0, v1)


def _moba_kernel(q_ref, k0_ref, k1_ref, v0_ref, v1_ref, bias_ref, o_ref, km_ref, m_ref, acc_ref,
                 *, n_blk, tq, n_pairs):
    blk = MOBA_BLOCK
    tk = 2 * blk
    nq = tq // blk
    a = pl.program_id(2)
    gate_lane0 = (HEAD_DIM, 0)
    lane = lax.broadcasted_iota(I32, (1, LANES), 1)
    left = lane < HEAD_DIM
    mine = (left, jnp.logical_not(left))
    k_refs = (k0_ref, k1_ref)
    v_refs = (v0_ref, v1_ref)

    @pl.when(a == 0)
    def _():
        km_ref[...] = jnp.zeros_like(km_ref)
        for pr in range(n_pairs):
            lanes = slice(pr * LANES, (pr + 1) * LANES)
            for n in range(n_blk):
                rows = slice(n * blk, (n + 1) * blk)
                kb = jnp.where(left, k0_ref[0, rows, lanes], k1_ref[0, rows, lanes]).astype(F32)
                mean = jnp.sum(kb, axis=0, keepdims=True) * (1.0 / blk)
                for lane0 in gate_lane0:
                    km_ref[pr, lane0 + n:lane0 + n + 1, :] = mean

    nb = -(-n_blk // 8) * 8
    blkf = lax.broadcasted_iota(I32, (nb, tq), 0).astype(F32)
    own = (lax.broadcasted_iota(I32, (nb, tq), 1) // blk + a * nq).astype(F32)
    chains = []
    for pr in range(n_pairs):
        lanes = slice(pr * LANES, (pr + 1) * LANES)
        q = q_ref[0, :, lanes]
        km_hi, km_lo = _split_bf16(km_ref[pr])
        for j in range(2):
            qj = jnp.where(mine[j], q, jnp.zeros_like(q))
            lane0 = gate_lane0[j]
            gate = (_dot_nt(km_hi, qj) + _dot_nt(km_lo, qj))[lane0:lane0 + nb]
            gate = jnp.where(blkf < own, gate, NEG_INF)
            pen = jnp.where(blkf == own, 0.0, -MASK_BIG)
            for _ in range(MOBA_TOPK):
                mx = jnp.max(gate, axis=0, keepdims=True)
                cand = jnp.where(gate == mx, jnp.where(mx > NEG_INF, blkf, float(LANES)), float(LANES))
                pick = blkf == jnp.min(cand, axis=0, keepdims=True)
                pen = jnp.where(pick, 0.0, pen)
                gate = jnp.where(pick, NEG_INF, gate)
            parts = [pen, jnp.zeros((LANES - lane0 - nb, tq), F32)]
            if lane0:
                parts.insert(0, jnp.zeros((lane0, tq), F32))
            pen_q = jnp.concatenate(parts, axis=0).T
            chains.append((jnp.where(mine[j], q, pen_q.astype(BF16)), k_refs[j], v_refs[j], lanes, 2 * pr + j))

    m_ref[...] = jnp.full(m_ref.shape, NEG_INF, F32)
    acc_ref[...] = jnp.zeros(acc_ref.shape, F32)

    def step(i, dist, r0):
        off = pl.multiple_of(i * tk, tk)
        rows = pl.ds(r0 * blk, tq - r0 * blk)
        for qc, k_ref, v_ref, lanes, h in chains:
            bias = jnp.concatenate(
                [jnp.concatenate([bias_ref[h, dist(r, c)] for c in range(2)], axis=1)
                 for r in range(r0, nq)], axis=0)
            s = _dot_nt(qc[r0 * blk:], k_ref[0, pl.ds(off, tk), lanes]) + bias
            _flash_update(s, v_ref[0, pl.ds(off, tk), lanes], m_ref.at[h, rows], acc_ref.at[h, rows])

    n_full = a * (nq // 2)

    def body(i, carry):
        step(i, lambda r, c: a * nq + r - 2 * i - c, 0)
        return carry

    lax.fori_loop(0, n_full, body, 0)
    for e in range(nq // 2):
        step(n_full + e, lambda r, c, e=e: max(r - 2 * e - c, 0), 2 * e)
    for pr in range(n_pairs):
        o_ref[0, :, pr * LANES:(pr + 1) * LANES] = _finish_pair(acc_ref, left, 2 * pr).astype(o_ref.dtype)


def _moba_attention(q, k0, k1, v0, v1, bias_tiles):
    b, s, db = q.shape
    n_pairs = MOBA_PAIRS_PER_STEP
    width = n_pairs * LANES
    blk = MOBA_BLOCK
    n_blk = s // blk
    tq = min(TQ_MOBA, s)
    assert n_blk <= HEAD_DIM, "block gates of one head must fit in the other head's lanes"
    assert s % tq == 0 and tq % (2 * blk) == 0
    kv = pl.BlockSpec((1, s, width), lambda h, bi, qi: (bi, 0, h))
    return pl.pallas_call(
        functools.partial(_moba_kernel, n_blk=n_blk, tq=tq, n_pairs=n_pairs),
        grid=(db // width, b, s // tq),
        in_specs=[pl.BlockSpec((1, tq, width), lambda h, bi, qi: (bi, qi, h)), kv, kv, kv, kv,
                  pl.BlockSpec((2 * n_pairs, n_blk, blk, blk), lambda h, bi, qi: (h, 0, 0, 0),
                               pipeline_mode=pl.Buffered(1))],
        out_specs=pl.BlockSpec((1, tq, width), lambda h, bi, qi: (bi, qi, h)),
        out_shape=jax.ShapeDtypeStruct((b, s, db), BF16),
        scratch_shapes=[pltpu.VMEM((n_pairs, LANES, LANES), F32)]
                       + [pltpu.VMEM((2 * n_pairs, tq, LANES), F32)] * 2,
        compiler_params=_cparams(("parallel", "parallel", "arbitrary")),
        name="moba_attention",
    )(q, k0, k1, v0, v1, bias_tiles)


def _dilated_kernel(q_ref, k_ref, v_ref, bias_ref, o_ref, m_ref, acc_ref, *, s_len):
    lane = lax.broadcasted_iota(I32, (1, LANES), 1)
    left = lane < HEAD_DIM
    ones = jnp.ones((1, LANES), BF16)
    order = sorted(range(len(DIL_PATTERNS)), key=lambda i: -DIL_PATTERNS[i][1])
    for g in order:
        window, dil = DIL_PATTERNS[g]
        merge = g != order[0]
        span = window // dil
        unit = span * dil
        nc = s_len // unit
        n_u = min(DIL_CHUNKS_PER_STEP, nc)
        groups = nc // n_u
        n_res = min(DIL_CHUNKS_PER_STEP // n_u, dil)

        def rows(ref, start, dil=dil, span=span):
            if dil == 1:
                return ref[0, pl.ds(start, span), :]
            return ref[0, pl.ds(start, span, stride=dil), :]

        def get(ref, j, start, dil=dil, span=span):
            if dil == 1:
                return ref[j, pl.ds(start, span), :]
            return ref[j, pl.ds(start, span, stride=dil), :]

        def put(ref, j, start, val, dil=dil, span=span):
            if dil == 1:
                ref[j, pl.ds(start, span), :] = val
            else:
                ref[j, pl.ds(start, span, stride=dil), :] = val

        def body(it, carry, g=g, merge=merge, n_u=n_u, n_res=n_res, groups=groups, unit=unit, rows=rows,
                 put=put, get=get):
            r0 = (it // groups) * n_res
            grp = it - (it // groups) * groups
            is_first = grp == 0
            results = []
            for dr in range(n_res):
                start0 = r0 + dr + grp * (n_u * unit)
                prev0 = start0 - jnp.where(is_first, 0, unit)
                starts = [start0 + u * unit for u in range(n_u)]
                kc = [rows(k_ref, st).astype(BF16) for st in [prev0] + starts]
                vc = [rows(v_ref, st).astype(BF16) for st in [prev0] + starts]
                vcs = ([jnp.where(left, v, ones) for v in vc], [jnp.where(left, ones, v) for v in vc])
                for u, start in enumerate(starts):
                    var = 2 * g + jnp.where(is_first, 1, 0) if u == 0 else 2 * g
                    q = rows(q_ref, start)
                    kb = jnp.concatenate([kc[u], kc[u + 1]], axis=0)
                    qq = jnp.concatenate([jnp.where(left, q, 0.0), jnp.where(left, 0.0, q)],
                                         axis=0).astype(BF16)
                    s_both = _dot_nt(qq, kb)
                    for j in range(2):
                        s = s_both[j * span:(j + 1) * span] + bias_ref[j, var]
                        m_new = jnp.broadcast_to(jnp.max(s, axis=1, keepdims=True), (span, LANES))
                        if merge:
                            m_prev = get(m_ref, j, start)
                            m_new = jnp.maximum(m_new, m_prev)
                        p = jnp.exp2(s - _tile_lanes(m_new, 2 * span))
                        acc_new = _dot(p.astype(BF16), jnp.concatenate([vcs[j][u], vcs[j][u + 1]], axis=0))
                        if merge:
                            acc_new = jnp.exp2(m_prev - m_new) * get(acc_ref, j, start) + acc_new
                        results.append((j, start, m_new, acc_new))
            for j, start, m_new, acc_new in results:
                put(m_ref, j, start, m_new)
                put(acc_ref, j, start, acc_new)
            return carry

        lax.fori_loop(0, (dil // n_res) * groups, body, 0)
    o_ref[0] = _finish_pair(acc_ref, left).astype(o_ref.dtype)


def _dilated_attention(q, k, v, bias_tiles):
    b, s, dq = q.shape
    hp = dq // LANES
    for window, dil in DIL_PATTERNS:
        assert s % window == 0, "sequence must be a whole number of dilated units"
    qkv = pl.BlockSpec((1, s, LANES), lambda h, bi: (bi, 0, h))
    n_var, span, band = bias_tiles.shape[1:]
    return pl.pallas_call(
        functools.partial(_dilated_kernel, s_len=s),
        grid=(hp, b),
        in_specs=[qkv, qkv, qkv,
                  pl.BlockSpec((2, n_var, span, band), lambda h, bi: (h, 0, 0, 0))],
        out_specs=pl.BlockSpec((1, s, LANES), lambda h, bi: (bi, 0, h)),
        out_shape=jax.ShapeDtypeStruct((b, s, dq), BF16),
        scratch_shapes=[pltpu.VMEM((2, s, LANES), F32)] * 2,
        compiler_params=_cparams(("parallel", "parallel")),
        name="dilated_attention",
    )(q, k, v, bias_tiles)


def _mixer_residual(o_parts, w_parts, h, gain, gate, rows=slice(None)):
    y = _dot(o_parts[0][0, rows], w_parts[0][...])
    for o_ref, w_ref in zip(o_parts[1:], w_parts[1:]):
        y = y + _dot(o_ref[0, rows], w_ref[...])
    return h + gate * (_rms(y) * gain)


def _mixer_operands(o_parts, w_out, tm):
    w_out = w_out.astype(BF16)
    cuts = np.cumsum([p.shape[-1] for p in o_parts])[:-1].tolist()
    w_parts = jnp.split(w_out, cuts, axis=0) if cuts else [w_out]
    specs = [pl.BlockSpec((1, tm, p.shape[-1]), lambda bi, si: (bi, si, 0)) for p in o_parts]
    specs += [pl.BlockSpec(w.shape, lambda bi, si: (0, 0)) for w in w_parts]
    return list(o_parts) + list(w_parts), specs


def _ffn_kernel(*refs, n_parts):
    o_parts = refs[:n_parts]
    w_parts = refs[n_parts:2 * n_parts]
    (h_ref, gain1_ref, gate1_ref, g_ref, sc_ref, sh_ref, wg_ref, wu_ref, wd_ref, gain_ref, gate_ref,
     o_ref) = refs[2 * n_parts:]
    h = _mixer_residual(o_parts, w_parts, h_ref[0], gain1_ref[...], gate1_ref[0])
    u = _modulate(h, g_ref[...], sc_ref[0], sh_ref[0]).astype(BF16)
    hid = (_silu(_dot(u, wg_ref[...])) * _dot(u, wu_ref[...])).astype(BF16)
    y = _dot(hid, wd_ref[...])
    o_ref[0] = h + gate_ref[0] * (_rms(y) * gain_ref[...])


def _mixer_tail_ffn(o_parts, w_out, h, gain1, gate1, gain_in, scale, shift, w_gate, w_up, w_down,
                    gain_out, gate):
    b, s, d = h.shape
    tm = TM_FFN
    ff = w_gate.shape[1]
    resident = functools.partial(pl.BlockSpec, pipeline_mode=pl.Buffered(1))
    vec = pl.BlockSpec((1, 1, d), lambda bi, si: (bi, 0, 0))
    row = pl.BlockSpec((1, d), lambda bi, si: (0, 0))
    mix_ops, mix_specs = _mixer_operands(o_parts, w_out, tm)
    return pl.pallas_call(
        functools.partial(_ffn_kernel, n_parts=len(o_parts)),
        grid=(b, s // tm),
        in_specs=mix_specs + [pl.BlockSpec((1, tm, d), lambda bi, si: (bi, si, 0)), row, vec,
                              row, vec, vec,
                              resident((d, ff), lambda bi, si: (0, 0)),
                              resident((d, ff), lambda bi, si: (0, 0)),
                              resident((ff, d), lambda bi, si: (0, 0)),
                              row, vec],
        out_specs=pl.BlockSpec((1, tm, d), lambda bi, si: (bi, si, 0)),
        out_shape=jax.ShapeDtypeStruct((b, s, d), F32),
        compiler_params=_cparams(("parallel", "parallel")),
        name="dense_swiglu",
    )(*mix_ops, h, gain1.reshape(1, d), gate1.reshape(b, 1, d),
      gain_in.reshape(1, d), scale.reshape(b, 1, d), shift.reshape(b, 1, d),
      w_gate.astype(BF16), w_up.astype(BF16), w_down.astype(BF16),
      gain_out.reshape(1, d), gate.reshape(b, 1, d))


def _router_kernel(o_ref, wo_ref, h_ref, gain1_ref, gate1_ref, g_ref, sc_ref, sh_ref, rw_ref,
                   hout_ref, mi_ref, mf_ref, cnt_ref, carry_ref, *, tm):
    @pl.when((pl.program_id(0) == 0) & (pl.program_id(1) == 0))
    def _():
        carry_ref[...] = jnp.zeros_like(carry_ref)

    tr = tm // ROUTER_ROW_CHAINS
    lanef = lax.broadcasted_iota(I32, (tr, LANES), 1).astype(F32)
    r = lax.broadcasted_iota(I32, (tr, tr), 0)
    c = lax.broadcasted_iota(I32, (tr, tr), 1)
    before = jnp.where(c < r, 1.0, 0.0).astype(BF16)
    counts = carry_ref[0:1, :]
    for ch in range(ROUTER_ROW_CHAINS):
        rows = slice(ch * tr, (ch + 1) * tr)
        h = _mixer_residual([o_ref], [wo_ref], h_ref[0, rows], gain1_ref[...], gate1_ref[0], rows)
        hout_ref[0, rows] = h
        u = _modulate(h, g_ref[...], sc_ref[0], sh_ref[0])
        logits = _dot_split(u, rw_ref[...])
        lg = jnp.where(lanef < N_EXPERTS, logits, NEG_INF)
        v1 = jnp.max(lg, axis=1, keepdims=True)
        i1 = jnp.min(jnp.where(lg == v1, lanef, float(LANES)), axis=1, keepdims=True)
        lg2 = jnp.where(lanef == i1, NEG_INF, lg)
        v2 = jnp.max(lg2, axis=1, keepdims=True)
        i2 = jnp.min(jnp.where(lg2 == v2, lanef, float(LANES)), axis=1, keepdims=True)
        e2 = jnp.exp(v2 - v1)
        p1 = 1.0 / (1.0 + e2)
        p2 = e2 / (1.0 + e2)
        oh1 = jnp.where(lanef == i1, 1.0, 0.0)
        oh2 = jnp.where(lanef == i2, 1.0, 0.0)
        oh = oh1 + oh2
        tot = _dot(before, oh.astype(BF16)) + counts
        rank1 = jnp.sum(oh1 * tot, axis=1, keepdims=True)
        rank2 = jnp.sum(oh2 * tot, axis=1, keepdims=True)
        counts = counts + jnp.sum(oh, axis=0, keepdims=True)
        mi = jnp.where(lanef == 0.0, i1, jnp.where(lanef == 1.0, i2,
             jnp.where(lanef == 2.0, rank1, jnp.where(lanef == 3.0, rank2, 0.0))))
        mi_ref[rows] = mi.astype(I32)
        mf_ref[rows] = jnp.where(lanef == 0.0, p1, jnp.where(lanef == 1.0, p2, 0.0))
    carry_ref[...] = jnp.broadcast_to(counts, carry_ref.shape)
    cnt_ref[...] = carry_ref[...]


def _mixer_tail_router(o, w_out, h, gain1, gate1, gain, scale, shift, router_w):
    b, s, d = h.shape
    tm = TM_ROUTE
    n = b * s
    ns = s // tm
    rw = jnp.pad(router_w.astype(F32), ((0, 0), (0, LANES - router_w.shape[1])))
    vec = pl.BlockSpec((1, 1, d), lambda bi, si: (bi, 0, 0))
    row = pl.BlockSpec((1, d), lambda bi, si: (0, 0))
    act = pl.BlockSpec((1, tm, d), lambda bi, si: (bi, si, 0))
    meta = pl.BlockSpec((tm, LANES), lambda bi, si: (bi * ns + si, 0))
    mix_ops, mix_specs = _mixer_operands([o], w_out, tm)
    return pl.pallas_call(
        functools.partial(_router_kernel, tm=tm),
        grid=(b, ns),
        in_specs=mix_specs + [act, row, vec, row, vec, vec, pl.BlockSpec(rw.shape, lambda bi, si: (0, 0))],
        out_specs=[act, meta, meta, pl.BlockSpec((8, LANES), lambda bi, si: (0, 0))],
        out_shape=[jax.ShapeDtypeStruct((b, s, d), F32),
                   jax.ShapeDtypeStruct((n, LANES), I32),
                   jax.ShapeDtypeStruct((n, LANES), F32),
                   jax.ShapeDtypeStruct((8, LANES), F32)],
        scratch_shapes=[pltpu.VMEM((8, LANES), F32)],
        compiler_params=_cparams(("arbitrary", "arbitrary")),
        name="moe_router",
    )(*mix_ops, h, gain1.reshape(1, d), gate1.reshape(b, 1, d),
      gain.reshape(1, d), scale.reshape(b, 1, d), shift.reshape(b, 1, d), rw)


def _scatter_kernel(dest_ref, ztile_ref, h_ref, g_ref, sc_ref, sh_ref, xs_ref, ubuf, zbuf, sems, zsem,
                    *, tm, ns):
    step = pl.program_id(0) * ns + pl.program_id(1)
    nsteps = pl.num_programs(0) * ns
    slot = lax.rem(step, 2)
    tz = zbuf.shape[0]

    @pl.when(step == 0)
    def _():
        zbuf[...] = jnp.zeros_like(zbuf)
        for e in range(2 * N_EXPERTS):
            @pl.when(ztile_ref[e] >= 0)
            def _():
                row0 = pl.multiple_of(ztile_ref[e] * tz, tz)
                pltpu.make_async_copy(zbuf, xs_ref.at[pl.ds(row0, tz), :], zsem).start()
        for e in range(2 * N_EXPERTS):
            @pl.when(ztile_ref[e] >= 0)
            def _():
                pltpu.make_async_copy(zbuf, xs_ref.at[pl.ds(0, tz), :], zsem).wait()

    def wait_slot(sl):
        for _ in range(2):
            pltpu.make_async_copy(ubuf.at[sl], ubuf.at[sl], sems.at[sl]).wait()

    base = step * tm

    def run(sl):
        @pl.when(step >= 2)
        def _():
            wait_slot(sl)

        ubuf[sl] = _modulate(h_ref[0], g_ref[...], sc_ref[0], sh_ref[0])

        def issue(i, carry):
            t = 2 * (base + i)
            src = ubuf.at[sl, pl.ds(i, 1), :]
            pltpu.make_async_copy(src, xs_ref.at[pl.ds(dest_ref[t], 1), :], sems.at[sl]).start()
            pltpu.make_async_copy(src, xs_ref.at[pl.ds(dest_ref[t + 1], 1), :], sems.at[sl]).start()
            return carry

        lax.fori_loop(0, tm, issue, 0, unroll=DMA_UNROLL)

        @pl.when(step == nsteps - 1)
        def _():
            wait_slot(sl)

            @pl.when(nsteps >= 2)
            def _():
                wait_slot(1 - sl)

    for sl in range(2):
        pl.when(slot == sl)(functools.partial(run, sl))


def _scatter(dest, zero_tile, h, gain, scale, shift, m_pad):
    b, s, d = h.shape
    tm = TM_SCATTER
    ns = s // tm
    vec = pl.BlockSpec((1, 1, d), lambda bi, si, dest, zt: (bi, 0, 0))
    grid_spec = pltpu.PrefetchScalarGridSpec(
        num_scalar_prefetch=2,
        grid=(b, ns),
        in_specs=[pl.BlockSpec((1, tm, d), lambda bi, si, dest, zt: (bi, si, 0)),
                  pl.BlockSpec((1, d), lambda bi, si, dest, zt: (0, 0)),
                  vec, vec],
        out_specs=pl.BlockSpec(memory_space=pl.ANY),
        scratch_shapes=[pltpu.VMEM((2, tm, d), F32), pltpu.VMEM((TM_EXPERT, d), F32),
                        pltpu.SemaphoreType.DMA((2,)), pltpu.SemaphoreType.DMA],
    )
    return pl.pallas_call(
        functools.partial(_scatter_kernel, tm=tm, ns=ns),
        grid_spec=grid_spec,
        out_shape=jax.ShapeDtypeStruct((m_pad, d), F32),
        compiler_params=_cparams(("arbitrary", "arbitrary")),
        name="moe_scatter",
    )(dest, zero_tile, h, gain.reshape(1, d), scale.reshape(b, 1, d), shift.reshape(b, 1, d))


def _expert_kernel(te_ref, tv_ref, tx_ref, x_ref, wg_ref, wu_ref, wd_ref, o_ref):
    t = pl.program_id(0)
    f = pl.program_id(1)

    @pl.when(tv_ref[t] == 1)
    def _():
        x = x_ref[...].astype(BF16)
        hid = (_silu(_dot(x, wg_ref[0])) * _dot(x, wu_ref[0])).astype(BF16)
        y = _dot(hid, wd_ref[0])

        @pl.when(f == 0)
        def _():
            o_ref[...] = y

        @pl.when(f > 0)
        def _():
            o_ref[...] = o_ref[...] + y

    @pl.when((tv_ref[t] == 0) & (f == 0))
    def _():
        o_ref[...] = jnp.zeros_like(o_ref)


def _experts(tile_expert, tile_valid, tile_x, xs, w_gate, w_up, w_down):
    m_pad, d = xs.shape
    tm = TM_EXPERT
    n_tiles = tile_expert.shape[0]
    ff = w_gate.shape[2]
    fs = FF_STEPS_EXPERT
    tf = ff // fs

    def ff_idx(f, tv, t):
        return f * tv[t] + (fs - 1) * (1 - tv[t])

    grid_spec = pltpu.PrefetchScalarGridSpec(
        num_scalar_prefetch=3,
        grid=(n_tiles, fs),
        in_specs=[pl.BlockSpec((tm, d), lambda t, f, te, tv, tx: (tx[t], 0)),
                  pl.BlockSpec((1, d, tf), lambda t, f, te, tv, tx: (te[t], 0, ff_idx(f, tv, t))),
                  pl.BlockSpec((1, d, tf), lambda t, f, te, tv, tx: (te[t], 0, ff_idx(f, tv, t))),
                  pl.BlockSpec((1, tf, d), lambda t, f, te, tv, tx: (te[t], ff_idx(f, tv, t), 0))],
        out_specs=pl.BlockSpec((tm, d), lambda t, f, te, tv, tx: (t, 0)),
    )
    return pl.pallas_call(
        _expert_kernel,
        grid_spec=grid_spec,
        out_shape=jax.ShapeDtypeStruct((m_pad, d), F32),
        compiler_params=_cparams(("arbitrary", "arbitrary")),
        name="moe_experts",
    )(tile_expert, tile_valid, tile_x, xs,
      w_gate.astype(BF16), w_up.astype(BF16), w_down.astype(BF16))


def _combine_kernel(dest_ref, y_ref, mf_ref, h_ref, gain_ref, gate_ref, o_ref, ybuf, sems, *, tm, ns):
    step = pl.program_id(0) * ns + pl.program_id(1)
    nsteps = pl.num_programs(0) * ns
    slot = lax.rem(step, 2)

    def issue(st, sl):
        base = st * tm

        def body(i, carry):
            t = 2 * (base + i)
            pltpu.make_async_copy(y_ref.at[pl.ds(dest_ref[t], 1), :],
                                  ybuf.at[sl, 0, pl.ds(i, 1), :], sems.at[sl]).start()
            pltpu.make_async_copy(y_ref.at[pl.ds(dest_ref[t + 1], 1), :],
                                  ybuf.at[sl, 1, pl.ds(i, 1), :], sems.at[sl]).start()
            return carry

        lax.fori_loop(0, tm, body, 0, unroll=DMA_UNROLL)

    @pl.when(step == 0)
    def _():
        issue(0, 0)

    def run(sl):
        @pl.when(step + 1 < nsteps)
        def _():
            issue(step + 1, 1 - sl)

        for k in range(2):
            pltpu.make_async_copy(ybuf.at[sl, k], ybuf.at[sl, k], sems.at[sl]).wait()
        mf = mf_ref[...]
        y = mf[:, 0:1] * ybuf[sl, 0] + mf[:, 1:2] * ybuf[sl, 1]
        o_ref[0] = h_ref[0] + gate_ref[0] * (_rms(y) * gain_ref[...])

    for sl in range(2):
        pl.when(slot == sl)(functools.partial(run, sl))


def _combine(dest, ys, mf, h, gain, gate):
    b, s, d = h.shape
    tm = TM_COMBINE
    ns = s // tm
    grid_spec = pltpu.PrefetchScalarGridSpec(
        num_scalar_prefetch=1,
        grid=(b, ns),
        in_specs=[pl.BlockSpec(memory_space=pl.ANY),
                  pl.BlockSpec((tm, LANES), lambda bi, si, dest: (bi * ns + si, 0)),
                  pl.BlockSpec((1, tm, d), lambda bi, si, dest: (bi, si, 0)),
                  pl.BlockSpec((1, d), lambda bi, si, dest: (0, 0)),
                  pl.BlockSpec((1, 1, d), lambda bi, si, dest: (bi, 0, 0))],
        out_specs=pl.BlockSpec((1, tm, d), lambda bi, si, dest: (bi, si, 0)),
        scratch_shapes=[pltpu.VMEM((2, 2, tm, d), F32), pltpu.SemaphoreType.DMA((2,))],
    )
    return pl.pallas_call(
        functools.partial(_combine_kernel, tm=tm, ns=ns),
        grid_spec=grid_spec,
        out_shape=jax.ShapeDtypeStruct((b, s, d), F32),
        compiler_params=_cparams(("arbitrary", "arbitrary")),
        name="moe_combine",
    )(dest, ys, mf, h, gain.reshape(1, d), gate.reshape(b, 1, d))


def _mixer_tail_moe(o, w_out, h, gain1, gate1, gain_in, scale, shift, router_w, w_gate, w_up, w_down,
                    gain_out, gate):
    b, s, d = h.shape
    n = b * s
    tm = TM_EXPERT
    h, mi, mf, cnt = _mixer_tail_router(o, w_out, h, gain1, gate1, gain_in, scale, shift, router_w)
    counts = cnt[0, :N_EXPERTS].astype(I32)
    tiles_per = (counts + tm - 1) // tm
    seg_start = (jnp.cumsum(tiles_per) - tiles_per) * tm
    dest = (seg_start[mi[:, 0:2]] + mi[:, 2:4]).reshape(2 * n)
    n_tiles = (2 * n) // tm + N_EXPERTS
    m_pad = n_tiles * tm
    tile_end = jnp.cumsum(tiles_per)
    tidx = jnp.arange(n_tiles, dtype=I32)
    tile_valid = (tidx < tile_end[-1]).astype(I32)
    tile_expert = jnp.minimum(jnp.searchsorted(tile_end, tidx, side="right"), N_EXPERTS - 1).astype(I32)
    tile_x = jnp.minimum(tidx, tile_end[-1] - 1)
    tail = tile_end[-1] + jnp.arange(N_EXPERTS, dtype=I32)
    zero_tile = jnp.concatenate([jnp.where(tiles_per > 0, tile_end - 1, -1),
                                 jnp.where(tail < n_tiles, tail, -1)]).astype(I32)
    xs = _scatter(dest, zero_tile, h, gain_in, scale, shift, m_pad)
    ys = _experts(tile_expert, tile_valid, tile_x, xs, w_gate, w_up, w_down)
    return _combine(dest, ys, mf, h, gain_out, gate)


def kernel(x, c, mod_w, mod_b, norm_g, attn_in_w_even, fox_gate_bias, attn_out_w_even,
           attn_in_w_odd, attn_out_w_odd, rel_bias_table, ffn_w_gate, ffn_w_up, ffn_w_down,
           router_w, exp_w_gate, exp_w_up, exp_w_down):
    depth = mod_w.shape[0]
    s_len = x.shape[1]
    mods = _mods(c, mod_w, mod_b)
    dil_bias, moba_bias = _bias_tiles(rel_bias_table, s_len)
    h = x
    for layer in range(depth):
        j = layer // 2
        sh1, sc1, g1, sh2, sc2, g2 = jnp.split(mods[layer], 6, axis=-1)
        gains = norm_g[layer]
        if layer % 2 == 0:
            fox_in, moba_in = _inproj_even(h, gains[0], sc1, sh1, attn_in_w_even[j], fox_gate_bias[j])
            o_parts = [_fox_attention(*fox_in), _moba_attention(*moba_in, moba_bias)]
            h = _mixer_tail_ffn(o_parts, attn_out_w_even[j], h, gains[1], g1, gains[2], sc2, sh2,
                                ffn_w_gate[j], ffn_w_up[j], ffn_w_down[j], gains[3], g2)
        else:
            q, k, v = _inproj_odd(h, gains[0], sc1, sh1, attn_in_w_odd[j])
            o = _dilated_attention(q, k, v, dil_bias)
            h = _mixer_tail_moe(o, attn_out_w_odd[j], h, gains[1], g1, gains[2], sc2, sh2, router_w[j],
                                exp_w_gate[j], exp_w_up[j], exp_w_down[j], gains[3], g2)
    return h
```

```python
import functools
import math

import numpy as np
import jax
import jax.numpy as jnp
from jax import lax
from jax.experimental import pallas as pl
from jax.experimental.pallas import tpu as pltpu

F32 = jnp.float32
BF16 = jnp.bfloat16
I32 = jnp.int32

HEAD_DIM = 64
LANES = 128
N_HEADS = 16
N_HEADS_FOX = 8
ATTN_SCALE = HEAD_DIM ** -0.5
LOG2E = math.log2(math.e)
Q_SCALE = ATTN_SCALE * LOG2E
N_DECAY_PIECES = 3
MOBA_BLOCK = 256
MOBA_TOPK = 3
DIL_PATTERNS = ((128, 1), (512, 4), (2048, 16))
NUM_BUCKETS = 32
MAX_DISTANCE = 2048
N_EXPERTS = 8
NORM_EPS = 1e-6
NEG_INF = float("-inf")
MASK_BIG = 1e30

VMEM_LIMIT = 56 * 1024 * 1024

TM_PROJ = 1024
TM_FFN = 1024
TQ_FOX = 2048
TQ_MOBA = 2048
TK_FOX = 512
FOX_PAIRS_PER_STEP = 1
MOBA_PAIRS_PER_STEP = 1
TM_ROUTE = 512
ROUTER_ROW_CHAINS = 2
TM_SCATTER = 256
TM_EXPERT = 512
TM_COMBINE = 256
FF_STEPS_EXPERT = 2
DMA_UNROLL = True
DIL_CHUNKS_PER_STEP = 16


def _cparams(sem):
    return pltpu.CompilerParams(dimension_semantics=sem, vmem_limit_bytes=VMEM_LIMIT)


def _t5_bucket_np(n):
    n = np.maximum(n, 0)
    max_exact = NUM_BUCKETS // 2
    nf = np.maximum(n, 1).astype(np.float64)
    large = max_exact + (np.log(nf / max_exact) / math.log(MAX_DISTANCE / max_exact)
                         * (NUM_BUCKETS - max_exact)).astype(np.int64)
    large = np.minimum(large, NUM_BUCKETS - 1)
    return np.where(n < max_exact, n, large)


_MAX_DIST = 1 << 16
_BUCKET_OF = _t5_bucket_np(np.arange(_MAX_DIST))
_BUCKET_THR = [int(np.searchsorted(_BUCKET_OF, k, side="left")) for k in range(NUM_BUCKETS)]


def _bias_from_dist(tab_ref, h, dist, dlo, dhi):
    lo_b = int(_BUCKET_OF[max(dlo, 0)])
    hi_b = int(_BUCKET_OF[dhi])
    val = jnp.zeros(dist.shape, F32) + tab_ref[lo_b, h]
    for k in range(lo_b + 1, hi_b + 1):
        val = jnp.where(dist >= _BUCKET_THR[k], tab_ref[k, h], val)
    return val


def _dil_bias_kernel(tab_ref, o_ref):
    h = pl.program_id(0)
    for g, (window, dil) in enumerate(DIL_PATTERNS):
        span = window // dil
        i = lax.broadcasted_iota(I32, (span, 2 * span), 0)
        j = lax.broadcasted_iota(I32, (span, 2 * span), 1)
        rel = i + span - j
        val = _bias_from_dist(tab_ref, h, rel * dil, 0, span * dil) * LOG2E
        band = jnp.where(rel >= 0, jnp.where(rel <= span, val, NEG_INF), NEG_INF)
        o_ref[0, 2 * g] = band
        o_ref[0, 2 * g + 1] = jnp.where(j >= span, band, NEG_INF)


def _moba_bias_kernel(tab_ref, o_ref, *, n_blk, head0):
    h = pl.program_id(0) + head0
    i = lax.broadcasted_iota(I32, (MOBA_BLOCK, MOBA_BLOCK), 0)
    j = lax.broadcasted_iota(I32, (MOBA_BLOCK, MOBA_BLOCK), 1)
    for d in range(n_blk):
        dist = d * MOBA_BLOCK + i - j
        val = _bias_from_dist(tab_ref, h, dist, d * MOBA_BLOCK - (MOBA_BLOCK - 1),
                              d * MOBA_BLOCK + (MOBA_BLOCK - 1)) * LOG2E
        if d == 0:
            val = jnp.where(dist >= 0, val, NEG_INF)
        o_ref[0, d] = val


def _bias_tiles(rel_bias_table, s_len):
    n_blk = s_len // MOBA_BLOCK
    span = DIL_PATTERNS[0][0]
    n_var = 2 * len(DIL_PATTERNS)
    smem = pl.BlockSpec(memory_space=pltpu.SMEM)
    dil = pl.pallas_call(
        _dil_bias_kernel,
        grid=(N_HEADS,),
        in_specs=[smem],
        out_specs=pl.BlockSpec((1, n_var, span, 2 * span), lambda h: (h, 0, 0, 0)),
        out_shape=jax.ShapeDtypeStruct((N_HEADS, n_var, span, 2 * span), F32),
        compiler_params=_cparams(("parallel",)),
        name="dil_bias",
    )(rel_bias_table)
    n_moba = N_HEADS - N_HEADS_FOX
    moba = pl.pallas_call(
        functools.partial(_moba_bias_kernel, n_blk=n_blk, head0=N_HEADS_FOX),
        grid=(n_moba,),
        in_specs=[smem],
        out_specs=pl.BlockSpec((1, n_blk, MOBA_BLOCK, MOBA_BLOCK), lambda h: (h, 0, 0, 0)),
        out_shape=jax.ShapeDtypeStruct((n_moba, n_blk, MOBA_BLOCK, MOBA_BLOCK), F32),
        compiler_params=_cparams(("parallel",)),
        name="moba_bias",
    )(rel_bias_table)
    return dil, moba


def _split_bf16(a):
    hi = a.astype(BF16)
    lo = (a - hi.astype(F32)).astype(BF16)
    return hi, lo


def _dot(a, b):
    return jnp.dot(a, b, preferred_element_type=F32)


def _dot_nt(a, b):
    return lax.dot_general(a, b, (((1,), (1,)), ((), ())), preferred_element_type=F32)


def _dot_split(a, b):
    a_hi, a_lo = _split_bf16(a)
    b_hi, b_lo = _split_bf16(b)
    return _dot(a_hi, b_hi) + (_dot(a_hi, b_lo) + _dot(a_lo, b_hi))


def _rms(x):
    return x * lax.rsqrt(jnp.mean(x * x, axis=-1, keepdims=True) + NORM_EPS)


def _modulate(x, gain, scale, shift):
    return (_rms(x) * gain) * (1.0 + scale) + shift


def _silu(x):
    return x * jax.nn.sigmoid(x)


def _mods_kernel(c_ref, w_ref, b_ref, o_ref):
    o_ref[0] = _dot_split(_silu(c_ref[...]), w_ref[0]) + b_ref[0]


def _mods(c, mod_w, mod_b):
    depth, d, e = mod_w.shape
    b = c.shape[0]
    tn = 1536
    return pl.pallas_call(
        _mods_kernel,
        grid=(depth, e // tn),
        in_specs=[pl.BlockSpec((b, d), lambda l, j: (0, 0)),
                  pl.BlockSpec((1, d, tn), lambda l, j: (l, 0, j)),
                  pl.BlockSpec((1, 1, tn), lambda l, j: (l, 0, j))],
        out_specs=pl.BlockSpec((1, b, tn), lambda l, j: (l, 0, j)),
        out_shape=jax.ShapeDtypeStruct((depth, b, e), F32),
        compiler_params=_cparams(("parallel", "parallel")),
        name="adaln_mods",
    )(c, mod_w, mod_b.reshape(depth, 1, e))


def _inproj_even_kernel(h_ref, g_ref, sc_ref, sh_ref, w_ref, wf_ref, gb_ref,
                        qa_ref, k0a_ref, k1a_ref, v0a_ref, v1a_ref,
                        qb_ref, k0b_ref, k1b_ref, v0b_ref, v1b_ref, carry_ref, *, tm):
    si = pl.program_id(1)
    u = _modulate(h_ref[0], g_ref[...], sc_ref[0], sh_ref[0]).astype(BF16)
    width = qa_ref.shape[-1]
    n_pairs = width // LANES

    def proj(i):
        return _dot(u, w_ref[:, i * width:(i + 1) * width])

    lane = lax.broadcasted_iota(I32, (1, LANES), 1)
    left = lane < HEAD_DIM
    row = lax.broadcasted_iota(I32, (tm, LANES), 0)

    x = _dot(u, wf_ref[...]) + gb_ref[...]
    lf = jnp.where(lane < N_HEADS_FOX, jnp.minimum(x, 0.0) - jnp.log1p(jnp.exp(-jnp.abs(x))), 0.0)
    k = 1
    while k < tm:
        lf = lf + jnp.where(row >= k, pltpu.roll(lf, k, axis=0), 0.0)
        k *= 2

    @pl.when(si == 0)
    def _():
        carry_ref[...] = jnp.zeros_like(carry_ref)

    cum = lf + carry_ref[0:1, :]
    carry_ref[...] = jnp.broadcast_to(cum[tm - 1:tm, :], carry_ref.shape)
    rest = cum * (-LOG2E)
    decay = jnp.zeros((tm, LANES), F32)
    for p in range(N_DECAY_PIECES):
        piece = rest.astype(BF16).astype(F32)
        rest = rest - piece
        decay = decay + (pltpu.roll(piece, p * N_HEADS_FOX, axis=1) if p else piece)
    decay_lo = decay.astype(BF16)
    decay_hi = pltpu.roll(decay, HEAD_DIM, axis=1).astype(BF16)

    ones = jnp.ones((1, LANES), BF16)
    blk = (si * tm + row) // MOBA_BLOCK
    blk_lo = jnp.where(lane == blk, 1.0, 0.0).astype(BF16)
    blk_hi = jnp.where(lane == blk + HEAD_DIM, 1.0, 0.0).astype(BF16)

    def emit(first, k0_ref, k1_ref, v0_ref, v1_ref, k_lo, k_hi):
        kk = proj(first + 1).astype(BF16)
        vv = proj(first + 2).astype(BF16)
        for hp in range(n_pairs):
            sl = slice(hp * LANES, (hp + 1) * LANES)
            k0_ref[0, :, sl] = jnp.where(left, kk[:, sl], k_hi)
            k1_ref[0, :, sl] = jnp.where(left, k_lo, kk[:, sl])
            v0_ref[0, :, sl] = jnp.where(left, vv[:, sl], ones)
            v1_ref[0, :, sl] = jnp.where(left, ones, vv[:, sl])

    qa_ref[0] = (proj(0) * Q_SCALE).astype(BF16)
    emit(0, k0a_ref, k1a_ref, v0a_ref, v1a_ref, decay_lo, decay_hi)
    qb_ref[0] = (proj(3) * Q_SCALE).astype(BF16)
    emit(3, k0b_ref, k1b_ref, v0b_ref, v1b_ref, blk_lo, blk_hi)


def _inproj_even(h, gain, scale, shift, w_in, gate_bias):
    b, s, d = h.shape
    tm = TM_PROJ
    da = N_HEADS_FOX * HEAD_DIM
    assert s // MOBA_BLOCK <= HEAD_DIM and N_DECAY_PIECES * N_HEADS_FOX <= HEAD_DIM
    cuts = np.cumsum([da, da, da, N_HEADS_FOX, da, da]).tolist()
    q_a, k_a, v_a, f_a, q_b, k_b, v_b = jnp.split(w_in, cuts, axis=1)
    w = jnp.concatenate([q_a, k_a, v_a, q_b, k_b, v_b], axis=1).astype(BF16)
    wf = jnp.pad(f_a, ((0, 0), (0, LANES - N_HEADS_FOX))).astype(BF16)
    gb = jnp.pad(gate_bias.astype(F32), (0, LANES - N_HEADS_FOX)).reshape(1, LANES)
    act = jax.ShapeDtypeStruct((b, s, da), BF16)
    act_spec = pl.BlockSpec((1, tm, da), lambda bi, si: (bi, si, 0))
    vec = pl.BlockSpec((1, 1, d), lambda bi, si: (bi, 0, 0))
    outs = pl.pallas_call(
        functools.partial(_inproj_even_kernel, tm=tm),
        grid=(b, s // tm),
        in_specs=[pl.BlockSpec((1, tm, d), lambda bi, si: (bi, si, 0)),
                  pl.BlockSpec((1, d), lambda bi, si: (0, 0)),
                  vec, vec,
                  pl.BlockSpec(w.shape, lambda bi, si: (0, 0)),
                  pl.BlockSpec(wf.shape, lambda bi, si: (0, 0)),
                  pl.BlockSpec(gb.shape, lambda bi, si: (0, 0))],
        out_specs=[act_spec] * 10,
        out_shape=[act] * 10,
        scratch_shapes=[pltpu.VMEM((8, LANES), F32)],
        compiler_params=_cparams(("parallel", "arbitrary")),
        name="inproj_even",
    )(h, gain.reshape(1, d), scale.reshape(b, 1, d), shift.reshape(b, 1, d), w, wf, gb)
    return outs[:5], outs[5:]


def _inproj_odd_kernel(h_ref, g_ref, sc_ref, sh_ref, w_ref, q_ref, k_ref, v_ref):
    u = _modulate(h_ref[0], g_ref[...], sc_ref[0], sh_ref[0]).astype(BF16)
    width = q_ref.shape[-1]
    q_ref[0] = _dot(u, w_ref[:, 0:width]) * Q_SCALE
    k_ref[0] = _dot(u, w_ref[:, width:2 * width])
    v_ref[0] = _dot(u, w_ref[:, 2 * width:3 * width])


def _inproj_odd(h, gain, scale, shift, w_in):
    b, s, d = h.shape
    tm = TM_PROJ
    dq = w_in.shape[1] // 3
    act = jax.ShapeDtypeStruct((b, s, dq), F32)
    act_spec = pl.BlockSpec((1, tm, dq), lambda bi, si: (bi, si, 0))
    vec = pl.BlockSpec((1, 1, d), lambda bi, si: (bi, 0, 0))
    return pl.pallas_call(
        _inproj_odd_kernel,
        grid=(b, s // tm),
        in_specs=[pl.BlockSpec((1, tm, d), lambda bi, si: (bi, si, 0)),
                  pl.BlockSpec((1, d), lambda bi, si: (0, 0)),
                  vec, vec,
                  pl.BlockSpec(w_in.shape, lambda bi, si: (0, 0))],
        out_specs=[act_spec] * 3,
        out_shape=[act] * 3,
        compiler_params=_cparams(("parallel", "parallel")),
        name="inproj_odd",
    )(h, gain.reshape(1, d), scale.reshape(b, 1, d), shift.reshape(b, 1, d), w_in.astype(BF16))


def _tile_lanes(x, width):
    return jnp.concatenate([x] * (width // LANES), axis=1)


def _flash_update(s, v, m_ref, acc_ref):
    m_prev = m_ref[...]
    m_new = jnp.maximum(m_prev, jnp.max(s, axis=1, keepdims=True))
    p = jnp.exp2(s - _tile_lanes(m_new, s.shape[1]))
    acc_ref[...] = jnp.exp2(m_prev - m_new) * acc_ref[...] + _dot(p.astype(BF16), v)
    m_ref[...] = m_new


def _finish_pair(acc_ref, left, first=0):
    acc0 = acc_ref[first]
    acc1 = acc_ref[first + 1]
    den = pltpu.roll(jnp.where(left, acc1, acc0), HEAD_DIM, axis=1)
    return jnp.where(left, acc0, acc1) / den


def _fox_kernel(q_ref, k0_ref, k1_ref, v0_ref, v1_ref, o_ref, m_ref, acc_ref, *, tq, tk, n_pairs):
    qi = pl.program_id(2)
    lane = lax.broadcasted_iota(I32, (1, LANES), 1)
    left = lane < HEAD_DIM

    def piece_lanes(lane0):
        hit = lane == lane0
        for p in range(1, N_DECAY_PIECES):
            hit = jnp.logical_or(hit, lane == lane0 + p * N_HEADS_FOX)
        return jnp.where(hit, 1.0, 0.0).astype(BF16)

    chains = []
    for pr in range(n_pairs):
        hp = pl.program_id(1) * n_pairs + pr
        lanes = slice(pr * LANES, (pr + 1) * LANES)
        q = q_ref[0, :, lanes]
        chains.append((jnp.where(left, q, piece_lanes(HEAD_DIM + 2 * hp)), k0_ref, v0_ref, lanes))
        chains.append((jnp.where(left, piece_lanes(2 * hp + 1), q), k1_ref, v1_ref, lanes))
    row = lax.broadcasted_iota(I32, (tq, tk), 0)
    col = lax.broadcasted_iota(I32, (tq, tk), 1)
    m_ref[...] = jnp.full(m_ref.shape, NEG_INF, F32)
    acc_ref[...] = jnp.zeros(acc_ref.shape, F32)
    n_sub = tq // tk

    def step(kv, mask, row0=0):
        off = pl.multiple_of(kv * tk, tk)
        rows = pl.ds(row0, tq - row0)
        for c, (qc, k_ref, v_ref, lanes) in enumerate(chains):
            s = _dot_nt(qc[row0:], k_ref[0, pl.ds(off, tk), lanes])
            if mask is not None:
                s = jnp.where(mask[row0:], s, NEG_INF)
            _flash_update(s, v_ref[0, pl.ds(off, tk), lanes], m_ref.at[c, rows], acc_ref.at[c, rows])

    def body(kv, carry):
        step(kv, None)
        return carry

    lax.fori_loop(0, qi * n_sub, body, 0)
    for d in range(n_sub):
        step(qi * n_sub + d, col + d * tk <= row, d * tk)
    for pr in range(n_pairs):
        o_ref[0, :, pr * LANES:(pr + 1) * LANES] = _finish_pair(acc_ref, left, 2 * pr).astype(o_ref.dtype)


def _fox_attention(q, k0, k1, v0, v1):
    b, s, da = q.shape
    n_pairs = FOX_PAIRS_PER_STEP
    width = n_pairs * LANES
    tq = min(TQ_FOX, s)
    kv = pl.BlockSpec((1, s, width), lambda bi, h, qi: (bi, 0, h))
    return pl.pallas_call(
        functools.partial(_fox_kernel, tq=tq, tk=min(TK_FOX, tq), n_pairs=n_pairs),
        grid=(b, da // width, s // tq),
        in_specs=[pl.BlockSpec((1, tq, width), lambda bi, h, qi: (bi, qi, h)), kv, kv, kv, kv],
        out_specs=pl.BlockSpec((1, tq, width), lambda bi, h, qi: (bi, qi, h)),
        out_shape=jax.ShapeDtypeStruct((b, s, da), BF16),
        scratch_shapes=[pltpu.VMEM((2 * n_pairs, tq, LANES), F32)] * 2,
        compiler_params=_cparams(("parallel", "parallel", "arbitrary")),
        name="fox_attention",
    )(q, k0, k1, v0, v1)


def _moba_kernel(q_ref, k0_ref, k1_ref, v0_ref, v1_ref, bias_ref, o_ref, km_ref, m_ref, acc_ref,
                 *, n_blk, tq, n_pairs):
    blk = MOBA_BLOCK
    tk = 2 * blk
    nq = tq // blk
    a = pl.program_id(2)
    gate_lane0 = (HEAD_DIM, 0)
    lane = lax.broadcasted_iota(I32, (1, LANES), 1)
    left = lane < HEAD_DIM
    mine = (left, jnp.logical_not(left))
    k_refs = (k0_ref, k1_ref)
    v_refs = (v0_ref, v1_ref)

    @pl.when(a == 0)
    def _():
        km_ref[...] = jnp.zeros_like(km_ref)
        for pr in range(n_pairs):
            lanes = slice(pr * LANES, (pr + 1) * LANES)
            for n in range(n_blk):
                rows = slice(n * blk, (n + 1) * blk)
                kb = jnp.where(left, k0_ref[0, rows, lanes], k1_ref[0, rows, lanes]).astype(F32)
                mean = jnp.sum(kb, axis=0, keepdims=True) * (1.0 / blk)
                for lane0 in gate_lane0:
                    km_ref[pr, lane0 + n:lane0 + n + 1, :] = mean

    nb = -(-n_blk // 8) * 8
    blkf = lax.broadcasted_iota(I32, (nb, tq), 0).astype(F32)
    own = (lax.broadcasted_iota(I32, (nb, tq), 1) // blk + a * nq).astype(F32)
    chains = []
    for pr in range(n_pairs):
        lanes = slice(pr * LANES, (pr + 1) * LANES)
        q = q_ref[0, :, lanes]
        km_hi, km_lo = _split_bf16(km_ref[pr])
        for j in range(2):
            qj = jnp.where(mine[j], q, jnp.zeros_like(q))
            lane0 = gate_lane0[j]
            gate = (_dot_nt(km_hi, qj) + _dot_nt(km_lo, qj))[lane0:lane0 + nb]
            gate = jnp.where(blkf < own, gate, NEG_INF)
            pen = jnp.where(blkf == own, 0.0, -MASK_BIG)
            for _ in range(MOBA_TOPK):
                mx = jnp.max(gate, axis=0, keepdims=True)
                cand = jnp.where(gate == mx, jnp.where(mx > NEG_INF, blkf, float(LANES)), float(LANES))
                pick = blkf == jnp.min(cand, axis=0, keepdims=True)
                pen = jnp.where(pick, 0.0, pen)
                gate = jnp.where(pick, NEG_INF, gate)
            parts = [pen, jnp.zeros((LANES - lane0 - nb, tq), F32)]
            if lane0:
                parts.insert(0, jnp.zeros((lane0, tq), F32))
            pen_q = jnp.concatenate(parts, axis=0).T
            chains.append((jnp.where(mine[j], q, pen_q.astype(BF16)), k_refs[j], v_refs[j], lanes, 2 * pr + j))

    m_ref[...] = jnp.full(m_ref.shape, NEG_INF, F32)
    acc_ref[...] = jnp.zeros(acc_ref.shape, F32)

    def step(i, dist, r0):
        off = pl.multiple_of(i * tk, tk)
        rows = pl.ds(r0 * blk, tq - r0 * blk)
        for qc, k_ref, v_ref, lanes, h in chains:
            bias = jnp.concatenate(
                [jnp.concatenate([bias_ref[h, dist(r, c)] for c in range(2)], axis=1)
                 for r in range(r0, nq)], axis=0)
            s = _dot_nt(qc[r0 * blk:], k_ref[0, pl.ds(off, tk), lanes]) + bias
            _flash_update(s, v_ref[0, pl.ds(off, tk), lanes], m_ref.at[h, rows], acc_ref.at[h, rows])

    n_full = a * (nq // 2)

    def body(i, carry):
        step(i, lambda r, c: a * nq + r - 2 * i - c, 0)
        return carry

    lax.fori_loop(0, n_full, body, 0)
    for e in range(nq // 2):
        step(n_full + e, lambda r, c, e=e: max(r - 2 * e - c, 0), 2 * e)
    for pr in range(n_pairs):
        o_ref[0, :, pr * LANES:(pr + 1) * LANES] = _finish_pair(acc_ref, left, 2 * pr).astype(o_ref.dtype)


def _moba_attention(q, k0, k1, v0, v1, bias_tiles):
    b, s, db = q.shape
    n_pairs = MOBA_PAIRS_PER_STEP
    width = n_pairs * LANES
    blk = MOBA_BLOCK
    n_blk = s // blk
    tq = min(TQ_MOBA, s)
    assert n_blk <= HEAD_DIM, "block gates of one head must fit in the other head's lanes"
    assert s % tq == 0 and tq % (2 * blk) == 0
    kv = pl.BlockSpec((1, s, width), lambda h, bi, qi: (bi, 0, h))
    return pl.pallas_call(
        functools.partial(_moba_kernel, n_blk=n_blk, tq=tq, n_pairs=n_pairs),
        grid=(db // width, b, s // tq),
        in_specs=[pl.BlockSpec((1, tq, width), lambda h, bi, qi: (bi, qi, h)), kv, kv, kv, kv,
                  pl.BlockSpec((2 * n_pairs, n_blk, blk, blk), lambda h, bi, qi: (h, 0, 0, 0),
                               pipeline_mode=pl.Buffered(1))],
        out_specs=pl.BlockSpec((1, tq, width), lambda h, bi, qi: (bi, qi, h)),
        out_shape=jax.ShapeDtypeStruct((b, s, db), BF16),
        scratch_shapes=[pltpu.VMEM((n_pairs, LANES, LANES), F32)]
                       + [pltpu.VMEM((2 * n_pairs, tq, LANES), F32)] * 2,
        compiler_params=_cparams(("parallel", "parallel", "arbitrary")),
        name="moba_attention",
    )(q, k0, k1, v0, v1, bias_tiles)


def _dilated_kernel(q_ref, k_ref, v_ref, bias_ref, o_ref, m_ref, acc_ref, *, s_len):
    lane = lax.broadcasted_iota(I32, (1, LANES), 1)
    left = lane < HEAD_DIM
    ones = jnp.ones((1, LANES), BF16)
    order = sorted(range(len(DIL_PATTERNS)), key=lambda i: -DIL_PATTERNS[i][1])
    for g in order:
        window, dil = DIL_PATTERNS[g]
        merge = g != order[0]
        span = window // dil
        unit = span * dil
        nc = s_len // unit
        n_u = min(DIL_CHUNKS_PER_STEP, nc)
        groups = nc // n_u
        n_res = min(DIL_CHUNKS_PER_STEP // n_u, dil)

        def rows(ref, start, dil=dil, span=span):
            if dil == 1:
                return ref[0, pl.ds(start, span), :]
            return ref[0, pl.ds(start, span, stride=dil), :]

        def get(ref, j, start, dil=dil, span=span):
            if dil == 1:
                return ref[j, pl.ds(start, span), :]
            return ref[j, pl.ds(start, span, stride=dil), :]

        def put(ref, j, start, val, dil=dil, span=span):
            if dil == 1:
                ref[j, pl.ds(start, span), :] = val
            else:
                ref[j, pl.ds(start, span, stride=dil), :] = val

        def body(it, carry, g=g, merge=merge, n_u=n_u, n_res=n_res, groups=groups, unit=unit, rows=rows,
                 put=put, get=get):
            r0 = (it // groups) * n_res
            grp = it - (it // groups) * groups
            is_first = grp == 0
            results = []
            for dr in range(n_res):
                start0 = r0 + dr + grp * (n_u * unit)
                prev0 = start0 - jnp.where(is_first, 0, unit)
                starts = [start0 + u * unit for u in range(n_u)]
                kc = [rows(k_ref, st).astype(BF16) for st in [prev0] + starts]
                vc = [rows(v_ref, st).astype(BF16) for st in [prev0] + starts]
                vcs = ([jnp.where(left, v, ones) for v in vc], [jnp.where(left, ones, v) for v in vc])
                for u, start in enumerate(starts):
                    var = 2 * g + jnp.where(is_first, 1, 0) if u == 0 else 2 * g
                    q = rows(q_ref, start)
                    kb = jnp.concatenate([kc[u], kc[u + 1]], axis=0)
                    qq = jnp.concatenate([jnp.where(left, q, 0.0), jnp.where(left, 0.0, q)],
                                         axis=0).astype(BF16)
                    s_both = _dot_nt(qq, kb)
                    for j in range(2):
                        s = s_both[j * span:(j + 1) * span] + bias_ref[j, var]
                        m_new = jnp.broadcast_to(jnp.max(s, axis=1, keepdims=True), (span, LANES))
                        if merge:
                            m_prev = get(m_ref, j, start)
                            m_new = jnp.maximum(m_new, m_prev)
                        p = jnp.exp2(s - _tile_lanes(m_new, 2 * span))
                        acc_new = _dot(p.astype(BF16), jnp.concatenate([vcs[j][u], vcs[j][u + 1]], axis=0))
                        if merge:
                            acc_new = jnp.exp2(m_prev - m_new) * get(acc_ref, j, start) + acc_new
                        results.append((j, start, m_new, acc_new))
            for j, start, m_new, acc_new in results:
                put(m_ref, j, start, m_new)
                put(acc_ref, j, start, acc_new)
            return carry

        lax.fori_loop(0, (dil // n_res) * groups, body, 0)
    o_ref[0] = _finish_pair(acc_ref, left).astype(o_ref.dtype)


def _dilated_attention(q, k, v, bias_tiles):
    b, s, dq = q.shape
    hp = dq // LANES
    for window, dil in DIL_PATTERNS:
        assert s % window == 0, "sequence must be a whole number of dilated units"
    qkv = pl.BlockSpec((1, s, LANES), lambda h, bi: (bi, 0, h))
    n_var, span, band = bias_tiles.shape[1:]
    return pl.pallas_call(
        functools.partial(_dilated_kernel, s_len=s),
        grid=(hp, b),
        in_specs=[qkv, qkv, qkv,
                  pl.BlockSpec((2, n_var, span, band), lambda h, bi: (h, 0, 0, 0))],
        out_specs=pl.BlockSpec((1, s, LANES), lambda h, bi: (bi, 0, h)),
        out_shape=jax.ShapeDtypeStruct((b, s, dq), BF16),
        scratch_shapes=[pltpu.VMEM((2, s, LANES), F32)] * 2,
        compiler_params=_cparams(("parallel", "parallel")),
        name="dilated_attention",
    )(q, k, v, bias_tiles)


def _mixer_residual(o_parts, w_parts, h, gain, gate, rows=slice(None)):
    y = _dot(o_parts[0][0, rows], w_parts[0][...])
    for o_ref, w_ref in zip(o_parts[1:], w_parts[1:]):
        y = y + _dot(o_ref[0, rows], w_ref[...])
    return h + gate * (_rms(y) * gain)


def _mixer_operands(o_parts, w_out, tm):
    w_out = w_out.astype(BF16)
    cuts = np.cumsum([p.shape[-1] for p in o_parts])[:-1].tolist()
    w_parts = jnp.split(w_out, cuts, axis=0) if cuts else [w_out]
    specs = [pl.BlockSpec((1, tm, p.shape[-1]), lambda bi, si: (bi, si, 0)) for p in o_parts]
    specs += [pl.BlockSpec(w.shape, lambda bi, si: (0, 0)) for w in w_parts]
    return list(o_parts) + list(w_parts), specs


def _ffn_kernel(*refs, n_parts):
    o_parts = refs[:n_parts]
    w_parts = refs[n_parts:2 * n_parts]
    (h_ref, gain1_ref, gate1_ref, g_ref, sc_ref, sh_ref, wg_ref, wu_ref, wd_ref, gain_ref, gate_ref,
     o_ref) = refs[2 * n_parts:]
    h = _mixer_residual(o_parts, w_parts, h_ref[0], gain1_ref[...], gate1_ref[0])
    u = _modulate(h, g_ref[...], sc_ref[0], sh_ref[0]).astype(BF16)
    hid = (_silu(_dot(u, wg_ref[...])) * _dot(u, wu_ref[...])).astype(BF16)
    y = _dot(hid, wd_ref[...])
    o_ref[0] = h + gate_ref[0] * (_rms(y) * gain_ref[...])


def _mixer_tail_ffn(o_parts, w_out, h, gain1, gate1, gain_in, scale, shift, w_gate, w_up, w_down,
                    gain_out, gate):
    b, s, d = h.shape
    tm = TM_FFN
    ff = w_gate.shape[1]
    resident = functools.partial(pl.BlockSpec, pipeline_mode=pl.Buffered(1))
    vec = pl.BlockSpec((1, 1, d), lambda bi, si: (bi, 0, 0))
    row = pl.BlockSpec((1, d), lambda bi, si: (0, 0))
    mix_ops, mix_specs = _mixer_operands(o_parts, w_out, tm)
    return pl.pallas_call(
        functools.partial(_ffn_kernel, n_parts=len(o_parts)),
        grid=(b, s // tm),
        in_specs=mix_specs + [pl.BlockSpec((1, tm, d), lambda bi, si: (bi, si, 0)), row, vec,
                              row, vec, vec,
                              resident((d, ff), lambda bi, si: (0, 0)),
                              resident((d, ff), lambda bi, si: (0, 0)),
                              resident((ff, d), lambda bi, si: (0, 0)),
                              row, vec],
        out_specs=pl.BlockSpec((1, tm, d), lambda bi, si: (bi, si, 0)),
        out_shape=jax.ShapeDtypeStruct((b, s, d), F32),
        compiler_params=_cparams(("parallel", "parallel")),
        name="dense_swiglu",
    )(*mix_ops, h, gain1.reshape(1, d), gate1.reshape(b, 1, d),
      gain_in.reshape(1, d), scale.reshape(b, 1, d), shift.reshape(b, 1, d),
      w_gate.astype(BF16), w_up.astype(BF16), w_down.astype(BF16),
      gain_out.reshape(1, d), gate.reshape(b, 1, d))


def _router_kernel(o_ref, wo_ref, h_ref, gain1_ref, gate1_ref, g_ref, sc_ref, sh_ref, rw_ref,
                   hout_ref, mi_ref, mf_ref, cnt_ref, carry_ref, *, tm):
    @pl.when((pl.program_id(0) == 0) & (pl.program_id(1) == 0))
    def _():
        carry_ref[...] = jnp.zeros_like(carry_ref)

    tr = tm // ROUTER_ROW_CHAINS
    lanef = lax.broadcasted_iota(I32, (tr, LANES), 1).astype(F32)
    r = lax.broadcasted_iota(I32, (tr, tr), 0)
    c = lax.broadcasted_iota(I32, (tr, tr), 1)
    before = jnp.where(c < r, 1.0, 0.0).astype(BF16)
    counts = carry_ref[0:1, :]
    for ch in range(ROUTER_ROW_CHAINS):
        rows = slice(ch * tr, (ch + 1) * tr)
        h = _mixer_residual([o_ref], [wo_ref], h_ref[0, rows], gain1_ref[...], gate1_ref[0], rows)
        hout_ref[0, rows] = h
        u = _modulate(h, g_ref[...], sc_ref[0], sh_ref[0])
        logits = _dot_split(u, rw_ref[...])
        lg = jnp.where(lanef < N_EXPERTS, logits, NEG_INF)
        v1 = jnp.max(lg, axis=1, keepdims=True)
        i1 = jnp.min(jnp.where(lg == v1, lanef, float(LANES)), axis=1, keepdims=True)
        lg2 = jnp.where(lanef == i1, NEG_INF, lg)
        v2 = jnp.max(lg2, axis=1, keepdims=True)
        i2 = jnp.min(jnp.where(lg2 == v2, lanef, float(LANES)), axis=1, keepdims=True)
        e2 = jnp.exp(v2 - v1)
        p1 = 1.0 / (1.0 + e2)
        p2 = e2 / (1.0 + e2)
        oh1 = jnp.where(lanef == i1, 1.0, 0.0)
        oh2 = jnp.where(lanef == i2, 1.0, 0.0)
        oh = oh1 + oh2
        tot = _dot(before, oh.astype(BF16)) + counts
        rank1 = jnp.sum(oh1 * tot, axis=1, keepdims=True)
        rank2 = jnp.sum(oh2 * tot, axis=1, keepdims=True)
        counts = counts + jnp.sum(oh, axis=0, keepdims=True)
        mi = jnp.where(lanef == 0.0, i1, jnp.where(lanef == 1.0, i2,
             jnp.where(lanef == 2.0, rank1, jnp.where(lanef == 3.0, rank2, 0.0))))
        mi_ref[rows] = mi.astype(I32)
        mf_ref[rows] = jnp.where(lanef == 0.0, p1, jnp.where(lanef == 1.0, p2, 0.0))
    carry_ref[...] = jnp.broadcast_to(counts, carry_ref.shape)
    cnt_ref[...] = carry_ref[...]


def _mixer_tail_router(o, w_out, h, gain1, gate1, gain, scale, shift, router_w):
    b, s, d = h.shape
    tm = TM_ROUTE
    n = b * s
    ns = s // tm
    rw = jnp.pad(router_w.astype(F32), ((0, 0), (0, LANES - router_w.shape[1])))
    vec = pl.BlockSpec((1, 1, d), lambda bi, si: (bi, 0, 0))
    row = pl.BlockSpec((1, d), lambda bi, si: (0, 0))
    act = pl.BlockSpec((1, tm, d), lambda bi, si: (bi, si, 0))
    meta = pl.BlockSpec((tm, LANES), lambda bi, si: (bi * ns + si, 0))
    mix_ops, mix_specs = _mixer_operands([o], w_out, tm)
    return pl.pallas_call(
        functools.partial(_router_kernel, tm=tm),
        grid=(b, ns),
        in_specs=mix_specs + [act, row, vec, row, vec, vec, pl.BlockSpec(rw.shape, lambda bi, si: (0, 0))],
        out_specs=[act, meta, meta, pl.BlockSpec((8, LANES), lambda bi, si: (0, 0))],
        out_shape=[jax.ShapeDtypeStruct((b, s, d), F32),
                   jax.ShapeDtypeStruct((n, LANES), I32),
                   jax.ShapeDtypeStruct((n, LANES), F32),
                   jax.ShapeDtypeStruct((8, LANES), F32)],
        scratch_shapes=[pltpu.VMEM((8, LANES), F32)],
        compiler_params=_cparams(("arbitrary", "arbitrary")),
        name="moe_router",
    )(*mix_ops, h, gain1.reshape(1, d), gate1.reshape(b, 1, d),
      gain.reshape(1, d), scale.reshape(b, 1, d), shift.reshape(b, 1, d), rw)


def _scatter_kernel(dest_ref, ztile_ref, h_ref, g_ref, sc_ref, sh_ref, xs_ref, ubuf, zbuf, sems, zsem,
                    *, tm, ns):
    step = pl.program_id(0) * ns + pl.program_id(1)
    nsteps = pl.num_programs(0) * ns
    slot = lax.rem(step, 2)
    tz = zbuf.shape[0]

    @pl.when(step == 0)
    def _():
        zbuf[...] = jnp.zeros_like(zbuf)
        for e in range(2 * N_EXPERTS):
            @pl.when(ztile_ref[e] >= 0)
            def _():
                row0 = pl.multiple_of(ztile_ref[e] * tz, tz)
                pltpu.make_async_copy(zbuf, xs_ref.at[pl.ds(row0, tz), :], zsem).start()
        for e in range(2 * N_EXPERTS):
            @pl.when(ztile_ref[e] >= 0)
            def _():
                pltpu.make_async_copy(zbuf, xs_ref.at[pl.ds(0, tz), :], zsem).wait()

    def wait_slot(sl):
        for _ in range(2):
            pltpu.make_async_copy(ubuf.at[sl], ubuf.at[sl], sems.at[sl]).wait()

    base = step * tm

    def run(sl):
        @pl.when(step >= 2)
        def _():
            wait_slot(sl)

        ubuf[sl] = _modulate(h_ref[0], g_ref[...], sc_ref[0], sh_ref[0])

        def issue(i, carry):
            t = 2 * (base + i)
            src = ubuf.at[sl, pl.ds(i, 1), :]
            pltpu.make_async_copy(src, xs_ref.at[pl.ds(dest_ref[t], 1), :], sems.at[sl]).start(priority=0)
            pltpu.make_async_copy(src, xs_ref.at[pl.ds(dest_ref[t + 1], 1), :], sems.at[sl]).start(priority=1)
            return carry

        lax.fori_loop(0, tm, issue, 0, unroll=DMA_UNROLL)

        @pl.when(step == nsteps - 1)
        def _():
            wait_slot(sl)

            @pl.when(nsteps >= 2)
            def _():
                wait_slot(1 - sl)

    for sl in range(2):
        pl.when(slot == sl)(functools.partial(run, sl))


def _scatter(dest, zero_tile, h, gain, scale, shift, m_pad):
    b, s, d = h.shape
    tm = TM_SCATTER
    ns = s // tm
    vec = pl.BlockSpec((1, 1, d), lambda bi, si, dest, zt: (bi, 0, 0))
    grid_spec = pltpu.PrefetchScalarGridSpec(
        num_scalar_prefetch=2,
        grid=(b, ns),
        in_specs=[pl.BlockSpec((1, tm, d), lambda bi, si, dest, zt: (bi, si, 0)),
                  pl.BlockSpec((1, d), lambda bi, si, dest, zt: (0, 0)),
                  vec, vec],
        out_specs=pl.BlockSpec(memory_space=pl.ANY),
        scratch_shapes=[pltpu.VMEM((2, tm, d), F32), pltpu.VMEM((TM_EXPERT, d), F32),
                        pltpu.SemaphoreType.DMA((2,)), pltpu.SemaphoreType.DMA],
    )
    return pl.pallas_call(
        functools.partial(_scatter_kernel, tm=tm, ns=ns),
        grid_spec=grid_spec,
        out_shape=jax.ShapeDtypeStruct((m_pad, d), F32),
        compiler_params=_cparams(("arbitrary", "arbitrary")),
        name="moe_scatter",
    )(dest, zero_tile, h, gain.reshape(1, d), scale.reshape(b, 1, d), shift.reshape(b, 1, d))


def _expert_kernel(te_ref, tv_ref, tx_ref, x_ref, wg_ref, wu_ref, wd_ref, o_ref):
    t = pl.program_id(0)
    f = pl.program_id(1)

    @pl.when(tv_ref[t] == 1)
    def _():
        x = x_ref[...].astype(BF16)
        hid = (_silu(_dot(x, wg_ref[0])) * _dot(x, wu_ref[0])).astype(BF16)
        y = _dot(hid, wd_ref[0])

        @pl.when(f == 0)
        def _():
            o_ref[...] = y

        @pl.when(f > 0)
        def _():
            o_ref[...] = o_ref[...] + y

    @pl.when((tv_ref[t] == 0) & (f == 0))
    def _():
        o_ref[...] = jnp.zeros_like(o_ref)


def _experts(tile_expert, tile_valid, tile_x, xs, w_gate, w_up, w_down):
    m_pad, d = xs.shape
    tm = TM_EXPERT
    n_tiles = tile_expert.shape[0]
    ff = w_gate.shape[2]
    fs = FF_STEPS_EXPERT
    tf = ff // fs

    def ff_idx(f, tv, t):
        return f * tv[t] + (fs - 1) * (1 - tv[t])

    grid_spec = pltpu.PrefetchScalarGridSpec(
        num_scalar_prefetch=3,
        grid=(n_tiles, fs),
        in_specs=[pl.BlockSpec((tm, d), lambda t, f, te, tv, tx: (tx[t], 0)),
                  pl.BlockSpec((1, d, tf), lambda t, f, te, tv, tx: (te[t], 0, ff_idx(f, tv, t))),
                  pl.BlockSpec((1, d, tf), lambda t, f, te, tv, tx: (te[t], 0, ff_idx(f, tv, t))),
                  pl.BlockSpec((1, tf, d), lambda t, f, te, tv, tx: (te[t], ff_idx(f, tv, t), 0))],
        out_specs=pl.BlockSpec((tm, d), lambda t, f, te, tv, tx: (t, 0)),
    )
    return pl.pallas_call(
        _expert_kernel,
        grid_spec=grid_spec,
        out_shape=jax.ShapeDtypeStruct((m_pad, d), F32),
        compiler_params=_cparams(("arbitrary", "arbitrary")),
        name="moe_experts",
    )(tile_expert, tile_valid, tile_x, xs,
      w_gate.astype(BF16), w_up.astype(BF16), w_down.astype(BF16))


def _combine_kernel(dest_ref, y_ref, mf_ref, h_ref, gain_ref, gate_ref, o_ref, ybuf, sems, *, tm, ns):
    step = pl.program_id(0) * ns + pl.program_id(1)
    nsteps = pl.num_programs(0) * ns
    slot = lax.rem(step, 2)

    def issue(st, sl):
        base = st * tm

        def body(i, carry):
            t = 2 * (base + i)
            pltpu.make_async_copy(y_ref.at[pl.ds(dest_ref[t], 1), :],
                                  ybuf.at[sl, 0, pl.ds(i, 1), :], sems.at[sl]).start(priority=0)
            pltpu.make_async_copy(y_ref.at[pl.ds(dest_ref[t + 1], 1), :],
                                  ybuf.at[sl, 1, pl.ds(i, 1), :], sems.at[sl]).start(priority=1)
            return carry

        lax.fori_loop(0, tm, body, 0, unroll=DMA_UNROLL)

    @pl.when(step == 0)
    def _():
        issue(0, 0)

    def run(sl):
        @pl.when(step + 1 < nsteps)
        def _():
            issue(step + 1, 1 - sl)

        for k in range(2):
            pltpu.make_async_copy(ybuf.at[sl, k], ybuf.at[sl, k], sems.at[sl]).wait()
        mf = mf_ref[...]
        y = mf[:, 0:1] * ybuf[sl, 0] + mf[:, 1:2] * ybuf[sl, 1]
        o_ref[0] = h_ref[0] + gate_ref[0] * (_rms(y) * gain_ref[...])

    for sl in range(2):
        pl.when(slot == sl)(functools.partial(run, sl))


def _combine(dest, ys, mf, h, gain, gate):
    b, s, d = h.shape
    tm = TM_COMBINE
    ns = s // tm
    grid_spec = pltpu.PrefetchScalarGridSpec(
        num_scalar_prefetch=1,
        grid=(b, ns),
        in_specs=[pl.BlockSpec(memory_space=pl.ANY),
                  pl.BlockSpec((tm, LANES), lambda bi, si, dest: (bi * ns + si, 0)),
                  pl.BlockSpec((1, tm, d), lambda bi, si, dest: (bi, si, 0)),
                  pl.BlockSpec((1, d), lambda bi, si, dest: (0, 0)),
                  pl.BlockSpec((1, 1, d), lambda bi, si, dest: (bi, 0, 0))],
        out_specs=pl.BlockSpec((1, tm, d), lambda bi, si, dest: (bi, si, 0)),
        scratch_shapes=[pltpu.VMEM((2, 2, tm, d), F32), pltpu.SemaphoreType.DMA((2,))],
    )
    return pl.pallas_call(
        functools.partial(_combine_kernel, tm=tm, ns=ns),
        grid_spec=grid_spec,
        out_shape=jax.ShapeDtypeStruct((b, s, d), F32),
        compiler_params=_cparams(("arbitrary", "arbitrary")),
        name="moe_combine",
    )(dest, ys, mf, h, gain.reshape(1, d), gate.reshape(b, 1, d))


def _mixer_tail_moe(o, w_out, h, gain1, gate1, gain_in, scale, shift, router_w, w_gate, w_up, w_down,
                    gain_out, gate):
    b, s, d = h.shape
    n = b * s
    tm = TM_EXPERT
    h, mi, mf, cnt = _mixer_tail_router(o, w_out, h, gain1, gate1, gain_in, scale, shift, router_w)
    counts = cnt[0, :N_EXPERTS].astype(I32)
    tiles_per = (counts + tm - 1) // tm
    seg_start = (jnp.cumsum(tiles_per) - tiles_per) * tm
    dest = (seg_start[mi[:, 0:2]] + mi[:, 2:4]).reshape(2 * n)
    n_tiles = (2 * n) // tm + N_EXPERTS
    m_pad = n_tiles * tm
    tile_end = jnp.cumsum(tiles_per)
    tidx = jnp.arange(n_tiles, dtype=I32)
    tile_valid = (tidx < tile_end[-1]).astype(I32)
    tile_expert = jnp.minimum(jnp.searchsorted(tile_end, tidx, side="right"), N_EXPERTS - 1).astype(I32)
    tile_x = jnp.minimum(tidx, tile_end[-1] - 1)
    tail = tile_end[-1] + jnp.arange(N_EXPERTS, dtype=I32)
    zero_tile = jnp.concatenate([jnp.where(tiles_per > 0, tile_end - 1, -1),
                                 jnp.where(tail < n_tiles, tail, -1)]).astype(I32)
    xs = _scatter(dest, zero_tile, h, gain_in, scale, shift, m_pad)
    ys = _experts(tile_expert, tile_valid, tile_x, xs, w_gate, w_up, w_down)
    return _combine(dest, ys, mf, h, gain_out, gate)


def kernel(x, c, mod_w, mod_b, norm_g, attn_in_w_even, fox_gate_bias, attn_out_w_even,
           attn_in_w_odd, attn_out_w_odd, rel_bias_table, ffn_w_gate, ffn_w_up, ffn_w_down,
           router_w, exp_w_gate, exp_w_up, exp_w_down):
    depth = mod_w.shape[0]
    s_len = x.shape[1]
    mods = _mods(c, mod_w, mod_b)
    dil_bias, moba_bias = _bias_tiles(rel_bias_table, s_len)
    h = x
    for layer in range(depth):
        j = layer // 2
        sh1, sc1, g1, sh2, sc2, g2 = jnp.split(mods[layer], 6, axis=-1)
        gains = norm_g[layer]
        if layer % 2 == 0:
            fox_in, moba_in = _inproj_even(h, gains[0], sc1, sh1, attn_in_w_even[j], fox_gate_bias[j])
            o_parts = [_fox_attention(*fox_in), _moba_attention(*moba_in, moba_bias)]
            h = _mixer_tail_ffn(o_parts, attn_out_w_even[j], h, gains[1], g1, gains[2], sc2, sh2,
                                ffn_w_gate[j], ffn_w_up[j], ffn_w_down[j], gains[3], g2)
        else:
            q, k, v = _inproj_odd(h, gains[0], sc1, sh1, attn_in_w_odd[j])
            o = _dilated_attention(q, k, v, dil_bias)
            h = _mixer_tail_moe(o, attn_out_w_odd[j], h, gains[1], g1, gains[2], sc2, sh2, router_w[j],
                                exp_w_gate[j], exp_w_up[j], exp_w_down[j], gains[3], g2)
    return h
```

```python
import functools
import math

import numpy as np
import jax
import jax.numpy as jnp
from jax import lax
from jax.experimental import pallas as pl
from jax.experimental.pallas import tpu as pltpu

F32 = jnp.float32
BF16 = jnp.bfloat16
I32 = jnp.int32

HEAD_DIM = 64
LANES = 128
N_HEADS = 16
N_HEADS_FOX = 8
ATTN_SCALE = HEAD_DIM ** -0.5
LOG2E = math.log2(math.e)
Q_SCALE = ATTN_SCALE * LOG2E
N_DECAY_PIECES = 3
MOBA_BLOCK = 256
MOBA_TOPK = 3
DIL_PATTERNS = ((128, 1), (512, 4), (2048, 16))
NUM_BUCKETS = 32
MAX_DISTANCE = 2048
N_EXPERTS = 8
NORM_EPS = 1e-6
NEG_INF = float("-inf")
MASK_BIG = 1e30

VMEM_LIMIT = 56 * 1024 * 1024

TM_PROJ = 1024
TM_FFN = 1024
TQ_FOX = 2048
TQ_MOBA = 2048
TK_FOX = 512
FOX_PAIRS_PER_STEP = 1
MOBA_PAIRS_PER_STEP = 1
TM_ROUTE = 512
ROUTER_ROW_CHAINS = 2
TM_SCATTER = 512
TM_EXPERT = 512
TM_COMBINE = 512
FF_STEPS_EXPERT = 2
DMA_UNROLL = True
DIL_CHUNKS_PER_STEP = 16


def _cparams(sem):
    return pltpu.CompilerParams(dimension_semantics=sem, vmem_limit_bytes=VMEM_LIMIT)


def _t5_bucket_np(n):
    n = np.maximum(n, 0)
    max_exact = NUM_BUCKETS // 2
    nf = np.maximum(n, 1).astype(np.float64)
    large = max_exact + (np.log(nf / max_exact) / math.log(MAX_DISTANCE / max_exact)
                         * (NUM_BUCKETS - max_exact)).astype(np.int64)
    large = np.minimum(large, NUM_BUCKETS - 1)
    return np.where(n < max_exact, n, large)


_MAX_DIST = 1 << 16
_BUCKET_OF = _t5_bucket_np(np.arange(_MAX_DIST))
_BUCKET_THR = [int(np.searchsorted(_BUCKET_OF, k, side="left")) for k in range(NUM_BUCKETS)]


def _bias_from_dist(tab_ref, h, dist, dlo, dhi):
    lo_b = int(_BUCKET_OF[max(dlo, 0)])
    hi_b = int(_BUCKET_OF[dhi])
    val = jnp.zeros(dist.shape, F32) + tab_ref[lo_b, h]
    for k in range(lo_b + 1, hi_b + 1):
        val = jnp.where(dist >= _BUCKET_THR[k], tab_ref[k, h], val)
    return val


def _dil_bias_kernel(tab_ref, o_ref):
    h = pl.program_id(0)
    for g, (window, dil) in enumerate(DIL_PATTERNS):
        span = window // dil
        i = lax.broadcasted_iota(I32, (span, 2 * span), 0)
        j = lax.broadcasted_iota(I32, (span, 2 * span), 1)
        rel = i + span - j
        val = _bias_from_dist(tab_ref, h, rel * dil, 0, span * dil) * LOG2E
        band = jnp.where(rel >= 0, jnp.where(rel <= span, val, NEG_INF), NEG_INF)
        o_ref[0, 2 * g] = band
        o_ref[0, 2 * g + 1] = jnp.where(j >= span, band, NEG_INF)


def _moba_bias_kernel(tab_ref, o_ref, *, n_blk, head0):
    h = pl.program_id(0) + head0
    i = lax.broadcasted_iota(I32, (MOBA_BLOCK, MOBA_BLOCK), 0)
    j = lax.broadcasted_iota(I32, (MOBA_BLOCK, MOBA_BLOCK), 1)
    for d in range(n_blk):
        dist = d * MOBA_BLOCK + i - j
        val = _bias_from_dist(tab_ref, h, dist, d * MOBA_BLOCK - (MOBA_BLOCK - 1),
                              d * MOBA_BLOCK + (MOBA_BLOCK - 1)) * LOG2E
        if d == 0:
            val = jnp.where(dist >= 0, val, NEG_INF)
        o_ref[0, d] = val


def _bias_tiles(rel_bias_table, s_len):
    n_blk = s_len // MOBA_BLOCK
    span = DIL_PATTERNS[0][0]
    n_var = 2 * len(DIL_PATTERNS)
    smem = pl.BlockSpec(memory_space=pltpu.SMEM)
    dil = pl.pallas_call(
        _dil_bias_kernel,
        grid=(N_HEADS,),
        in_specs=[smem],
        out_specs=pl.BlockSpec((1, n_var, span, 2 * span), lambda h: (h, 0, 0, 0)),
        out_shape=jax.ShapeDtypeStruct((N_HEADS, n_var, span, 2 * span), F32),
        compiler_params=_cparams(("parallel",)),
        name="dil_bias",
    )(rel_bias_table)
    n_moba = N_HEADS - N_HEADS_FOX
    moba = pl.pallas_call(
        functools.partial(_moba_bias_kernel, n_blk=n_blk, head0=N_HEADS_FOX),
        grid=(n_moba,),
        in_specs=[smem],
        out_specs=pl.BlockSpec((1, n_blk, MOBA_BLOCK, MOBA_BLOCK), lambda h: (h, 0, 0, 0)),
        out_shape=jax.ShapeDtypeStruct((n_moba, n_blk, MOBA_BLOCK, MOBA_BLOCK), F32),
        compiler_params=_cparams(("parallel",)),
        name="moba_bias",
    )(rel_bias_table)
    return dil, moba


def _split_bf16(a):
    hi = a.astype(BF16)
    lo = (a - hi.astype(F32)).astype(BF16)
    return hi, lo


def _dot(a, b):
    return jnp.dot(a, b, preferred_element_type=F32)


def _dot_nt(a, b):
    return lax.dot_general(a, b, (((1,), (1,)), ((), ())), preferred_element_type=F32)


def _dot_split(a, b):
    a_hi, a_lo = _split_bf16(a)
    b_hi, b_lo = _split_bf16(b)
    return _dot(a_hi, b_hi) + (_dot(a_hi, b_lo) + _dot(a_lo, b_hi))


def _rms(x):
    return x * lax.rsqrt(jnp.mean(x * x, axis=-1, keepdims=True) + NORM_EPS)


def _modulate(x, gain, scale, shift):
    return (_rms(x) * gain) * (1.0 + scale) + shift


def _silu(x):
    return x * jax.nn.sigmoid(x)


def _mods_kernel(c_ref, w_ref, b_ref, o_ref):
    o_ref[0] = _dot_split(_silu(c_ref[...]), w_ref[0]) + b_ref[0]


def _mods(c, mod_w, mod_b):
    depth, d, e = mod_w.shape
    b = c.shape[0]
    tn = 1536
    return pl.pallas_call(
        _mods_kernel,
        grid=(depth, e // tn),
        in_specs=[pl.BlockSpec((b, d), lambda l, j: (0, 0)),
                  pl.BlockSpec((1, d, tn), lambda l, j: (l, 0, j)),
                  pl.BlockSpec((1, 1, tn), lambda l, j: (l, 0, j))],
        out_specs=pl.BlockSpec((1, b, tn), lambda l, j: (l, 0, j)),
        out_shape=jax.ShapeDtypeStruct((depth, b, e), F32),
        compiler_params=_cparams(("parallel", "parallel")),
        name="adaln_mods",
    )(c, mod_w, mod_b.reshape(depth, 1, e))


def _inproj_even_kernel(h_ref, g_ref, sc_ref, sh_ref, w_ref, wf_ref, gb_ref,
                        qa_ref, k0a_ref, k1a_ref, v0a_ref, v1a_ref,
                        qb_ref, k0b_ref, k1b_ref, v0b_ref, v1b_ref, carry_ref, *, tm):
    si = pl.program_id(1)
    u = _modulate(h_ref[0], g_ref[...], sc_ref[0], sh_ref[0]).astype(BF16)
    width = qa_ref.shape[-1]
    n_pairs = width // LANES

    def proj(i):
        return _dot(u, w_ref[:, i * width:(i + 1) * width])

    lane = lax.broadcasted_iota(I32, (1, LANES), 1)
    left = lane < HEAD_DIM
    row = lax.broadcasted_iota(I32, (tm, LANES), 0)

    x = _dot(u, wf_ref[...]) + gb_ref[...]
    lf = jnp.where(lane < N_HEADS_FOX, jnp.minimum(x, 0.0) - jnp.log1p(jnp.exp(-jnp.abs(x))), 0.0)
    k = 1
    while k < tm:
        lf = lf + jnp.where(row >= k, pltpu.roll(lf, k, axis=0), 0.0)
        k *= 2

    @pl.when(si == 0)
    def _():
        carry_ref[...] = jnp.zeros_like(carry_ref)

    cum = lf + carry_ref[0:1, :]
    carry_ref[...] = jnp.broadcast_to(cum[tm - 1:tm, :], carry_ref.shape)
    rest = cum * (-LOG2E)
    decay = jnp.zeros((tm, LANES), F32)
    for p in range(N_DECAY_PIECES):
        piece = rest.astype(BF16).astype(F32)
        rest = rest - piece
        decay = decay + (pltpu.roll(piece, p * N_HEADS_FOX, axis=1) if p else piece)
    decay_lo = decay.astype(BF16)
    decay_hi = pltpu.roll(decay, HEAD_DIM, axis=1).astype(BF16)

    ones = jnp.ones((1, LANES), BF16)
    blk = (si * tm + row) // MOBA_BLOCK
    blk_lo = jnp.where(lane == blk, 1.0, 0.0).astype(BF16)
    blk_hi = jnp.where(lane == blk + HEAD_DIM, 1.0, 0.0).astype(BF16)

    def emit(first, k0_ref, k1_ref, v0_ref, v1_ref, k_lo, k_hi):
        kk = proj(first + 1).astype(BF16)
        vv = proj(first + 2).astype(BF16)
        for hp in range(n_pairs):
            sl = slice(hp * LANES, (hp + 1) * LANES)
            k0_ref[0, :, sl] = jnp.where(left, kk[:, sl], k_hi)
            k1_ref[0, :, sl] = jnp.where(left, k_lo, kk[:, sl])
            v0_ref[0, :, sl] = jnp.where(left, vv[:, sl], ones)
            v1_ref[0, :, sl] = jnp.where(left, ones, vv[:, sl])

    qa_ref[0] = (proj(0) * Q_SCALE).astype(BF16)
    emit(0, k0a_ref, k1a_ref, v0a_ref, v1a_ref, decay_lo, decay_hi)
    qb_ref[0] = (proj(3) * Q_SCALE).astype(BF16)
    emit(3, k0b_ref, k1b_ref, v0b_ref, v1b_ref, blk_lo, blk_hi)


def _inproj_even(h, gain, scale, shift, w_in, gate_bias):
    b, s, d = h.shape
    tm = TM_PROJ
    da = N_HEADS_FOX * HEAD_DIM
    assert s // MOBA_BLOCK <= HEAD_DIM and N_DECAY_PIECES * N_HEADS_FOX <= HEAD_DIM
    cuts = np.cumsum([da, da, da, N_HEADS_FOX, da, da]).tolist()
    q_a, k_a, v_a, f_a, q_b, k_b, v_b = jnp.split(w_in, cuts, axis=1)
    w = jnp.concatenate([q_a, k_a, v_a, q_b, k_b, v_b], axis=1).astype(BF16)
    wf = jnp.pad(f_a, ((0, 0), (0, LANES - N_HEADS_FOX))).astype(BF16)
    gb = jnp.pad(gate_bias.astype(F32), (0, LANES - N_HEADS_FOX)).reshape(1, LANES)
    act = jax.ShapeDtypeStruct((b, s, da), BF16)
    act_spec = pl.BlockSpec((1, tm, da), lambda bi, si: (bi, si, 0))
    vec = pl.BlockSpec((1, 1, d), lambda bi, si: (bi, 0, 0))
    outs = pl.pallas_call(
        functools.partial(_inproj_even_kernel, tm=tm),
        grid=(b, s // tm),
        in_specs=[pl.BlockSpec((1, tm, d), lambda bi, si: (bi, si, 0)),
                  pl.BlockSpec((1, d), lambda bi, si: (0, 0)),
                  vec, vec,
                  pl.BlockSpec(w.shape, lambda bi, si: (0, 0)),
                  pl.BlockSpec(wf.shape, lambda bi, si: (0, 0)),
                  pl.BlockSpec(gb.shape, lambda bi, si: (0, 0))],
        out_specs=[act_spec] * 10,
        out_shape=[act] * 10,
        scratch_shapes=[pltpu.VMEM((8, LANES), F32)],
        compiler_params=_cparams(("parallel", "arbitrary")),
        name="inproj_even",
    )(h, gain.reshape(1, d), scale.reshape(b, 1, d), shift.reshape(b, 1, d), w, wf, gb)
    return outs[:5], outs[5:]


def _inproj_odd_kernel(h_ref, g_ref, sc_ref, sh_ref, w_ref, q_ref, k_ref, v_ref):
    u = _modulate(h_ref[0], g_ref[...], sc_ref[0], sh_ref[0]).astype(BF16)
    width = q_ref.shape[-1]
    q_ref[0] = _dot(u, w_ref[:, 0:width]) * Q_SCALE
    k_ref[0] = _dot(u, w_ref[:, width:2 * width])
    v_ref[0] = _dot(u, w_ref[:, 2 * width:3 * width])


def _inproj_odd(h, gain, scale, shift, w_in):
    b, s, d = h.shape
    tm = TM_PROJ
    dq = w_in.shape[1] // 3
    act = jax.ShapeDtypeStruct((b, s, dq), F32)
    act_spec = pl.BlockSpec((1, tm, dq), lambda bi, si: (bi, si, 0))
    vec = pl.BlockSpec((1, 1, d), lambda bi, si: (bi, 0, 0))
    return pl.pallas_call(
        _inproj_odd_kernel,
        grid=(b, s // tm),
        in_specs=[pl.BlockSpec((1, tm, d), lambda bi, si: (bi, si, 0)),
                  pl.BlockSpec((1, d), lambda bi, si: (0, 0)),
                  vec, vec,
                  pl.BlockSpec(w_in.shape, lambda bi, si: (0, 0))],
        out_specs=[act_spec] * 3,
        out_shape=[act] * 3,
        compiler_params=_cparams(("parallel", "parallel")),
        name="inproj_odd",
    )(h, gain.reshape(1, d), scale.reshape(b, 1, d), shift.reshape(b, 1, d), w_in.astype(BF16))


def _tile_lanes(x, width):
    return jnp.concatenate([x] * (width // LANES), axis=1)


def _flash_update(s, v, m_ref, acc_ref):
    m_prev = m_ref[...]
    m_new = jnp.maximum(m_prev, jnp.max(s, axis=1, keepdims=True))
    p = jnp.exp2(s - _tile_lanes(m_new, s.shape[1]))
    acc_ref[...] = jnp.exp2(m_prev - m_new) * acc_ref[...] + _dot(p.astype(BF16), v)
    m_ref[...] = m_new


def _finish_pair(acc_ref, left, first=0):
    acc0 = acc_ref[first]
    acc1 = acc_ref[first + 1]
    den = pltpu.roll(jnp.where(left, acc1, acc0), HEAD_DIM, axis=1)
    return jnp.where(left, acc0, acc1) / den


def _fox_kernel(q_ref, k0_ref, k1_ref, v0_ref, v1_ref, o_ref, m_ref, acc_ref, *, tq, tk, n_pairs):
    qi = pl.program_id(2)
    lane = lax.broadcasted_iota(I32, (1, LANES), 1)
    left = lane < HEAD_DIM

    def piece_lanes(lane0):
        hit = lane == lane0
        for p in range(1, N_DECAY_PIECES):
            hit = jnp.logical_or(hit, lane == lane0 + p * N_HEADS_FOX)
        return jnp.where(hit, 1.0, 0.0).astype(BF16)

    chains = []
    for pr in range(n_pairs):
        hp = pl.program_id(1) * n_pairs + pr
        lanes = slice(pr * LANES, (pr + 1) * LANES)
        q = q_ref[0, :, lanes]
        chains.append((jnp.where(left, q, piece_lanes(HEAD_DIM + 2 * hp)), k0_ref, v0_ref, lanes))
        chains.append((jnp.where(left, piece_lanes(2 * hp + 1), q), k1_ref, v1_ref, lanes))
    row = lax.broadcasted_iota(I32, (tq, tk), 0)
    col = lax.broadcasted_iota(I32, (tq, tk), 1)
    m_ref[...] = jnp.full(m_ref.shape, NEG_INF, F32)
    acc_ref[...] = jnp.zeros(acc_ref.shape, F32)
    n_sub = tq // tk

    def step(kv, mask, row0=0):
        off = pl.multiple_of(kv * tk, tk)
        rows = pl.ds(row0, tq - row0)
        for c, (qc, k_ref, v_ref, lanes) in enumerate(chains):
            s = _dot_nt(qc[row0:], k_ref[0, pl.ds(off, tk), lanes])
            if mask is not None:
                s = jnp.where(mask[row0:], s, NEG_INF)
            _flash_update(s, v_ref[0, pl.ds(off, tk), lanes], m_ref.at[c, rows], acc_ref.at[c, rows])

    def body(kv, carry):
        step(kv, None)
        return carry

    lax.fori_loop(0, qi * n_sub, body, 0)
    for d in range(n_sub):
        step(qi * n_sub + d, col + d * tk <= row, d * tk)
    for pr in range(n_pairs):
        o_ref[0, :, pr * LANES:(pr + 1) * LANES] = _finish_pair(acc_ref, left, 2 * pr).astype(o_ref.dtype)


def _fox_attention(q, k0, k1, v0, v1):
    b, s, da = q.shape
    n_pairs = FOX_PAIRS_PER_STEP
    width = n_pairs * LANES
    tq = min(TQ_FOX, s)
    kv = pl.BlockSpec((1, s, width), lambda bi, h, qi: (bi, 0, h))
    return pl.pallas_call(
        functools.partial(_fox_kernel, tq=tq, tk=min(TK_FOX, tq), n_pairs=n_pairs),
        grid=(b, da // width, s // tq),
        in_specs=[pl.BlockSpec((1, tq, width), lambda bi, h, qi: (bi, qi, h)), kv, kv, kv, kv],
        out_specs=pl.BlockSpec((1, tq, width), lambda bi, h, qi: (bi, qi, h)),
        out_shape=jax.ShapeDtypeStruct((b, s, da), BF16),
        scratch_shapes=[pltpu.VMEM((2 * n_pairs, tq, LANES), F32)] * 2,
        compiler_params=_cparams(("parallel", "parallel", "arbitrary")),
        name="fox_attention",
    )(q, k0, k1, v0, v1)


def _moba_kernel(q_ref, k0_ref, k1_ref, v0_ref, v1_ref, bias_ref, o_ref, km_ref, m_ref, acc_ref,
                 *, n_blk, tq, n_pairs):
    blk = MOBA_BLOCK
    tk = 2 * blk
    nq = tq // blk
    a = pl.program_id(2)
    gate_lane0 = (HEAD_DIM, 0)
    lane = lax.broadcasted_iota(I32, (1, LANES), 1)
    left = lane < HEAD_DIM
    mine = (left, jnp.logical_not(left))
    k_refs = (k0_ref, k1_ref)
    v_refs = (v0_ref, v1_ref)

    @pl.when(a == 0)
    def _():
        km_ref[...] = jnp.zeros_like(km_ref)
        for pr in range(n_pairs):
            lanes = slice(pr * LANES, (pr + 1) * LANES)
            for n in range(n_blk):
                rows = slice(n * blk, (n + 1) * blk)
                kb = jnp.where(left, k0_ref[0, rows, lanes], k1_ref[0, rows, lanes]).astype(F32)
                mean = jnp.sum(kb, axis=0, keepdims=True) * (1.0 / blk)
                for lane0 in gate_lane0:
                    km_ref[pr, lane0 + n:lane0 + n + 1, :] = mean

    nb = -(-n_blk // 8) * 8
    blkf = lax.broadcasted_iota(I32, (nb, tq), 0).astype(F32)
    own = (lax.broadcasted_iota(I32, (nb, tq), 1) // blk + a * nq).astype(F32)
    chains = []
    for pr in range(n_pairs):
        lanes = slice(pr * LANES, (pr + 1) * LANES)
        q = q_ref[0, :, lanes]
        km_hi, km_lo = _split_bf16(km_ref[pr])
        for j in range(2):
            qj = jnp.where(mine[j], q, jnp.zeros_like(q))
            lane0 = gate_lane0[j]
            gate = (_dot_nt(km_hi, qj) + _dot_nt(km_lo, qj))[lane0:lane0 + nb]
            gate = jnp.where(blkf < own, gate, NEG_INF)
            pen = jnp.where(blkf == own, 0.0, -MASK_BIG)
            for _ in range(MOBA_TOPK):
                mx = jnp.max(gate, axis=0, keepdims=True)
                cand = jnp.where(gate == mx, jnp.where(mx > NEG_INF, blkf, float(LANES)), float(LANES))
                pick = blkf == jnp.min(cand, axis=0, keepdims=True)
                pen = jnp.where(pick, 0.0, pen)
                gate = jnp.where(pick, NEG_INF, gate)
            parts = [pen, jnp.zeros((LANES - lane0 - nb, tq), F32)]
            if lane0:
                parts.insert(0, jnp.zeros((lane0, tq), F32))
            pen_q = jnp.concatenate(parts, axis=0).T
            chains.append((jnp.where(mine[j], q, pen_q.astype(BF16)), k_refs[j], v_refs[j], lanes, 2 * pr + j))

    m_ref[...] = jnp.full(m_ref.shape, NEG_INF, F32)
    acc_ref[...] = jnp.zeros(acc_ref.shape, F32)

    def step(i, dist, r0):
        off = pl.multiple_of(i * tk, tk)
        rows = pl.ds(r0 * blk, tq - r0 * blk)
        for qc, k_ref, v_ref, lanes, h in chains:
            bias = jnp.concatenate(
                [jnp.concatenate([bias_ref[h, dist(r, c)] for c in range(2)], axis=1)
                 for r in range(r0, nq)], axis=0)
            s = _dot_nt(qc[r0 * blk:], k_ref[0, pl.ds(off, tk), lanes]) + bias
            _flash_update(s, v_ref[0, pl.ds(off, tk), lanes], m_ref.at[h, rows], acc_ref.at[h, rows])

    n_full = a * (nq // 2)

    def body(i, carry):
        step(i, lambda r, c: a * nq + r - 2 * i - c, 0)
        return carry

    lax.fori_loop(0, n_full, body, 0)
    for e in range(nq // 2):
        step(n_full + e, lambda r, c, e=e: max(r - 2 * e - c, 0), 2 * e)
    for pr in range(n_pairs):
        o_ref[0, :, pr * LANES:(pr + 1) * LANES] = _finish_pair(acc_ref, left, 2 * pr).astype(o_ref.dtype)


def _moba_attention(q, k0, k1, v0, v1, bias_tiles):
    b, s, db = q.shape
    n_pairs = MOBA_PAIRS_PER_STEP
    width = n_pairs * LANES
    blk = MOBA_BLOCK
    n_blk = s // blk
    tq = min(TQ_MOBA, s)
    assert n_blk <= HEAD_DIM, "block gates of one head must fit in the other head's lanes"
    assert s % tq == 0 and tq % (2 * blk) == 0
    kv = pl.BlockSpec((1, s, width), lambda h, bi, qi: (bi, 0, h))
    return pl.pallas_call(
        functools.partial(_moba_kernel, n_blk=n_blk, tq=tq, n_pairs=n_pairs),
        grid=(db // width, b, s // tq),
        in_specs=[pl.BlockSpec((1, tq, width), lambda h, bi, qi: (bi, qi, h)), kv, kv, kv, kv,
                  pl.BlockSpec((2 * n_pairs, n_blk, blk, blk), lambda h, bi, qi: (h, 0, 0, 0),
                               pipeline_mode=pl.Buffered(1))],
        out_specs=pl.BlockSpec((1, tq, width), lambda h, bi, qi: (bi, qi, h)),
        out_shape=jax.ShapeDtypeStruct((b, s, db), BF16),
        scratch_shapes=[pltpu.VMEM((n_pairs, LANES, LANES), F32)]
                       + [pltpu.VMEM((2 * n_pairs, tq, LANES), F32)] * 2,
        compiler_params=_cparams(("parallel", "parallel", "arbitrary")),
        name="moba_attention",
    )(q, k0, k1, v0, v1, bias_tiles)


def _dilated_kernel(q_ref, k_ref, v_ref, bias_ref, o_ref, m_ref, acc_ref, *, s_len):
    lane = lax.broadcasted_iota(I32, (1, LANES), 1)
    left = lane < HEAD_DIM
    ones = jnp.ones((1, LANES), BF16)
    order = sorted(range(len(DIL_PATTERNS)), key=lambda i: -DIL_PATTERNS[i][1])
    for g in order:
        window, dil = DIL_PATTERNS[g]
        merge = g != order[0]
        span = window // dil
        unit = span * dil
        nc = s_len // unit
        n_u = min(DIL_CHUNKS_PER_STEP, nc)
        groups = nc // n_u
        n_res = min(DIL_CHUNKS_PER_STEP // n_u, dil)

        def rows(ref, start, dil=dil, span=span):
            if dil == 1:
                return ref[0, pl.ds(start, span), :]
            return ref[0, pl.ds(start, span, stride=dil), :]

        def get(ref, j, start, dil=dil, span=span):
            if dil == 1:
                return ref[j, pl.ds(start, span), :]
            return ref[j, pl.ds(start, span, stride=dil), :]

        def put(ref, j, start, val, dil=dil, span=span):
            if dil == 1:
                ref[j, pl.ds(start, span), :] = val
            else:
                ref[j, pl.ds(start, span, stride=dil), :] = val

        def body(it, carry, g=g, merge=merge, n_u=n_u, n_res=n_res, groups=groups, unit=unit, rows=rows,
                 put=put, get=get):
            r0 = (it // groups) * n_res
            grp = it - (it // groups) * groups
            is_first = grp == 0
            results = []
            for dr in range(n_res):
                start0 = r0 + dr + grp * (n_u * unit)
                prev0 = start0 - jnp.where(is_first, 0, unit)
                starts = [start0 + u * unit for u in range(n_u)]
                kc = [rows(k_ref, st).astype(BF16) for st in [prev0] + starts]
                vc = [rows(v_ref, st).astype(BF16) for st in [prev0] + starts]
                vcs = ([jnp.where(left, v, ones) for v in vc], [jnp.where(left, ones, v) for v in vc])
                for u, start in enumerate(starts):
                    var = 2 * g + jnp.where(is_first, 1, 0) if u == 0 else 2 * g
                    q = rows(q_ref, start)
                    kb = jnp.concatenate([kc[u], kc[u + 1]], axis=0)
                    qq = jnp.concatenate([jnp.where(left, q, 0.0), jnp.where(left, 0.0, q)],
                                         axis=0).astype(BF16)
                    s_both = _dot_nt(qq, kb)
                    for j in range(2):
                        s = s_both[j * span:(j + 1) * span] + bias_ref[j, var]
                        m_new = jnp.broadcast_to(jnp.max(s, axis=1, keepdims=True), (span, LANES))
                        if merge:
                            m_prev = get(m_ref, j, start)
                            m_new = jnp.maximum(m_new, m_prev)
                        p = jnp.exp2(s - _tile_lanes(m_new, 2 * span))
                        acc_new = _dot(p.astype(BF16), jnp.concatenate([vcs[j][u], vcs[j][u + 1]], axis=0))
                        if merge:
                            acc_new = jnp.exp2(m_prev - m_new) * get(acc_ref, j, start) + acc_new
                        results.append((j, start, m_new, acc_new))
            for j, start, m_new, acc_new in results:
                put(m_ref, j, start, m_new)
                put(acc_ref, j, start, acc_new)
            return carry

        lax.fori_loop(0, (dil // n_res) * groups, body, 0)
    o_ref[0] = _finish_pair(acc_ref, left).astype(o_ref.dtype)


def _dilated_attention(q, k, v, bias_tiles):
    b, s, dq = q.shape
    hp = dq // LANES
    for window, dil in DIL_PATTERNS:
        assert s % window == 0, "sequence must be a whole number of dilated units"
    qkv = pl.BlockSpec((1, s, LANES), lambda h, bi: (bi, 0, h))
    n_var, span, band = bias_tiles.shape[1:]
    return pl.pallas_call(
        functools.partial(_dilated_kernel, s_len=s),
        grid=(hp, b),
        in_specs=[qkv, qkv, qkv,
                  pl.BlockSpec((2, n_var, span, band), lambda h, bi: (h, 0, 0, 0))],
        out_specs=pl.BlockSpec((1, s, LANES), lambda h, bi: (bi, 0, h)),
        out_shape=jax.ShapeDtypeStruct((b, s, dq), BF16),
        scratch_shapes=[pltpu.VMEM((2, s, LANES), F32)] * 2,
        compiler_params=_cparams(("parallel", "parallel")),
        name="dilated_attention",
    )(q, k, v, bias_tiles)


def _mixer_residual(o_parts, w_parts, h, gain, gate, rows=slice(None)):
    y = _dot(o_parts[0][0, rows], w_parts[0][...])
    for o_ref, w_ref in zip(o_parts[1:], w_parts[1:]):
        y = y + _dot(o_ref[0, rows], w_ref[...])
    return h + gate * (_rms(y) * gain)


def _mixer_operands(o_parts, w_out, tm):
    w_out = w_out.astype(BF16)
    cuts = np.cumsum([p.shape[-1] for p in o_parts])[:-1].tolist()
    w_parts = jnp.split(w_out, cuts, axis=0) if cuts else [w_out]
    specs = [pl.BlockSpec((1, tm, p.shape[-1]), lambda bi, si: (bi, si, 0)) for p in o_parts]
    specs += [pl.BlockSpec(w.shape, lambda bi, si: (0, 0)) for w in w_parts]
    return list(o_parts) + list(w_parts), specs


def _ffn_kernel(*refs, n_parts):
    o_parts = refs[:n_parts]
    w_parts = refs[n_parts:2 * n_parts]
    (h_ref, gain1_ref, gate1_ref, g_ref, sc_ref, sh_ref, wg_ref, wu_ref, wd_ref, gain_ref, gate_ref,
     o_ref) = refs[2 * n_parts:]
    h = _mixer_residual(o_parts, w_parts, h_ref[0], gain1_ref[...], gate1_ref[0])
    u = _modulate(h, g_ref[...], sc_ref[0], sh_ref[0]).astype(BF16)
    hid = (_silu(_dot(u, wg_ref[...])) * _dot(u, wu_ref[...])).astype(BF16)
    y = _dot(hid, wd_ref[...])
    o_ref[0] = h + gate_ref[0] * (_rms(y) * gain_ref[...])


def _mixer_tail_ffn(o_parts, w_out, h, gain1, gate1, gain_in, scale, shift, w_gate, w_up, w_down,
                    gain_out, gate):
    b, s, d = h.shape
    tm = TM_FFN
    ff = w_gate.shape[1]
    resident = functools.partial(pl.BlockSpec, pipeline_mode=pl.Buffered(1))
    vec = pl.BlockSpec((1, 1, d), lambda bi, si: (bi, 0, 0))
    row = pl.BlockSpec((1, d), lambda bi, si: (0, 0))
    mix_ops, mix_specs = _mixer_operands(o_parts, w_out, tm)
    return pl.pallas_call(
        functools.partial(_ffn_kernel, n_parts=len(o_parts)),
        grid=(b, s // tm),
        in_specs=mix_specs + [pl.BlockSpec((1, tm, d), lambda bi, si: (bi, si, 0)), row, vec,
                              row, vec, vec,
                              resident((d, ff), lambda bi, si: (0, 0)),
                              resident((d, ff), lambda bi, si: (0, 0)),
                              resident((ff, d), lambda bi, si: (0, 0)),
                              row, vec],
        out_specs=pl.BlockSpec((1, tm, d), lambda bi, si: (bi, si, 0)),
        out_shape=jax.ShapeDtypeStruct((b, s, d), F32),
        compiler_params=_cparams(("parallel", "parallel")),
        name="dense_swiglu",
    )(*mix_ops, h, gain1.reshape(1, d), gate1.reshape(b, 1, d),
      gain_in.reshape(1, d), scale.reshape(b, 1, d), shift.reshape(b, 1, d),
      w_gate.astype(BF16), w_up.astype(BF16), w_down.astype(BF16),
      gain_out.reshape(1, d), gate.reshape(b, 1, d))


def _router_kernel(o_ref, wo_ref, h_ref, gain1_ref, gate1_ref, g_ref, sc_ref, sh_ref, rw_ref,
                   hout_ref, mi_ref, mf_ref, cnt_ref, carry_ref, *, tm):
    @pl.when((pl.program_id(0) == 0) & (pl.program_id(1) == 0))
    def _():
        carry_ref[...] = jnp.zeros_like(carry_ref)

    tr = tm // ROUTER_ROW_CHAINS
    lanef = lax.broadcasted_iota(I32, (tr, LANES), 1).astype(F32)
    r = lax.broadcasted_iota(I32, (tr, tr), 0)
    c = lax.broadcasted_iota(I32, (tr, tr), 1)
    before = jnp.where(c < r, 1.0, 0.0).astype(BF16)
    counts = carry_ref[0:1, :]
    for ch in range(ROUTER_ROW_CHAINS):
        rows = slice(ch * tr, (ch + 1) * tr)
        h = _mixer_residual([o_ref], [wo_ref], h_ref[0, rows], gain1_ref[...], gate1_ref[0], rows)
        hout_ref[0, rows] = h
        u = _modulate(h, g_ref[...], sc_ref[0], sh_ref[0])
        logits = _dot_split(u, rw_ref[...])
        lg = jnp.where(lanef < N_EXPERTS, logits, NEG_INF)
        v1 = jnp.max(lg, axis=1, keepdims=True)
        i1 = jnp.min(jnp.where(lg == v1, lanef, float(LANES)), axis=1, keepdims=True)
        lg2 = jnp.where(lanef == i1, NEG_INF, lg)
        v2 = jnp.max(lg2, axis=1, keepdims=True)
        i2 = jnp.min(jnp.where(lg2 == v2, lanef, float(LANES)), axis=1, keepdims=True)
        e2 = jnp.exp(v2 - v1)
        p1 = 1.0 / (1.0 + e2)
        p2 = e2 / (1.0 + e2)
        oh1 = jnp.where(lanef == i1, 1.0, 0.0)
        oh2 = jnp.where(lanef == i2, 1.0, 0.0)
        oh = oh1 + oh2
        tot = _dot(before, oh.astype(BF16)) + counts
        rank1 = jnp.sum(oh1 * tot, axis=1, keepdims=True)
        rank2 = jnp.sum(oh2 * tot, axis=1, keepdims=True)
        counts = counts + jnp.sum(oh, axis=0, keepdims=True)
        mi = jnp.where(lanef == 0.0, i1, jnp.where(lanef == 1.0, i2,
             jnp.where(lanef == 2.0, rank1, jnp.where(lanef == 3.0, rank2, 0.0))))
        mi_ref[rows] = mi.astype(I32)
        mf_ref[rows] = jnp.where(lanef == 0.0, p1, jnp.where(lanef == 1.0, p2, 0.0))
    carry_ref[...] = jnp.broadcast_to(counts, carry_ref.shape)
    cnt_ref[...] = carry_ref[...]


def _mixer_tail_router(o, w_out, h, gain1, gate1, gain, scale, shift, router_w):
    b, s, d = h.shape
    tm = TM_ROUTE
    n = b * s
    ns = s // tm
    rw = jnp.pad(router_w.astype(F32), ((0, 0), (0, LANES - router_w.shape[1])))
    vec = pl.BlockSpec((1, 1, d), lambda bi, si: (bi, 0, 0))
    row = pl.BlockSpec((1, d), lambda bi, si: (0, 0))
    act = pl.BlockSpec((1, tm, d), lambda bi, si: (bi, si, 0))
    meta = pl.BlockSpec((tm, LANES), lambda bi, si: (bi * ns + si, 0))
    mix_ops, mix_specs = _mixer_operands([o], w_out, tm)
    return pl.pallas_call(
        functools.partial(_router_kernel, tm=tm),
        grid=(b, ns),
        in_specs=mix_specs + [act, row, vec, row, vec, vec, pl.BlockSpec(rw.shape, lambda bi, si: (0, 0))],
        out_specs=[act, meta, meta, pl.BlockSpec((8, LANES), lambda bi, si: (0, 0))],
        out_shape=[jax.ShapeDtypeStruct((b, s, d), F32),
                   jax.ShapeDtypeStruct((n, LANES), I32),
                   jax.ShapeDtypeStruct((n, LANES), F32),
                   jax.ShapeDtypeStruct((8, LANES), F32)],
        scratch_shapes=[pltpu.VMEM((8, LANES), F32)],
        compiler_params=_cparams(("arbitrary", "arbitrary")),
        name="moe_router",
    )(*mix_ops, h, gain1.reshape(1, d), gate1.reshape(b, 1, d),
      gain.reshape(1, d), scale.reshape(b, 1, d), shift.reshape(b, 1, d), rw)


def _scatter_kernel(dest_ref, ztile_ref, h_ref, g_ref, sc_ref, sh_ref, xs_ref, ubuf, zbuf, sems, zsem,
                    *, tm, ns):
    step = pl.program_id(0) * ns + pl.program_id(1)
    nsteps = pl.num_programs(0) * ns
    slot = lax.rem(step, 2)
    tz = zbuf.shape[0]

    @pl.when(step == 0)
    def _():
        zbuf[...] = jnp.zeros_like(zbuf)
        for e in range(2 * N_EXPERTS):
            @pl.when(ztile_ref[e] >= 0)
            def _():
                row0 = pl.multiple_of(ztile_ref[e] * tz, tz)
                pltpu.make_async_copy(zbuf, xs_ref.at[pl.ds(row0, tz), :], zsem).start()
        for e in range(2 * N_EXPERTS):
            @pl.when(ztile_ref[e] >= 0)
            def _():
                pltpu.make_async_copy(zbuf, xs_ref.at[pl.ds(0, tz), :], zsem).wait()

    def wait_slot(sl):
        for _ in range(2):
            pltpu.make_async_copy(ubuf.at[sl], ubuf.at[sl], sems.at[sl]).wait()

    base = step * tm

    def run(sl):
        @pl.when(step >= 2)
        def _():
            wait_slot(sl)

        ubuf[sl] = _modulate(h_ref[0], g_ref[...], sc_ref[0], sh_ref[0])

        def issue(i, carry):
            t = 2 * (base + i)
            src = ubuf.at[sl, pl.ds(i, 1), :]
            pltpu.make_async_copy(src, xs_ref.at[pl.ds(dest_ref[t], 1), :], sems.at[sl]).start(priority=0)
            pltpu.make_async_copy(src, xs_ref.at[pl.ds(dest_ref[t + 1], 1), :], sems.at[sl]).start(priority=1)
            return carry

        lax.fori_loop(0, tm, issue, 0, unroll=DMA_UNROLL)

        @pl.when(step == nsteps - 1)
        def _():
            wait_slot(sl)

            @pl.when(nsteps >= 2)
            def _():
                wait_slot(1 - sl)

    for sl in range(2):
        pl.when(slot == sl)(functools.partial(run, sl))


def _scatter(dest, zero_tile, h, gain, scale, shift, m_pad):
    b, s, d = h.shape
    tm = TM_SCATTER
    ns = s // tm
    vec = pl.BlockSpec((1, 1, d), lambda bi, si, dest, zt: (bi, 0, 0))
    grid_spec = pltpu.PrefetchScalarGridSpec(
        num_scalar_prefetch=2,
        grid=(b, ns),
        in_specs=[pl.BlockSpec((1, tm, d), lambda bi, si, dest, zt: (bi, si, 0)),
                  pl.BlockSpec((1, d), lambda bi, si, dest, zt: (0, 0)),
                  vec, vec],
        out_specs=pl.BlockSpec(memory_space=pl.ANY),
        scratch_shapes=[pltpu.VMEM((2, tm, d), F32), pltpu.VMEM((TM_EXPERT, d), F32),
                        pltpu.SemaphoreType.DMA((2,)), pltpu.SemaphoreType.DMA],
    )
    return pl.pallas_call(
        functools.partial(_scatter_kernel, tm=tm, ns=ns),
        grid_spec=grid_spec,
        out_shape=jax.ShapeDtypeStruct((m_pad, d), F32),
        compiler_params=_cparams(("arbitrary", "arbitrary")),
        name="moe_scatter",
    )(dest, zero_tile, h, gain.reshape(1, d), scale.reshape(b, 1, d), shift.reshape(b, 1, d))


def _expert_kernel(te_ref, tv_ref, tx_ref, x_ref, wg_ref, wu_ref, wd_ref, o_ref):
    t = pl.program_id(0)
    f = pl.program_id(1)

    @pl.when(tv_ref[t] == 1)
    def _():
        x = x_ref[...].astype(BF16)
        hid = (_silu(_dot(x, wg_ref[0])) * _dot(x, wu_ref[0])).astype(BF16)
        y = _dot(hid, wd_ref[0])

        @pl.when(f == 0)
        def _():
            o_ref[...] = y

        @pl.when(f > 0)
        def _():
            o_ref[...] = o_ref[...] + y

    @pl.when((tv_ref[t] == 0) & (f == 0))
    def _():
        o_ref[...] = jnp.zeros_like(o_ref)


def _experts(tile_expert, tile_valid, tile_x, xs, w_gate, w_up, w_down):
    m_pad, d = xs.shape
    tm = TM_EXPERT
    n_tiles = tile_expert.shape[0]
    ff = w_gate.shape[2]
    fs = FF_STEPS_EXPERT
    tf = ff // fs

    def ff_idx(f, tv, t):
        return f * tv[t] + (fs - 1) * (1 - tv[t])

    grid_spec = pltpu.PrefetchScalarGridSpec(
        num_scalar_prefetch=3,
        grid=(n_tiles, fs),
        in_specs=[pl.BlockSpec((tm, d), lambda t, f, te, tv, tx: (tx[t], 0)),
                  pl.BlockSpec((1, d, tf), lambda t, f, te, tv, tx: (te[t], 0, ff_idx(f, tv, t))),
                  pl.BlockSpec((1, d, tf), lambda t, f, te, tv, tx: (te[t], 0, ff_idx(f, tv, t))),
                  pl.BlockSpec((1, tf, d), lambda t, f, te, tv, tx: (te[t], ff_idx(f, tv, t), 0))],
        out_specs=pl.BlockSpec((tm, d), lambda t, f, te, tv, tx: (t, 0)),
    )
    return pl.pallas_call(
        _expert_kernel,
        grid_spec=grid_spec,
        out_shape=jax.ShapeDtypeStruct((m_pad, d), F32),
        compiler_params=_cparams(("arbitrary", "arbitrary")),
        name="moe_experts",
    )(tile_expert, tile_valid, tile_x, xs,
      w_gate.astype(BF16), w_up.astype(BF16), w_down.astype(BF16))


def _combine_kernel(dest_ref, y_ref, mf_ref, h_ref, gain_ref, gate_ref, o_ref, ybuf, sems, *, tm, ns):
    step = pl.program_id(0) * ns + pl.program_id(1)
    nsteps = pl.num_programs(0) * ns
    slot = lax.rem(step, 2)

    def issue(st, sl):
        base = st * tm

        def body(i, carry):
            t = 2 * (base + i)
            pltpu.make_async_copy(y_ref.at[pl.ds(dest_ref[t], 1), :],
                                  ybuf.at[sl, 0, pl.ds(i, 1), :], sems.at[sl]).start(priority=0)
            pltpu.make_async_copy(y_ref.at[pl.ds(dest_ref[t + 1], 1), :],
                                  ybuf.at[sl, 1, pl.ds(i, 1), :], sems.at[sl]).start(priority=1)
            return carry

        lax.fori_loop(0, tm, body, 0, unroll=DMA_UNROLL)

    @pl.when(step == 0)
    def _():
        issue(0, 0)

    def run(sl):
        @pl.when(step + 1 < nsteps)
        def _():
            issue(step + 1, 1 - sl)

        for k in range(2):
            pltpu.make_async_copy(ybuf.at[sl, k], ybuf.at[sl, k], sems.at[sl]).wait()
        mf = mf_ref[...]
        y = mf[:, 0:1] * ybuf[sl, 0] + mf[:, 1:2] * ybuf[sl, 1]
        o_ref[0] = h_ref[0] + gate_ref[0] * (_rms(y) * gain_ref[...])

    for sl in range(2):
        pl.when(slot == sl)(functools.partial(run, sl))


def _combine(dest, ys, mf, h, gain, gate):
    b, s, d = h.shape
    tm = TM_COMBINE
    ns = s // tm
    grid_spec = pltpu.PrefetchScalarGridSpec(
        num_scalar_prefetch=1,
        grid=(b, ns),
        in_specs=[pl.BlockSpec(memory_space=pl.ANY),
                  pl.BlockSpec((tm, LANES), lambda bi, si, dest: (bi * ns + si, 0)),
                  pl.BlockSpec((1, tm, d), lambda bi, si, dest: (bi, si, 0)),
                  pl.BlockSpec((1, d), lambda bi, si, dest: (0, 0)),
                  pl.BlockSpec((1, 1, d), lambda bi, si, dest: (bi, 0, 0))],
        out_specs=pl.BlockSpec((1, tm, d), lambda bi, si, dest: (bi, si, 0)),
        scratch_shapes=[pltpu.VMEM((2, 2, tm, d), F32), pltpu.SemaphoreType.DMA((2,))],
    )
    return pl.pallas_call(
        functools.partial(_combine_kernel, tm=tm, ns=ns),
        grid_spec=grid_spec,
        out_shape=jax.ShapeDtypeStruct((b, s, d), F32),
        compiler_params=_cparams(("arbitrary", "arbitrary")),
        name="moe_combine",
    )(dest, ys, mf, h, gain.reshape(1, d), gate.reshape(b, 1, d))


def _mixer_tail_moe(o, w_out, h, gain1, gate1, gain_in, scale, shift, router_w, w_gate, w_up, w_down,
                    gain_out, gate):
    b, s, d = h.shape
    n = b * s
    tm = TM_EXPERT
    h, mi, mf, cnt = _mixer_tail_router(o, w_out, h, gain1, gate1, gain_in, scale, shift, router_w)
    counts = cnt[0, :N_EXPERTS].astype(I32)
    tiles_per = (counts + tm - 1) // tm
    seg_start = (jnp.cumsum(tiles_per) - tiles_per) * tm
    dest = (seg_start[mi[:, 0:2]] + mi[:, 2:4]).reshape(2 * n)
    n_tiles = (2 * n) // tm + N_EXPERTS
    m_pad = n_tiles * tm
    tile_end = jnp.cumsum(tiles_per)
    tidx = jnp.arange(n_tiles, dtype=I32)
    tile_valid = (tidx < tile_end[-1]).astype(I32)
    tile_expert = jnp.minimum(jnp.searchsorted(tile_end, tidx, side="right"), N_EXPERTS - 1).astype(I32)
    tile_x = jnp.minimum(tidx, tile_end[-1] - 1)
    tail = tile_end[-1] + jnp.arange(N_EXPERTS, dtype=I32)
    zero_tile = jnp.concatenate([jnp.where(tiles_per > 0, tile_end - 1, -1),
                                 jnp.where(tail < n_tiles, tail, -1)]).astype(I32)
    xs = _scatter(dest, zero_tile, h, gain_in, scale, shift, m_pad)
    ys = _experts(tile_expert, tile_valid, tile_x, xs, w_gate, w_up, w_down)
    return _combine(dest, ys, mf, h, gain_out, gate)


def kernel(x, c, mod_w, mod_b, norm_g, attn_in_w_even, fox_gate_bias, attn_out_w_even,
           attn_in_w_odd, attn_out_w_odd, rel_bias_table, ffn_w_gate, ffn_w_up, ffn_w_down,
           router_w, exp_w_gate, exp_w_up, exp_w_down):
    depth = mod_w.shape[0]
    s_len = x.shape[1]
    mods = _mods(c, mod_w, mod_b)
    dil_bias, moba_bias = _bias_tiles(rel_bias_table, s_len)
    h = x
    for layer in range(depth):
        j = layer // 2
        sh1, sc1, g1, sh2, sc2, g2 = jnp.split(mods[layer], 6, axis=-1)
        gains = norm_g[layer]
        if layer % 2 == 0:
            fox_in, moba_in = _inproj_even(h, gains[0], sc1, sh1, attn_in_w_even[j], fox_gate_bias[j])
            o_parts = [_fox_attention(*fox_in), _moba_attention(*moba_in, moba_bias)]
            h = _mixer_tail_ffn(o_parts, attn_out_w_even[j], h, gains[1], g1, gains[2], sc2, sh2,
                                ffn_w_gate[j], ffn_w_up[j], ffn_w_down[j], gains[3], g2)
        else:
            q, k, v = _inproj_odd(h, gains[0], sc1, sh1, attn_in_w_odd[j])
            o = _dilated_attention(q, k, v, dil_bias)
            h = _mixer_tail_moe(o, attn_out_w_odd[j], h, gains[1], g1, gains[2], sc2, sh2, router_w[j],
                                exp_w_gate[j], exp_w_up[j], exp_w_down[j], gains[3], g2)
    return h
```
